```python
import math
import jax, jax.numpy as jnp
from jax import lax
import numpy as np

D_MODEL = 2048
BATCH = 8
SEQ = 4096
DEPTH = 4

GRID_W = 64
HEAD_DIM = 128
N_Q_HEADS = 16
N_KV_HEADS = 4
GROUP = N_Q_HEADS // N_KV_HEADS
ATTN_WIDTH = N_Q_HEADS * HEAD_DIM
KV_WIDTH = N_KV_HEADS * HEAD_DIM
CONV_WIDTH = D_MODEL
CONV_K = 3
D_FF = 4 * D_MODEL
Q_BLOCK = 128
ROPE_THETA = 10000.0
RMS_EPS = 1e-6
AXIS_DIM = HEAD_DIM // 2
N_FREQ = AXIS_DIM // 2
IN_SPLITS = (CONV_WIDTH, CONV_WIDTH, CONV_WIDTH, ATTN_WIDTH, KV_WIDTH, KV_WIDTH, D_MODEL, D_MODEL)
IN_WIDTH = sum(IN_SPLITS)
IN_OFFSETS = tuple(int(o) for o in np.cumsum(IN_SPLITS)[:-1])

kernel_name = 'hybrid_shortconv_gqa_axial_encoder'


def rmsnorm(x, g):
    xf = x.astype(jnp.float32)
    y = xf * lax.rsqrt(jnp.mean(xf * xf, axis=-1, keepdims=True) + RMS_EPS)
    return (y * g.astype(jnp.float32)).astype(x.dtype)


def axial_rope_tables(seq_len):
    rows = seq_len // GRID_W
    row_idx = jnp.repeat(jnp.arange(rows, dtype=jnp.int32), GRID_W)
    col_idx = jnp.tile(jnp.arange(GRID_W, dtype=jnp.int32), rows)
    inv_freq = ROPE_THETA ** (-jnp.arange(0, AXIS_DIM, 2, dtype=jnp.float32) / AXIS_DIM)
    ang = jnp.stack([row_idx.astype(jnp.float32)[:, None] * inv_freq,
                     col_idx.astype(jnp.float32)[:, None] * inv_freq], axis=1)
    return jnp.cos(ang), jnp.sin(ang)


def apply_axial_rope(x, cos, sin):
    b, s, h, _ = x.shape
    xr = x.astype(jnp.float32).reshape(b, s, h, 2, 2, N_FREQ)
    x1, x2 = xr[..., 0, :], xr[..., 1, :]
    c, sn = cos[None, :, None], sin[None, :, None]
    out = jnp.stack([x1 * c - x2 * sn, x2 * c + x1 * sn], axis=-2)
    return out.reshape(b, s, h, HEAD_DIM).astype(x.dtype)


def short_conv_mixer(conv_b, conv_c, h_in, w_conv, w_out):
    u = conv_c * h_in
    up = jnp.pad(u, ((0, 0), (1, 1), (0, 0)))
    conv = w_conv[0] * up[:, :-2] + w_conv[1] * up[:, 1:-1] + w_conv[2] * up[:, 2:]
    return (conv_b * conv) @ w_out


def block_gqa(q, k, v):
    b, s, _, _ = q.shape
    nb = s // Q_BLOCK
    scale = 1.0 / math.sqrt(HEAD_DIM)
    qb = (q * scale).reshape(b, nb, Q_BLOCK, N_KV_HEADS, GROUP, HEAD_DIM).transpose(1, 0, 2, 3, 4, 5)

    def one_block(q_blk):
        scores = jnp.einsum('bqkgd,bskd->bkgqs', q_blk, k).astype(jnp.float32)
        p = jax.nn.softmax(scores, axis=-1).astype(v.dtype)
        return jnp.einsum('bkgqs,bskd->bqkgd', p, v)

    o = lax.map(one_block, qb)
    return o.transpose(1, 0, 2, 3, 4, 5).reshape(b, s, ATTN_WIDTH)


def _fwd_setup_inputs(seed: int = 0) -> dict:
    key = jax.random.key(seed)
    ks = jax.random.split(key, 16)
    f32 = jnp.float32
    nrm = lambda k, shape, scale: jax.random.normal(k, shape, f32) * scale
    gain = lambda k, shape: 1.0 + 0.02 * jax.random.normal(k, shape, f32)
    return {
        'x': jax.random.normal(ks[0], (BATCH, SEQ, D_MODEL), f32),
        'norm_mix_pre': gain(ks[1], (DEPTH, D_MODEL)),
        'w_in': nrm(ks[2], (DEPTH, D_MODEL, IN_WIDTH), D_MODEL ** -0.5),
        'gate_bias': nrm(ks[3], (DEPTH, 2 * D_MODEL), 0.01),
        'conv_w': nrm(ks[4], (DEPTH, CONV_K, CONV_WIDTH), CONV_K ** -0.5),
        'q_norm': gain(ks[5], (DEPTH, HEAD_DIM)),
        'k_norm': gain(ks[6], (DEPTH, HEAD_DIM)),
        'w_out_conv': nrm(ks[7], (DEPTH, CONV_WIDTH, D_MODEL), CONV_WIDTH ** -0.5),
        'w_out_attn': nrm(ks[8], (DEPTH, ATTN_WIDTH, D_MODEL), ATTN_WIDTH ** -0.5),
        'w_merge': nrm(ks[9], (DEPTH, D_MODEL, D_MODEL), D_MODEL ** -0.5),
        'norm_mix_post': gain(ks[10], (DEPTH, D_MODEL)),
        'norm_mlp_pre': gain(ks[11], (DEPTH, D_MODEL)),
        'w_up': nrm(ks[12], (DEPTH, D_MODEL, D_FF), D_MODEL ** -0.5),
        'w_down': nrm(ks[13], (DEPTH, D_FF, D_MODEL), D_FF ** -0.5),
        'norm_mlp_post': gain(ks[14], (DEPTH, D_MODEL)),
    }


def _fwd_reference(x, norm_mix_pre, w_in, gate_bias, conv_w, q_norm, k_norm, w_out_conv, w_out_attn,
              w_merge, norm_mix_post, norm_mlp_pre, w_up, w_down, norm_mlp_post):
    b, s, _ = x.shape
    cos, sin = axial_rope_tables(s)
    for l in range(DEPTH):
        h = rmsnorm(x, norm_mix_pre[l])
        z = h @ w_in[l]
        conv_b, conv_c, conv_in, q, k, v, g_a, g_b = jnp.split(z, IN_OFFSETS, axis=-1)

        y_a = short_conv_mixer(conv_b, conv_c, conv_in, conv_w[l], w_out_conv[l])

        q = apply_axial_rope(rmsnorm(q.reshape(b, s, N_Q_HEADS, HEAD_DIM), q_norm[l]), cos, sin)
        k = apply_axial_rope(rmsnorm(k.reshape(b, s, N_KV_HEADS, HEAD_DIM), k_norm[l]), cos, sin)
        v = v.reshape(b, s, N_KV_HEADS, HEAD_DIM)
        y_b = block_gqa(q, k, v) @ w_out_attn[l]

        gates = jax.nn.sigmoid(jnp.concatenate([g_a, g_b], axis=-1) + gate_bias[l])
        gate_a, gate_b = jnp.split(gates, 2, axis=-1)
        mixed = (gate_a * y_a + gate_b * y_b) @ w_merge[l]
        x = x + rmsnorm(mixed, norm_mix_post[l])

        h = rmsnorm(x, norm_mlp_pre[l])
        f = jnp.square(jax.nn.relu(h @ w_up[l])) @ w_down[l]
        x = x + rmsnorm(f, norm_mlp_post[l])
    return x


import jax as _jax
import jax.numpy as _jnp

TWIN_FORMAT = 'train_step'
FWD_PARAMS = ['x', 'norm_mix_pre', 'w_in', 'gate_bias', 'conv_w', 'q_norm', 'k_norm', 'w_out_conv', 'w_out_attn', 'w_merge', 'norm_mix_post', 'norm_mlp_pre', 'w_up', 'w_down', 'norm_mlp_post']
TWIN_WEIGHTS = ['norm_mix_pre', 'w_in', 'gate_bias', 'conv_w', 'q_norm', 'k_norm', 'w_out_conv', 'w_out_attn', 'w_merge', 'norm_mix_post', 'norm_mlp_pre', 'w_up', 'w_down', 'norm_mlp_post']
TWIN_DIFF_INPUT = 'x'
TWIN_INPUTS = ['x', 'norm_mix_pre', 'w_in', 'gate_bias', 'conv_w', 'q_norm', 'k_norm', 'w_out_conv', 'w_out_attn', 'w_merge', 'norm_mix_post', 'norm_mlp_pre', 'w_up', 'w_down', 'norm_mlp_post', 'loss_target', 'm_norm_mix_pre', 'm_w_in', 'm_gate_bias', 'm_conv_w', 'm_q_norm', 'm_k_norm', 'm_w_out_conv', 'm_w_out_attn', 'm_w_merge', 'm_norm_mix_post', 'm_norm_mlp_pre', 'm_w_up', 'm_w_down', 'm_norm_mlp_post', 'v_norm_mix_pre', 'v_w_in', 'v_gate_bias', 'v_conv_w', 'v_q_norm', 'v_k_norm', 'v_w_out_conv', 'v_w_out_attn', 'v_w_merge', 'v_norm_mix_post', 'v_norm_mlp_pre', 'v_w_up', 'v_w_down', 'v_norm_mlp_post']
TWIN_OUTPUTS = ['loss', 'grad_x', 'grad_norm_mix_pre', 'grad_w_in', 'grad_gate_bias', 'grad_conv_w', 'grad_q_norm', 'grad_k_norm', 'grad_w_out_conv', 'grad_w_out_attn', 'grad_w_merge', 'grad_norm_mix_post', 'grad_norm_mlp_pre', 'grad_w_up', 'grad_w_down', 'grad_norm_mlp_post', 'delta_norm_mix_pre', 'delta_w_in', 'delta_gate_bias', 'delta_conv_w', 'delta_q_norm', 'delta_k_norm', 'delta_w_out_conv', 'delta_w_out_attn', 'delta_w_merge', 'delta_norm_mix_post', 'delta_norm_mlp_pre', 'delta_w_up', 'delta_w_down', 'delta_norm_mlp_post', 'new_m_norm_mix_pre', 'new_m_w_in', 'new_m_gate_bias', 'new_m_conv_w', 'new_m_q_norm', 'new_m_k_norm', 'new_m_w_out_conv', 'new_m_w_out_attn', 'new_m_w_merge', 'new_m_norm_mix_post', 'new_m_norm_mlp_pre', 'new_m_w_up', 'new_m_w_down', 'new_m_norm_mlp_post', 'new_v_norm_mix_pre', 'new_v_w_in', 'new_v_gate_bias', 'new_v_conv_w', 'new_v_q_norm', 'new_v_k_norm', 'new_v_w_out_conv', 'new_v_w_out_attn', 'new_v_w_merge', 'new_v_norm_mix_post', 'new_v_norm_mlp_pre', 'new_v_w_up', 'new_v_w_down', 'new_v_norm_mlp_post']
TWIN_LEAF_KINDS = {'loss': 'loss', 'grad_x': 'grad_x', 'grad_norm_mix_pre': 'grad_w', 'grad_w_in': 'grad_w', 'grad_gate_bias': 'grad_w', 'grad_conv_w': 'grad_w', 'grad_q_norm': 'grad_w', 'grad_k_norm': 'grad_w', 'grad_w_out_conv': 'grad_w', 'grad_w_out_attn': 'grad_w', 'grad_w_merge': 'grad_w', 'grad_norm_mix_post': 'grad_w', 'grad_norm_mlp_pre': 'grad_w', 'grad_w_up': 'grad_w', 'grad_w_down': 'grad_w', 'grad_norm_mlp_post': 'grad_w', 'delta_norm_mix_pre': 'delta_w', 'delta_w_in': 'delta_w', 'delta_gate_bias': 'delta_w', 'delta_conv_w': 'delta_w', 'delta_q_norm': 'delta_w', 'delta_k_norm': 'delta_w', 'delta_w_out_conv': 'delta_w', 'delta_w_out_attn': 'delta_w', 'delta_w_merge': 'delta_w', 'delta_norm_mix_post': 'delta_w', 'delta_norm_mlp_pre': 'delta_w', 'delta_w_up': 'delta_w', 'delta_w_down': 'delta_w', 'delta_norm_mlp_post': 'delta_w', 'new_m_norm_mix_pre': 'new_m', 'new_m_w_in': 'new_m', 'new_m_gate_bias': 'new_m', 'new_m_conv_w': 'new_m', 'new_m_q_norm': 'new_m', 'new_m_k_norm': 'new_m', 'new_m_w_out_conv': 'new_m', 'new_m_w_out_attn': 'new_m', 'new_m_w_merge': 'new_m', 'new_m_norm_mix_post': 'new_m', 'new_m_norm_mlp_pre': 'new_m', 'new_m_w_up': 'new_m', 'new_m_w_down': 'new_m', 'new_m_norm_mlp_post': 'new_m', 'new_v_norm_mix_pre': 'new_v', 'new_v_w_in': 'new_v', 'new_v_gate_bias': 'new_v', 'new_v_conv_w': 'new_v', 'new_v_q_norm': 'new_v', 'new_v_k_norm': 'new_v', 'new_v_w_out_conv': 'new_v', 'new_v_w_out_attn': 'new_v', 'new_v_w_merge': 'new_v', 'new_v_norm_mix_post': 'new_v', 'new_v_norm_mlp_pre': 'new_v', 'new_v_w_up': 'new_v', 'new_v_w_down': 'new_v', 'new_v_norm_mlp_post': 'new_v'}


def _forward(args):
    return _fwd_reference(*[args[k] for k in FWD_PARAMS])


def _output_shape():
    def fwd():
        inp = _fwd_setup_inputs(0)
        return _fwd_reference(*[inp[k] for k in FWD_PARAMS])
    out = _jax.eval_shape(fwd)
    return out.shape, out.dtype

N_MICROBATCH = 1
ADAM_LR = 0.001
ADAM_B1 = 0.9
ADAM_B2 = 0.999
ADAM_EPS = 1e-08
ADAM_WD = 0.01
ADAM_STEP = 10
PER_EXAMPLE_BATCH_AXIS = {'x': 0, 'loss_target': 0}
SHARED_INPUTS = []
_WEIGHT_DTYPES = {'norm_mix_pre': _jnp.float32, 'w_in': _jnp.float32, 'gate_bias': _jnp.float32, 'conv_w': _jnp.float32, 'q_norm': _jnp.float32, 'k_norm': _jnp.float32, 'w_out_conv': _jnp.float32, 'w_out_attn': _jnp.float32, 'w_merge': _jnp.float32, 'norm_mix_post': _jnp.float32, 'norm_mlp_pre': _jnp.float32, 'w_up': _jnp.float32, 'w_down': _jnp.float32, 'norm_mlp_post': _jnp.float32}
MOMENT_SCALE = {'norm_mix_pre': 3.261886e+00, 'w_in': 1.276387e+00, 'gate_bias': 1.045837e+00, 'conv_w': 6.773231e-01, 'q_norm': 2.842276e-01, 'k_norm': 2.737540e-01, 'w_out_conv': 6.994905e-01, 'w_out_attn': 3.715397e+00, 'w_merge': 3.457574e+00, 'norm_mix_post': 1.640467e+01, 'norm_mlp_pre': 2.375648e+00, 'w_up': 1.180869e+00, 'w_down': 7.383603e+00, 'norm_mlp_post': 1.806315e+01}


def _to_microbatches(a, axis):
    t = _jnp.moveaxis(a, axis, 0)
    t = t.reshape((N_MICROBATCH, t.shape[0] // N_MICROBATCH) + t.shape[1:])
    return _jnp.moveaxis(t, 1, axis + 1)


def setup_inputs(seed: int = 0) -> dict:
    inp = _fwd_setup_inputs(seed)
    key = _jax.random.fold_in(_jax.random.key(seed), 7919)
    shape, _ = _output_shape()
    out = dict(inp)
    out["loss_target"] = _jax.random.normal(_jax.random.fold_in(key, 0), shape, _jnp.float32)
    for i, name in enumerate(TWIN_WEIGHTS):
        w = inp[name].astype(_jnp.float32)
        if MOMENT_SCALE is None:
            s = _jnp.sqrt(_jnp.mean(_jnp.square(w)) + 1e-30)
        else:
            s = MOMENT_SCALE[name]
        km, kv = _jax.random.split(_jax.random.fold_in(key, i + 1))
        out[name] = w
        out["m_" + name] = s * _jax.random.normal(km, w.shape, _jnp.float32)
        out["v_" + name] = (s * s) * _jax.random.uniform(kv, w.shape, _jnp.float32, 0.5, 1.5)
    if N_MICROBATCH > 1:
        for name, axis in PER_EXAMPLE_BATCH_AXIS.items():
            out[name] = _to_microbatches(out[name], axis)
    return {'x': out['x'], 'norm_mix_pre': out['norm_mix_pre'], 'w_in': out['w_in'], 'gate_bias': out['gate_bias'], 'conv_w': out['conv_w'], 'q_norm': out['q_norm'], 'k_norm': out['k_norm'], 'w_out_conv': out['w_out_conv'], 'w_out_attn': out['w_out_attn'], 'w_merge': out['w_merge'], 'norm_mix_post': out['norm_mix_post'], 'norm_mlp_pre': out['norm_mlp_pre'], 'w_up': out['w_up'], 'w_down': out['w_down'], 'norm_mlp_post': out['norm_mlp_post'], 'loss_target': out['loss_target'], 'm_norm_mix_pre': out['m_norm_mix_pre'], 'm_w_in': out['m_w_in'], 'm_gate_bias': out['m_gate_bias'], 'm_conv_w': out['m_conv_w'], 'm_q_norm': out['m_q_norm'], 'm_k_norm': out['m_k_norm'], 'm_w_out_conv': out['m_w_out_conv'], 'm_w_out_attn': out['m_w_out_attn'], 'm_w_merge': out['m_w_merge'], 'm_norm_mix_post': out['m_norm_mix_post'], 'm_norm_mlp_pre': out['m_norm_mlp_pre'], 'm_w_up': out['m_w_up'], 'm_w_down': out['m_w_down'], 'm_norm_mlp_post': out['m_norm_mlp_post'], 'v_norm_mix_pre': out['v_norm_mix_pre'], 'v_w_in': out['v_w_in'], 'v_gate_bias': out['v_gate_bias'], 'v_conv_w': out['v_conv_w'], 'v_q_norm': out['v_q_norm'], 'v_k_norm': out['v_k_norm'], 'v_w_out_conv': out['v_w_out_conv'], 'v_w_out_attn': out['v_w_out_attn'], 'v_w_merge': out['v_w_merge'], 'v_norm_mix_post': out['v_norm_mix_post'], 'v_norm_mlp_pre': out['v_norm_mlp_pre'], 'v_w_up': out['v_w_up'], 'v_w_down': out['v_w_down'], 'v_norm_mlp_post': out['v_norm_mlp_post']}


def _loss(weights, diff, rest, loss_target):
    with _jax.named_scope("forward"):
        args = {**rest, TWIN_DIFF_INPUT: diff, **{k: w.astype(_WEIGHT_DTYPES[k]) for k, w in weights.items()}}
        y = _forward(args)
    with _jax.named_scope("loss_head"):
        err = _jnp.square(y.astype(_jnp.float32) - loss_target)
        return 0.5 * _jnp.sum(_jnp.mean(err, axis=-1)) if err.ndim else 0.5 * err


def _adamw(w, g, m, v):
    m = ADAM_B1 * m + (1.0 - ADAM_B1) * g
    v = ADAM_B2 * v + (1.0 - ADAM_B2) * _jnp.square(g)
    m_hat = m / (1.0 - ADAM_B1 ** ADAM_STEP)
    v_hat = v / (1.0 - ADAM_B2 ** ADAM_STEP)
    delta = -ADAM_LR * (m_hat / (_jnp.sqrt(v_hat) + ADAM_EPS) + ADAM_WD * w)
    return delta, m, v


def reference(x, norm_mix_pre, w_in, gate_bias, conv_w, q_norm, k_norm, w_out_conv, w_out_attn, w_merge, norm_mix_post, norm_mlp_pre, w_up, w_down, norm_mlp_post, loss_target, m_norm_mix_pre, m_w_in, m_gate_bias, m_conv_w, m_q_norm, m_k_norm, m_w_out_conv, m_w_out_attn, m_w_merge, m_norm_mix_post, m_norm_mlp_pre, m_w_up, m_w_down, m_norm_mlp_post, v_norm_mix_pre, v_w_in, v_gate_bias, v_conv_w, v_q_norm, v_k_norm, v_w_out_conv, v_w_out_attn, v_w_merge, v_norm_mix_post, v_norm_mlp_pre, v_w_up, v_w_down, v_norm_mlp_post):
    given = dict(x=x, norm_mix_pre=norm_mix_pre, w_in=w_in, gate_bias=gate_bias, conv_w=conv_w, q_norm=q_norm, k_norm=k_norm, w_out_conv=w_out_conv, w_out_attn=w_out_attn, w_merge=w_merge, norm_mix_post=norm_mix_post, norm_mlp_pre=norm_mlp_pre, w_up=w_up, w_down=w_down, norm_mlp_post=norm_mlp_post, loss_target=loss_target, m_norm_mix_pre=m_norm_mix_pre, m_w_in=m_w_in, m_gate_bias=m_gate_bias, m_conv_w=m_conv_w, m_q_norm=m_q_norm, m_k_norm=m_k_norm, m_w_out_conv=m_w_out_conv, m_w_out_attn=m_w_out_attn, m_w_merge=m_w_merge, m_norm_mix_post=m_norm_mix_post, m_norm_mlp_pre=m_norm_mlp_pre, m_w_up=m_w_up, m_w_down=m_w_down, m_norm_mlp_post=m_norm_mlp_post, v_norm_mix_pre=v_norm_mix_pre, v_w_in=v_w_in, v_gate_bias=v_gate_bias, v_conv_w=v_conv_w, v_q_norm=v_q_norm, v_k_norm=v_k_norm, v_w_out_conv=v_w_out_conv, v_w_out_attn=v_w_out_attn, v_w_merge=v_w_merge, v_norm_mix_post=v_norm_mix_post, v_norm_mlp_pre=v_norm_mlp_pre, v_w_up=v_w_up, v_w_down=v_w_down, v_norm_mlp_post=v_norm_mlp_post)
    weights = {n: given[n] for n in TWIN_WEIGHTS}
    shared = {n: given[n] for n in SHARED_INPUTS}
    per_example = {n: given[n] for n in ['x']}
    grad_fn = _jax.value_and_grad(_loss, argnums=(0, 1))

    def one_microbatch(ex, loss_target):
        ex = dict(ex)
        diff = ex.pop(TWIN_DIFF_INPUT)
        return grad_fn(weights, diff, {**shared, **ex}, loss_target)

    if N_MICROBATCH == 1:
        loss, (grad_w, grad_x) = one_microbatch(per_example, given["loss_target"])
    else:
        def body(carry, xs):
            loss_sum, grad_sum = carry
            l_k, (gw_k, gx_k) = one_microbatch(xs[0], xs[1])
            with _jax.named_scope("update"):
                return (loss_sum + l_k, _jax.tree.map(_jnp.add, grad_sum, gw_k)), gx_k

        init = (_jnp.zeros((), _jnp.float32), _jax.tree.map(_jnp.zeros_like, weights))
        (loss, grad_w), grad_x = _jax.lax.scan(body, init, (per_example, given["loss_target"]))
    with _jax.named_scope("update"):
        delta_w, new_m, new_v = {}, {}, {}
        for n in TWIN_WEIGHTS:
            delta_w[n], new_m[n], new_v[n] = _adamw(weights[n], grad_w[n], given["m_" + n], given["v_" + n])
    return (loss, grad_x, *[grad_w[n] for n in TWIN_WEIGHTS], *[delta_w[n] for n in TWIN_WEIGHTS],
            *[new_m[n] for n in TWIN_WEIGHTS], *[new_v[n] for n in TWIN_WEIGHTS])
```

```python
import functools
import math

import jax
import jax.numpy as jnp
from jax import lax
from jax.experimental import pallas as pl
from jax.experimental.pallas import tpu as pltpu

F32 = jnp.float32
BF16 = jnp.bfloat16

HEAD_DIM = 128
GROUP = 4
GRID_W = 64
ROPE_THETA = 10000.0
RMS_EPS = 1e-6
ADAM_LR = 0.001
ADAM_B1 = 0.9
ADAM_B2 = 0.999
ADAM_EPS = 1e-08
ADAM_WD = 0.01
ADAM_STEP = 10

LANES = 128
N_CHIPS = 4
N_DEV = 8
VMEM_LIMIT_BYTES = 56 * 1024 * 1024
MESH = pl.DeviceIdType.MESH
ANY = pl.BlockSpec(memory_space=pl.ANY)


def _tile(dim, cap, mult):
    if dim <= cap:
        return dim
    t = (cap // mult) * mult
    while t >= mult:
        if dim % t == 0:
            return t
        t -= mult
    raise ValueError(f"no tile for {dim} under {cap} in multiples of {mult}")


def _params(sem=None):
    return pltpu.CompilerParams(dimension_semantics=sem, vmem_limit_bytes=VMEM_LIMIT_BYTES)


def _sds(shape, dtype):
    return jax.ShapeDtypeStruct(tuple(shape), dtype)


def _rstd(x):
    return lax.rsqrt(jnp.mean(x * x, axis=-1, keepdims=True) + RMS_EPS)


_DOT_DIMS = {"nn": ((1,), (0,)), "nt": ((1,), (1,)), "tn": ((0,), (0,))}


def _mm(a, b, mode, out_dtypes, *, name, epilogue=None, extras=(), tm=1024, tn=1024, tk=2048):
    if mode == "nn":
        (m, k), (k2, n) = a.shape, b.shape
    elif mode == "nt":
        (m, k), (n, k2) = a.shape, b.shape
    else:
        (k, m), (k2, n) = a.shape, b.shape
    assert k == k2, (a.shape, b.shape, mode)
    tm, tn, tk = _tile(m, tm, 8), _tile(n, tn, LANES), _tile(k, tk, LANES)
    nk = k // tk
    a_spec = pl.BlockSpec((tk, tm), lambda i, j, kk: (kk, i)) if mode == "tn" else pl.BlockSpec((tm, tk), lambda i, j, kk: (i, kk))
    b_spec = pl.BlockSpec((tn, tk), lambda i, j, kk: (j, kk)) if mode == "nt" else pl.BlockSpec((tk, tn), lambda i, j, kk: (kk, j))
    tile_spec = pl.BlockSpec((tm, tn), lambda i, j, kk: (i, j))
    n_extra, n_out = len(extras), len(out_dtypes)
    dims = (_DOT_DIMS[mode], ((), ()))

    def body(*refs):
        a_ref, b_ref = refs[:2]
        extra_refs = refs[2:2 + n_extra]
        out_refs = refs[2 + n_extra:2 + n_extra + n_out]
        part = lax.dot_general(a_ref[...].astype(BF16), b_ref[...].astype(BF16), dims, preferred_element_type=F32)

        def finish(total):
            res = epilogue(total, *[e[...] for e in extra_refs]) if epilogue is not None else (total,)
            for o, r in zip(out_refs, res):
                o[...] = r.astype(o.dtype)

        if nk == 1:
            finish(part)
        else:
            acc = refs[-1]
            kk = pl.program_id(2)

            @pl.when(kk == 0)
            def _():
                acc[...] = part

            @pl.when(kk > 0)
            def _():
                acc[...] += part

            @pl.when(kk == nk - 1)
            def _():
                finish(acc[...])

    outs = pl.pallas_call(
        body, name=name, grid=(m // tm, n // tn, nk),
        in_specs=[a_spec, b_spec] + [tile_spec] * n_extra,
        out_specs=[tile_spec] * n_out,
        out_shape=[_sds((m, n), d) for d in out_dtypes],
        scratch_shapes=[pltpu.VMEM((tm, tn), F32)] if nk > 1 else [],
        compiler_params=_params(("parallel", "parallel", "arbitrary")),
    )(a, b, *extras)
    return outs if n_out > 1 else outs[0]


ROW_TILE = 256


def _norm_first(x, g):
    s, d = x.shape
    ts = _tile(s, ROW_TILE, 8)

    def body(x_ref, g_ref, h_ref):
        xv = x_ref[...]
        h_ref[...] = (xv * _rstd(xv) * g_ref[...]).astype(h_ref.dtype)

    row = pl.BlockSpec((ts, d), lambda i: (i, 0))
    vec = pl.BlockSpec((1, d), lambda i: (0, 0))
    return pl.pallas_call(body, name="norm_first", grid=(s // ts,), in_specs=[row, vec], out_specs=row,
                          out_shape=_sds((s, d), BF16), compiler_params=_params(("parallel",)))(x, g)


def _resid_norm(xres, y, g_post, g_next):
    s, d = xres.shape
    ts = _tile(s, ROW_TILE, 8)

    def body(x_ref, y_ref, gp_ref, gn_ref, xn_ref, hn_ref):
        yv = y_ref[...]
        xn = x_ref[...] + yv * _rstd(yv) * gp_ref[...]
        xn_ref[...] = xn
        hn_ref[...] = (xn * _rstd(xn) * gn_ref[...]).astype(hn_ref.dtype)

    row = pl.BlockSpec((ts, d), lambda i: (i, 0))
    vec = pl.BlockSpec((1, d), lambda i: (0, 0))
    return pl.pallas_call(body, name="resid_norm", grid=(s // ts,), in_specs=[row, row, vec, vec], out_specs=[row, row],
                          out_shape=[_sds((s, d), F32), _sds((s, d), BF16)], compiler_params=_params(("parallel",)))(xres, y, g_post, g_next)


def _resid_norm_loss(xres, y, g_post, target):
    s, d = xres.shape
    ts = _tile(s, ROW_TILE, 8)
    n_steps = s // ts

    def body(x_ref, y_ref, gp_ref, t_ref, dout_ref, loss_ref, acc_ref):
        i = pl.program_id(0)
        yv = y_ref[...]
        err = x_ref[...] + yv * _rstd(yv) * gp_ref[...] - t_ref[...]
        dout_ref[...] = err / d
        part = jnp.sum(err * err, axis=0, keepdims=True)

        @pl.when(i == 0)
        def _():
            acc_ref[...] = part

        @pl.when(i > 0)
        def _():
            acc_ref[...] += part

        @pl.when(i == n_steps - 1)
        def _():
            loss_ref[...] = 0.5 * jnp.sum(acc_ref[...], axis=1, keepdims=True) / d

    row = pl.BlockSpec((ts, d), lambda i: (i, 0))
    vec = pl.BlockSpec((1, d), lambda i: (0, 0))
    one = pl.BlockSpec((1, 1), lambda i: (0, 0))
    return pl.pallas_call(body, name="resid_norm_loss", grid=(n_steps,), in_specs=[row, row, vec, row], out_specs=[row, one],
                          out_shape=[_sds((s, d), F32), _sds((1, 1), F32)], scratch_shapes=[pltpu.VMEM((1, d), F32)],
                          compiler_params=_params(("arbitrary",)))(xres, y, g_post, target)


def _norm_bwd(xin, g, dout, dres, out_dtype, name):
    s, d = xin.shape
    ts = _tile(s, ROW_TILE, 8)
    has_res = dres is not None

    def body(*refs):
        x_ref, g_ref, do_ref = refs[:3]
        dx_ref, dg_ref = refs[-2:]
        i = pl.program_id(0)
        xv, dov = x_ref[...], do_ref[...]
        r = _rstd(xv)
        xhat = xv * r
        dg = jnp.sum(dov * xhat, axis=0, keepdims=True)
        dxh = dov * g_ref[...]
        dx = r * (dxh - xhat * jnp.mean(dxh * xhat, axis=-1, keepdims=True))
        if has_res:
            dx = dx + refs[3][...]
        dx_ref[...] = dx.astype(dx_ref.dtype)

        @pl.when(i == 0)
        def _():
            dg_ref[...] = dg

        @pl.when(i > 0)
        def _():
            dg_ref[...] += dg

    row = pl.BlockSpec((ts, d), lambda i: (i, 0))
    vec = pl.BlockSpec((1, d), lambda i: (0, 0))
    ops = [xin, g, dout] + ([dres] if has_res else [])
    return pl.pallas_call(body, name=name, grid=(s // ts,), in_specs=[row, vec, row] + ([row] if has_res else []),
                          out_specs=[row, vec], out_shape=[_sds((s, d), out_dtype), _sds((1, d), F32)],
                          compiler_params=_params(("arbitrary",)))(*ops)


class _Cols:
    def __init__(self, d):
        self.d = d
        self.kv = d // GROUP
        self.cb, self.cc, self.ci, self.q = 0, d, 2 * d, 3 * d
        self.k = 4 * d
        self.v = 4 * d + self.kv
        self.ga = 4 * d + 2 * self.kv
        self.gb = 5 * d + 2 * self.kv
        self.width = 6 * d + 2 * self.kv


CONV_COLS = 128


def _shift_rows(u, down):
    s = u.shape[0]
    rows = lax.broadcasted_iota(jnp.int32, u.shape, 0)
    if down:
        return jnp.where(rows == 0, 0.0, pltpu.roll(u, 1, 0))
    return jnp.where(rows == s - 1, 0.0, pltpu.roll(u, s - 1, 0))


def _conv_fwd(z, w, cols):
    s, d = z.shape[0], cols.d
    cw = CONV_COLS

    def body(cb_ref, cc_ref, ci_ref, w_ref, a_ref):
        u = cc_ref[...] * ci_ref[...]
        wv = w_ref[...]
        conv = wv[0:1] * _shift_rows(u, True) + wv[1:2] * u + wv[2:3] * _shift_rows(u, False)
        a_ref[...] = (cb_ref[...] * conv).astype(a_ref.dtype)

    def zspec(off):
        return pl.BlockSpec((s, cw), lambda j: (0, off // cw + j))

    return pl.pallas_call(body, name="conv_fwd", grid=(d // cw,),
                          in_specs=[zspec(cols.cb), zspec(cols.cc), zspec(cols.ci), pl.BlockSpec((3, cw), lambda j: (0, j))],
                          out_specs=pl.BlockSpec((s, cw), lambda j: (0, j)), out_shape=_sds((s, d), BF16),
                          compiler_params=_params(("parallel",)))(z, z, z, w)


def _conv_bwd(z, w, da, cols):
    s, d = z.shape[0], cols.d
    cw = CONV_COLS

    def body(cb_ref, cc_ref, ci_ref, w_ref, da_ref, dcb_ref, dcc_ref, dci_ref, dw_ref):
        cb, cc, ci, dav = cb_ref[...], cc_ref[...], ci_ref[...], da_ref[...]
        wv = w_ref[...]
        u = cc * ci
        um, up = _shift_rows(u, True), _shift_rows(u, False)
        conv = wv[0:1] * um + wv[1:2] * u + wv[2:3] * up
        dcb_ref[...] = (dav * conv).astype(dcb_ref.dtype)
        dconv = dav * cb
        dw_ref[0:1, :] = jnp.sum(dconv * um, axis=0, keepdims=True)
        dw_ref[1:2, :] = jnp.sum(dconv * u, axis=0, keepdims=True)
        dw_ref[2:3, :] = jnp.sum(dconv * up, axis=0, keepdims=True)
        du = wv[0:1] * _shift_rows(dconv, False) + wv[1:2] * dconv + wv[2:3] * _shift_rows(dconv, True)
        dcc_ref[...] = (du * ci).astype(dcc_ref.dtype)
        dci_ref[...] = (du * cc).astype(dci_ref.dtype)

    def zspec(off):
        return pl.BlockSpec((s, cw), lambda j: (0, off // cw + j))

    col = pl.BlockSpec((s, cw), lambda j: (0, j))
    wspec = pl.BlockSpec((3, cw), lambda j: (0, j))
    return pl.pallas_call(body, name="conv_bwd", grid=(d // cw,),
                          in_specs=[zspec(cols.cb), zspec(cols.cc), zspec(cols.ci), wspec, col],
                          out_specs=[col, col, col, wspec],
                          out_shape=[_sds((s, d), BF16)] * 3 + [_sds((3, d), F32)],
                          compiler_params=_params(("parallel",)))(z, z, z, w, da)


def _gate_fwd(z, bias, y_a, y_b, cols):
    s, d = y_a.shape
    ts, cw = _tile(s, ROW_TILE, 8), cols.kv
    nj = d // cw

    def body(ga_ref, gb_ref, ba_ref, bb_ref, ya_ref, yb_ref, o_ref):
        gate_a = jax.nn.sigmoid(ga_ref[...] + ba_ref[...])
        gate_b = jax.nn.sigmoid(gb_ref[...] + bb_ref[...])
        o_ref[...] = (gate_a * ya_ref[...] + gate_b * yb_ref[...]).astype(o_ref.dtype)

    tile = pl.BlockSpec((ts, cw), lambda i, j: (i, j))
    return pl.pallas_call(
        body, name="gate_fwd", grid=(s // ts, nj),
        in_specs=[pl.BlockSpec((ts, cw), lambda i, j: (i, cols.ga // cw + j)), pl.BlockSpec((ts, cw), lambda i, j: (i, cols.gb // cw + j)),
                  pl.BlockSpec((1, cw), lambda i, j: (0, j)), pl.BlockSpec((1, cw), lambda i, j: (0, nj + j)), tile, tile],
        out_specs=tile, out_shape=_sds((s, d), BF16), compiler_params=_params(("parallel", "parallel")))(z, z, bias, bias, y_a, y_b)


def _gate_bwd(z, bias, y_a, y_b, dmix, cols):
    s, d = y_a.shape
    ts, cw = _tile(s, ROW_TILE, 8), cols.kv
    nj = d // cw

    def body(ga_ref, gb_ref, ba_ref, bb_ref, ya_ref, yb_ref, dm_ref, dya_ref, dyb_ref, dga_ref, dgb_ref, dba_ref, dbb_ref):
        i = pl.program_id(1)
        gate_a = jax.nn.sigmoid(ga_ref[...] + ba_ref[...])
        gate_b = jax.nn.sigmoid(gb_ref[...] + bb_ref[...])
        dm = dm_ref[...]
        dya_ref[...] = (dm * gate_a).astype(dya_ref.dtype)
        dyb_ref[...] = (dm * gate_b).astype(dyb_ref.dtype)
        dga = dm * ya_ref[...] * (gate_a * (1.0 - gate_a))
        dgb = dm * yb_ref[...] * (gate_b * (1.0 - gate_b))
        dga_ref[...] = dga.astype(dga_ref.dtype)
        dgb_ref[...] = dgb.astype(dgb_ref.dtype)
        sa = jnp.sum(dga, axis=0, keepdims=True)
        sb = jnp.sum(dgb, axis=0, keepdims=True)

        @pl.when(i == 0)
        def _():
            dba_ref[...] = sa
            dbb_ref[...] = sb

        @pl.when(i > 0)
        def _():
            dba_ref[...] += sa
            dbb_ref[...] += sb

    tile = pl.BlockSpec((ts, cw), lambda j, i: (i, j))
    vec = pl.BlockSpec((1, cw), lambda j, i: (0, j))
    return pl.pallas_call(
        body, name="gate_bwd", grid=(nj, s // ts),
        in_specs=[pl.BlockSpec((ts, cw), lambda j, i: (i, cols.ga // cw + j)), pl.BlockSpec((ts, cw), lambda j, i: (i, cols.gb // cw + j)),
                  vec, pl.BlockSpec((1, cw), lambda j, i: (0, nj + j)), tile, tile, tile],
        out_specs=[tile, tile, tile, tile, vec, vec],
        out_shape=[_sds((s, d), BF16)] * 4 + [_sds((1, d), F32)] * 2,
        compiler_params=_params(("parallel", "arbitrary")))(z, z, bias, bias, y_a, y_b, dmix)


def _rope_tables(s):
    axis_dim = HEAD_DIM // 2
    n_freq = axis_dim // 2
    rows = s // GRID_W
    row_idx = jnp.repeat(jnp.arange(rows, dtype=jnp.int32), GRID_W)
    col_idx = jnp.tile(jnp.arange(GRID_W, dtype=jnp.int32), rows)
    inv_freq = ROPE_THETA ** (-jnp.arange(0, axis_dim, 2, dtype=F32) / axis_dim)
    ang = jnp.stack([row_idx.astype(F32)[:, None] * inv_freq, col_idx.astype(F32)[:, None] * inv_freq], axis=1)
    cos, sin = jnp.cos(ang), jnp.sin(ang)
    cos_t = jnp.stack([cos, cos], axis=2).reshape(s, HEAD_DIM)
    sin_t = jnp.stack([-sin, sin], axis=2).reshape(s, HEAD_DIM)
    return cos_t, sin_t


def _partner(x):
    n = x.shape[-1]
    lane = lax.broadcasted_iota(jnp.int32, x.shape, x.ndim - 1)
    quarter = HEAD_DIM // 4
    return jnp.where(lane % (2 * quarter) < quarter, pltpu.roll(x, n - quarter, x.ndim - 1), pltpu.roll(x, quarter, x.ndim - 1))


def _rope_fwd(z, qn, kn, cos_t, sin_t, cols):
    s, d, kv = z.shape[0], cols.d, cols.kv
    ts = _tile(s, ROW_TILE, 8)
    scale = 1.0 / math.sqrt(HEAD_DIM)

    def body(q_ref, k_ref, v_ref, qn_ref, kn_ref, c_ref, s_ref, qo_ref, ko_ref, vo_ref):
        c, sn = c_ref[...], s_ref[...]

        def head(xh, g):
            xn = xh * _rstd(xh) * g
            return xn * c + _partner(xn) * sn

        for h in range(d // HEAD_DIM):
            sl = slice(h * HEAD_DIM, (h + 1) * HEAD_DIM)
            qo_ref[:, sl] = (head(q_ref[:, sl], qn_ref[...]) * scale).astype(qo_ref.dtype)
        for h in range(kv // HEAD_DIM):
            sl = slice(h * HEAD_DIM, (h + 1) * HEAD_DIM)
            ko_ref[:, sl] = head(k_ref[:, sl], kn_ref[...]).astype(ko_ref.dtype)
        vo_ref[...] = v_ref[...].astype(vo_ref.dtype)

    vec = pl.BlockSpec((1, HEAD_DIM), lambda i: (0, 0))
    tab = pl.BlockSpec((ts, HEAD_DIM), lambda i: (i, 0))
    return pl.pallas_call(
        body, name="rope_fwd", grid=(s // ts,),
        in_specs=[pl.BlockSpec((ts, d), lambda i: (i, cols.q // d)), pl.BlockSpec((ts, kv), lambda i: (i, cols.k // kv)),
                  pl.BlockSpec((ts, kv), lambda i: (i, cols.v // kv)), vec, vec, tab, tab],
        out_specs=[pl.BlockSpec((ts, d), lambda i: (i, 0)), pl.BlockSpec((ts, kv), lambda i: (i, 0)), pl.BlockSpec((ts, kv), lambda i: (i, 0))],
        out_shape=[_sds((s, d), BF16), _sds((s, kv), BF16), _sds((s, kv), BF16)],
        compiler_params=_params(("parallel",)))(z, z, z, qn, kn, cos_t, sin_t)


def _rope_bwd(z, qn, kn, cos_t, sin_t, dq, dk, dv, cols):
    s, d, kv = z.shape[0], cols.d, cols.kv
    ts = _tile(s, ROW_TILE, 8)
    scale = 1.0 / math.sqrt(HEAD_DIM)

    def body(q_ref, k_ref, qn_ref, kn_ref, c_ref, s_ref, dq_ref, dk_ref, dv_ref, dzq_ref, dzk_ref, dzv_ref, dqn_ref, dkn_ref):
        i = pl.program_id(0)
        c, sn = c_ref[...], s_ref[...]

        def head_bwd(xh, g, drot):
            dxn = drot * c + _partner(drot * sn)
            r = _rstd(xh)
            xhat = xh * r
            dgain = jnp.sum(dxn * xhat, axis=0, keepdims=True)
            dxh = dxn * g
            return r * (dxh - xhat * jnp.mean(dxh * xhat, axis=-1, keepdims=True)), dgain

        dqn = jnp.zeros((1, HEAD_DIM), F32)
        for h in range(d // HEAD_DIM):
            sl = slice(h * HEAD_DIM, (h + 1) * HEAD_DIM)
            dx, dg = head_bwd(q_ref[:, sl], qn_ref[...], dq_ref[:, sl] * scale)
            dzq_ref[:, sl] = dx.astype(dzq_ref.dtype)
            dqn = dqn + dg
        dkn = jnp.zeros((1, HEAD_DIM), F32)
        for h in range(kv // HEAD_DIM):
            sl = slice(h * HEAD_DIM, (h + 1) * HEAD_DIM)
            dx, dg = head_bwd(k_ref[:, sl], kn_ref[...], dk_ref[:, sl])
            dzk_ref[:, sl] = dx.astype(dzk_ref.dtype)
            dkn = dkn + dg
        dzv_ref[...] = dv_ref[...].astype(dzv_ref.dtype)

        @pl.when(i == 0)
        def _():
            dqn_ref[...] = dqn
            dkn_ref[...] = dkn

        @pl.when(i > 0)
        def _():
            dqn_ref[...] += dqn
            dkn_ref[...] += dkn

    vec = pl.BlockSpec((1, HEAD_DIM), lambda i: (0, 0))
    tab = pl.BlockSpec((ts, HEAD_DIM), lambda i: (i, 0))
    qrow = pl.BlockSpec((ts, d), lambda i: (i, 0))
    krow = pl.BlockSpec((ts, kv), lambda i: (i, 0))
    return pl.pallas_call(
        body, name="rope_bwd", grid=(s // ts,),
        in_specs=[pl.BlockSpec((ts, d), lambda i: (i, cols.q // d)), pl.BlockSpec((ts, kv), lambda i: (i, cols.k // kv)),
                  vec, vec, tab, tab, qrow, krow, krow],
        out_specs=[qrow, krow, krow, vec, vec],
        out_shape=[_sds((s, d), BF16), _sds((s, kv), BF16), _sds((s, kv), BF16), _sds((1, HEAD_DIM), F32), _sds((1, HEAD_DIM), F32)],
        compiler_params=_params(("arbitrary",)))(z, z, qn, kn, cos_t, sin_t, dq, dk, dv)


Q_TILE = 256
_NT = (((1,), (1,)), ((), ()))
_NN = (((1,), (0,)), ((), ()))


def _attn_fwd(q, k, v):
    s, d = q.shape
    kvh = k.shape[1] // HEAD_DIM
    tq = _tile(s, Q_TILE, LANES)
    gw = GROUP * HEAD_DIM

    def body(q_ref, k_ref, v_ref, o_ref, lse_ref):
        kk, vv = k_ref[...], v_ref[...]
        for g in range(GROUP):
            sl = slice(g * HEAD_DIM, (g + 1) * HEAD_DIM)
            sc = lax.dot_general(q_ref[:, sl], kk, _NT, preferred_element_type=F32)
            mx = jnp.max(sc, axis=-1, keepdims=True)
            p = jnp.exp(sc - mx)
            l = jnp.sum(p, axis=-1, keepdims=True)
            p = p * (1.0 / l)
            o_ref[:, sl] = lax.dot_general(p.astype(BF16), vv, _NN, preferred_element_type=F32).astype(o_ref.dtype)
            lse_ref[:, g:g + 1] = mx + jnp.log(l)

    return pl.pallas_call(
        body, name="attn_fwd", grid=(kvh, s // tq),
        in_specs=[pl.BlockSpec((tq, gw), lambda j, i: (i, j)), pl.BlockSpec((s, HEAD_DIM), lambda j, i: (0, j)), pl.BlockSpec((s, HEAD_DIM), lambda j, i: (0, j))],
        out_specs=[pl.BlockSpec((tq, gw), lambda j, i: (i, j)), pl.BlockSpec((None, tq, GROUP), lambda j, i: (j, i, 0))],
        out_shape=[_sds((s, d), BF16), _sds((kvh, s, GROUP), F32)],
        compiler_params=_params(("parallel", "parallel")))(q, k, v)


def _attn_bwd_dq(q, k, v, do, lse):
    s, d = q.shape
    kvh = k.shape[1] // HEAD_DIM
    tq = _tile(s, Q_TILE, LANES)
    gw = GROUP * HEAD_DIM

    def body(q_ref, k_ref, v_ref, do_ref, lse_ref, dq_ref, delta_ref):
        kk, vv = k_ref[...], v_ref[...]
        for g in range(GROUP):
            sl = slice(g * HEAD_DIM, (g + 1) * HEAD_DIM)
            sc = lax.dot_general(q_ref[:, sl], kk, _NT, preferred_element_type=F32)
            p = jnp.exp(sc - lse_ref[:, g:g + 1])
            dp = lax.dot_general(do_ref[:, sl], vv, _NT, preferred_element_type=F32)
            delta = jnp.sum(p * dp, axis=-1, keepdims=True)
            ds = p * (dp - delta)
            dq_ref[:, sl] = lax.dot_general(ds.astype(BF16), kk, _NN, preferred_element_type=F32)
            delta_ref[:, g:g + 1] = delta

    qspec = pl.BlockSpec((tq, gw), lambda j, i: (i, j))
    kspec = pl.BlockSpec((s, HEAD_DIM), lambda j, i: (0, j))
    stat = pl.BlockSpec((None, tq, GROUP), lambda j, i: (j, i, 0))
    return pl.pallas_call(
        body, name="attn_bwd_dq", grid=(kvh, s // tq),
        in_specs=[qspec, kspec, kspec, qspec, stat], out_specs=[qspec, stat],
        out_shape=[_sds((s, d), F32), _sds((kvh, s, GROUP), F32)],
        compiler_params=_params(("parallel", "parallel")))(q, k, v, do, lse)


def _attn_bwd_dkv(q, k, v, do, lse_t, delta_t):
    s, d = q.shape
    kv = k.shape[1]
    kvh = kv // HEAD_DIM
    tq = _tile(s, Q_TILE, LANES)
    gw = GROUP * HEAD_DIM

    def body(q_ref, k_ref, v_ref, do_ref, lse_ref, delta_ref, dk_ref, dv_ref):
        i = pl.program_id(1)
        kk, vv = k_ref[...], v_ref[...]
        dk = jnp.zeros((s, HEAD_DIM), F32)
        dv = jnp.zeros((s, HEAD_DIM), F32)
        for g in range(GROUP):
            sl = slice(g * HEAD_DIM, (g + 1) * HEAD_DIM)
            qg, dog = q_ref[:, sl], do_ref[:, sl]
            sc_t = lax.dot_general(kk, qg, _NT, preferred_element_type=F32)
            p_t = jnp.exp(sc_t - lse_ref[g:g + 1, :])
            dp_t = lax.dot_general(vv, dog, _NT, preferred_element_type=F32)
            ds_t = p_t * (dp_t - delta_ref[g:g + 1, :])
            dk = dk + lax.dot_general(ds_t.astype(BF16), qg, _NN, preferred_element_type=F32)
            dv = dv + lax.dot_general(p_t.astype(BF16), dog, _NN, preferred_element_type=F32)

        @pl.when(i == 0)
        def _():
            dk_ref[...] = dk
            dv_ref[...] = dv

        @pl.when(i > 0)
        def _():
            dk_ref[...] += dk
            dv_ref[...] += dv

    qspec = pl.BlockSpec((tq, gw), lambda j, i: (i, j))
    kspec = pl.BlockSpec((s, HEAD_DIM), lambda j, i: (0, j))
    stat = pl.BlockSpec((None, GROUP, tq), lambda j, i: (j, 0, i))
    return pl.pallas_call(
        body, name="attn_bwd_dkv", grid=(kvh, s // tq),
        in_specs=[qspec, kspec, kspec, qspec, stat, stat], out_specs=[kspec, kspec],
        out_shape=[_sds((s, kv), F32), _sds((s, kv), F32)],
        compiler_params=_params(("parallel", "arbitrary")))(q, k, v, do, lse_t, delta_t)


ELEM_BLOCK_BYTES = 1 << 20


def _elem_tiles(rows, width):
    tc = _tile(width, 2048, LANES)
    tr = _tile(rows, max(8, ELEM_BLOCK_BYTES // (4 * tc)), 8)
    return tr, tc


def _adamw(w, g, m, v, name):
    shape = w.shape
    width = shape[-1]
    w2, g2, m2, v2 = (a.reshape(-1, width) for a in (w, g, m, v))
    rows = w2.shape[0]
    tr, tc = _elem_tiles(rows, width)

    def body(w_ref, g_ref, m_ref, v_ref, d_ref, nm_ref, nv_ref):
        gv = g_ref[...]
        mn = ADAM_B1 * m_ref[...] + (1.0 - ADAM_B1) * gv
        vn = ADAM_B2 * v_ref[...] + (1.0 - ADAM_B2) * jnp.square(gv)
        m_hat = mn / (1.0 - ADAM_B1 ** ADAM_STEP)
        v_hat = vn / (1.0 - ADAM_B2 ** ADAM_STEP)
        d_ref[...] = -ADAM_LR * (m_hat / (jnp.sqrt(v_hat) + ADAM_EPS) + ADAM_WD * w_ref[...])
        nm_ref[...] = mn
        nv_ref[...] = vn

    tile = pl.BlockSpec((tr, tc), lambda i, j: (i, j))
    outs = pl.pallas_call(body, name=name, grid=(rows // tr, width // tc), in_specs=[tile] * 4, out_specs=[tile] * 3,
                          out_shape=[_sds((rows, width), F32)] * 3, compiler_params=_params(("parallel", "parallel")))(w2, g2, m2, v2)
    return tuple(o.reshape(shape) for o in outs)


def _pair_add(g, landed, core, col_sharded, name):
    rows, width = landed.shape
    tr, tc = _elem_tiles(rows, width)
    nr, nc = rows // tr, width // tc

    def body(c_ref, g_ref, l_ref, o_ref):
        o_ref[...] = (g_ref[...] + l_ref[...]).astype(o_ref.dtype)

    if col_sharded:
        g_spec = pl.BlockSpec((tr, tc), lambda i, j, c: (c[0] * nr + i, j))
    else:
        g_spec = pl.BlockSpec((tr, tc), lambda i, j, c: (i, c[0] * nc + j))
    tile = pl.BlockSpec((tr, tc), lambda i, j, c: (i, j))
    return pl.pallas_call(
        body, name=name,
        grid_spec=pltpu.PrefetchScalarGridSpec(num_scalar_prefetch=1, grid=(nr, nc), in_specs=[g_spec, tile], out_specs=tile),
        out_shape=_sds((rows, width), BF16), compiler_params=_params(("parallel", "parallel")))(core, g, landed)


def _sum_chips(q, name):
    _, rows, width = q.shape
    tr, tc = _elem_tiles(rows, width)

    def body(q_ref, o_ref):
        acc = q_ref[0].astype(F32)
        for j in range(1, N_CHIPS):
            acc = acc + q_ref[j].astype(F32)
        o_ref[...] = acc

    return pl.pallas_call(body, name=name, grid=(rows // tr, width // tc),
                          in_specs=[pl.BlockSpec((N_CHIPS, tr, tc), lambda i, j: (0, i, j))], out_specs=pl.BlockSpec((tr, tc), lambda i, j: (i, j)),
                          out_shape=_sds((rows, width), F32), compiler_params=_params(("parallel", "parallel")))(q)


def _place():
    x, y, c = lax.axis_index("x"), lax.axis_index("y"), lax.axis_index("c")
    others = [(1 - x, y), (x, 1 - y), (1 - x, 1 - y)]
    return x, y, c, others


def _chip_index(px, py):
    return 2 * px + py


def _remote(src, dst, send_sems, recv_sems, k, to):
    return pltpu.make_async_remote_copy(src_ref=src, dst_ref=dst, send_sem=send_sems.at[k], recv_sem=recv_sems.at[k],
                                        device_id=to, device_id_type=MESH)


BIG = (("w_in", True), ("w_out_conv", False), ("w_out_attn", False), ("w_merge", False), ("w_up", True), ("w_down", False))
N_BIG = len(BIG)


def _shard_region(ref, col_sharded, chip, n_shard, lead=()):
    start = pl.multiple_of(chip * n_shard, LANES if col_sharded else 8)
    if col_sharded:
        return ref.at[(*lead, slice(None), pl.ds(start, n_shard))]
    return ref.at[(*lead, pl.ds(start, n_shard), slice(None))]


def _row_half(ref, half, n_rows):
    return ref.at[pl.ds(pl.multiple_of(half * (n_rows // 2), 8), n_rows // 2), :]


def _gather_weights(shards, conv_shard):
    full_shapes = []
    for (_, col), sh in zip(BIG, shards):
        full_shapes.append((sh.shape[0], sh.shape[1] * N_CHIPS) if col else (sh.shape[0] * N_CHIPS, sh.shape[1]))
    conv_full = (conv_shard.shape[0], conv_shard.shape[1] * N_CHIPS)
    n_ici = 3 * N_BIG

    def body(*refs):
        shard_refs = refs[:N_BIG]
        conv_ref = refs[N_BIG]
        out_refs = refs[N_BIG + 1:2 * N_BIG + 1]
        conv_out = refs[2 * N_BIG + 1]
        send_sems, recv_sems, local_sems = refs[2 * N_BIG + 2:]
        x, y, c, others = _place()
        me = _chip_index(x, y)
        sibling = (x, y, 1 - c)

        def region(a, chip, half=None):
            col = BIG[a][1]
            n_shard = shard_refs[a].shape[1] if col else shard_refs[a].shape[0]
            reg = _shard_region(out_refs[a], col, chip, n_shard)
            return reg if half is None else _row_half(reg, half, reg.shape[0])

        local = [pltpu.make_async_copy(shard_refs[a], region(a, me), local_sems.at[a]) for a in range(N_BIG)]
        local.append(pltpu.make_async_copy(conv_ref, _shard_region(conv_out, True, me, conv_ref.shape[1]), local_sems.at[N_BIG]))
        for cp in local:
            cp.start()
        sends = []
        for a in range(N_BIG):
            src = _row_half(shard_refs[a], c, shard_refs[a].shape[0])
            for j, (ox, oy) in enumerate(others):
                sends.append(_remote(src, region(a, me, c), send_sems, recv_sems, 3 * a + j, (ox, oy, c)))
        for j, (ox, oy) in enumerate(others):
            sends.append(_remote(conv_ref, _shard_region(conv_out, True, me, conv_ref.shape[1]), send_sems, recv_sems, 2 * n_ici + j, (ox, oy, c)))
        for cp in sends:
            cp.start()
        passed = []
        for a in range(N_BIG):
            for j, (ox, oy) in enumerate(others):
                landed = region(a, _chip_index(ox, oy), c)
                _remote(landed, landed, send_sems, recv_sems, 3 * a + j, sibling).wait_recv()
                fwd = _remote(landed, landed, send_sems, recv_sems, n_ici + 3 * a + j, sibling)
                fwd.start()
                passed.append(fwd)
        for a in range(N_BIG):
            for j, (ox, oy) in enumerate(others):
                from_sibling = region(a, _chip_index(ox, oy), 1 - c)
                _remote(from_sibling, from_sibling, send_sems, recv_sems, n_ici + 3 * a + j, sibling).wait_recv()
        for j, (ox, oy) in enumerate(others):
            landed = _shard_region(conv_out, True, _chip_index(ox, oy), conv_ref.shape[1])
            _remote(landed, landed, send_sems, recv_sems, 2 * n_ici + j, sibling).wait_recv()
        for cp in sends + passed:
            cp.wait_send()
        for cp in local:
            cp.wait()

    n_sem = 2 * n_ici + 3
    outs = pl.pallas_call(
        body, name="gather_weights", in_specs=[ANY] * (N_BIG + 1), out_specs=[ANY] * (N_BIG + 1),
        out_shape=[_sds(s, BF16) for s in full_shapes] + [_sds(conv_full, F32)],
        scratch_shapes=[pltpu.SemaphoreType.DMA((n_sem,)), pltpu.SemaphoreType.DMA((n_sem,)), pltpu.SemaphoreType.DMA((N_BIG + 1,))],
        compiler_params=pltpu.CompilerParams(has_side_effects=True),
    )(*shards, conv_shard)
    return outs[:N_BIG], outs[N_BIG]


def _half_of(ref, col_sharded, half):
    rows, width = ref.shape
    if col_sharded:
        return ref.at[pl.ds(pl.multiple_of(half * (rows // 2), 8), rows // 2), :]
    return ref.at[:, pl.ds(pl.multiple_of(half * (width // 2), LANES), width // 2)]


def _pair_exchange(grads):
    half_shapes = [(g.shape[0] // 2, g.shape[1]) if col else (g.shape[0], g.shape[1] // 2) for (_, col), g in zip(BIG, grads)]

    def body(*refs):
        g_refs, land_refs = refs[:N_BIG], refs[N_BIG:2 * N_BIG]
        send_sems, recv_sems = refs[2 * N_BIG:]
        x, y, c, _ = _place()
        copies = [_remote(_half_of(g_refs[a], BIG[a][1], 1 - c), land_refs[a], send_sems, recv_sems, a, (x, y, 1 - c)) for a in range(N_BIG)]
        for cp in copies:
            cp.start()
        for cp in copies:
            cp.wait_recv()
        for cp in copies:
            cp.wait_send()

    return pl.pallas_call(
        body, name="grad_pair_exchange", in_specs=[ANY] * N_BIG, out_specs=[ANY] * N_BIG,
        out_shape=[_sds(s, F32) for s in half_shapes],
        scratch_shapes=[pltpu.SemaphoreType.DMA((N_BIG,)), pltpu.SemaphoreType.DMA((N_BIG,))],
        compiler_params=pltpu.CompilerParams(has_side_effects=True),
    )(*grads)


def _scatter_to_chips(pair_sums):
    piece_shapes = [(p.shape[0], p.shape[1] // N_CHIPS) if col else (p.shape[0] // N_CHIPS, p.shape[1]) for (_, col), p in zip(BIG, pair_sums)]

    def body(*refs):
        p_refs, q_refs = refs[:N_BIG], refs[N_BIG:2 * N_BIG]
        send_sems, recv_sems, local_sems = refs[2 * N_BIG:]
        x, y, c, others = _place()
        me = _chip_index(x, y)

        def piece(a, chip):
            col = BIG[a][1]
            return _shard_region(p_refs[a], col, chip, piece_shapes[a][1] if col else piece_shapes[a][0])

        local = [pltpu.make_async_copy(piece(a, me), q_refs[a].at[me], local_sems.at[a]) for a in range(N_BIG)]
        for cp in local:
            cp.start()
        copies = []
        for a in range(N_BIG):
            for j, (ox, oy) in enumerate(others):
                copies.append(_remote(piece(a, _chip_index(ox, oy)), q_refs[a].at[me], send_sems, recv_sems, 3 * a + j, (ox, oy, c)))
        for cp in copies:
            cp.start()
        for a in range(N_BIG):
            for j, (ox, oy) in enumerate(others):
                slot = q_refs[a].at[_chip_index(ox, oy)]
                _remote(slot, slot, send_sems, recv_sems, 3 * a + j, (ox, oy, c)).wait_recv()
        for cp in copies:
            cp.wait_send()
        for cp in local:
            cp.wait()

    return pl.pallas_call(
        body, name="grad_scatter_to_chips", in_specs=[ANY] * N_BIG, out_specs=[ANY] * N_BIG,
        out_shape=[_sds((N_CHIPS, *s), BF16) for s in piece_shapes],
        scratch_shapes=[pltpu.SemaphoreType.DMA((3 * N_BIG,)), pltpu.SemaphoreType.DMA((3 * N_BIG,)), pltpu.SemaphoreType.DMA((N_BIG,))],
        compiler_params=pltpu.CompilerParams(has_side_effects=True),
    )(*pair_sums)


def _join_halves(reduced, depth):
    shard_shapes = [(r.shape[0] * 2, r.shape[1]) if col else (r.shape[0], r.shape[1] * 2) for (_, col), r in zip(BIG, reduced[0])]
    n = depth * N_BIG

    def body(*refs):
        r_refs, out_refs = refs[:n], refs[n:n + N_BIG]
        send_sems, recv_sems, local_sems = refs[n + N_BIG:]
        x, y, c, _ = _place()
        local, copies, waits = [], [], []
        for l in range(depth):
            for a in range(N_BIG):
                k = l * N_BIG + a
                mine = _half_of(out_refs[a].at[l], BIG[a][1], c)
                local.append(pltpu.make_async_copy(r_refs[k], mine, local_sems.at[k]))
                copies.append(_remote(r_refs[k], mine, send_sems, recv_sems, k, (x, y, 1 - c)))
                theirs = _half_of(out_refs[a].at[l], BIG[a][1], 1 - c)
                waits.append(_remote(theirs, theirs, send_sems, recv_sems, k, (x, y, 1 - c)))
        for cp in local + copies:
            cp.start()
        for cp in waits:
            cp.wait_recv()
        for cp in copies:
            cp.wait_send()
        for cp in local:
            cp.wait()

    flat = [r for layer in reduced for r in layer]
    return pl.pallas_call(
        body, name="grad_join_halves", in_specs=[ANY] * n, out_specs=[ANY] * N_BIG,
        out_shape=[_sds((depth, *s), F32) for s in shard_shapes],
        scratch_shapes=[pltpu.SemaphoreType.DMA((n,)), pltpu.SemaphoreType.DMA((n,)), pltpu.SemaphoreType.DMA((n,))],
        compiler_params=pltpu.CompilerParams(has_side_effects=True),
    )(*flat)


def _allreduce_small(vec):
    rows = vec.shape[0]
    masks = [(dx, dy, dc) for dx in (0, 1) for dy in (0, 1) for dc in (0, 1)][1:]

    def body(v_ref, o_ref, gather_ref, send_sems, recv_sems):
        x, y, c, _ = _place()
        me = 4 * x + 2 * y + c
        gather_ref[me] = v_ref[...]
        copies = []
        for k, (dx, dy, dc) in enumerate(masks):
            peer = (x ^ dx, y ^ dy, c ^ dc)
            copies.append(_remote(v_ref, gather_ref.at[me], send_sems, recv_sems, k, peer))
        for cp in copies:
            cp.start()
        for k, (dx, dy, dc) in enumerate(masks):
            slot = gather_ref.at[4 * (x ^ dx) + 2 * (y ^ dy) + (c ^ dc)]
            _remote(slot, slot, send_sems, recv_sems, k, (x, y, c)).wait_recv()
        for cp in copies:
            cp.wait_send()
        acc = gather_ref[0]
        for dev in range(1, N_DEV):
            acc = acc + gather_ref[dev]
        o_ref[...] = acc

    vm = pl.BlockSpec(memory_space=pltpu.VMEM)
    return pl.pallas_call(
        body, name="allreduce_small", in_specs=[vm], out_specs=vm, out_shape=_sds((rows, LANES), F32),
        scratch_shapes=[pltpu.VMEM((N_DEV, rows, LANES), F32), pltpu.SemaphoreType.DMA((N_DEV - 1,)), pltpu.SemaphoreType.DMA((N_DEV - 1,))],
        compiler_params=pltpu.CompilerParams(has_side_effects=True, vmem_limit_bytes=VMEM_LIMIT_BYTES),
    )(vec)


def _relu2(acc):
    r = jnp.maximum(acc, 0.0)
    return acc, r * r


def _relu2_bwd(acc, up):
    return (acc * (2.0 * jnp.maximum(up, 0.0)),)


def _layer_fwd(x, h, p, cols, tables, last, target=None):
    cos_t, sin_t = tables
    z = _mm(h, p["w_in"], "nn", [F32], name="mm_in")
    a = _conv_fwd(z, p["conv_w"], cols)
    q, k, v = _rope_fwd(z, p["q_norm"], p["k_norm"], cos_t, sin_t, cols)
    o, lse = _attn_fwd(q, k, v)
    y_a = _mm(a, p["w_out_conv"], "nn", [F32], name="mm_out_conv")
    y_b = _mm(o, p["w_out_attn"], "nn", [F32], name="mm_out_attn")
    mix = _gate_fwd(z, p["gate_bias"], y_a, y_b, cols)
    mixed = _mm(mix, p["w_merge"], "nn", [F32], name="mm_merge")
    x1, h2 = _resid_norm(x, mixed, p["norm_mix_post"], p["norm_mlp_pre"])
    up, act = _mm(h2, p["w_up"], "nn", [F32, BF16], name="mm_up", epilogue=_relu2)
    f = _mm(act, p["w_down"], "nn", [F32], name="mm_down")
    kept = dict(x=x, h=h, z=z, a=a, q=q, k=k, v=v, o=o, lse=lse, y_a=y_a, y_b=y_b, mix=mix, mixed=mixed, x1=x1, h2=h2, up=up, act=act, f=f)
    if last:
        return _resid_norm_loss(x1, f, p["norm_mlp_post"], target), kept
    return _resid_norm(x1, f, p["norm_mlp_post"], p["norm_next"]), kept


def _layer_bwd(dx_out, p, kept, cols, tables):
    cos_t, sin_t = tables
    t = kept
    df, d_norm_mlp_post = _norm_bwd(t["f"], p["norm_mlp_post"], dx_out, None, BF16, "norm_bwd_mlp_post")
    dup = _mm(df, p["w_down"], "nt", [BF16], name="mm_d_act", epilogue=_relu2_bwd, extras=(t["up"],))
    g_w_down = _mm(t["act"], df, "tn", [F32], name="mm_g_down")
    g_w_up = _mm(t["h2"], dup, "tn", [F32], name="mm_g_up")
    dh2 = _mm(dup, p["w_up"], "nt", [F32], name="mm_d_h2")
    dx1, d_norm_mlp_pre = _norm_bwd(t["x1"], p["norm_mlp_pre"], dh2, dx_out, F32, "norm_bwd_mlp_pre")
    dmixed, d_norm_mix_post = _norm_bwd(t["mixed"], p["norm_mix_post"], dx1, None, BF16, "norm_bwd_mix_post")
    dmix = _mm(dmixed, p["w_merge"], "nt", [F32], name="mm_d_mix")
    g_w_merge = _mm(t["mix"], dmixed, "tn", [F32], name="mm_g_merge")
    dy_a, dy_b, dz_ga, dz_gb, dbias_a, dbias_b = _gate_bwd(t["z"], p["gate_bias"], t["y_a"], t["y_b"], dmix, cols)
    g_w_out_conv = _mm(t["a"], dy_a, "tn", [F32], name="mm_g_out_conv")
    da = _mm(dy_a, p["w_out_conv"], "nt", [F32], name="mm_d_a")
    g_w_out_attn = _mm(t["o"], dy_b, "tn", [F32], name="mm_g_out_attn")
    do = _mm(dy_b, p["w_out_attn"], "nt", [BF16], name="mm_d_o")
    dz_cb, dz_cc, dz_ci, d_conv_w = _conv_bwd(t["z"], p["conv_w"], da, cols)
    dq, delta = _attn_bwd_dq(t["q"], t["k"], t["v"], do, t["lse"])
    dk, dv = _attn_bwd_dkv(t["q"], t["k"], t["v"], do, jnp.swapaxes(t["lse"], 1, 2), jnp.swapaxes(delta, 1, 2))
    dz_q, dz_k, dz_v, d_q_norm, d_k_norm = _rope_bwd(t["z"], p["q_norm"], p["k_norm"], cos_t, sin_t, dq, dk, dv, cols)
    dz = jnp.concatenate([dz_cb, dz_cc, dz_ci, dz_q, dz_k, dz_v, dz_ga, dz_gb], axis=1)
    g_w_in = _mm(t["h"], dz, "tn", [F32], name="mm_g_in")
    dh = _mm(dz, p["w_in"], "nt", [F32], name="mm_d_h")
    dx_in, d_norm_mix_pre = _norm_bwd(t["x"], p["norm_mix_pre"], dh, dx1, F32, "norm_bwd_mix_pre")
    big = [g_w_in, g_w_out_conv, g_w_out_attn, g_w_merge, g_w_up, g_w_down]
    small = dict(norm_mix_pre=d_norm_mix_pre, gate_bias=jnp.concatenate([dbias_a, dbias_b], axis=1), conv_w=d_conv_w, q_norm=d_q_norm,
                 k_norm=d_k_norm, norm_mix_post=d_norm_mix_post, norm_mlp_pre=d_norm_mlp_pre, norm_mlp_post=d_norm_mlp_post)
    return dx_in, big, small


SMALL = ("norm_mix_pre", "gate_bias", "conv_w", "q_norm", "k_norm", "norm_mix_post", "norm_mlp_pre", "norm_mlp_post")
WEIGHTS = ("norm_mix_pre", "w_in", "gate_bias", "conv_w", "q_norm", "k_norm", "w_out_conv", "w_out_attn", "w_merge",
           "norm_mix_post", "norm_mlp_pre", "w_up", "w_down", "norm_mlp_post")


def _pack(parts):
    flat = jnp.concatenate([a.reshape(-1) for a in parts])
    rows = -(-flat.shape[0] // LANES)
    pad = (-rows) % 8
    flat = jnp.pad(flat, (0, (rows + pad) * LANES - flat.shape[0]))
    return flat.reshape(rows + pad, LANES)


def _unpack(packed, shapes):
    flat = packed.reshape(-1)
    out, off = [], 0
    for shp in shapes:
        n = math.prod(shp)
        out.append(flat[off:off + n].reshape(shp))
        off += n
    return out


def kernel(x, norm_mix_pre, w_in, gate_bias, conv_w, q_norm, k_norm, w_out_conv, w_out_attn, w_merge, norm_mix_post, norm_mlp_pre, w_up, w_down, norm_mlp_post, loss_target, m_norm_mix_pre, m_w_in, m_gate_bias, m_conv_w, m_q_norm, m_k_norm, m_w_out_conv, m_w_out_attn, m_w_merge, m_norm_mix_post, m_norm_mlp_pre, m_w_up, m_w_down, m_norm_mlp_post, v_norm_mix_pre, v_w_in, v_gate_bias, v_conv_w, v_q_norm, v_k_norm, v_w_out_conv, v_w_out_attn, v_w_merge, v_norm_mix_post, v_norm_mlp_pre, v_w_up, v_w_down, v_norm_mlp_post):
    w = dict(norm_mix_pre=norm_mix_pre, w_in=w_in, gate_bias=gate_bias, conv_w=conv_w, q_norm=q_norm, k_norm=k_norm, w_out_conv=w_out_conv,
             w_out_attn=w_out_attn, w_merge=w_merge, norm_mix_post=norm_mix_post, norm_mlp_pre=norm_mlp_pre, w_up=w_up, w_down=w_down,
             norm_mlp_post=norm_mlp_post)
    mom = dict(norm_mix_pre=m_norm_mix_pre, w_in=m_w_in, gate_bias=m_gate_bias, conv_w=m_conv_w, q_norm=m_q_norm, k_norm=m_k_norm,
               w_out_conv=m_w_out_conv, w_out_attn=m_w_out_attn, w_merge=m_w_merge, norm_mix_post=m_norm_mix_post, norm_mlp_pre=m_norm_mlp_pre,
               w_up=m_w_up, w_down=m_w_down, norm_mlp_post=m_norm_mlp_post)
    var = dict(norm_mix_pre=v_norm_mix_pre, w_in=v_w_in, gate_bias=v_gate_bias, conv_w=v_conv_w, q_norm=v_q_norm, k_norm=v_k_norm,
               w_out_conv=v_w_out_conv, w_out_attn=v_w_out_attn, w_merge=v_w_merge, norm_mix_post=v_norm_mix_post, norm_mlp_pre=v_norm_mlp_pre,
               w_up=v_w_up, w_down=v_w_down, norm_mlp_post=v_norm_mlp_post)
    depth = w_in.shape[0]
    _, s, d = x.shape
    cols = _Cols(d)
    x0 = x.reshape(s, d)
    target = loss_target.reshape(s, d)
    tables = _rope_tables(s)
    chip = 2 * lax.axis_index("x") + lax.axis_index("y")
    core = lax.axis_index("c").astype(jnp.int32).reshape(1)

    layers = []
    for l in range(depth):
        shards = [w[name][l].astype(BF16) for name, _ in BIG]
        full, conv_full = _gather_weights(shards, conv_w[l])
        p = {name: f for (name, _), f in zip(BIG, full)}
        p["conv_w"] = conv_full
        p["gate_bias"] = gate_bias[l].reshape(1, -1)
        for name in ("norm_mix_pre", "q_norm", "k_norm", "norm_mix_post", "norm_mlp_pre", "norm_mlp_post"):
            p[name] = w[name][l].reshape(1, -1)
        layers.append(p)
    for l in range(depth - 1):
        layers[l]["norm_next"] = layers[l + 1]["norm_mix_pre"]

    kept = []
    xl, h = x0, _norm_first(x0, layers[0]["norm_mix_pre"])
    for l in range(depth):
        last = l == depth - 1
        (xl, h), t = _layer_fwd(xl, h, layers[l], cols, tables, last, target if last else None)
        kept.append(t)
    dx, loss_local = xl, h

    chip_parts = [None] * depth
    small = [None] * depth
    for l in reversed(range(depth)):
        dx, big, small[l] = _layer_bwd(dx, layers[l], kept[l], cols, tables)
        landed = _pair_exchange(big)
        pair_sums = [_pair_add(g, ld, core, col, "pair_add_" + name) for (name, col), g, ld in zip(BIG, big, landed)]
        chip_parts[l] = _scatter_to_chips(pair_sums)
    reduced = [[_sum_chips(q, "sum_chips_" + name) for (name, _), q in zip(BIG, chip_parts[l])] for l in range(depth)]
    big_grads = dict(zip([name for name, _ in BIG], _join_halves(reduced, depth)))

    small_full_shapes = [(depth,) + small[0][name].shape for name in SMALL]
    packed = _pack([jnp.stack([small[l][name] for l in range(depth)]) for name in SMALL] + [jnp.broadcast_to(loss_local.reshape(1), (LANES,))])
    summed = _allreduce_small(packed)
    small_sum = _unpack(summed, small_full_shapes + [(LANES,)])
    loss = small_sum[-1][0]
    grads = dict(big_grads)
    for name, g in zip(SMALL, small_sum[:-1]):
        if name == "conv_w":
            n_shard = conv_w.shape[-1]
            g = lax.dynamic_slice_in_dim(g, chip * n_shard, n_shard, axis=2)
        grads[name] = g.reshape(w[name].shape)

    delta, new_m, new_v = {}, {}, {}
    for name, _ in BIG:
        delta[name], new_m[name], new_v[name] = _adamw(w[name], grads[name], mom[name], var[name], "adamw_" + name)
    small_shapes = [w[name].shape for name in SMALL]
    packs = [_pack([src[name] for name in SMALL]) for src in (w, grads, mom, var)]
    for dst, out in zip((delta, new_m, new_v), _adamw(*packs, "adamw_small")):
        for name, val in zip(SMALL, _unpack(out, small_shapes)):
            dst[name] = val

    grad_x = dx.reshape(x.shape)
    return (loss, grad_x, *[grads[n] for n in WEIGHTS], *[delta[n] for n in WEIGHTS], *[new_m[n] for n in WEIGHTS], *[new_v[n] for n in WEIGHTS])
```

```python
import functools
import math

import jax
import jax.numpy as jnp
from jax import lax
from jax.experimental import pallas as pl
from jax.experimental.pallas import tpu as pltpu

F32 = jnp.float32
BF16 = jnp.bfloat16

HEAD_DIM = 128
GROUP = 4
GRID_W = 64
ROPE_THETA = 10000.0
RMS_EPS = 1e-6
ADAM_LR = 0.001
ADAM_B1 = 0.9
ADAM_B2 = 0.999
ADAM_EPS = 1e-08
ADAM_WD = 0.01
ADAM_STEP = 10

LANES = 128
N_CHIPS = 4
N_DEV = 8
VMEM_LIMIT_BYTES = 56 * 1024 * 1024
MESH = pl.DeviceIdType.MESH
ANY = pl.BlockSpec(memory_space=pl.ANY)


def _tile(dim, cap, mult):
    if dim <= cap:
        return dim
    t = (cap // mult) * mult
    while t >= mult:
        if dim % t == 0:
            return t
        t -= mult
    raise ValueError(f"no tile for {dim} under {cap} in multiples of {mult}")


def _params(sem=None):
    return pltpu.CompilerParams(dimension_semantics=sem, vmem_limit_bytes=VMEM_LIMIT_BYTES)


def _sds(shape, dtype):
    return jax.ShapeDtypeStruct(tuple(shape), dtype)


def _rstd(x):
    return lax.rsqrt(jnp.mean(x * x, axis=-1, keepdims=True) + RMS_EPS)


_DOT_DIMS = {"nn": ((1,), (0,)), "nt": ((1,), (1,)), "tn": ((0,), (0,))}


def _mm(a, b, mode, out_dtypes, *, name, epilogue=None, extras=(), tm=1024, tn=1024, tk=2048):
    if mode == "nn":
        (m, k), (k2, n) = a.shape, b.shape
    elif mode == "nt":
        (m, k), (n, k2) = a.shape, b.shape
    else:
        (k, m), (k2, n) = a.shape, b.shape
    assert k == k2, (a.shape, b.shape, mode)
    tm, tn, tk = _tile(m, tm, 8), _tile(n, tn, LANES), _tile(k, tk, LANES)
    nk = k // tk
    a_spec = pl.BlockSpec((tk, tm), lambda i, j, kk: (kk, i)) if mode == "tn" else pl.BlockSpec((tm, tk), lambda i, j, kk: (i, kk))
    b_spec = pl.BlockSpec((tn, tk), lambda i, j, kk: (j, kk)) if mode == "nt" else pl.BlockSpec((tk, tn), lambda i, j, kk: (kk, j))
    tile_spec = pl.BlockSpec((tm, tn), lambda i, j, kk: (i, j))
    n_extra, n_out = len(extras), len(out_dtypes)
    dims = (_DOT_DIMS[mode], ((), ()))

    def body(*refs):
        a_ref, b_ref = refs[:2]
        extra_refs = refs[2:2 + n_extra]
        out_refs = refs[2 + n_extra:2 + n_extra + n_out]
        part = lax.dot_general(a_ref[...].astype(BF16), b_ref[...].astype(BF16), dims, preferred_element_type=F32)

        def finish(total):
            res = epilogue(total, *[e[...] for e in extra_refs]) if epilogue is not None else (total,)
            for o, r in zip(out_refs, res):
                o[...] = r.astype(o.dtype)

        if nk == 1:
            finish(part)
        else:
            acc = refs[-1]
            kk = pl.program_id(2)

            @pl.when(kk == 0)
            def _():
                acc[...] = part

            @pl.when(kk > 0)
            def _():
                acc[...] += part

            @pl.when(kk == nk - 1)
            def _():
                finish(acc[...])

    outs = pl.pallas_call(
        body, name=name, grid=(m // tm, n // tn, nk),
        in_specs=[a_spec, b_spec] + [tile_spec] * n_extra,
        out_specs=[tile_spec] * n_out,
        out_shape=[_sds((m, n), d) for d in out_dtypes],
        scratch_shapes=[pltpu.VMEM((tm, tn), F32)] if nk > 1 else [],
        compiler_params=_params(("parallel", "parallel", "arbitrary")),
    )(a, b, *extras)
    return outs if n_out > 1 else outs[0]


ROW_TILE = 256


def _norm_first(x, g):
    s, d = x.shape
    ts = _tile(s, ROW_TILE, 8)

    def body(x_ref, g_ref, h_ref):
        xv = x_ref[...]
        h_ref[...] = (xv * _rstd(xv) * g_ref[...]).astype(h_ref.dtype)

    row = pl.BlockSpec((ts, d), lambda i: (i, 0))
    vec = pl.BlockSpec((1, d), lambda i: (0, 0))
    return pl.pallas_call(body, name="norm_first", grid=(s // ts,), in_specs=[row, vec], out_specs=row,
                          out_shape=_sds((s, d), BF16), compiler_params=_params(("parallel",)))(x, g)


def _resid_norm(xres, y, g_post, g_next):
    s, d = xres.shape
    ts = _tile(s, ROW_TILE, 8)

    def body(x_ref, y_ref, gp_ref, gn_ref, xn_ref, hn_ref):
        yv = y_ref[...]
        xn = x_ref[...] + yv * _rstd(yv) * gp_ref[...]
        xn_ref[...] = xn
        hn_ref[...] = (xn * _rstd(xn) * gn_ref[...]).astype(hn_ref.dtype)

    row = pl.BlockSpec((ts, d), lambda i: (i, 0))
    vec = pl.BlockSpec((1, d), lambda i: (0, 0))
    return pl.pallas_call(body, name="resid_norm", grid=(s // ts,), in_specs=[row, row, vec, vec], out_specs=[row, row],
                          out_shape=[_sds((s, d), F32), _sds((s, d), BF16)], compiler_params=_params(("parallel",)))(xres, y, g_post, g_next)


def _resid_norm_loss(xres, y, g_post, target):
    s, d = xres.shape
    ts = _tile(s, ROW_TILE, 8)
    n_steps = s // ts

    def body(x_ref, y_ref, gp_ref, t_ref, dout_ref, loss_ref, acc_ref):
        i = pl.program_id(0)
        yv = y_ref[...]
        err = x_ref[...] + yv * _rstd(yv) * gp_ref[...] - t_ref[...]
        dout_ref[...] = err / d
        part = jnp.sum(err * err, axis=0, keepdims=True)

        @pl.when(i == 0)
        def _():
            acc_ref[...] = part

        @pl.when(i > 0)
        def _():
            acc_ref[...] += part

        @pl.when(i == n_steps - 1)
        def _():
            loss_ref[...] = 0.5 * jnp.sum(acc_ref[...], axis=1, keepdims=True) / d

    row = pl.BlockSpec((ts, d), lambda i: (i, 0))
    vec = pl.BlockSpec((1, d), lambda i: (0, 0))
    one = pl.BlockSpec((1, 1), lambda i: (0, 0))
    return pl.pallas_call(body, name="resid_norm_loss", grid=(n_steps,), in_specs=[row, row, vec, row], out_specs=[row, one],
                          out_shape=[_sds((s, d), F32), _sds((1, 1), F32)], scratch_shapes=[pltpu.VMEM((1, d), F32)],
                          compiler_params=_params(("arbitrary",)))(xres, y, g_post, target)


def _norm_bwd(xin, g, dout, dres, out_dtype, name):
    s, d = xin.shape
    ts = _tile(s, ROW_TILE, 8)
    has_res = dres is not None

    def body(*refs):
        x_ref, g_ref, do_ref = refs[:3]
        dx_ref, dg_ref = refs[-2:]
        i = pl.program_id(0)
        xv, dov = x_ref[...], do_ref[...]
        r = _rstd(xv)
        xhat = xv * r
        dg = jnp.sum(dov * xhat, axis=0, keepdims=True)
        dxh = dov * g_ref[...]
        dx = r * (dxh - xhat * jnp.mean(dxh * xhat, axis=-1, keepdims=True))
        if has_res:
            dx = dx + refs[3][...]
        dx_ref[...] = dx.astype(dx_ref.dtype)

        @pl.when(i == 0)
        def _():
            dg_ref[...] = dg

        @pl.when(i > 0)
        def _():
            dg_ref[...] += dg

    row = pl.BlockSpec((ts, d), lambda i: (i, 0))
    vec = pl.BlockSpec((1, d), lambda i: (0, 0))
    ops = [xin, g, dout] + ([dres] if has_res else [])
    return pl.pallas_call(body, name=name, grid=(s // ts,), in_specs=[row, vec, row] + ([row] if has_res else []),
                          out_specs=[row, vec], out_shape=[_sds((s, d), out_dtype), _sds((1, d), F32)],
                          compiler_params=_params(("arbitrary",)))(*ops)


class _Cols:
    def __init__(self, d):
        self.d = d
        self.kv = d // GROUP
        self.cb, self.cc, self.ci, self.q = 0, d, 2 * d, 3 * d
        self.k = 4 * d
        self.v = 4 * d + self.kv
        self.ga = 4 * d + 2 * self.kv
        self.gb = 5 * d + 2 * self.kv
        self.width = 6 * d + 2 * self.kv


CONV_COLS = 128


def _shift_rows(u, down):
    s = u.shape[0]
    rows = lax.broadcasted_iota(jnp.int32, u.shape, 0)
    if down:
        return jnp.where(rows == 0, 0.0, pltpu.roll(u, 1, 0))
    return jnp.where(rows == s - 1, 0.0, pltpu.roll(u, s - 1, 0))


def _conv_fwd(z, w, cols):
    s, d = z.shape[0], cols.d
    cw = CONV_COLS

    def body(cb_ref, cc_ref, ci_ref, w_ref, a_ref):
        u = cc_ref[...] * ci_ref[...]
        wv = w_ref[...]
        conv = wv[0:1] * _shift_rows(u, True) + wv[1:2] * u + wv[2:3] * _shift_rows(u, False)
        a_ref[...] = (cb_ref[...] * conv).astype(a_ref.dtype)

    def zspec(off):
        return pl.BlockSpec((s, cw), lambda j: (0, off // cw + j))

    return pl.pallas_call(body, name="conv_fwd", grid=(d // cw,),
                          in_specs=[zspec(cols.cb), zspec(cols.cc), zspec(cols.ci), pl.BlockSpec((3, cw), lambda j: (0, j))],
                          out_specs=pl.BlockSpec((s, cw), lambda j: (0, j)), out_shape=_sds((s, d), BF16),
                          compiler_params=_params(("parallel",)))(z, z, z, w)


def _conv_bwd(z, w, da, cols):
    s, d = z.shape[0], cols.d
    cw = CONV_COLS

    def body(cb_ref, cc_ref, ci_ref, w_ref, da_ref, dcb_ref, dcc_ref, dci_ref, dw_ref):
        cb, cc, ci, dav = cb_ref[...], cc_ref[...], ci_ref[...], da_ref[...]
        wv = w_ref[...]
        u = cc * ci
        um, up = _shift_rows(u, True), _shift_rows(u, False)
        conv = wv[0:1] * um + wv[1:2] * u + wv[2:3] * up
        dcb_ref[...] = (dav * conv).astype(dcb_ref.dtype)
        dconv = dav * cb
        dw_ref[0:1, :] = jnp.sum(dconv * um, axis=0, keepdims=True)
        dw_ref[1:2, :] = jnp.sum(dconv * u, axis=0, keepdims=True)
        dw_ref[2:3, :] = jnp.sum(dconv * up, axis=0, keepdims=True)
        du = wv[0:1] * _shift_rows(dconv, False) + wv[1:2] * dconv + wv[2:3] * _shift_rows(dconv, True)
        dcc_ref[...] = (du * ci).astype(dcc_ref.dtype)
        dci_ref[...] = (du * cc).astype(dci_ref.dtype)

    def zspec(off):
        return pl.BlockSpec((s, cw), lambda j: (0, off // cw + j))

    col = pl.BlockSpec((s, cw), lambda j: (0, j))
    wspec = pl.BlockSpec((3, cw), lambda j: (0, j))
    return pl.pallas_call(body, name="conv_bwd", grid=(d // cw,),
                          in_specs=[zspec(cols.cb), zspec(cols.cc), zspec(cols.ci), wspec, col],
                          out_specs=[col, col, col, wspec],
                          out_shape=[_sds((s, d), BF16)] * 3 + [_sds((3, d), F32)],
                          compiler_params=_params(("parallel",)))(z, z, z, w, da)


def _gate_fwd(z, bias, y_a, y_b, cols):
    s, d = y_a.shape
    ts, cw = _tile(s, ROW_TILE, 8), cols.kv
    nj = d // cw

    def body(ga_ref, gb_ref, ba_ref, bb_ref, ya_ref, yb_ref, o_ref):
        gate_a = jax.nn.sigmoid(ga_ref[...] + ba_ref[...])
        gate_b = jax.nn.sigmoid(gb_ref[...] + bb_ref[...])
        o_ref[...] = (gate_a * ya_ref[...] + gate_b * yb_ref[...]).astype(o_ref.dtype)

    tile = pl.BlockSpec((ts, cw), lambda i, j: (i, j))
    return pl.pallas_call(
        body, name="gate_fwd", grid=(s // ts, nj),
        in_specs=[pl.BlockSpec((ts, cw), lambda i, j: (i, cols.ga // cw + j)), pl.BlockSpec((ts, cw), lambda i, j: (i, cols.gb // cw + j)),
                  pl.BlockSpec((1, cw), lambda i, j: (0, j)), pl.BlockSpec((1, cw), lambda i, j: (0, nj + j)), tile, tile],
        out_specs=tile, out_shape=_sds((s, d), BF16), compiler_params=_params(("parallel", "parallel")))(z, z, bias, bias, y_a, y_b)


def _gate_bwd(z, bias, y_a, y_b, dmix, cols):
    s, d = y_a.shape
    ts, cw = _tile(s, ROW_TILE, 8), cols.kv
    nj = d // cw

    def body(ga_ref, gb_ref, ba_ref, bb_ref, ya_ref, yb_ref, dm_ref, dya_ref, dyb_ref, dga_ref, dgb_ref, dba_ref, dbb_ref):
        i = pl.program_id(1)
        gate_a = jax.nn.sigmoid(ga_ref[...] + ba_ref[...])
        gate_b = jax.nn.sigmoid(gb_ref[...] + bb_ref[...])
        dm = dm_ref[...]
        dya_ref[...] = (dm * gate_a).astype(dya_ref.dtype)
        dyb_ref[...] = (dm * gate_b).astype(dyb_ref.dtype)
        dga = dm * ya_ref[...] * (gate_a * (1.0 - gate_a))
        dgb = dm * yb_ref[...] * (gate_b * (1.0 - gate_b))
        dga_ref[...] = dga.astype(dga_ref.dtype)
        dgb_ref[...] = dgb.astype(dgb_ref.dtype)
        sa = jnp.sum(dga, axis=0, keepdims=True)
        sb = jnp.sum(dgb, axis=0, keepdims=True)

        @pl.when(i == 0)
        def _():
            dba_ref[...] = sa
            dbb_ref[...] = sb

        @pl.when(i > 0)
        def _():
            dba_ref[...] += sa
            dbb_ref[...] += sb

    tile = pl.BlockSpec((ts, cw), lambda j, i: (i, j))
    vec = pl.BlockSpec((1, cw), lambda j, i: (0, j))
    return pl.pallas_call(
        body, name="gate_bwd", grid=(nj, s // ts),
        in_specs=[pl.BlockSpec((ts, cw), lambda j, i: (i, cols.ga // cw + j)), pl.BlockSpec((ts, cw), lambda j, i: (i, cols.gb // cw + j)),
                  vec, pl.BlockSpec((1, cw), lambda j, i: (0, nj + j)), tile, tile, tile],
        out_specs=[tile, tile, tile, tile, vec, vec],
        out_shape=[_sds((s, d), BF16)] * 4 + [_sds((1, d), F32)] * 2,
        compiler_params=_params(("parallel", "arbitrary")))(z, z, bias, bias, y_a, y_b, dmix)


def _rope_tables(s):
    axis_dim = HEAD_DIM // 2
    n_freq = axis_dim // 2
    rows = s // GRID_W
    row_idx = jnp.repeat(jnp.arange(rows, dtype=jnp.int32), GRID_W)
    col_idx = jnp.tile(jnp.arange(GRID_W, dtype=jnp.int32), rows)
    inv_freq = ROPE_THETA ** (-jnp.arange(0, axis_dim, 2, dtype=F32) / axis_dim)
    ang = jnp.stack([row_idx.astype(F32)[:, None] * inv_freq, col_idx.astype(F32)[:, None] * inv_freq], axis=1)
    cos, sin = jnp.cos(ang), jnp.sin(ang)
    cos_t = jnp.stack([cos, cos], axis=2).reshape(s, HEAD_DIM)
    sin_t = jnp.stack([-sin, sin], axis=2).reshape(s, HEAD_DIM)
    return cos_t, sin_t


def _partner(x):
    n = x.shape[-1]
    lane = lax.broadcasted_iota(jnp.int32, x.shape, x.ndim - 1)
    quarter = HEAD_DIM // 4
    return jnp.where(lane % (2 * quarter) < quarter, pltpu.roll(x, n - quarter, x.ndim - 1), pltpu.roll(x, quarter, x.ndim - 1))


def _rope_fwd(z, qn, kn, cos_t, sin_t, cols):
    s, d, kv = z.shape[0], cols.d, cols.kv
    ts = _tile(s, ROW_TILE, 8)
    scale = 1.0 / math.sqrt(HEAD_DIM)

    def body(q_ref, k_ref, v_ref, qn_ref, kn_ref, c_ref, s_ref, qo_ref, ko_ref, vo_ref):
        c, sn = c_ref[...], s_ref[...]

        def head(xh, g):
            xn = xh * _rstd(xh) * g
            return xn * c + _partner(xn) * sn

        for h in range(d // HEAD_DIM):
            sl = slice(h * HEAD_DIM, (h + 1) * HEAD_DIM)
            qo_ref[:, sl] = (head(q_ref[:, sl], qn_ref[...]) * scale).astype(qo_ref.dtype)
        for h in range(kv // HEAD_DIM):
            sl = slice(h * HEAD_DIM, (h + 1) * HEAD_DIM)
            ko_ref[:, sl] = head(k_ref[:, sl], kn_ref[...]).astype(ko_ref.dtype)
        vo_ref[...] = v_ref[...].astype(vo_ref.dtype)

    vec = pl.BlockSpec((1, HEAD_DIM), lambda i: (0, 0))
    tab = pl.BlockSpec((ts, HEAD_DIM), lambda i: (i, 0))
    return pl.pallas_call(
        body, name="rope_fwd", grid=(s // ts,),
        in_specs=[pl.BlockSpec((ts, d), lambda i: (i, cols.q // d)), pl.BlockSpec((ts, kv), lambda i: (i, cols.k // kv)),
                  pl.BlockSpec((ts, kv), lambda i: (i, cols.v // kv)), vec, vec, tab, tab],
        out_specs=[pl.BlockSpec((ts, d), lambda i: (i, 0)), pl.BlockSpec((ts, kv), lambda i: (i, 0)), pl.BlockSpec((ts, kv), lambda i: (i, 0))],
        out_shape=[_sds((s, d), BF16), _sds((s, kv), BF16), _sds((s, kv), BF16)],
        compiler_params=_params(("parallel",)))(z, z, z, qn, kn, cos_t, sin_t)


def _rope_bwd(z, qn, kn, cos_t, sin_t, dq, dk, dv, cols):
    s, d, kv = z.shape[0], cols.d, cols.kv
    ts = _tile(s, ROW_TILE, 8)
    scale = 1.0 / math.sqrt(HEAD_DIM)

    def body(q_ref, k_ref, qn_ref, kn_ref, c_ref, s_ref, dq_ref, dk_ref, dv_ref, dzq_ref, dzk_ref, dzv_ref, dqn_ref, dkn_ref):
        i = pl.program_id(0)
        c, sn = c_ref[...], s_ref[...]

        def head_bwd(xh, g, drot):
            dxn = drot * c + _partner(drot * sn)
            r = _rstd(xh)
            xhat = xh * r
            dgain = jnp.sum(dxn * xhat, axis=0, keepdims=True)
            dxh = dxn * g
            return r * (dxh - xhat * jnp.mean(dxh * xhat, axis=-1, keepdims=True)), dgain

        dqn = jnp.zeros((1, HEAD_DIM), F32)
        for h in range(d // HEAD_DIM):
            sl = slice(h * HEAD_DIM, (h + 1) * HEAD_DIM)
            dx, dg = head_bwd(q_ref[:, sl], qn_ref[...], dq_ref[:, sl] * scale)
            dzq_ref[:, sl] = dx.astype(dzq_ref.dtype)
            dqn = dqn + dg
        dkn = jnp.zeros((1, HEAD_DIM), F32)
        for h in range(kv // HEAD_DIM):
            sl = slice(h * HEAD_DIM, (h + 1) * HEAD_DIM)
            dx, dg = head_bwd(k_ref[:, sl], kn_ref[...], dk_ref[:, sl])
            dzk_ref[:, sl] = dx.astype(dzk_ref.dtype)
            dkn = dkn + dg
        dzv_ref[...] = dv_ref[...].astype(dzv_ref.dtype)

        @pl.when(i == 0)
        def _():
            dqn_ref[...] = dqn
            dkn_ref[...] = dkn

        @pl.when(i > 0)
        def _():
            dqn_ref[...] += dqn
            dkn_ref[...] += dkn

    vec = pl.BlockSpec((1, HEAD_DIM), lambda i: (0, 0))
    tab = pl.BlockSpec((ts, HEAD_DIM), lambda i: (i, 0))
    qrow = pl.BlockSpec((ts, d), lambda i: (i, 0))
    krow = pl.BlockSpec((ts, kv), lambda i: (i, 0))
    return pl.pallas_call(
        body, name="rope_bwd", grid=(s // ts,),
        in_specs=[pl.BlockSpec((ts, d), lambda i: (i, cols.q // d)), pl.BlockSpec((ts, kv), lambda i: (i, cols.k // kv)),
                  vec, vec, tab, tab, qrow, krow, krow],
        out_specs=[qrow, krow, krow, vec, vec],
        out_shape=[_sds((s, d), BF16), _sds((s, kv), BF16), _sds((s, kv), BF16), _sds((1, HEAD_DIM), F32), _sds((1, HEAD_DIM), F32)],
        compiler_params=_params(("arbitrary",)))(z, z, qn, kn, cos_t, sin_t, dq, dk, dv)


Q_TILE = 256
_NT = (((1,), (1,)), ((), ()))
_NN = (((1,), (0,)), ((), ()))


def _attn_fwd(q, k, v):
    s, d = q.shape
    kvh = k.shape[1] // HEAD_DIM
    tq = _tile(s, Q_TILE, LANES)
    gw = GROUP * HEAD_DIM

    def body(q_ref, k_ref, v_ref, o_ref, lse_ref):
        kk, vv = k_ref[...], v_ref[...]
        for g in range(GROUP):
            sl = slice(g * HEAD_DIM, (g + 1) * HEAD_DIM)
            sc = lax.dot_general(q_ref[:, sl], kk, _NT, preferred_element_type=F32)
            mx = jnp.max(sc, axis=-1, keepdims=True)
            p = jnp.exp(sc - mx)
            l = jnp.sum(p, axis=-1, keepdims=True)
            p = p * (1.0 / l)
            o_ref[:, sl] = lax.dot_general(p.astype(BF16), vv, _NN, preferred_element_type=F32).astype(o_ref.dtype)
            lse_ref[:, g:g + 1] = mx + jnp.log(l)

    return pl.pallas_call(
        body, name="attn_fwd", grid=(kvh, s // tq),
        in_specs=[pl.BlockSpec((tq, gw), lambda j, i: (i, j)), pl.BlockSpec((s, HEAD_DIM), lambda j, i: (0, j)), pl.BlockSpec((s, HEAD_DIM), lambda j, i: (0, j))],
        out_specs=[pl.BlockSpec((tq, gw), lambda j, i: (i, j)), pl.BlockSpec((None, tq, GROUP), lambda j, i: (j, i, 0))],
        out_shape=[_sds((s, d), BF16), _sds((kvh, s, GROUP), F32)],
        compiler_params=_params(("parallel", "parallel")))(q, k, v)


def _attn_bwd_dq(q, k, v, do, lse):
    s, d = q.shape
    kvh = k.shape[1] // HEAD_DIM
    tq = _tile(s, Q_TILE, LANES)
    gw = GROUP * HEAD_DIM

    def body(q_ref, k_ref, v_ref, do_ref, lse_ref, dq_ref, delta_ref):
        kk, vv = k_ref[...], v_ref[...]
        for g in range(GROUP):
            sl = slice(g * HEAD_DIM, (g + 1) * HEAD_DIM)
            sc = lax.dot_general(q_ref[:, sl], kk, _NT, preferred_element_type=F32)
            p = jnp.exp(sc - lse_ref[:, g:g + 1])
            dp = lax.dot_general(do_ref[:, sl], vv, _NT, preferred_element_type=F32)
            delta = jnp.sum(p * dp, axis=-1, keepdims=True)
            ds = p * (dp - delta)
            dq_ref[:, sl] = lax.dot_general(ds.astype(BF16), kk, _NN, preferred_element_type=F32)
            delta_ref[:, g:g + 1] = delta

    qspec = pl.BlockSpec((tq, gw), lambda j, i: (i, j))
    kspec = pl.BlockSpec((s, HEAD_DIM), lambda j, i: (0, j))
    stat = pl.BlockSpec((None, tq, GROUP), lambda j, i: (j, i, 0))
    return pl.pallas_call(
        body, name="attn_bwd_dq", grid=(kvh, s // tq),
        in_specs=[qspec, kspec, kspec, qspec, stat], out_specs=[qspec, stat],
        out_shape=[_sds((s, d), F32), _sds((kvh, s, GROUP), F32)],
        compiler_params=_params(("parallel", "parallel")))(q, k, v, do, lse)


def _attn_bwd_dkv(q, k, v, do, lse_t, delta_t):
    s, d = q.shape
    kv = k.shape[1]
    kvh = kv // HEAD_DIM
    tq = _tile(s, Q_TILE, LANES)
    gw = GROUP * HEAD_DIM

    def body(q_ref, k_ref, v_ref, do_ref, lse_ref, delta_ref, dk_ref, dv_ref):
        i = pl.program_id(1)
        kk, vv = k_ref[...], v_ref[...]
        dk = jnp.zeros((s, HEAD_DIM), F32)
        dv = jnp.zeros((s, HEAD_DIM), F32)
        for g in range(GROUP):
            sl = slice(g * HEAD_DIM, (g + 1) * HEAD_DIM)
            qg, dog = q_ref[:, sl], do_ref[:, sl]
            sc_t = lax.dot_general(kk, qg, _NT, preferred_element_type=F32)
            p_t = jnp.exp(sc_t - lse_ref[g:g + 1, :])
            dp_t = lax.dot_general(vv, dog, _NT, preferred_element_type=F32)
            ds_t = p_t * (dp_t - delta_ref[g:g + 1, :])
            dk = dk + lax.dot_general(ds_t.astype(BF16), qg, _NN, preferred_element_type=F32)
            dv = dv + lax.dot_general(p_t.astype(BF16), dog, _NN, preferred_element_type=F32)

        @pl.when(i == 0)
        def _():
            dk_ref[...] = dk
            dv_ref[...] = dv

        @pl.when(i > 0)
        def _():
            dk_ref[...] += dk
            dv_ref[...] += dv

    qspec = pl.BlockSpec((tq, gw), lambda j, i: (i, j))
    kspec = pl.BlockSpec((s, HEAD_DIM), lambda j, i: (0, j))
    stat = pl.BlockSpec((None, GROUP, tq), lambda j, i: (j, 0, i))
    return pl.pallas_call(
        body, name="attn_bwd_dkv", grid=(kvh, s // tq),
        in_specs=[qspec, kspec, kspec, qspec, stat, stat], out_specs=[kspec, kspec],
        out_shape=[_sds((s, kv), F32), _sds((s, kv), F32)],
        compiler_params=_params(("parallel", "arbitrary")))(q, k, v, do, lse_t, delta_t)


ELEM_BLOCK_BYTES = 1 << 20

BIG = (("w_in", True), ("w_out_conv", False), ("w_out_attn", False), ("w_merge", False), ("w_up", True), ("w_down", False))
N_BIG = len(BIG)


def _elem_tiles(rows, width):
    tc = _tile(width, 2048, LANES)
    tr = _tile(rows, max(8, ELEM_BLOCK_BYTES // (4 * tc)), 8)
    return tr, tc


def _scalar_grid(grid, in_specs, out_specs):
    return pltpu.PrefetchScalarGridSpec(num_scalar_prefetch=1, grid=grid, in_specs=in_specs, out_specs=out_specs)


def _cast_place(w_stack, layer, chip, col_sharded, name):
    _, rows, width = w_stack.shape
    tr, tc = _elem_tiles(rows, width)
    nr, nc = rows // tr, width // tc

    def body(sc_ref, x_ref, o_ref):
        o_ref[...] = x_ref[...].astype(o_ref.dtype)

    if col_sharded:
        full, out_spec = (rows, width * N_CHIPS), pl.BlockSpec((tr, tc), lambda i, j, sc: (i, sc[0] * nc + j))
    else:
        full, out_spec = (rows * N_CHIPS, width), pl.BlockSpec((tr, tc), lambda i, j, sc: (sc[0] * nr + i, j))
    return pl.pallas_call(
        body, name=name,
        grid_spec=_scalar_grid((nr, nc), [pl.BlockSpec((None, tr, tc), lambda i, j, sc: (layer, i, j))], out_spec),
        out_shape=_sds(full, BF16), compiler_params=_params(("parallel", "parallel")))(chip, w_stack)


def _pair_add(g, landed, core, col_sharded, name):
    rows, width = landed.shape
    tr, tc = _elem_tiles(rows, width)
    nr, nc = rows // tr, width // tc

    def body(c_ref, g_ref, l_ref, o_ref):
        o_ref[...] = (g_ref[...] + l_ref[...]).astype(o_ref.dtype)

    if col_sharded:
        g_spec = pl.BlockSpec((tr, tc), lambda i, j, c: (c[0] * nr + i, j))
    else:
        g_spec = pl.BlockSpec((tr, tc), lambda i, j, c: (i, c[0] * nc + j))
    tile = pl.BlockSpec((tr, tc), lambda i, j, c: (i, j))
    return pl.pallas_call(
        body, name=name, grid_spec=_scalar_grid((nr, nc), [g_spec, tile], tile),
        out_shape=_sds((rows, width), BF16), compiler_params=_params(("parallel", "parallel")))(core, g, landed)


def _slot_of_relation(rel):
    return jnp.where(rel == 2, 0, jnp.where(rel == 1, 1, 2))


def _sum_chips(pair_sum, landed, chip, col_sharded, name):
    _, rows, width = landed.shape
    tr, tc = _elem_tiles(rows, width)
    nr, nc = rows // tr, width // tc

    def body(chip_ref, own_ref, q_ref, o_ref):
        me = chip_ref[0]
        own = own_ref[...].astype(F32)
        acc = None
        for t in range(N_CHIPS):
            rel = me ^ t
            term = jnp.where(rel == 0, own, q_ref[_slot_of_relation(rel)].astype(F32))
            acc = term if acc is None else acc + term
        o_ref[...] = acc

    if col_sharded:
        own_spec = pl.BlockSpec((tr, tc), lambda i, j, c: (i, c[0] * nc + j))
    else:
        own_spec = pl.BlockSpec((tr, tc), lambda i, j, c: (c[0] * nr + i, j))
    return pl.pallas_call(
        body, name=name,
        grid_spec=_scalar_grid((nr, nc), [own_spec, pl.BlockSpec((N_CHIPS - 1, tr, tc), lambda i, j, c: (0, i, j))],
                               pl.BlockSpec((tr, tc), lambda i, j, c: (i, j))),
        out_shape=_sds((rows, width), F32), compiler_params=_params(("parallel", "parallel")))(chip, pair_sum, landed)


def _adamw_math(w, g, m, v):
    mn = ADAM_B1 * m + (1.0 - ADAM_B1) * g
    vn = ADAM_B2 * v + (1.0 - ADAM_B2) * jnp.square(g)
    m_hat = mn / (1.0 - ADAM_B1 ** ADAM_STEP)
    v_hat = vn / (1.0 - ADAM_B2 ** ADAM_STEP)
    return -ADAM_LR * (m_hat / (jnp.sqrt(v_hat) + ADAM_EPS) + ADAM_WD * w), mn, vn


def _adamw(w, g, m, v, name):
    shape = w.shape
    width = shape[-1]
    w2, g2, m2, v2 = (a.reshape(-1, width) for a in (w, g, m, v))
    rows = w2.shape[0]
    tr, tc = _elem_tiles(rows, width)

    def body(w_ref, g_ref, m_ref, v_ref, d_ref, nm_ref, nv_ref):
        d_ref[...], nm_ref[...], nv_ref[...] = _adamw_math(w_ref[...], g_ref[...], m_ref[...], v_ref[...])

    tile = pl.BlockSpec((tr, tc), lambda i, j: (i, j))
    outs = pl.pallas_call(body, name=name, grid=(rows // tr, width // tc), in_specs=[tile] * 4, out_specs=[tile] * 3,
                          out_shape=[_sds((rows, width), F32)] * 3, compiler_params=_params(("parallel", "parallel")))(w2, g2, m2, v2)
    return tuple(o.reshape(shape) for o in outs)


def _adamw_layer(w, m, v, g_mine, g_sibling, carried, layer, core, col_sharded, name):
    depth, rows, width = w.shape
    pr, pc = g_mine.shape
    tr, tc = _elem_tiles(pr, pc)
    n_half = pr // tr if col_sharded else pc // tc
    if carried is None:
        carried = tuple(lax.empty((depth, rows, width), F32) for _ in range(4))

    def body(sc_ref, w_ref, m_ref, v_ref, gm_ref, gs_ref, *rest):
        g_ref, d_ref, nm_ref, nv_ref = rest[-4:]
        pos = pl.program_id(0) if col_sharded else pl.program_id(1)
        gv = jnp.where(pos // n_half == sc_ref[0], gm_ref[...], gs_ref[...])
        g_ref[...] = gv
        d_ref[...], nm_ref[...], nv_ref[...] = _adamw_math(w_ref[...], gv, m_ref[...], v_ref[...])

    stacked = pl.BlockSpec((None, tr, tc), lambda i, j, sc: (layer, i, j))
    if col_sharded:
        half = pl.BlockSpec((tr, tc), lambda i, j, sc: (i % n_half, j))
    else:
        half = pl.BlockSpec((tr, tc), lambda i, j, sc: (i, j % n_half))
    return pl.pallas_call(
        body, name=name,
        grid_spec=_scalar_grid((rows // tr, width // tc), [stacked] * 3 + [half] * 2 + [ANY] * 4, [stacked] * 4),
        out_shape=[_sds((depth, rows, width), F32)] * 4, input_output_aliases={6: 0, 7: 1, 8: 2, 9: 3},
        compiler_params=_params(("parallel", "parallel")))(core, w, m, v, g_mine, g_sibling, *carried)


_SEM = pl.BlockSpec(memory_space=pltpu.SEMAPHORE)
_HBM = pl.BlockSpec(memory_space=pltpu.HBM)
_VMEM = pl.BlockSpec(memory_space=pltpu.VMEM)
_EFFECT = pltpu.SideEffectType.DATAFLOW_SIDE_EFFECTING


def _place():
    x, y, c = lax.axis_index("x"), lax.axis_index("y"), lax.axis_index("c")
    others = [(1 - x, y), (x, 1 - y), (1 - x, 1 - y)]
    return x, y, c, others


def _chip_index(px, py):
    return 2 * px + py


def _remote(src, dst, send_sems, recv_sems, k, to):
    return pltpu.make_async_remote_copy(src_ref=src, dst_ref=dst, send_sem=send_sems.at[k], recv_sem=recv_sems.at[k],
                                        device_id=to, device_id_type=MESH)


def _phase(name, bufs, waits, wait_fn, n_start, start_fn, deps):
    nb, nd = len(bufs), len(deps)

    def body(*refs):
        buf_refs = refs[:nb]
        pos = nb
        if waits is not None:
            wait_fn(buf_refs, refs[pos], refs[pos + 1])
            pos += 2
        pos += nd
        if n_start:
            start_fn(buf_refs, refs[pos], refs[pos + 1])
            pos += 2
        token = refs[pos + nb]
        token[...] = jnp.zeros_like(token)

    n_sem_out = 2 if n_start else 0
    if waits is None:
        bufs = [pltpu.with_memory_space_constraint(b, pltpu.HBM) for b in bufs]
    outs = pl.pallas_call(
        body, name=name,
        in_specs=[_HBM] * nb + ([_SEM] * 2 if waits is not None else []) + [ANY] * nd,
        out_specs=[_SEM] * n_sem_out + [_HBM] * nb + [_VMEM],
        out_shape=[pltpu.SemaphoreType.DMA((n_start,))] * n_sem_out + [pltpu.HBM(b.shape, b.dtype) for b in bufs] + [_sds((8, LANES), F32)],
        input_output_aliases={i: n_sem_out + i for i in range(nb)},
        compiler_params=pltpu.CompilerParams(has_side_effects=_EFFECT),
    )(*bufs, *(waits if waits is not None else ()), *deps)
    sems = tuple(outs[:2]) if n_start else None
    return sems, list(outs[n_sem_out:n_sem_out + nb]), outs[-1]


class _Chains:
    def __init__(self):
        self.active = []
        self.last = None
        self.dep = None
        self.pinned = []
        self.token = None
        self.at = None

    def phase(self, name, bufs, waits, wait_fn, n_start, start_fn):
        deps = [a for a in (self.last, self.dep) if a is not None] + self.pinned
        sems, thru, token = _phase(name, bufs, waits, wait_fn, n_start, start_fn, deps)
        self.last, self.dep, self.token, self.pinned = token, None, token, []
        return sems, thru

    def pin(self, result):
        self.pinned.append(result)

    def add(self, gen):
        self.active.append(gen)

    def tick(self, dep, at=None):
        self.at, self.token, self.dep = at, None, dep
        for gen in list(self.active):
            if next(gen, "done") == "done":
                self.active.remove(gen)
        return self.token


def _after(small, token):
    return small if token is None else small + token[0, 0]


def _shard_region(ref, col_sharded, chip, n_shard):
    start = pl.multiple_of(chip * n_shard, LANES if col_sharded else 8)
    if col_sharded:
        return ref.at[:, pl.ds(start, n_shard)]
    return ref.at[pl.ds(start, n_shard), :]


def _row_half(ref, half):
    n_rows = ref.shape[0]
    return ref.at[pl.ds(pl.multiple_of(half * (n_rows // 2), 8), n_rows // 2), :]


def _half_of(ref, col_sharded, half):
    rows, width = ref.shape
    if col_sharded:
        return ref.at[pl.ds(pl.multiple_of(half * (rows // 2), 8), rows // 2), :]
    return ref.at[:, pl.ds(pl.multiple_of(half * (width // 2), LANES), width // 2)]


def _gather_chain(chains, layer, bufs, out, wait_for):
    n = 3 * N_BIG

    def region(refs, a, chip, half):
        col = BIG[a][1]
        n_shard = refs[a].shape[1] // N_CHIPS if col else refs[a].shape[0] // N_CHIPS
        return _row_half(_shard_region(refs[a], col, chip, n_shard), half)

    def start_ici(refs, send, recv):
        x, y, c, others = _place()
        for a in range(N_BIG):
            mine = region(refs, a, _chip_index(x, y), c)
            for j, (ox, oy) in enumerate(others):
                _remote(mine, mine, send, recv, 3 * a + j, (ox, oy, c)).start()

    def wait_ici(refs, send, recv):
        x, y, c, others = _place()
        for a in range(N_BIG):
            for j, (ox, oy) in enumerate(others):
                landed = region(refs, a, _chip_index(ox, oy), c)
                cp = _remote(landed, landed, send, recv, 3 * a + j, (x, y, 1 - c))
                cp.wait_recv()
                cp.wait_send()

    def start_d2d(refs, send, recv):
        x, y, c, others = _place()
        for a in range(N_BIG):
            for j, (ox, oy) in enumerate(others):
                landed = region(refs, a, _chip_index(ox, oy), c)
                _remote(landed, landed, send, recv, 3 * a + j, (x, y, 1 - c)).start()

    def wait_d2d(refs, send, recv):
        x, y, c, others = _place()
        for a in range(N_BIG):
            for j, (ox, oy) in enumerate(others):
                theirs = region(refs, a, _chip_index(ox, oy), 1 - c)
                cp = _remote(theirs, theirs, send, recv, 3 * a + j, (x, y, 1 - c))
                cp.wait_recv()
                cp.wait_send()

    sems, bufs = chains.phase(f"gather_ici_start_{layer}", bufs, None, None, n, start_ici)
    yield
    while wait_for is not None and chains.at != wait_for:
        yield
    sems, bufs = chains.phase(f"gather_forward_{layer}", bufs, sems, wait_ici, n, start_d2d)
    yield
    _, bufs = chains.phase(f"gather_done_{layer}", bufs, sems, wait_d2d, 0, None)
    out[layer] = bufs


def _grad_chain(chains, layer, group, grads, core, chip, w, mom, var, carried):
    n = len(group)
    tag = f"{layer}_{group[0]}"
    kinds = [BIG[a][1] for a in group]
    names = [BIG[a][0] for a in group]
    sibling_of = lambda x, y, c: (x, y, 1 - c)

    def pair_start(refs, send, recv):
        x, y, c, _ = _place()
        for i in range(n):
            _remote(_half_of(refs[i], kinds[i], 1 - c), refs[n + i], send, recv, i, sibling_of(x, y, c)).start()

    def pair_wait(refs, send, recv):
        x, y, c, _ = _place()
        for i in range(n):
            cp = _remote(_half_of(refs[i], kinds[i], 1 - c), refs[n + i], send, recv, i, sibling_of(x, y, c))
            cp.wait_recv()
            cp.wait_send()

    def piece(ref, col, chip_idx):
        return _shard_region(ref, col, chip_idx, ref.shape[1] // N_CHIPS if col else ref.shape[0] // N_CHIPS)

    def scatter_start(refs, send, recv):
        x, y, c, others = _place()
        for i in range(n):
            for j, (ox, oy) in enumerate(others):
                _remote(piece(refs[i], kinds[i], _chip_index(ox, oy)), refs[n + i].at[j], send, recv, 3 * i + j, (ox, oy, c)).start()

    def scatter_wait(refs, send, recv):
        x, y, c, others = _place()
        for i in range(n):
            for j, (ox, oy) in enumerate(others):
                cp = _remote(piece(refs[i], kinds[i], _chip_index(ox, oy)), refs[n + i].at[j], send, recv, 3 * i + j, (ox, oy, c))
                cp.wait_recv()
                cp.wait_send()

    def join_start(refs, send, recv):
        x, y, c, _ = _place()
        for i in range(n):
            _remote(refs[i], refs[n + i], send, recv, i, sibling_of(x, y, c)).start()

    def join_wait(refs, send, recv):
        x, y, c, _ = _place()
        for i in range(n):
            cp = _remote(refs[i], refs[n + i], send, recv, i, sibling_of(x, y, c))
            cp.wait_recv()
            cp.wait_send()

    half_shapes = [(g.shape[0] // 2, g.shape[1]) if col else (g.shape[0], g.shape[1] // 2) for col, g in zip(kinds, grads)]
    lands = [lax.empty(s, F32) for s in half_shapes]
    sems, bufs = chains.phase(f"pair_start_{tag}", list(grads) + lands, None, None, n, pair_start)
    yield
    _, bufs = chains.phase(f"pair_wait_{tag}", bufs, sems, pair_wait, 0, None)
    pair_sums = [_pair_add(bufs[i], bufs[n + i], core, kinds[i], "pair_add_" + names[i]) for i in range(n)]
    piece_shapes = [(s[0], s[1] // N_CHIPS) if col else (s[0] // N_CHIPS, s[1]) for col, s in zip(kinds, half_shapes)]
    slots = [lax.empty((N_CHIPS - 1, *s), BF16) for s in piece_shapes]
    sems, bufs = chains.phase(f"scatter_start_{tag}", pair_sums + slots, None, None, 3 * n, scatter_start)
    yield
    _, bufs = chains.phase(f"scatter_wait_{tag}", bufs, sems, scatter_wait, 0, None)
    reduced = [_sum_chips(bufs[i], bufs[n + i], chip, kinds[i], "sum_chips_" + names[i]) for i in range(n)]
    theirs = [lax.empty(s, F32) for s in piece_shapes]
    sems, bufs = chains.phase(f"join_start_{tag}", reduced + theirs, None, None, n, join_start)
    yield
    _, bufs = chains.phase(f"join_wait_{tag}", bufs, sems, join_wait, 0, None)
    for i in range(n):
        carried[names[i]] = _adamw_layer(w[names[i]], mom[names[i]], var[names[i]], bufs[i], bufs[n + i], carried.get(names[i]),
                                         layer, core, kinds[i], "adamw_" + names[i])
        chains.pin(carried[names[i]][0])


def _allreduce_small(vec, name):
    rows = vec.shape[0]
    masks = [(dx, dy, dc) for dx in (0, 1) for dy in (0, 1) for dc in (0, 1)][1:]

    def body(v_ref, o_ref, gather_ref, send_sems, recv_sems):
        x, y, c, _ = _place()
        me = 4 * x + 2 * y + c
        gather_ref[me] = v_ref[...]
        copies = []
        for k, (dx, dy, dc) in enumerate(masks):
            peer = (x ^ dx, y ^ dy, c ^ dc)
            copies.append(_remote(v_ref, gather_ref.at[me], send_sems, recv_sems, k, peer))
        for cp in copies:
            cp.start()
        for k, (dx, dy, dc) in enumerate(masks):
            slot = gather_ref.at[4 * (x ^ dx) + 2 * (y ^ dy) + (c ^ dc)]
            _remote(slot, slot, send_sems, recv_sems, k, (x, y, c)).wait_recv()
        for cp in copies:
            cp.wait_send()
        acc = gather_ref[0]
        for dev in range(1, N_DEV):
            acc = acc + gather_ref[dev]
        o_ref[...] = acc

    return pl.pallas_call(
        body, name=name, in_specs=[_VMEM], out_specs=_VMEM, out_shape=_sds((rows, LANES), F32),
        scratch_shapes=[pltpu.VMEM((N_DEV, rows, LANES), F32), pltpu.SemaphoreType.DMA((N_DEV - 1,)), pltpu.SemaphoreType.DMA((N_DEV - 1,))],
        compiler_params=pltpu.CompilerParams(has_side_effects=True, vmem_limit_bytes=VMEM_LIMIT_BYTES),
    )(vec)


def _relu2(acc):
    r = jnp.maximum(acc, 0.0)
    return acc, r * r


def _relu2_bwd(acc, up):
    return (acc * (2.0 * jnp.maximum(up, 0.0)),)


def _layer_fwd(x, h, p, cols, tables, tick, last, target=None):
    cos_t, sin_t = tables
    w = p()
    z = _mm(h, w["w_in"], "nn", [F32], name="mm_in")
    a = _conv_fwd(z, w["conv_w"], cols)
    q, k, v = _rope_fwd(z, w["q_norm"], w["k_norm"], cos_t, sin_t, cols)
    o, lse = _attn_fwd(q, k, v)
    y_a = _mm(a, w["w_out_conv"], "nn", [F32], name="mm_out_conv")
    y_b = _mm(o, w["w_out_attn"], "nn", [F32], name="mm_out_attn")
    mix = _gate_fwd(z, w["gate_bias"], y_a, y_b, cols)
    mixed = _mm(mix, w["w_merge"], "nn", [F32], name="mm_merge")
    x1, h2 = _resid_norm(x, mixed, _after(w["norm_mix_post"], tick(mixed, 0)), w["norm_mlp_pre"])
    up, act = _mm(h2, w["w_up"], "nn", [F32, BF16], name="mm_up", epilogue=_relu2)
    f = _mm(act, w["w_down"], "nn", [F32], name="mm_down")
    kept = dict(x=x, h=h, z=z, a=a, q=q, k=k, v=v, o=o, lse=lse, y_a=y_a, y_b=y_b, mix=mix, mixed=mixed, x1=x1, h2=h2, up=up, act=act, f=f)
    g_post = _after(w["norm_mlp_post"], tick(f, 1))
    if last:
        return _resid_norm_loss(x1, f, g_post, target), kept
    return _resid_norm(x1, f, g_post, w["norm_next"]), kept


def _layer_bwd(dx_out, w, kept, cols, tables, tick, emit):
    cos_t, sin_t = tables
    t = kept
    df, d_norm_mlp_post = _norm_bwd(t["f"], _after(w["norm_mlp_post"], tick(dx_out, 0)), dx_out, None, BF16, "norm_bwd_mlp_post")
    dup = _mm(df, w["w_down"], "nt", [BF16], name="mm_d_act", epilogue=_relu2_bwd, extras=(t["up"],))
    g_w_down = _mm(t["act"], df, "tn", [F32], name="mm_g_down")
    g_w_up = _mm(t["h2"], dup, "tn", [F32], name="mm_g_up")
    emit((4, 5), [g_w_up, g_w_down])
    dh2 = _mm(dup, w["w_up"], "nt", [F32], name="mm_d_h2")
    dx1, d_norm_mlp_pre = _norm_bwd(t["x1"], _after(w["norm_mlp_pre"], tick(dh2, 1)), dh2, dx_out, F32, "norm_bwd_mlp_pre")
    dmixed, d_norm_mix_post = _norm_bwd(t["mixed"], w["norm_mix_post"], dx1, None, BF16, "norm_bwd_mix_post")
    dmix = _mm(dmixed, w["w_merge"], "nt", [F32], name="mm_d_mix")
    g_w_merge = _mm(t["mix"], dmixed, "tn", [F32], name="mm_g_merge")
    dy_a, dy_b, dz_ga, dz_gb, dbias_a, dbias_b = _gate_bwd(t["z"], _after(w["gate_bias"], tick(g_w_merge, 2)), t["y_a"], t["y_b"], dmix, cols)
    g_w_out_conv = _mm(t["a"], dy_a, "tn", [F32], name="mm_g_out_conv")
    da = _mm(dy_a, w["w_out_conv"], "nt", [F32], name="mm_d_a")
    g_w_out_attn = _mm(t["o"], dy_b, "tn", [F32], name="mm_g_out_attn")
    emit((1, 2, 3), [g_w_out_conv, g_w_out_attn, g_w_merge])
    do = _mm(dy_b, w["w_out_attn"], "nt", [BF16], name="mm_d_o")
    dz_cb, dz_cc, dz_ci, d_conv_w = _conv_bwd(t["z"], _after(w["conv_w"], tick(do, 3)), da, cols)
    dq, delta = _attn_bwd_dq(t["q"], t["k"], t["v"], do, t["lse"])
    dk, dv = _attn_bwd_dkv(t["q"], t["k"], t["v"], do, jnp.swapaxes(t["lse"], 1, 2), jnp.swapaxes(delta, 1, 2))
    dz_q, dz_k, dz_v, d_q_norm, d_k_norm = _rope_bwd(t["z"], _after(w["q_norm"], tick(dv, 4)), w["k_norm"], cos_t, sin_t, dq, dk, dv, cols)
    dz = jnp.concatenate([dz_cb, dz_cc, dz_ci, dz_q, dz_k, dz_v, dz_ga, dz_gb], axis=1)
    g_w_in = _mm(t["h"], dz, "tn", [F32], name="mm_g_in")
    emit((0,), [g_w_in])
    dh = _mm(dz, w["w_in"], "nt", [F32], name="mm_d_h")
    dx_in, d_norm_mix_pre = _norm_bwd(t["x"], _after(w["norm_mix_pre"], tick(dh, 5)), dh, dx1, F32, "norm_bwd_mix_pre")
    small = dict(norm_mix_pre=d_norm_mix_pre, gate_bias=jnp.concatenate([dbias_a, dbias_b], axis=1), conv_w=d_conv_w, q_norm=d_q_norm,
                 k_norm=d_k_norm, norm_mix_post=d_norm_mix_post, norm_mlp_pre=d_norm_mlp_pre, norm_mlp_post=d_norm_mlp_post)
    return dx_in, small


SMALL = ("norm_mix_pre", "gate_bias", "conv_w", "q_norm", "k_norm", "norm_mix_post", "norm_mlp_pre", "norm_mlp_post")
WEIGHTS = ("norm_mix_pre", "w_in", "gate_bias", "conv_w", "q_norm", "k_norm", "w_out_conv", "w_out_attn", "w_merge",
           "norm_mix_post", "norm_mlp_pre", "w_up", "w_down", "norm_mlp_post")


def _pack(parts):
    flat = jnp.concatenate([a.reshape(-1) for a in parts])
    rows = -(-flat.shape[0] // LANES)
    pad = (-rows) % 8
    flat = jnp.pad(flat, (0, (rows + pad) * LANES - flat.shape[0]))
    return flat.reshape(rows + pad, LANES)


def _unpack(packed, shapes):
    flat = packed.reshape(-1)
    out, off = [], 0
    for shp in shapes:
        n = math.prod(shp)
        out.append(flat[off:off + n].reshape(shp))
        off += n
    return out


def kernel(x, norm_mix_pre, w_in, gate_bias, conv_w, q_norm, k_norm, w_out_conv, w_out_attn, w_merge, norm_mix_post, norm_mlp_pre, w_up, w_down, norm_mlp_post, loss_target, m_norm_mix_pre, m_w_in, m_gate_bias, m_conv_w, m_q_norm, m_k_norm, m_w_out_conv, m_w_out_attn, m_w_merge, m_norm_mix_post, m_norm_mlp_pre, m_w_up, m_w_down, m_norm_mlp_post, v_norm_mix_pre, v_w_in, v_gate_bias, v_conv_w, v_q_norm, v_k_norm, v_w_out_conv, v_w_out_attn, v_w_merge, v_norm_mix_post, v_norm_mlp_pre, v_w_up, v_w_down, v_norm_mlp_post):
    w = dict(norm_mix_pre=norm_mix_pre, w_in=w_in, gate_bias=gate_bias, conv_w=conv_w, q_norm=q_norm, k_norm=k_norm, w_out_conv=w_out_conv,
             w_out_attn=w_out_attn, w_merge=w_merge, norm_mix_post=norm_mix_post, norm_mlp_pre=norm_mlp_pre, w_up=w_up, w_down=w_down,
             norm_mlp_post=norm_mlp_post)
    mom = dict(norm_mix_pre=m_norm_mix_pre, w_in=m_w_in, gate_bias=m_gate_bias, conv_w=m_conv_w, q_norm=m_q_norm, k_norm=m_k_norm,
               w_out_conv=m_w_out_conv, w_out_attn=m_w_out_attn, w_merge=m_w_merge, norm_mix_post=m_norm_mix_post, norm_mlp_pre=m_norm_mlp_pre,
               w_up=m_w_up, w_down=m_w_down, norm_mlp_post=m_norm_mlp_post)
    var = dict(norm_mix_pre=v_norm_mix_pre, w_in=v_w_in, gate_bias=v_gate_bias, conv_w=v_conv_w, q_norm=v_q_norm, k_norm=v_k_norm,
               w_out_conv=v_w_out_conv, w_out_attn=v_w_out_attn, w_merge=v_w_merge, norm_mix_post=v_norm_mix_post, norm_mlp_pre=v_norm_mlp_pre,
               w_up=v_w_up, w_down=v_w_down, norm_mlp_post=v_norm_mlp_post)
    depth = w_in.shape[0]
    _, s, d = x.shape
    cols = _Cols(d)
    x0 = x.reshape(s, d)
    target = loss_target.reshape(s, d)
    tables = _rope_tables(s)
    chip = (2 * lax.axis_index("x") + lax.axis_index("y")).astype(jnp.int32)
    core = lax.axis_index("c").astype(jnp.int32)
    chip_vec, core_vec = chip.reshape(1), core.reshape(1)

    n_conv = conv_w.shape[-1]
    placed = lax.dynamic_update_slice_in_dim(jnp.zeros((depth, conv_w.shape[1], n_conv * N_CHIPS), F32), conv_w, chip * n_conv, axis=2)
    conv_full = _unpack(_allreduce_small(_pack([jnp.where(core == 0, placed, 0.0)]), "gather_conv_w"), [placed.shape])[0]

    chains = _Chains()
    full = [None] * depth
    gathers = []
    for l in range(depth):
        bufs = [_cast_place(w[name], l, chip_vec, col, "cast_place_" + name) for name, col in BIG]
        gathers.append(_gather_chain(chains, l, bufs, full, ("fwd", l - 1, 0) if l else None))
        next(gathers[l])
    next(gathers[0])
    next(gathers[0], None)
    for g in gathers[1:]:
        chains.add(g)

    def layer_params(l):
        p = {name: f for (name, _), f in zip(BIG, full[l])}
        p["conv_w"] = conv_full[l]
        p["gate_bias"] = gate_bias[l].reshape(1, -1)
        for name in ("norm_mix_pre", "q_norm", "k_norm", "norm_mix_post", "norm_mlp_pre", "norm_mlp_post"):
            p[name] = w[name][l].reshape(1, -1)
        if l + 1 < depth:
            p["norm_next"] = w["norm_mix_pre"][l + 1].reshape(1, -1)
        return p

    kept = []
    xl, h = x0, _norm_first(x0, norm_mix_pre[0].reshape(1, -1))
    for l in range(depth):
        last = l == depth - 1
        (xl, h), t = _layer_fwd(xl, h, functools.partial(layer_params, l), cols, tables,
                                lambda dep, k, l=l: chains.tick(dep, ("fwd", l, k)), last, target if last else None)
        kept.append(t)
    dx, loss_local = xl, h

    carried = {}
    small = [None] * depth
    for l in reversed(range(depth)):
        def emit(group, grads, l=l):
            chains.add(_grad_chain(chains, l, group, grads, core_vec, chip_vec, w, mom, var, carried))

        dx, small[l] = _layer_bwd(dx, layer_params(l), kept[l], cols, tables, lambda dep, k: chains.tick(dep), emit)
    while chains.active:
        chains.tick(None)
    grads = {name: carried[name][0] for name, _ in BIG}
    delta = {name: carried[name][1] for name, _ in BIG}
    new_m = {name: carried[name][2] for name, _ in BIG}
    new_v = {name: carried[name][3] for name, _ in BIG}

    small_full_shapes = [(depth,) + small[0][name].shape for name in SMALL]
    packed = _pack([jnp.stack([small[l][name] for l in range(depth)]) for name in SMALL] + [jnp.broadcast_to(loss_local.reshape(1), (LANES,))])
    small_sum = _unpack(_allreduce_small(packed, "allreduce_small"), small_full_shapes + [(LANES,)])
    loss = small_sum[-1][0]
    for name, g in zip(SMALL, small_sum[:-1]):
        if name == "conv_w":
            g = lax.dynamic_slice_in_dim(g, chip * n_conv, n_conv, axis=2)
        grads[name] = g.reshape(w[name].shape)
    small_shapes = [w[name].shape for name in SMALL]
    packs = [_pack([src[name] for name in SMALL]) for src in (w, grads, mom, var)]
    for dst, out in zip((delta, new_m, new_v), _adamw(*packs, "adamw_small")):
        for name, val in zip(SMALL, _unpack(out, small_shapes)):
            dst[name] = val

    grad_x = dx.reshape(x.shape)
    return (loss, grad_x, *[grads[n] for n in WEIGHTS], *[delta[n] for n in WEIGHTS], *[new_m[n] for n in WEIGHTS], *[new_v[n] for n in WEIGHTS])
```

```python
import functools
import math

import jax
import jax.numpy as jnp
from jax import lax
from jax.experimental import pallas as pl
from jax.experimental.pallas import tpu as pltpu

F32 = jnp.float32
BF16 = jnp.bfloat16

HEAD_DIM = 128
GROUP = 4
GRID_W = 64
ROPE_THETA = 10000.0
RMS_EPS = 1e-6
ADAM_LR = 0.001
ADAM_B1 = 0.9
ADAM_B2 = 0.999
ADAM_EPS = 1e-08
ADAM_WD = 0.01
ADAM_STEP = 10

LANES = 128
N_CHIPS = 4
N_DEV = 8
VMEM_LIMIT_BYTES = 56 * 1024 * 1024
MESH = pl.DeviceIdType.MESH
ANY = pl.BlockSpec(memory_space=pl.ANY)


def _tile(dim, cap, mult):
    if dim <= cap:
        return dim
    t = (cap // mult) * mult
    while t >= mult:
        if dim % t == 0:
            return t
        t -= mult
    raise ValueError(f"no tile for {dim} under {cap} in multiples of {mult}")


def _params(sem=None):
    return pltpu.CompilerParams(dimension_semantics=sem, vmem_limit_bytes=VMEM_LIMIT_BYTES)


def _sds(shape, dtype):
    return jax.ShapeDtypeStruct(tuple(shape), dtype)


def _rstd(x):
    return lax.rsqrt(jnp.mean(x * x, axis=-1, keepdims=True) + RMS_EPS)


_DOT_DIMS = {"nn": ((1,), (0,)), "nt": ((1,), (1,)), "tn": ((0,), (0,))}


def _mm(a, b, mode, out_dtypes, *, name, epilogue=None, extras=(), tm=1024, tn=1024, tk=2048):
    if mode == "nn":
        (m, k), (k2, n) = a.shape, b.shape
    elif mode == "nt":
        (m, k), (n, k2) = a.shape, b.shape
    else:
        (k, m), (k2, n) = a.shape, b.shape
    assert k == k2, (a.shape, b.shape, mode)
    tm, tn, tk = _tile(m, tm, 8), _tile(n, tn, LANES), _tile(k, tk, LANES)
    nk = k // tk
    a_spec = pl.BlockSpec((tk, tm), lambda i, j, kk: (kk, i)) if mode == "tn" else pl.BlockSpec((tm, tk), lambda i, j, kk: (i, kk))
    b_spec = pl.BlockSpec((tn, tk), lambda i, j, kk: (j, kk)) if mode == "nt" else pl.BlockSpec((tk, tn), lambda i, j, kk: (kk, j))
    tile_spec = pl.BlockSpec((tm, tn), lambda i, j, kk: (i, j))
    n_extra, n_out = len(extras), len(out_dtypes)
    dims = (_DOT_DIMS[mode], ((), ()))

    def body(*refs):
        a_ref, b_ref = refs[:2]
        extra_refs = refs[2:2 + n_extra]
        out_refs = refs[2 + n_extra:2 + n_extra + n_out]
        part = lax.dot_general(a_ref[...].astype(BF16), b_ref[...].astype(BF16), dims, preferred_element_type=F32)

        def finish(total):
            res = epilogue(total, *[e[...] for e in extra_refs]) if epilogue is not None else (total,)
            for o, r in zip(out_refs, res):
                o[...] = r.astype(o.dtype)

        if nk == 1:
            finish(part)
        else:
            acc = refs[-1]
            kk = pl.program_id(2)

            @pl.when(kk == 0)
            def _():
                acc[...] = part

            @pl.when(kk > 0)
            def _():
                acc[...] += part

            @pl.when(kk == nk - 1)
            def _():
                finish(acc[...])

    outs = pl.pallas_call(
        body, name=name, grid=(m // tm, n // tn, nk),
        in_specs=[a_spec, b_spec] + [tile_spec] * n_extra,
        out_specs=[tile_spec] * n_out,
        out_shape=[_sds((m, n), d) for d in out_dtypes],
        scratch_shapes=[pltpu.VMEM((tm, tn), F32)] if nk > 1 else [],
        compiler_params=_params(("parallel", "parallel", "arbitrary")),
    )(a, b, *extras)
    return outs if n_out > 1 else outs[0]


ROW_TILE = 256


def _norm_first(x, g):
    s, d = x.shape
    ts = _tile(s, ROW_TILE, 8)

    def body(x_ref, g_ref, h_ref):
        xv = x_ref[...]
        h_ref[...] = (xv * _rstd(xv) * g_ref[...]).astype(h_ref.dtype)

    row = pl.BlockSpec((ts, d), lambda i: (i, 0))
    vec = pl.BlockSpec((1, d), lambda i: (0, 0))
    return pl.pallas_call(body, name="norm_first", grid=(s // ts,), in_specs=[row, vec], out_specs=row,
                          out_shape=_sds((s, d), BF16), compiler_params=_params(("parallel",)))(x, g)


def _resid_norm(xres, y, g_post, g_next):
    s, d = xres.shape
    ts = _tile(s, ROW_TILE, 8)

    def body(x_ref, y_ref, gp_ref, gn_ref, xn_ref, hn_ref):
        yv = y_ref[...]
        xn = x_ref[...] + yv * _rstd(yv) * gp_ref[...]
        xn_ref[...] = xn
        hn_ref[...] = (xn * _rstd(xn) * gn_ref[...]).astype(hn_ref.dtype)

    row = pl.BlockSpec((ts, d), lambda i: (i, 0))
    vec = pl.BlockSpec((1, d), lambda i: (0, 0))
    return pl.pallas_call(body, name="resid_norm", grid=(s // ts,), in_specs=[row, row, vec, vec], out_specs=[row, row],
                          out_shape=[_sds((s, d), F32), _sds((s, d), BF16)], compiler_params=_params(("parallel",)))(xres, y, g_post, g_next)


def _resid_norm_loss(xres, y, g_post, target):
    s, d = xres.shape
    ts = _tile(s, ROW_TILE, 8)
    n_steps = s // ts

    def body(x_ref, y_ref, gp_ref, t_ref, dout_ref, loss_ref, acc_ref):
        i = pl.program_id(0)
        yv = y_ref[...]
        err = x_ref[...] + yv * _rstd(yv) * gp_ref[...] - t_ref[...]
        dout_ref[...] = err / d
        part = jnp.sum(err * err, axis=0, keepdims=True)

        @pl.when(i == 0)
        def _():
            acc_ref[...] = part

        @pl.when(i > 0)
        def _():
            acc_ref[...] += part

        @pl.when(i == n_steps - 1)
        def _():
            loss_ref[...] = 0.5 * jnp.sum(acc_ref[...], axis=1, keepdims=True) / d

    row = pl.BlockSpec((ts, d), lambda i: (i, 0))
    vec = pl.BlockSpec((1, d), lambda i: (0, 0))
    one = pl.BlockSpec((1, 1), lambda i: (0, 0))
    return pl.pallas_call(body, name="resid_norm_loss", grid=(n_steps,), in_specs=[row, row, vec, row], out_specs=[row, one],
                          out_shape=[_sds((s, d), F32), _sds((1, 1), F32)], scratch_shapes=[pltpu.VMEM((1, d), F32)],
                          compiler_params=_params(("arbitrary",)))(xres, y, g_post, target)


def _norm_bwd(xin, g, dout, dres, out_dtype, name):
    s, d = xin.shape
    ts = _tile(s, ROW_TILE, 8)
    has_res = dres is not None

    def body(*refs):
        x_ref, g_ref, do_ref = refs[:3]
        dx_ref, dg_ref = refs[-2:]
        i = pl.program_id(0)
        xv, dov = x_ref[...], do_ref[...]
        r = _rstd(xv)
        xhat = xv * r
        dg = jnp.sum(dov * xhat, axis=0, keepdims=True)
        dxh = dov * g_ref[...]
        dx = r * (dxh - xhat * jnp.mean(dxh * xhat, axis=-1, keepdims=True))
        if has_res:
            dx = dx + refs[3][...]
        dx_ref[...] = dx.astype(dx_ref.dtype)

        @pl.when(i == 0)
        def _():
            dg_ref[...] = dg

        @pl.when(i > 0)
        def _():
            dg_ref[...] += dg

    row = pl.BlockSpec((ts, d), lambda i: (i, 0))
    vec = pl.BlockSpec((1, d), lambda i: (0, 0))
    ops = [xin, g, dout] + ([dres] if has_res else [])
    return pl.pallas_call(body, name=name, grid=(s // ts,), in_specs=[row, vec, row] + ([row] if has_res else []),
                          out_specs=[row, vec], out_shape=[_sds((s, d), out_dtype), _sds((1, d), F32)],
                          compiler_params=_params(("arbitrary",)))(*ops)


class _Cols:
    def __init__(self, d):
        self.d = d
        self.kv = d // GROUP
        self.cb, self.cc, self.ci, self.q = 0, d, 2 * d, 3 * d
        self.k = 4 * d
        self.v = 4 * d + self.kv
        self.ga = 4 * d + 2 * self.kv
        self.gb = 5 * d + 2 * self.kv
        self.width = 6 * d + 2 * self.kv


CONV_COLS = 128


def _shift_rows(u, down):
    s = u.shape[0]
    rows = lax.broadcasted_iota(jnp.int32, u.shape, 0)
    if down:
        return jnp.where(rows == 0, 0.0, pltpu.roll(u, 1, 0))
    return jnp.where(rows == s - 1, 0.0, pltpu.roll(u, s - 1, 0))


def _conv_fwd(z, w, cols):
    s, d = z.shape[0], cols.d
    cw = CONV_COLS

    def body(cb_ref, cc_ref, ci_ref, w_ref, a_ref):
        u = cc_ref[...] * ci_ref[...]
        wv = w_ref[...]
        conv = wv[0:1] * _shift_rows(u, True) + wv[1:2] * u + wv[2:3] * _shift_rows(u, False)
        a_ref[...] = (cb_ref[...] * conv).astype(a_ref.dtype)

    def zspec(off):
        return pl.BlockSpec((s, cw), lambda j: (0, off // cw + j))

    return pl.pallas_call(body, name="conv_fwd", grid=(d // cw,),
                          in_specs=[zspec(cols.cb), zspec(cols.cc), zspec(cols.ci), pl.BlockSpec((3, cw), lambda j: (0, j))],
                          out_specs=pl.BlockSpec((s, cw), lambda j: (0, j)), out_shape=_sds((s, d), BF16),
                          compiler_params=_params(("parallel",)))(z, z, z, w)


def _conv_bwd(z, w, da, cols):
    s, d = z.shape[0], cols.d
    cw = CONV_COLS

    def body(cb_ref, cc_ref, ci_ref, w_ref, da_ref, dcb_ref, dcc_ref, dci_ref, dw_ref):
        cb, cc, ci, dav = cb_ref[...], cc_ref[...], ci_ref[...], da_ref[...]
        wv = w_ref[...]
        u = cc * ci
        um, up = _shift_rows(u, True), _shift_rows(u, False)
        conv = wv[0:1] * um + wv[1:2] * u + wv[2:3] * up
        dcb_ref[...] = (dav * conv).astype(dcb_ref.dtype)
        dconv = dav * cb
        dw_ref[0:1, :] = jnp.sum(dconv * um, axis=0, keepdims=True)
        dw_ref[1:2, :] = jnp.sum(dconv * u, axis=0, keepdims=True)
        dw_ref[2:3, :] = jnp.sum(dconv * up, axis=0, keepdims=True)
        du = wv[0:1] * _shift_rows(dconv, False) + wv[1:2] * dconv + wv[2:3] * _shift_rows(dconv, True)
        dcc_ref[...] = (du * ci).astype(dcc_ref.dtype)
        dci_ref[...] = (du * cc).astype(dci_ref.dtype)

    def zspec(off):
        return pl.BlockSpec((s, cw), lambda j: (0, off // cw + j))

    col = pl.BlockSpec((s, cw), lambda j: (0, j))
    wspec = pl.BlockSpec((3, cw), lambda j: (0, j))
    return pl.pallas_call(body, name="conv_bwd", grid=(d // cw,),
                          in_specs=[zspec(cols.cb), zspec(cols.cc), zspec(cols.ci), wspec, col],
                          out_specs=[col, col, col, wspec],
                          out_shape=[_sds((s, d), BF16)] * 3 + [_sds((3, d), F32)],
                          compiler_params=_params(("parallel",)))(z, z, z, w, da)


def _gate_fwd(z, bias, y_a, y_b, cols):
    s, d = y_a.shape
    ts, cw = _tile(s, ROW_TILE, 8), cols.kv
    nj = d // cw

    def body(ga_ref, gb_ref, ba_ref, bb_ref, ya_ref, yb_ref, o_ref):
        gate_a = jax.nn.sigmoid(ga_ref[...] + ba_ref[...])
        gate_b = jax.nn.sigmoid(gb_ref[...] + bb_ref[...])
        o_ref[...] = (gate_a * ya_ref[...] + gate_b * yb_ref[...]).astype(o_ref.dtype)

    tile = pl.BlockSpec((ts, cw), lambda i, j: (i, j))
    return pl.pallas_call(
        body, name="gate_fwd", grid=(s // ts, nj),
        in_specs=[pl.BlockSpec((ts, cw), lambda i, j: (i, cols.ga // cw + j)), pl.BlockSpec((ts, cw), lambda i, j: (i, cols.gb // cw + j)),
                  pl.BlockSpec((1, cw), lambda i, j: (0, j)), pl.BlockSpec((1, cw), lambda i, j: (0, nj + j)), tile, tile],
        out_specs=tile, out_shape=_sds((s, d), BF16), compiler_params=_params(("parallel", "parallel")))(z, z, bias, bias, y_a, y_b)


def _gate_bwd(z, bias, y_a, y_b, dmix, cols):
    s, d = y_a.shape
    ts, cw = _tile(s, ROW_TILE, 8), cols.kv
    nj = d // cw

    def body(ga_ref, gb_ref, ba_ref, bb_ref, ya_ref, yb_ref, dm_ref, dya_ref, dyb_ref, dga_ref, dgb_ref, dba_ref, dbb_ref):
        i = pl.program_id(1)
        gate_a = jax.nn.sigmoid(ga_ref[...] + ba_ref[...])
        gate_b = jax.nn.sigmoid(gb_ref[...] + bb_ref[...])
        dm = dm_ref[...]
        dya_ref[...] = (dm * gate_a).astype(dya_ref.dtype)
        dyb_ref[...] = (dm * gate_b).astype(dyb_ref.dtype)
        dga = dm * ya_ref[...] * (gate_a * (1.0 - gate_a))
        dgb = dm * yb_ref[...] * (gate_b * (1.0 - gate_b))
        dga_ref[...] = dga.astype(dga_ref.dtype)
        dgb_ref[...] = dgb.astype(dgb_ref.dtype)
        sa = jnp.sum(dga, axis=0, keepdims=True)
        sb = jnp.sum(dgb, axis=0, keepdims=True)

        @pl.when(i == 0)
        def _():
            dba_ref[...] = sa
            dbb_ref[...] = sb

        @pl.when(i > 0)
        def _():
            dba_ref[...] += sa
            dbb_ref[...] += sb

    tile = pl.BlockSpec((ts, cw), lambda j, i: (i, j))
    vec = pl.BlockSpec((1, cw), lambda j, i: (0, j))
    return pl.pallas_call(
        body, name="gate_bwd", grid=(nj, s // ts),
        in_specs=[pl.BlockSpec((ts, cw), lambda j, i: (i, cols.ga // cw + j)), pl.BlockSpec((ts, cw), lambda j, i: (i, cols.gb // cw + j)),
                  vec, pl.BlockSpec((1, cw), lambda j, i: (0, nj + j)), tile, tile, tile],
        out_specs=[tile, tile, tile, tile, vec, vec],
        out_shape=[_sds((s, d), BF16)] * 4 + [_sds((1, d), F32)] * 2,
        compiler_params=_params(("parallel", "arbitrary")))(z, z, bias, bias, y_a, y_b, dmix)


def _rope_tables(s):
    axis_dim = HEAD_DIM // 2
    n_freq = axis_dim // 2
    rows = s // GRID_W
    row_idx = jnp.repeat(jnp.arange(rows, dtype=jnp.int32), GRID_W)
    col_idx = jnp.tile(jnp.arange(GRID_W, dtype=jnp.int32), rows)
    inv_freq = ROPE_THETA ** (-jnp.arange(0, axis_dim, 2, dtype=F32) / axis_dim)
    ang = jnp.stack([row_idx.astype(F32)[:, None] * inv_freq, col_idx.astype(F32)[:, None] * inv_freq], axis=1)
    cos, sin = jnp.cos(ang), jnp.sin(ang)
    cos_t = jnp.stack([cos, cos], axis=2).reshape(s, HEAD_DIM)
    sin_t = jnp.stack([-sin, sin], axis=2).reshape(s, HEAD_DIM)
    return cos_t, sin_t


def _partner(x):
    n = x.shape[-1]
    lane = lax.broadcasted_iota(jnp.int32, x.shape, x.ndim - 1)
    quarter = HEAD_DIM // 4
    return jnp.where(lane % (2 * quarter) < quarter, pltpu.roll(x, n - quarter, x.ndim - 1), pltpu.roll(x, quarter, x.ndim - 1))


LOG2E = math.log2(math.e)
Q_SCALE = LOG2E / math.sqrt(HEAD_DIM)


def _rope_fwd(z, qn, kn, cos_t, sin_t, cols):
    s, d, kv = z.shape[0], cols.d, cols.kv
    ts = _tile(s, ROW_TILE, 8)
    scale = Q_SCALE

    def body(q_ref, k_ref, v_ref, qn_ref, kn_ref, c_ref, s_ref, qo_ref, ko_ref, vo_ref):
        c, sn = c_ref[...], s_ref[...]

        def head(xh, g):
            xn = xh * _rstd(xh) * g
            return xn * c + _partner(xn) * sn

        for h in range(d // HEAD_DIM):
            sl = slice(h * HEAD_DIM, (h + 1) * HEAD_DIM)
            qo_ref[:, sl] = (head(q_ref[:, sl], qn_ref[...]) * scale).astype(qo_ref.dtype)
        for h in range(kv // HEAD_DIM):
            sl = slice(h * HEAD_DIM, (h + 1) * HEAD_DIM)
            ko_ref[:, sl] = head(k_ref[:, sl], kn_ref[...]).astype(ko_ref.dtype)
        vo_ref[...] = v_ref[...].astype(vo_ref.dtype)

    vec = pl.BlockSpec((1, HEAD_DIM), lambda i: (0, 0))
    tab = pl.BlockSpec((ts, HEAD_DIM), lambda i: (i, 0))
    return pl.pallas_call(
        body, name="rope_fwd", grid=(s // ts,),
        in_specs=[pl.BlockSpec((ts, d), lambda i: (i, cols.q // d)), pl.BlockSpec((ts, kv), lambda i: (i, cols.k // kv)),
                  pl.BlockSpec((ts, kv), lambda i: (i, cols.v // kv)), vec, vec, tab, tab],
        out_specs=[pl.BlockSpec((ts, d), lambda i: (i, 0)), pl.BlockSpec((ts, kv), lambda i: (i, 0)), pl.BlockSpec((ts, kv), lambda i: (i, 0))],
        out_shape=[_sds((s, d), BF16), _sds((s, kv), BF16), _sds((s, kv), BF16)],
        compiler_params=_params(("parallel",)))(z, z, z, qn, kn, cos_t, sin_t)


def _rope_bwd(z, qn, kn, cos_t, sin_t, dq, dk_t, dv_t, cols):
    s, d, kv = z.shape[0], cols.d, cols.kv
    ts = _tile(s, ROW_TILE, 8)
    scale = 1.0 / math.sqrt(HEAD_DIM)

    def body(q_ref, k_ref, qn_ref, kn_ref, c_ref, s_ref, dq_ref, dkt_ref, dvt_ref, dzq_ref, dzk_ref, dzv_ref, dqn_ref, dkn_ref):
        i = pl.program_id(0)
        c, sn = c_ref[...], s_ref[...]
        dk_all = dkt_ref[...].T * (1.0 / LOG2E)

        def head_bwd(xh, g, drot):
            dxn = drot * c + _partner(drot * sn)
            r = _rstd(xh)
            xhat = xh * r
            dgain = jnp.sum(dxn * xhat, axis=0, keepdims=True)
            dxh = dxn * g
            return r * (dxh - xhat * jnp.mean(dxh * xhat, axis=-1, keepdims=True)), dgain

        dqn = jnp.zeros((1, HEAD_DIM), F32)
        for h in range(d // HEAD_DIM):
            sl = slice(h * HEAD_DIM, (h + 1) * HEAD_DIM)
            dx, dg = head_bwd(q_ref[:, sl], qn_ref[...], dq_ref[:, sl] * scale)
            dzq_ref[:, sl] = dx.astype(dzq_ref.dtype)
            dqn = dqn + dg
        dkn = jnp.zeros((1, HEAD_DIM), F32)
        for h in range(kv // HEAD_DIM):
            sl = slice(h * HEAD_DIM, (h + 1) * HEAD_DIM)
            dx, dg = head_bwd(k_ref[:, sl], kn_ref[...], dk_all[:, sl])
            dzk_ref[:, sl] = dx.astype(dzk_ref.dtype)
            dkn = dkn + dg
        dzv_ref[...] = dvt_ref[...].T.astype(dzv_ref.dtype)

        @pl.when(i == 0)
        def _():
            dqn_ref[...] = dqn
            dkn_ref[...] = dkn

        @pl.when(i > 0)
        def _():
            dqn_ref[...] += dqn
            dkn_ref[...] += dkn

    vec = pl.BlockSpec((1, HEAD_DIM), lambda i: (0, 0))
    tab = pl.BlockSpec((ts, HEAD_DIM), lambda i: (i, 0))
    qrow = pl.BlockSpec((ts, d), lambda i: (i, 0))
    krow = pl.BlockSpec((ts, kv), lambda i: (i, 0))
    kcol = pl.BlockSpec((kv, ts), lambda i: (0, i))
    return pl.pallas_call(
        body, name="rope_bwd", grid=(s // ts,),
        in_specs=[pl.BlockSpec((ts, d), lambda i: (i, cols.q // d)), pl.BlockSpec((ts, kv), lambda i: (i, cols.k // kv)),
                  vec, vec, tab, tab, qrow, kcol, kcol],
        out_specs=[qrow, krow, krow, vec, vec],
        out_shape=[_sds((s, d), BF16), _sds((s, kv), BF16), _sds((s, kv), BF16), _sds((1, HEAD_DIM), F32), _sds((1, HEAD_DIM), F32)],
        compiler_params=_params(("arbitrary",)))(z, z, qn, kn, cos_t, sin_t, dq, dk_t, dv_t)


Q_TILE = 256
_NT = (((1,), (1,)), ((), ()))
_NN = (((1,), (0,)), ((), ()))


def _attn_fwd(q, k, v):
    s, d = q.shape
    kvh = k.shape[1] // HEAD_DIM
    tq = _tile(s, Q_TILE, LANES)
    gw = GROUP * HEAD_DIM

    def body(q_ref, k_ref, v_ref, o_ref, lse_ref):
        kk, vv = k_ref[...], v_ref[...]
        for g in range(GROUP):
            sl = slice(g * HEAD_DIM, (g + 1) * HEAD_DIM)
            sc = lax.dot_general(q_ref[:, sl], kk, _NT, preferred_element_type=F32)
            mx = jnp.max(sc, axis=-1, keepdims=True)
            p = jnp.exp2(sc - mx)
            l = jnp.sum(p, axis=-1, keepdims=True)
            o = lax.dot_general(p.astype(BF16), vv, _NN, preferred_element_type=F32) * (1.0 / l)
            o_ref[:, sl] = o.astype(o_ref.dtype)
            lse_ref[:, g:g + 1] = mx + jnp.log(l) * LOG2E

    return pl.pallas_call(
        body, name="attn_fwd", grid=(kvh, s // tq),
        in_specs=[pl.BlockSpec((tq, gw), lambda j, i: (i, j)), pl.BlockSpec((s, HEAD_DIM), lambda j, i: (0, j)), pl.BlockSpec((s, HEAD_DIM), lambda j, i: (0, j))],
        out_specs=[pl.BlockSpec((tq, gw), lambda j, i: (i, j)), pl.BlockSpec((None, tq, GROUP), lambda j, i: (j, i, 0))],
        out_shape=[_sds((s, d), BF16), _sds((kvh, s, GROUP), F32)],
        compiler_params=_params(("parallel", "parallel")))(q, k, v)


_TN = (((0,), (0,)), ((), ()))


def _attn_bwd(q, k, v, do, lse):
    s, d = q.shape
    kv = k.shape[1]
    kvh = kv // HEAD_DIM
    tq = _tile(s, Q_TILE, LANES)
    gw = GROUP * HEAD_DIM

    def body(q_ref, k_ref, v_ref, do_ref, lse_ref, dq_ref, dkt_ref, dvt_ref):
        i = pl.program_id(1)
        kk, vv = k_ref[...], v_ref[...]

        @pl.when(i == 0)
        def _():
            dkt_ref[...] = jnp.zeros_like(dkt_ref)
            dvt_ref[...] = jnp.zeros_like(dvt_ref)

        for g in range(GROUP):
            sl = slice(g * HEAD_DIM, (g + 1) * HEAD_DIM)
            qg, dog = q_ref[:, sl], do_ref[:, sl]
            sc = lax.dot_general(qg, kk, _NT, preferred_element_type=F32)
            p = jnp.exp2(sc - lse_ref[:, g:g + 1])
            dp = lax.dot_general(dog, vv, _NT, preferred_element_type=F32)
            delta = jnp.sum(p * dp, axis=-1, keepdims=True)
            ds = (p * (dp - delta)).astype(BF16)
            dq_ref[:, sl] = lax.dot_general(ds, kk, _NN, preferred_element_type=F32)
            dkt_ref[...] += lax.dot_general(qg, ds, _TN, preferred_element_type=F32)
            dvt_ref[...] += lax.dot_general(dog, p.astype(BF16), _TN, preferred_element_type=F32)

    qspec = pl.BlockSpec((tq, gw), lambda j, i: (i, j))
    kspec = pl.BlockSpec((s, HEAD_DIM), lambda j, i: (0, j))
    stat = pl.BlockSpec((None, tq, GROUP), lambda j, i: (j, i, 0))
    tspec = pl.BlockSpec((HEAD_DIM, s), lambda j, i: (j, 0))
    return pl.pallas_call(
        body, name="attn_bwd", grid=(kvh, s // tq),
        in_specs=[qspec, kspec, kspec, qspec, stat], out_specs=[qspec, tspec, tspec],
        out_shape=[_sds((s, d), F32), _sds((kv, s), F32), _sds((kv, s), F32)],
        compiler_params=_params(("parallel", "arbitrary")))(q, k, v, do, lse)


ELEM_BLOCK_BYTES = 1 << 20

BIG = (("w_in", True), ("w_out_conv", False), ("w_out_attn", False), ("w_merge", False), ("w_up", True), ("w_down", False))
N_BIG = len(BIG)


def _elem_tiles(rows, width):
    tc = _tile(width, 2048, LANES)
    tr = _tile(rows, max(8, ELEM_BLOCK_BYTES // (4 * tc)), 8)
    return tr, tc


def _scalar_grid(grid, in_specs, out_specs):
    return pltpu.PrefetchScalarGridSpec(num_scalar_prefetch=1, grid=grid, in_specs=in_specs, out_specs=out_specs)


def _cast_place(w_stack, layer, chip, col_sharded, name):
    _, rows, width = w_stack.shape
    tr, tc = _elem_tiles(rows, width)
    nr, nc = rows // tr, width // tc

    def body(sc_ref, x_ref, o_ref):
        o_ref[...] = x_ref[...].astype(o_ref.dtype)

    if col_sharded:
        full, out_spec = (rows, width * N_CHIPS), pl.BlockSpec((tr, tc), lambda i, j, sc: (i, sc[0] * nc + j))
    else:
        full, out_spec = (rows * N_CHIPS, width), pl.BlockSpec((tr, tc), lambda i, j, sc: (sc[0] * nr + i, j))
    return pl.pallas_call(
        body, name=name,
        grid_spec=_scalar_grid((nr, nc), [pl.BlockSpec((None, tr, tc), lambda i, j, sc: (layer, i, j))], out_spec),
        out_shape=_sds(full, BF16), compiler_params=_params(("parallel", "parallel")))(chip, w_stack)


def _pair_add(g, landed, core, col_sharded, name):
    rows, width = landed.shape
    tr, tc = _elem_tiles(rows, width)
    nr, nc = rows // tr, width // tc

    def body(c_ref, g_ref, l_ref, o_ref):
        o_ref[...] = (g_ref[...] + l_ref[...]).astype(o_ref.dtype)

    if col_sharded:
        g_spec = pl.BlockSpec((tr, tc), lambda i, j, c: (c[0] * nr + i, j))
    else:
        g_spec = pl.BlockSpec((tr, tc), lambda i, j, c: (i, c[0] * nc + j))
    tile = pl.BlockSpec((tr, tc), lambda i, j, c: (i, j))
    return pl.pallas_call(
        body, name=name, grid_spec=_scalar_grid((nr, nc), [g_spec, tile], tile),
        out_shape=_sds((rows, width), BF16), compiler_params=_params(("parallel", "parallel")))(core, g, landed)


def _slot_of_relation(rel):
    return jnp.where(rel == 2, 0, jnp.where(rel == 1, 1, 2))


def _sum_chips(pair_sum, landed, chip, col_sharded, name):
    _, rows, width = landed.shape
    tr, tc = _elem_tiles(rows, width)
    nr, nc = rows // tr, width // tc

    def body(chip_ref, own_ref, q_ref, o_ref):
        me = chip_ref[0]
        own = own_ref[...].astype(F32)
        acc = None
        for t in range(N_CHIPS):
            rel = me ^ t
            term = jnp.where(rel == 0, own, q_ref[_slot_of_relation(rel)].astype(F32))
            acc = term if acc is None else acc + term
        o_ref[...] = acc

    if col_sharded:
        own_spec = pl.BlockSpec((tr, tc), lambda i, j, c: (i, c[0] * nc + j))
    else:
        own_spec = pl.BlockSpec((tr, tc), lambda i, j, c: (c[0] * nr + i, j))
    return pl.pallas_call(
        body, name=name,
        grid_spec=_scalar_grid((nr, nc), [own_spec, pl.BlockSpec((N_CHIPS - 1, tr, tc), lambda i, j, c: (0, i, j))],
                               pl.BlockSpec((tr, tc), lambda i, j, c: (i, j))),
        out_shape=_sds((rows, width), F32), compiler_params=_params(("parallel", "parallel")))(chip, pair_sum, landed)


def _adamw_math(w, g, m, v):
    mn = ADAM_B1 * m + (1.0 - ADAM_B1) * g
    vn = ADAM_B2 * v + (1.0 - ADAM_B2) * jnp.square(g)
    m_hat = mn / (1.0 - ADAM_B1 ** ADAM_STEP)
    v_hat = vn / (1.0 - ADAM_B2 ** ADAM_STEP)
    return -ADAM_LR * (m_hat / (jnp.sqrt(v_hat) + ADAM_EPS) + ADAM_WD * w), mn, vn


def _adamw(w, g, m, v, name):
    shape = w.shape
    width = shape[-1]
    w2, g2, m2, v2 = (a.reshape(-1, width) for a in (w, g, m, v))
    rows = w2.shape[0]
    tr, tc = _elem_tiles(rows, width)

    def body(w_ref, g_ref, m_ref, v_ref, d_ref, nm_ref, nv_ref):
        d_ref[...], nm_ref[...], nv_ref[...] = _adamw_math(w_ref[...], g_ref[...], m_ref[...], v_ref[...])

    tile = pl.BlockSpec((tr, tc), lambda i, j: (i, j))
    outs = pl.pallas_call(body, name=name, grid=(rows // tr, width // tc), in_specs=[tile] * 4, out_specs=[tile] * 3,
                          out_shape=[_sds((rows, width), F32)] * 3, compiler_params=_params(("parallel", "parallel")))(w2, g2, m2, v2)
    return tuple(o.reshape(shape) for o in outs)


def _adamw_layer(w, m, v, g_mine, g_sibling, carried, layer, core, col_sharded, name):
    depth, rows, width = w.shape
    pr, pc = g_mine.shape
    tr, tc = _elem_tiles(pr, pc)
    n_half = pr // tr if col_sharded else pc // tc
    if carried is None:
        carried = tuple(lax.empty((depth, rows, width), F32) for _ in range(4))

    def body(sc_ref, w_ref, m_ref, v_ref, gm_ref, gs_ref, *rest):
        g_ref, d_ref, nm_ref, nv_ref = rest[-4:]
        pos = pl.program_id(0) if col_sharded else pl.program_id(1)
        gv = jnp.where(pos // n_half == sc_ref[0], gm_ref[...], gs_ref[...])
        g_ref[...] = gv
        d_ref[...], nm_ref[...], nv_ref[...] = _adamw_math(w_ref[...], gv, m_ref[...], v_ref[...])

    stacked = pl.BlockSpec((None, tr, tc), lambda i, j, sc: (layer, i, j))
    if col_sharded:
        half = pl.BlockSpec((tr, tc), lambda i, j, sc: (i % n_half, j))
    else:
        half = pl.BlockSpec((tr, tc), lambda i, j, sc: (i, j % n_half))
    return pl.pallas_call(
        body, name=name,
        grid_spec=_scalar_grid((rows // tr, width // tc), [stacked] * 3 + [half] * 2 + [ANY] * 4, [stacked] * 4),
        out_shape=[_sds((depth, rows, width), F32)] * 4, input_output_aliases={6: 0, 7: 1, 8: 2, 9: 3},
        compiler_params=_params(("parallel", "parallel")))(core, w, m, v, g_mine, g_sibling, *carried)


_SEM = pl.BlockSpec(memory_space=pltpu.SEMAPHORE)
_HBM = pl.BlockSpec(memory_space=pltpu.HBM)
_VMEM = pl.BlockSpec(memory_space=pltpu.VMEM)
_EFFECT = pltpu.SideEffectType.DATAFLOW_SIDE_EFFECTING


def _place():
    x, y, c = lax.axis_index("x"), lax.axis_index("y"), lax.axis_index("c")
    others = [(1 - x, y), (x, 1 - y), (1 - x, 1 - y)]
    return x, y, c, others


def _chip_index(px, py):
    return 2 * px + py


def _remote(src, dst, send_sems, recv_sems, k, to):
    return pltpu.make_async_remote_copy(src_ref=src, dst_ref=dst, send_sem=send_sems.at[k], recv_sem=recv_sems.at[k],
                                        device_id=to, device_id_type=MESH)


def _phase(name, bufs, waits, wait_fn, n_start, start_fn, deps):
    nb, nd = len(bufs), len(deps)

    def body(*refs):
        buf_refs = refs[:nb]
        pos = nb
        if waits is not None:
            wait_fn(buf_refs, refs[pos], refs[pos + 1])
            pos += 2
        pos += nd
        if n_start:
            start_fn(buf_refs, refs[pos], refs[pos + 1])
            pos += 2
        token = refs[pos + nb]
        token[...] = jnp.zeros_like(token)

    n_sem_out = 2 if n_start else 0
    if waits is None:
        bufs = [pltpu.with_memory_space_constraint(b, pltpu.HBM) for b in bufs]
    outs = pl.pallas_call(
        body, name=name,
        in_specs=[_HBM] * nb + ([_SEM] * 2 if waits is not None else []) + [ANY] * nd,
        out_specs=[_SEM] * n_sem_out + [_HBM] * nb + [_VMEM],
        out_shape=[pltpu.SemaphoreType.DMA((n_start,))] * n_sem_out + [pltpu.HBM(b.shape, b.dtype) for b in bufs] + [_sds((8, LANES), F32)],
        input_output_aliases={i: n_sem_out + i for i in range(nb)},
        compiler_params=pltpu.CompilerParams(has_side_effects=_EFFECT),
    )(*bufs, *(waits if waits is not None else ()), *deps)
    sems = tuple(outs[:2]) if n_start else None
    return sems, list(outs[n_sem_out:n_sem_out + nb]), outs[-1]


class _Chains:
    def __init__(self):
        self.active = []
        self.last = None
        self.dep = None
        self.pinned = []
        self.token = None
        self.at = None

    def phase(self, name, bufs, waits, wait_fn, n_start, start_fn):
        deps = [a for a in (self.last, self.dep) if a is not None] + self.pinned
        sems, thru, token = _phase(name, bufs, waits, wait_fn, n_start, start_fn, deps)
        self.last, self.dep, self.token, self.pinned = token, None, token, []
        return sems, thru

    def pin(self, result):
        self.pinned.append(result)

    def add(self, gen):
        self.active.append(gen)

    def tick(self, dep, at=None):
        self.at, self.token, self.dep = at, None, dep
        for gen in list(self.active):
            if next(gen, "done") == "done":
                self.active.remove(gen)
        return self.token


def _after(small, token):
    return small if token is None else small + token[0, 0]


def _shard_region(ref, col_sharded, chip, n_shard):
    start = pl.multiple_of(chip * n_shard, LANES if col_sharded else 8)
    if col_sharded:
        return ref.at[:, pl.ds(start, n_shard)]
    return ref.at[pl.ds(start, n_shard), :]


def _row_half(ref, half):
    n_rows = ref.shape[0]
    return ref.at[pl.ds(pl.multiple_of(half * (n_rows // 2), 8), n_rows // 2), :]


def _half_of(ref, col_sharded, half):
    rows, width = ref.shape
    if col_sharded:
        return ref.at[pl.ds(pl.multiple_of(half * (rows // 2), 8), rows // 2), :]
    return ref.at[:, pl.ds(pl.multiple_of(half * (width // 2), LANES), width // 2)]


def _gather_chain(chains, layer, bufs, out, wait_for):
    n = 3 * N_BIG

    def region(refs, a, chip, half):
        col = BIG[a][1]
        n_shard = refs[a].shape[1] // N_CHIPS if col else refs[a].shape[0] // N_CHIPS
        return _row_half(_shard_region(refs[a], col, chip, n_shard), half)

    def start_ici(refs, send, recv):
        x, y, c, others = _place()
        for a in range(N_BIG):
            mine = region(refs, a, _chip_index(x, y), c)
            for j, (ox, oy) in enumerate(others):
                _remote(mine, mine, send, recv, 3 * a + j, (ox, oy, c)).start()

    def wait_ici(refs, send, recv):
        x, y, c, others = _place()
        for a in range(N_BIG):
            for j, (ox, oy) in enumerate(others):
                landed = region(refs, a, _chip_index(ox, oy), c)
                cp = _remote(landed, landed, send, recv, 3 * a + j, (x, y, 1 - c))
                cp.wait_recv()
                cp.wait_send()

    def start_d2d(refs, send, recv):
        x, y, c, others = _place()
        for a in range(N_BIG):
            for j, (ox, oy) in enumerate(others):
                landed = region(refs, a, _chip_index(ox, oy), c)
                _remote(landed, landed, send, recv, 3 * a + j, (x, y, 1 - c)).start()

    def wait_d2d(refs, send, recv):
        x, y, c, others = _place()
        for a in range(N_BIG):
            for j, (ox, oy) in enumerate(others):
                theirs = region(refs, a, _chip_index(ox, oy), 1 - c)
                cp = _remote(theirs, theirs, send, recv, 3 * a + j, (x, y, 1 - c))
                cp.wait_recv()
                cp.wait_send()

    sems, bufs = chains.phase(f"gather_ici_start_{layer}", bufs, None, None, n, start_ici)
    yield
    while wait_for is not None and chains.at != wait_for:
        yield
    sems, bufs = chains.phase(f"gather_forward_{layer}", bufs, sems, wait_ici, n, start_d2d)
    yield
    _, bufs = chains.phase(f"gather_done_{layer}", bufs, sems, wait_d2d, 0, None)
    out[layer] = bufs


def _grad_chain(chains, layer, group, grads, core, chip, w, mom, var, carried):
    n = len(group)
    tag = f"{layer}_{group[0]}"
    kinds = [BIG[a][1] for a in group]
    names = [BIG[a][0] for a in group]
    sibling_of = lambda x, y, c: (x, y, 1 - c)

    def pair_start(refs, send, recv):
        x, y, c, _ = _place()
        for i in range(n):
            _remote(_half_of(refs[i], kinds[i], 1 - c), refs[n + i], send, recv, i, sibling_of(x, y, c)).start()

    def pair_wait(refs, send, recv):
        x, y, c, _ = _place()
        for i in range(n):
            cp = _remote(_half_of(refs[i], kinds[i], 1 - c), refs[n + i], send, recv, i, sibling_of(x, y, c))
            cp.wait_recv()
            cp.wait_send()

    def piece(ref, col, chip_idx):
        return _shard_region(ref, col, chip_idx, ref.shape[1] // N_CHIPS if col else ref.shape[0] // N_CHIPS)

    def scatter_start(refs, send, recv):
        x, y, c, others = _place()
        for i in range(n):
            for j, (ox, oy) in enumerate(others):
                _remote(piece(refs[i], kinds[i], _chip_index(ox, oy)), refs[n + i].at[j], send, recv, 3 * i + j, (ox, oy, c)).start()

    def scatter_wait(refs, send, recv):
        x, y, c, others = _place()
        for i in range(n):
            for j, (ox, oy) in enumerate(others):
                cp = _remote(piece(refs[i], kinds[i], _chip_index(ox, oy)), refs[n + i].at[j], send, recv, 3 * i + j, (ox, oy, c))
                cp.wait_recv()
                cp.wait_send()

    def join_start(refs, send, recv):
        x, y, c, _ = _place()
        for i in range(n):
            _remote(refs[i], refs[n + i], send, recv, i, sibling_of(x, y, c)).start()

    def join_wait(refs, send, recv):
        x, y, c, _ = _place()
        for i in range(n):
            cp = _remote(refs[i], refs[n + i], send, recv, i, sibling_of(x, y, c))
            cp.wait_recv()
            cp.wait_send()

    half_shapes = [(g.shape[0] // 2, g.shape[1]) if col else (g.shape[0], g.shape[1] // 2) for col, g in zip(kinds, grads)]
    lands = [lax.empty(s, F32) for s in half_shapes]
    sems, bufs = chains.phase(f"pair_start_{tag}", list(grads) + lands, None, None, n, pair_start)
    yield
    _, bufs = chains.phase(f"pair_wait_{tag}", bufs, sems, pair_wait, 0, None)
    pair_sums = [_pair_add(bufs[i], bufs[n + i], core, kinds[i], "pair_add_" + names[i]) for i in range(n)]
    piece_shapes = [(s[0], s[1] // N_CHIPS) if col else (s[0] // N_CHIPS, s[1]) for col, s in zip(kinds, half_shapes)]
    slots = [lax.empty((N_CHIPS - 1, *s), BF16) for s in piece_shapes]
    sems, bufs = chains.phase(f"scatter_start_{tag}", pair_sums + slots, None, None, 3 * n, scatter_start)
    yield
    _, bufs = chains.phase(f"scatter_wait_{tag}", bufs, sems, scatter_wait, 0, None)
    reduced = [_sum_chips(bufs[i], bufs[n + i], chip, kinds[i], "sum_chips_" + names[i]) for i in range(n)]
    theirs = [lax.empty(s, F32) for s in piece_shapes]
    sems, bufs = chains.phase(f"join_start_{tag}", reduced + theirs, None, None, n, join_start)
    yield
    _, bufs = chains.phase(f"join_wait_{tag}", bufs, sems, join_wait, 0, None)
    for i in range(n):
        carried[names[i]] = _adamw_layer(w[names[i]], mom[names[i]], var[names[i]], bufs[i], bufs[n + i], carried.get(names[i]),
                                         layer, core, kinds[i], "adamw_" + names[i])
        chains.pin(carried[names[i]][0])


def _allreduce_small(vec, name):
    rows = vec.shape[0]
    masks = [(dx, dy, dc) for dx in (0, 1) for dy in (0, 1) for dc in (0, 1)][1:]

    def body(v_ref, o_ref, gather_ref, send_sems, recv_sems):
        x, y, c, _ = _place()
        me = 4 * x + 2 * y + c
        gather_ref[me] = v_ref[...]
        copies = []
        for k, (dx, dy, dc) in enumerate(masks):
            peer = (x ^ dx, y ^ dy, c ^ dc)
            copies.append(_remote(v_ref, gather_ref.at[me], send_sems, recv_sems, k, peer))
        for cp in copies:
            cp.start()
        for k, (dx, dy, dc) in enumerate(masks):
            slot = gather_ref.at[4 * (x ^ dx) + 2 * (y ^ dy) + (c ^ dc)]
            _remote(slot, slot, send_sems, recv_sems, k, (x, y, c)).wait_recv()
        for cp in copies:
            cp.wait_send()
        acc = gather_ref[0]
        for dev in range(1, N_DEV):
            acc = acc + gather_ref[dev]
        o_ref[...] = acc

    return pl.pallas_call(
        body, name=name, in_specs=[_VMEM], out_specs=_VMEM, out_shape=_sds((rows, LANES), F32),
        scratch_shapes=[pltpu.VMEM((N_DEV, rows, LANES), F32), pltpu.SemaphoreType.DMA((N_DEV - 1,)), pltpu.SemaphoreType.DMA((N_DEV - 1,))],
        compiler_params=pltpu.CompilerParams(has_side_effects=True, vmem_limit_bytes=VMEM_LIMIT_BYTES),
    )(vec)


def _relu2(acc):
    r = jnp.maximum(acc, 0.0)
    return acc, r * r


def _relu2_bwd(acc, up):
    return (acc * (2.0 * jnp.maximum(up, 0.0)),)


def _layer_fwd(x, h, p, cols, tables, tick, last, target=None):
    cos_t, sin_t = tables
    w = p()
    z = _mm(h, w["w_in"], "nn", [F32], name="mm_in")
    a = _conv_fwd(z, w["conv_w"], cols)
    q, k, v = _rope_fwd(z, w["q_norm"], w["k_norm"], cos_t, sin_t, cols)
    o, lse = _attn_fwd(q, k, v)
    y_a = _mm(a, w["w_out_conv"], "nn", [F32], name="mm_out_conv")
    y_b = _mm(o, w["w_out_attn"], "nn", [F32], name="mm_out_attn")
    mix = _gate_fwd(z, w["gate_bias"], y_a, y_b, cols)
    mixed = _mm(mix, w["w_merge"], "nn", [F32], name="mm_merge")
    x1, h2 = _resid_norm(x, mixed, _after(w["norm_mix_post"], tick(mixed, 0)), w["norm_mlp_pre"])
    up, act = _mm(h2, w["w_up"], "nn", [F32, BF16], name="mm_up", epilogue=_relu2)
    f = _mm(act, w["w_down"], "nn", [F32], name="mm_down")
    kept = dict(x=x, h=h, z=z, a=a, q=q, k=k, v=v, o=o, lse=lse, y_a=y_a, y_b=y_b, mix=mix, mixed=mixed, x1=x1, h2=h2, up=up, act=act, f=f)
    g_post = _after(w["norm_mlp_post"], tick(f, 1))
    if last:
        return _resid_norm_loss(x1, f, g_post, target), kept
    return _resid_norm(x1, f, g_post, w["norm_next"]), kept


def _layer_bwd(dx_out, w, kept, cols, tables, tick, emit):
    cos_t, sin_t = tables
    t = kept
    df, d_norm_mlp_post = _norm_bwd(t["f"], _after(w["norm_mlp_post"], tick(dx_out, 0)), dx_out, None, BF16, "norm_bwd_mlp_post")
    dup = _mm(df, w["w_down"], "nt", [BF16], name="mm_d_act", epilogue=_relu2_bwd, extras=(t["up"],))
    g_w_down = _mm(t["act"], df, "tn", [F32], name="mm_g_down")
    g_w_up = _mm(t["h2"], dup, "tn", [F32], name="mm_g_up")
    emit((4, 5), [g_w_up, g_w_down])
    dh2 = _mm(dup, w["w_up"], "nt", [F32], name="mm_d_h2")
    dx1, d_norm_mlp_pre = _norm_bwd(t["x1"], _after(w["norm_mlp_pre"], tick(dh2, 1)), dh2, dx_out, F32, "norm_bwd_mlp_pre")
    dmixed, d_norm_mix_post = _norm_bwd(t["mixed"], w["norm_mix_post"], dx1, None, BF16, "norm_bwd_mix_post")
    dmix = _mm(dmixed, w["w_merge"], "nt", [F32], name="mm_d_mix")
    g_w_merge = _mm(t["mix"], dmixed, "tn", [F32], name="mm_g_merge")
    dy_a, dy_b, dz_ga, dz_gb, dbias_a, dbias_b = _gate_bwd(t["z"], _after(w["gate_bias"], tick(g_w_merge, 2)), t["y_a"], t["y_b"], dmix, cols)
    g_w_out_conv = _mm(t["a"], dy_a, "tn", [F32], name="mm_g_out_conv")
    da = _mm(dy_a, w["w_out_conv"], "nt", [F32], name="mm_d_a")
    g_w_out_attn = _mm(t["o"], dy_b, "tn", [F32], name="mm_g_out_attn")
    emit((1, 2, 3), [g_w_out_conv, g_w_out_attn, g_w_merge])
    do = _mm(dy_b, w["w_out_attn"], "nt", [BF16], name="mm_d_o")
    dz_cb, dz_cc, dz_ci, d_conv_w = _conv_bwd(t["z"], _after(w["conv_w"], tick(do, 3)), da, cols)
    dq, dk_t, dv_t = _attn_bwd(t["q"], t["k"], t["v"], do, t["lse"])
    dz_q, dz_k, dz_v, d_q_norm, d_k_norm = _rope_bwd(t["z"], _after(w["q_norm"], tick(dv_t, 4)), w["k_norm"], cos_t, sin_t, dq, dk_t, dv_t, cols)
    dz = jnp.concatenate([dz_cb, dz_cc, dz_ci, dz_q, dz_k, dz_v, dz_ga, dz_gb], axis=1)
    g_w_in = _mm(t["h"], dz, "tn", [F32], name="mm_g_in")
    emit((0,), [g_w_in])
    dh = _mm(dz, w["w_in"], "nt", [F32], name="mm_d_h")
    dx_in, d_norm_mix_pre = _norm_bwd(t["x"], _after(w["norm_mix_pre"], tick(dh, 5)), dh, dx1, F32, "norm_bwd_mix_pre")
    small = dict(norm_mix_pre=d_norm_mix_pre, gate_bias=jnp.concatenate([dbias_a, dbias_b], axis=1), conv_w=d_conv_w, q_norm=d_q_norm,
                 k_norm=d_k_norm, norm_mix_post=d_norm_mix_post, norm_mlp_pre=d_norm_mlp_pre, norm_mlp_post=d_norm_mlp_post)
    return dx_in, small


SMALL = ("norm_mix_pre", "gate_bias", "conv_w", "q_norm", "k_norm", "norm_mix_post", "norm_mlp_pre", "norm_mlp_post")
WEIGHTS = ("norm_mix_pre", "w_in", "gate_bias", "conv_w", "q_norm", "k_norm", "w_out_conv", "w_out_attn", "w_merge",
           "norm_mix_post", "norm_mlp_pre", "w_up", "w_down", "norm_mlp_post")


def _pack(parts):
    flat = jnp.concatenate([a.reshape(-1) for a in parts])
    rows = -(-flat.shape[0] // LANES)
    pad = (-rows) % 8
    flat = jnp.pad(flat, (0, (rows + pad) * LANES - flat.shape[0]))
    return flat.reshape(rows + pad, LANES)


def _unpack(packed, shapes):
    flat = packed.reshape(-1)
    out, off = [], 0
    for shp in shapes:
        n = math.prod(shp)
        out.append(flat[off:off + n].reshape(shp))
        off += n
    return out


def kernel(x, norm_mix_pre, w_in, gate_bias, conv_w, q_norm, k_norm, w_out_conv, w_out_attn, w_merge, norm_mix_post, norm_mlp_pre, w_up, w_down, norm_mlp_post, loss_target, m_norm_mix_pre, m_w_in, m_gate_bias, m_conv_w, m_q_norm, m_k_norm, m_w_out_conv, m_w_out_attn, m_w_merge, m_norm_mix_post, m_norm_mlp_pre, m_w_up, m_w_down, m_norm_mlp_post, v_norm_mix_pre, v_w_in, v_gate_bias, v_conv_w, v_q_norm, v_k_norm, v_w_out_conv, v_w_out_attn, v_w_merge, v_norm_mix_post, v_norm_mlp_pre, v_w_up, v_w_down, v_norm_mlp_post):
    w = dict(norm_mix_pre=norm_mix_pre, w_in=w_in, gate_bias=gate_bias, conv_w=conv_w, q_norm=q_norm, k_norm=k_norm, w_out_conv=w_out_conv,
             w_out_attn=w_out_attn, w_merge=w_merge, norm_mix_post=norm_mix_post, norm_mlp_pre=norm_mlp_pre, w_up=w_up, w_down=w_down,
             norm_mlp_post=norm_mlp_post)
    mom = dict(norm_mix_pre=m_norm_mix_pre, w_in=m_w_in, gate_bias=m_gate_bias, conv_w=m_conv_w, q_norm=m_q_norm, k_norm=m_k_norm,
               w_out_conv=m_w_out_conv, w_out_attn=m_w_out_attn, w_merge=m_w_merge, norm_mix_post=m_norm_mix_post, norm_mlp_pre=m_norm_mlp_pre,
               w_up=m_w_up, w_down=m_w_down, norm_mlp_post=m_norm_mlp_post)
    var = dict(norm_mix_pre=v_norm_mix_pre, w_in=v_w_in, gate_bias=v_gate_bias, conv_w=v_conv_w, q_norm=v_q_norm, k_norm=v_k_norm,
               w_out_conv=v_w_out_conv, w_out_attn=v_w_out_attn, w_merge=v_w_merge, norm_mix_post=v_norm_mix_post, norm_mlp_pre=v_norm_mlp_pre,
               w_up=v_w_up, w_down=v_w_down, norm_mlp_post=v_norm_mlp_post)
    depth = w_in.shape[0]
    _, s, d = x.shape
    cols = _Cols(d)
    x0 = x.reshape(s, d)
    target = loss_target.reshape(s, d)
    tables = _rope_tables(s)
    chip = (2 * lax.axis_index("x") + lax.axis_index("y")).astype(jnp.int32)
    core = lax.axis_index("c").astype(jnp.int32)
    chip_vec, core_vec = chip.reshape(1), core.reshape(1)

    n_conv = conv_w.shape[-1]
    placed = lax.dynamic_update_slice_in_dim(jnp.zeros((depth, conv_w.shape[1], n_conv * N_CHIPS), F32), conv_w, chip * n_conv, axis=2)
    conv_full = _unpack(_allreduce_small(_pack([jnp.where(core == 0, placed, 0.0)]), "gather_conv_w"), [placed.shape])[0]

    chains = _Chains()
    chains.pin(conv_full)
    full = [None] * depth
    gathers = []
    for l in range(depth):
        bufs = [_cast_place(w[name], l, chip_vec, col, "cast_place_" + name) for name, col in BIG]
        gathers.append(_gather_chain(chains, l, bufs, full, ("fwd", l - 1, 0) if l else None))
        next(gathers[l])
    next(gathers[0])
    next(gathers[0], None)
    for g in gathers[1:]:
        chains.add(g)

    def layer_params(l):
        p = {name: f for (name, _), f in zip(BIG, full[l])}
        p["conv_w"] = conv_full[l]
        p["gate_bias"] = gate_bias[l].reshape(1, -1)
        for name in ("norm_mix_pre", "q_norm", "k_norm", "norm_mix_post", "norm_mlp_pre", "norm_mlp_post"):
            p[name] = w[name][l].reshape(1, -1)
        if l + 1 < depth:
            p["norm_next"] = w["norm_mix_pre"][l + 1].reshape(1, -1)
        return p

    kept = []
    xl, h = x0, _norm_first(x0, norm_mix_pre[0].reshape(1, -1))
    for l in range(depth):
        last = l == depth - 1
        (xl, h), t = _layer_fwd(xl, h, functools.partial(layer_params, l), cols, tables,
                                lambda dep, k, l=l: chains.tick(dep, ("fwd", l, k)), last, target if last else None)
        kept.append(t)
    dx, loss_local = xl, h

    carried = {}
    small = [None] * depth
    for l in reversed(range(depth)):
        def emit(group, grads, l=l):
            chains.add(_grad_chain(chains, l, group, grads, core_vec, chip_vec, w, mom, var, carried))

        dx, small[l] = _layer_bwd(dx, layer_params(l), kept[l], cols, tables, lambda dep, k: chains.tick(dep), emit)
    while chains.active:
        chains.tick(None)
    grads = {name: carried[name][0] for name, _ in BIG}
    delta = {name: carried[name][1] for name, _ in BIG}
    new_m = {name: carried[name][2] for name, _ in BIG}
    new_v = {name: carried[name][3] for name, _ in BIG}

    small_full_shapes = [(depth,) + small[0][name].shape for name in SMALL]
    packed = _pack([jnp.stack([small[l][name] for l in range(depth)]) for name in SMALL] + [jnp.broadcast_to(loss_local.reshape(1), (LANES,))])
    small_sum = _unpack(_allreduce_small(packed, "allreduce_small"), small_full_shapes + [(LANES,)])
    loss = small_sum[-1][0]
    for name, g in zip(SMALL, small_sum[:-1]):
        if name == "conv_w":
            g = lax.dynamic_slice_in_dim(g, chip * n_conv, n_conv, axis=2)
        grads[name] = g.reshape(w[name].shape)
    small_shapes = [w[name].shape for name in SMALL]
    packs = [_pack([src[name] for name in SMALL]) for src in (w, grads, mom, var)]
    for dst, out in zip((delta, new_m, new_v), _adamw(*packs, "adamw_small")):
        for name, val in zip(SMALL, _unpack(out, small_shapes)):
            dst[name] = val

    grad_x = dx.reshape(x.shape)
    return (loss, grad_x, *[grads[n] for n in WEIGHTS], *[delta[n] for n in WEIGHTS], *[new_m[n] for n in WEIGHTS], *[new_v[n] for n in WEIGHTS])
```

```python
import functools
import math

import jax
import jax.numpy as jnp
from jax import lax
from jax.experimental import pallas as pl
from jax.experimental.pallas import tpu as pltpu

F32 = jnp.float32
BF16 = jnp.bfloat16

HEAD_DIM = 128
GROUP = 4
GRID_W = 64
ROPE_THETA = 10000.0
RMS_EPS = 1e-6
ADAM_LR = 0.001
ADAM_B1 = 0.9
ADAM_B2 = 0.999
ADAM_EPS = 1e-08
ADAM_WD = 0.01
ADAM_STEP = 10

LANES = 128
N_CHIPS = 4
N_DEV = 8
VMEM_LIMIT_BYTES = 56 * 1024 * 1024
MESH = pl.DeviceIdType.MESH
ANY = pl.BlockSpec(memory_space=pl.ANY)


def _tile(dim, cap, mult):
    if dim <= cap:
        return dim
    t = (cap // mult) * mult
    while t >= mult:
        if dim % t == 0:
            return t
        t -= mult
    raise ValueError(f"no tile for {dim} under {cap} in multiples of {mult}")


def _params(sem=None):
    return pltpu.CompilerParams(dimension_semantics=sem, vmem_limit_bytes=VMEM_LIMIT_BYTES)


def _sds(shape, dtype):
    return jax.ShapeDtypeStruct(tuple(shape), dtype)


def _rstd(x):
    return lax.rsqrt(jnp.mean(x * x, axis=-1, keepdims=True) + RMS_EPS)


_DOT_DIMS = {"nn": ((1,), (0,)), "nt": ((1,), (1,)), "tn": ((0,), (0,))}


def _mm(a, b, mode, out_dtypes, *, name, epilogue=None, extras=(), tm=1024, tn=1024, tk=2048):
    if mode == "nn":
        (m, k), (k2, n) = a.shape, b.shape
    elif mode == "nt":
        (m, k), (n, k2) = a.shape, b.shape
    else:
        (k, m), (k2, n) = a.shape, b.shape
    assert k == k2, (a.shape, b.shape, mode)
    tm, tn, tk = _tile(m, tm, 8), _tile(n, tn, LANES), _tile(k, tk, LANES)
    nk = k // tk
    a_spec = pl.BlockSpec((tk, tm), lambda i, j, kk: (kk, i)) if mode == "tn" else pl.BlockSpec((tm, tk), lambda i, j, kk: (i, kk))
    b_spec = pl.BlockSpec((tn, tk), lambda i, j, kk: (j, kk)) if mode == "nt" else pl.BlockSpec((tk, tn), lambda i, j, kk: (kk, j))
    tile_spec = pl.BlockSpec((tm, tn), lambda i, j, kk: (i, j))
    n_extra, n_out = len(extras), len(out_dtypes)
    dims = (_DOT_DIMS[mode], ((), ()))

    def body(*refs):
        a_ref, b_ref = refs[:2]
        extra_refs = refs[2:2 + n_extra]
        out_refs = refs[2 + n_extra:2 + n_extra + n_out]
        part = lax.dot_general(a_ref[...].astype(BF16), b_ref[...].astype(BF16), dims, preferred_element_type=F32)

        def finish(total):
            res = epilogue(total, *[e[...] for e in extra_refs]) if epilogue is not None else (total,)
            for o, r in zip(out_refs, res):
                o[...] = r.astype(o.dtype)

        if nk == 1:
            finish(part)
        else:
            acc = refs[-1]
            kk = pl.program_id(2)

            @pl.when(kk == 0)
            def _():
                acc[...] = part

            @pl.when(kk > 0)
            def _():
                acc[...] += part

            @pl.when(kk == nk - 1)
            def _():
                finish(acc[...])

    outs = pl.pallas_call(
        body, name=name, grid=(m // tm, n // tn, nk),
        in_specs=[a_spec, b_spec] + [tile_spec] * n_extra,
        out_specs=[tile_spec] * n_out,
        out_shape=[_sds((m, n), d) for d in out_dtypes],
        scratch_shapes=[pltpu.VMEM((tm, tn), F32)] if nk > 1 else [],
        compiler_params=_params(("parallel", "parallel", "arbitrary")),
    )(a, b, *extras)
    return outs if n_out > 1 else outs[0]


ROW_TILE = 256


def _norm_first(x, g):
    s, d = x.shape
    ts = _tile(s, ROW_TILE, 8)

    def body(x_ref, g_ref, h_ref):
        xv = x_ref[...]
        h_ref[...] = (xv * _rstd(xv) * g_ref[...]).astype(h_ref.dtype)

    row = pl.BlockSpec((ts, d), lambda i: (i, 0))
    vec = pl.BlockSpec((1, d), lambda i: (0, 0))
    return pl.pallas_call(body, name="norm_first", grid=(s // ts,), in_specs=[row, vec], out_specs=row,
                          out_shape=_sds((s, d), BF16), compiler_params=_params(("parallel",)))(x, g)


def _resid_norm(xres, y, g_post, g_next):
    s, d = xres.shape
    ts = _tile(s, ROW_TILE, 8)

    def body(x_ref, y_ref, gp_ref, gn_ref, xn_ref, hn_ref):
        yv = y_ref[...]
        xn = x_ref[...] + yv * _rstd(yv) * gp_ref[...]
        xn_ref[...] = xn
        hn_ref[...] = (xn * _rstd(xn) * gn_ref[...]).astype(hn_ref.dtype)

    row = pl.BlockSpec((ts, d), lambda i: (i, 0))
    vec = pl.BlockSpec((1, d), lambda i: (0, 0))
    return pl.pallas_call(body, name="resid_norm", grid=(s // ts,), in_specs=[row, row, vec, vec], out_specs=[row, row],
                          out_shape=[_sds((s, d), F32), _sds((s, d), BF16)], compiler_params=_params(("parallel",)))(xres, y, g_post, g_next)


def _resid_norm_loss(xres, y, g_post, target):
    s, d = xres.shape
    ts = _tile(s, ROW_TILE, 8)
    n_steps = s // ts

    def body(x_ref, y_ref, gp_ref, t_ref, dout_ref, loss_ref, acc_ref):
        i = pl.program_id(0)
        yv = y_ref[...]
        err = x_ref[...] + yv * _rstd(yv) * gp_ref[...] - t_ref[...]
        dout_ref[...] = err / d
        part = jnp.sum(err * err, axis=0, keepdims=True)

        @pl.when(i == 0)
        def _():
            acc_ref[...] = part

        @pl.when(i > 0)
        def _():
            acc_ref[...] += part

        @pl.when(i == n_steps - 1)
        def _():
            loss_ref[...] = 0.5 * jnp.sum(acc_ref[...], axis=1, keepdims=True) / d

    row = pl.BlockSpec((ts, d), lambda i: (i, 0))
    vec = pl.BlockSpec((1, d), lambda i: (0, 0))
    one = pl.BlockSpec((1, 1), lambda i: (0, 0))
    return pl.pallas_call(body, name="resid_norm_loss", grid=(n_steps,), in_specs=[row, row, vec, row], out_specs=[row, one],
                          out_shape=[_sds((s, d), F32), _sds((1, 1), F32)], scratch_shapes=[pltpu.VMEM((1, d), F32)],
                          compiler_params=_params(("arbitrary",)))(xres, y, g_post, target)


def _norm_bwd(xin, g, dout, dres, out_dtype, name):
    s, d = xin.shape
    ts = _tile(s, ROW_TILE, 8)
    has_res = dres is not None

    def body(*refs):
        x_ref, g_ref, do_ref = refs[:3]
        dx_ref, dg_ref = refs[-2:]
        i = pl.program_id(0)
        xv, dov = x_ref[...], do_ref[...]
        r = _rstd(xv)
        xhat = xv * r
        dg = jnp.sum(dov * xhat, axis=0, keepdims=True)
        dxh = dov * g_ref[...]
        dx = r * (dxh - xhat * jnp.mean(dxh * xhat, axis=-1, keepdims=True))
        if has_res:
            dx = dx + refs[3][...]
        dx_ref[...] = dx.astype(dx_ref.dtype)

        @pl.when(i == 0)
        def _():
            dg_ref[...] = dg

        @pl.when(i > 0)
        def _():
            dg_ref[...] += dg

    row = pl.BlockSpec((ts, d), lambda i: (i, 0))
    vec = pl.BlockSpec((1, d), lambda i: (0, 0))
    ops = [xin, g, dout] + ([dres] if has_res else [])
    return pl.pallas_call(body, name=name, grid=(s // ts,), in_specs=[row, vec, row] + ([row] if has_res else []),
                          out_specs=[row, vec], out_shape=[_sds((s, d), out_dtype), _sds((1, d), F32)],
                          compiler_params=_params(("arbitrary",)))(*ops)


class _Cols:
    def __init__(self, d):
        self.d = d
        self.kv = d // GROUP
        self.cb, self.cc, self.ci, self.q = 0, d, 2 * d, 3 * d
        self.k = 4 * d
        self.v = 4 * d + self.kv
        self.ga = 4 * d + 2 * self.kv
        self.gb = 5 * d + 2 * self.kv
        self.width = 6 * d + 2 * self.kv


CONV_COLS = 128


def _shift_rows(u, down):
    s = u.shape[0]
    rows = lax.broadcasted_iota(jnp.int32, u.shape, 0)
    if down:
        return jnp.where(rows == 0, 0.0, pltpu.roll(u, 1, 0))
    return jnp.where(rows == s - 1, 0.0, pltpu.roll(u, s - 1, 0))


def _conv_fwd(z, w, cols):
    s, d = z.shape[0], cols.d
    cw = CONV_COLS

    def body(cb_ref, cc_ref, ci_ref, w_ref, a_ref):
        u = cc_ref[...].astype(F32) * ci_ref[...].astype(F32)
        wv = w_ref[...]
        conv = wv[0:1] * _shift_rows(u, True) + wv[1:2] * u + wv[2:3] * _shift_rows(u, False)
        a_ref[...] = (cb_ref[...].astype(F32) * conv).astype(a_ref.dtype)

    def zspec(off):
        return pl.BlockSpec((s, cw), lambda j: (0, off // cw + j))

    return pl.pallas_call(body, name="conv_fwd", grid=(d // cw,),
                          in_specs=[zspec(cols.cb), zspec(cols.cc), zspec(cols.ci), pl.BlockSpec((3, cw), lambda j: (0, j))],
                          out_specs=pl.BlockSpec((s, cw), lambda j: (0, j)), out_shape=_sds((s, d), BF16),
                          compiler_params=_params(("parallel",)))(z, z, z, w)


def _conv_bwd(z, w, da, cols):
    s, d = z.shape[0], cols.d
    cw = CONV_COLS

    def body(cb_ref, cc_ref, ci_ref, w_ref, da_ref, dcb_ref, dcc_ref, dci_ref, dw_ref):
        cb, cc, ci, dav = cb_ref[...].astype(F32), cc_ref[...].astype(F32), ci_ref[...].astype(F32), da_ref[...]
        wv = w_ref[...]
        u = cc * ci
        um, up = _shift_rows(u, True), _shift_rows(u, False)
        conv = wv[0:1] * um + wv[1:2] * u + wv[2:3] * up
        dcb_ref[...] = (dav * conv).astype(dcb_ref.dtype)
        dconv = dav * cb
        dw_ref[0:1, :] = jnp.sum(dconv * um, axis=0, keepdims=True)
        dw_ref[1:2, :] = jnp.sum(dconv * u, axis=0, keepdims=True)
        dw_ref[2:3, :] = jnp.sum(dconv * up, axis=0, keepdims=True)
        du = wv[0:1] * _shift_rows(dconv, False) + wv[1:2] * dconv + wv[2:3] * _shift_rows(dconv, True)
        dcc_ref[...] = (du * ci).astype(dcc_ref.dtype)
        dci_ref[...] = (du * cc).astype(dci_ref.dtype)

    def zspec(off):
        return pl.BlockSpec((s, cw), lambda j: (0, off // cw + j))

    col = pl.BlockSpec((s, cw), lambda j: (0, j))
    wspec = pl.BlockSpec((3, cw), lambda j: (0, j))
    return pl.pallas_call(body, name="conv_bwd", grid=(d // cw,),
                          in_specs=[zspec(cols.cb), zspec(cols.cc), zspec(cols.ci), wspec, col],
                          out_specs=[col, col, col, wspec],
                          out_shape=[_sds((s, d), BF16)] * 3 + [_sds((3, d), F32)],
                          compiler_params=_params(("parallel",)))(z, z, z, w, da)


def _gate_fwd(z, bias, y_a, y_b, cols):
    s, d = y_a.shape
    ts, cw = _tile(s, ROW_TILE, 8), cols.kv
    nj = d // cw

    def body(ga_ref, gb_ref, ba_ref, bb_ref, ya_ref, yb_ref, o_ref):
        gate_a = jax.nn.sigmoid(ga_ref[...].astype(F32) + ba_ref[...])
        gate_b = jax.nn.sigmoid(gb_ref[...].astype(F32) + bb_ref[...])
        o_ref[...] = (gate_a * ya_ref[...] + gate_b * yb_ref[...]).astype(o_ref.dtype)

    tile = pl.BlockSpec((ts, cw), lambda i, j: (i, j))
    return pl.pallas_call(
        body, name="gate_fwd", grid=(s // ts, nj),
        in_specs=[pl.BlockSpec((ts, cw), lambda i, j: (i, cols.ga // cw + j)), pl.BlockSpec((ts, cw), lambda i, j: (i, cols.gb // cw + j)),
                  pl.BlockSpec((1, cw), lambda i, j: (0, j)), pl.BlockSpec((1, cw), lambda i, j: (0, nj + j)), tile, tile],
        out_specs=tile, out_shape=_sds((s, d), BF16), compiler_params=_params(("parallel", "parallel")))(z, z, bias, bias, y_a, y_b)


def _gate_bwd(z, bias, y_a, y_b, dmix, cols):
    s, d = y_a.shape
    ts, cw = _tile(s, ROW_TILE, 8), cols.kv
    nj = d // cw

    def body(ga_ref, gb_ref, ba_ref, bb_ref, ya_ref, yb_ref, dm_ref, dya_ref, dyb_ref, dga_ref, dgb_ref, dba_ref, dbb_ref):
        i = pl.program_id(1)
        gate_a = jax.nn.sigmoid(ga_ref[...].astype(F32) + ba_ref[...])
        gate_b = jax.nn.sigmoid(gb_ref[...].astype(F32) + bb_ref[...])
        dm = dm_ref[...]
        dya_ref[...] = (dm * gate_a).astype(dya_ref.dtype)
        dyb_ref[...] = (dm * gate_b).astype(dyb_ref.dtype)
        dga = dm * ya_ref[...] * (gate_a * (1.0 - gate_a))
        dgb = dm * yb_ref[...] * (gate_b * (1.0 - gate_b))
        dga_ref[...] = dga.astype(dga_ref.dtype)
        dgb_ref[...] = dgb.astype(dgb_ref.dtype)
        sa = jnp.sum(dga, axis=0, keepdims=True)
        sb = jnp.sum(dgb, axis=0, keepdims=True)

        @pl.when(i == 0)
        def _():
            dba_ref[...] = sa
            dbb_ref[...] = sb

        @pl.when(i > 0)
        def _():
            dba_ref[...] += sa
            dbb_ref[...] += sb

    tile = pl.BlockSpec((ts, cw), lambda j, i: (i, j))
    vec = pl.BlockSpec((1, cw), lambda j, i: (0, j))
    return pl.pallas_call(
        body, name="gate_bwd", grid=(nj, s // ts),
        in_specs=[pl.BlockSpec((ts, cw), lambda j, i: (i, cols.ga // cw + j)), pl.BlockSpec((ts, cw), lambda j, i: (i, cols.gb // cw + j)),
                  vec, pl.BlockSpec((1, cw), lambda j, i: (0, nj + j)), tile, tile, tile],
        out_specs=[tile, tile, tile, tile, vec, vec],
        out_shape=[_sds((s, d), BF16)] * 4 + [_sds((1, d), F32)] * 2,
        compiler_params=_params(("parallel", "arbitrary")))(z, z, bias, bias, y_a, y_b, dmix)


def _rope_tables(s):
    axis_dim = HEAD_DIM // 2
    n_freq = axis_dim // 2
    rows = s // GRID_W
    row_idx = jnp.repeat(jnp.arange(rows, dtype=jnp.int32), GRID_W)
    col_idx = jnp.tile(jnp.arange(GRID_W, dtype=jnp.int32), rows)
    inv_freq = ROPE_THETA ** (-jnp.arange(0, axis_dim, 2, dtype=F32) / axis_dim)
    ang = jnp.stack([row_idx.astype(F32)[:, None] * inv_freq, col_idx.astype(F32)[:, None] * inv_freq], axis=1)
    cos, sin = jnp.cos(ang), jnp.sin(ang)
    cos_t = jnp.stack([cos, cos], axis=2).reshape(s, HEAD_DIM)
    sin_t = jnp.stack([-sin, sin], axis=2).reshape(s, HEAD_DIM)
    return cos_t, sin_t


def _partner(x):
    n = x.shape[-1]
    lane = lax.broadcasted_iota(jnp.int32, x.shape, x.ndim - 1)
    quarter = HEAD_DIM // 4
    return jnp.where(lane % (2 * quarter) < quarter, pltpu.roll(x, n - quarter, x.ndim - 1), pltpu.roll(x, quarter, x.ndim - 1))


LOG2E = math.log2(math.e)
Q_SCALE = LOG2E / math.sqrt(HEAD_DIM)


def _rope_fwd(z, qn, kn, cos_t, sin_t, cols):
    s, d, kv = z.shape[0], cols.d, cols.kv
    ts = _tile(s, ROW_TILE, 8)
    scale = Q_SCALE

    def body(q_ref, k_ref, v_ref, qn_ref, kn_ref, c_ref, s_ref, qo_ref, ko_ref, vo_ref):
        c, sn = c_ref[...], s_ref[...]

        def head(xh, g):
            xn = xh * _rstd(xh) * g
            return xn * c + _partner(xn) * sn

        for h in range(d // HEAD_DIM):
            sl = slice(h * HEAD_DIM, (h + 1) * HEAD_DIM)
            qo_ref[:, sl] = (head(q_ref[:, sl].astype(F32), qn_ref[...]) * scale).astype(qo_ref.dtype)
        for h in range(kv // HEAD_DIM):
            sl = slice(h * HEAD_DIM, (h + 1) * HEAD_DIM)
            ko_ref[:, sl] = head(k_ref[:, sl].astype(F32), kn_ref[...]).astype(ko_ref.dtype)
        vo_ref[...] = v_ref[...].astype(vo_ref.dtype)

    vec = pl.BlockSpec((1, HEAD_DIM), lambda i: (0, 0))
    tab = pl.BlockSpec((ts, HEAD_DIM), lambda i: (i, 0))
    return pl.pallas_call(
        body, name="rope_fwd", grid=(s // ts,),
        in_specs=[pl.BlockSpec((ts, d), lambda i: (i, cols.q // d)), pl.BlockSpec((ts, kv), lambda i: (i, cols.k // kv)),
                  pl.BlockSpec((ts, kv), lambda i: (i, cols.v // kv)), vec, vec, tab, tab],
        out_specs=[pl.BlockSpec((ts, d), lambda i: (i, 0)), pl.BlockSpec((ts, kv), lambda i: (i, 0)), pl.BlockSpec((ts, kv), lambda i: (i, 0))],
        out_shape=[_sds((s, d), BF16), _sds((s, kv), BF16), _sds((s, kv), BF16)],
        compiler_params=_params(("parallel",)))(z, z, z, qn, kn, cos_t, sin_t)


def _rope_bwd(z, qn, kn, cos_t, sin_t, dq, dk_t, dv_t, cols):
    s, d, kv = z.shape[0], cols.d, cols.kv
    ts = _tile(s, ROW_TILE, 8)
    scale = 1.0 / math.sqrt(HEAD_DIM)

    def body(q_ref, k_ref, qn_ref, kn_ref, c_ref, s_ref, dq_ref, dkt_ref, dvt_ref, dzq_ref, dzk_ref, dzv_ref, dqn_ref, dkn_ref):
        i = pl.program_id(0)
        c, sn = c_ref[...], s_ref[...]
        dk_all = dkt_ref[...].T * (1.0 / LOG2E)

        def head_bwd(xh, g, drot):
            dxn = drot * c + _partner(drot * sn)
            r = _rstd(xh)
            xhat = xh * r
            dgain = jnp.sum(dxn * xhat, axis=0, keepdims=True)
            dxh = dxn * g
            return r * (dxh - xhat * jnp.mean(dxh * xhat, axis=-1, keepdims=True)), dgain

        dqn = jnp.zeros((1, HEAD_DIM), F32)
        for h in range(d // HEAD_DIM):
            sl = slice(h * HEAD_DIM, (h + 1) * HEAD_DIM)
            dx, dg = head_bwd(q_ref[:, sl].astype(F32), qn_ref[...], dq_ref[:, sl] * scale)
            dzq_ref[:, sl] = dx.astype(dzq_ref.dtype)
            dqn = dqn + dg
        dkn = jnp.zeros((1, HEAD_DIM), F32)
        for h in range(kv // HEAD_DIM):
            sl = slice(h * HEAD_DIM, (h + 1) * HEAD_DIM)
            dx, dg = head_bwd(k_ref[:, sl].astype(F32), kn_ref[...], dk_all[:, sl])
            dzk_ref[:, sl] = dx.astype(dzk_ref.dtype)
            dkn = dkn + dg
        dzv_ref[...] = dvt_ref[...].T.astype(dzv_ref.dtype)

        @pl.when(i == 0)
        def _():
            dqn_ref[...] = dqn
            dkn_ref[...] = dkn

        @pl.when(i > 0)
        def _():
            dqn_ref[...] += dqn
            dkn_ref[...] += dkn

    vec = pl.BlockSpec((1, HEAD_DIM), lambda i: (0, 0))
    tab = pl.BlockSpec((ts, HEAD_DIM), lambda i: (i, 0))
    qrow = pl.BlockSpec((ts, d), lambda i: (i, 0))
    krow = pl.BlockSpec((ts, kv), lambda i: (i, 0))
    kcol = pl.BlockSpec((kv, ts), lambda i: (0, i))
    return pl.pallas_call(
        body, name="rope_bwd", grid=(s // ts,),
        in_specs=[pl.BlockSpec((ts, d), lambda i: (i, cols.q // d)), pl.BlockSpec((ts, kv), lambda i: (i, cols.k // kv)),
                  vec, vec, tab, tab, qrow, kcol, kcol],
        out_specs=[qrow, krow, krow, vec, vec],
        out_shape=[_sds((s, d), BF16), _sds((s, kv), BF16), _sds((s, kv), BF16), _sds((1, HEAD_DIM), F32), _sds((1, HEAD_DIM), F32)],
        compiler_params=_params(("arbitrary",)))(z, z, qn, kn, cos_t, sin_t, dq, dk_t, dv_t)


Q_TILE = 256
_NT = (((1,), (1,)), ((), ()))
_NN = (((1,), (0,)), ((), ()))


def _attn_fwd(q, k, v):
    s, d = q.shape
    kvh = k.shape[1] // HEAD_DIM
    tq = _tile(s, Q_TILE, LANES)
    gw = GROUP * HEAD_DIM

    def body(q_ref, k_ref, v_ref, o_ref, lse_ref):
        kk, vv = k_ref[...], v_ref[...]
        for g in range(GROUP):
            sl = slice(g * HEAD_DIM, (g + 1) * HEAD_DIM)
            sc = lax.dot_general(q_ref[:, sl], kk, _NT, preferred_element_type=F32)
            mx = jnp.max(sc, axis=-1, keepdims=True)
            p = jnp.exp2(sc - mx)
            l = jnp.sum(p, axis=-1, keepdims=True)
            o = lax.dot_general(p.astype(BF16), vv, _NN, preferred_element_type=F32) * (1.0 / l)
            o_ref[:, sl] = o.astype(o_ref.dtype)
            lse_ref[:, g:g + 1] = mx + jnp.log(l) * LOG2E

    return pl.pallas_call(
        body, name="attn_fwd", grid=(kvh, s // tq),
        in_specs=[pl.BlockSpec((tq, gw), lambda j, i: (i, j)), pl.BlockSpec((s, HEAD_DIM), lambda j, i: (0, j)), pl.BlockSpec((s, HEAD_DIM), lambda j, i: (0, j))],
        out_specs=[pl.BlockSpec((tq, gw), lambda j, i: (i, j)), pl.BlockSpec((None, tq, GROUP), lambda j, i: (j, i, 0))],
        out_shape=[_sds((s, d), BF16), _sds((kvh, s, GROUP), F32)],
        compiler_params=_params(("parallel", "parallel")))(q, k, v)


_TN = (((0,), (0,)), ((), ()))


def _attn_bwd(q, k, v, do, lse):
    s, d = q.shape
    kv = k.shape[1]
    kvh = kv // HEAD_DIM
    tq = _tile(s, Q_TILE, LANES)
    gw = GROUP * HEAD_DIM

    def body(q_ref, k_ref, v_ref, do_ref, lse_ref, dq_ref, dkt_ref, dvt_ref):
        i = pl.program_id(1)
        kk, vv = k_ref[...], v_ref[...]

        @pl.when(i == 0)
        def _():
            dkt_ref[...] = jnp.zeros_like(dkt_ref)
            dvt_ref[...] = jnp.zeros_like(dvt_ref)

        for g in range(GROUP):
            sl = slice(g * HEAD_DIM, (g + 1) * HEAD_DIM)
            qg, dog = q_ref[:, sl], do_ref[:, sl]
            sc = lax.dot_general(qg, kk, _NT, preferred_element_type=F32)
            p = jnp.exp2(sc - lse_ref[:, g:g + 1])
            dp = lax.dot_general(dog, vv, _NT, preferred_element_type=F32)
            delta = jnp.sum(p * dp, axis=-1, keepdims=True)
            ds = (p * (dp - delta)).astype(BF16)
            dq_ref[:, sl] = lax.dot_general(ds, kk, _NN, preferred_element_type=F32)
            dkt_ref[...] += lax.dot_general(qg, ds, _TN, preferred_element_type=F32)
            dvt_ref[...] += lax.dot_general(dog, p.astype(BF16), _TN, preferred_element_type=F32)

    qspec = pl.BlockSpec((tq, gw), lambda j, i: (i, j))
    kspec = pl.BlockSpec((s, HEAD_DIM), lambda j, i: (0, j))
    stat = pl.BlockSpec((None, tq, GROUP), lambda j, i: (j, i, 0))
    tspec = pl.BlockSpec((HEAD_DIM, s), lambda j, i: (j, 0))
    return pl.pallas_call(
        body, name="attn_bwd", grid=(kvh, s // tq),
        in_specs=[qspec, kspec, kspec, qspec, stat], out_specs=[qspec, tspec, tspec],
        out_shape=[_sds((s, d), F32), _sds((kv, s), F32), _sds((kv, s), F32)],
        compiler_params=_params(("parallel", "arbitrary")))(q, k, v, do, lse)


ELEM_BLOCK_BYTES = 1 << 20

BIG = (("w_in", True), ("w_out_conv", False), ("w_out_attn", False), ("w_merge", False), ("w_up", True), ("w_down", False))
N_BIG = len(BIG)


def _elem_tiles(rows, width):
    tc = _tile(width, 2048, LANES)
    tr = _tile(rows, max(8, ELEM_BLOCK_BYTES // (4 * tc)), 8)
    return tr, tc


def _scalar_grid(grid, in_specs, out_specs):
    return pltpu.PrefetchScalarGridSpec(num_scalar_prefetch=1, grid=grid, in_specs=in_specs, out_specs=out_specs)


def _cast_place(w_stack, layer, chip, col_sharded, name):
    _, rows, width = w_stack.shape
    tr, tc = _elem_tiles(rows, width)
    nr, nc = rows // tr, width // tc

    def body(sc_ref, x_ref, o_ref):
        o_ref[...] = x_ref[...].astype(o_ref.dtype)

    if col_sharded:
        full, out_spec = (rows, width * N_CHIPS), pl.BlockSpec((tr, tc), lambda i, j, sc: (i, sc[0] * nc + j))
    else:
        full, out_spec = (rows * N_CHIPS, width), pl.BlockSpec((tr, tc), lambda i, j, sc: (sc[0] * nr + i, j))
    return pl.pallas_call(
        body, name=name,
        grid_spec=_scalar_grid((nr, nc), [pl.BlockSpec((None, tr, tc), lambda i, j, sc: (layer, i, j))], out_spec),
        out_shape=_sds(full, BF16), compiler_params=_params(("parallel", "parallel")))(chip, w_stack)


def _pair_add(g, landed, core, col_sharded, name):
    rows, width = landed.shape
    tr, tc = _elem_tiles(rows, width)
    nr, nc = rows // tr, width // tc

    def body(c_ref, g_ref, l_ref, o_ref):
        o_ref[...] = (g_ref[...] + l_ref[...]).astype(o_ref.dtype)

    if col_sharded:
        g_spec = pl.BlockSpec((tr, tc), lambda i, j, c: (c[0] * nr + i, j))
    else:
        g_spec = pl.BlockSpec((tr, tc), lambda i, j, c: (i, c[0] * nc + j))
    tile = pl.BlockSpec((tr, tc), lambda i, j, c: (i, j))
    return pl.pallas_call(
        body, name=name, grid_spec=_scalar_grid((nr, nc), [g_spec, tile], tile),
        out_shape=_sds((rows, width), BF16), compiler_params=_params(("parallel", "parallel")))(core, g, landed)


def _slot_of_relation(rel):
    return jnp.where(rel == 2, 0, jnp.where(rel == 1, 1, 2))


def _sum_chips(pair_sum, landed, chip, col_sharded, name):
    _, rows, width = landed.shape
    tr, tc = _elem_tiles(rows, width)
    nr, nc = rows // tr, width // tc

    def body(chip_ref, own_ref, q_ref, o_ref):
        me = chip_ref[0]
        own = own_ref[...].astype(F32)
        acc = None
        for t in range(N_CHIPS):
            rel = me ^ t
            term = jnp.where(rel == 0, own, q_ref[_slot_of_relation(rel)].astype(F32))
            acc = term if acc is None else acc + term
        o_ref[...] = acc

    if col_sharded:
        own_spec = pl.BlockSpec((tr, tc), lambda i, j, c: (i, c[0] * nc + j))
    else:
        own_spec = pl.BlockSpec((tr, tc), lambda i, j, c: (c[0] * nr + i, j))
    return pl.pallas_call(
        body, name=name,
        grid_spec=_scalar_grid((nr, nc), [own_spec, pl.BlockSpec((N_CHIPS - 1, tr, tc), lambda i, j, c: (0, i, j))],
                               pl.BlockSpec((tr, tc), lambda i, j, c: (i, j))),
        out_shape=_sds((rows, width), F32), compiler_params=_params(("parallel", "parallel")))(chip, pair_sum, landed)


def _adamw_math(w, g, m, v):
    mn = ADAM_B1 * m + (1.0 - ADAM_B1) * g
    vn = ADAM_B2 * v + (1.0 - ADAM_B2) * jnp.square(g)
    m_hat = mn / (1.0 - ADAM_B1 ** ADAM_STEP)
    v_hat = vn / (1.0 - ADAM_B2 ** ADAM_STEP)
    return -ADAM_LR * (m_hat / (jnp.sqrt(v_hat) + ADAM_EPS) + ADAM_WD * w), mn, vn


def _adamw(w, g, m, v, name):
    shape = w.shape
    width = shape[-1]
    w2, g2, m2, v2 = (a.reshape(-1, width) for a in (w, g, m, v))
    rows = w2.shape[0]
    tr, tc = _elem_tiles(rows, width)

    def body(w_ref, g_ref, m_ref, v_ref, d_ref, nm_ref, nv_ref):
        d_ref[...], nm_ref[...], nv_ref[...] = _adamw_math(w_ref[...], g_ref[...], m_ref[...], v_ref[...])

    tile = pl.BlockSpec((tr, tc), lambda i, j: (i, j))
    outs = pl.pallas_call(body, name=name, grid=(rows // tr, width // tc), in_specs=[tile] * 4, out_specs=[tile] * 3,
                          out_shape=[_sds((rows, width), F32)] * 3, compiler_params=_params(("parallel", "parallel")))(w2, g2, m2, v2)
    return tuple(o.reshape(shape) for o in outs)


def _adamw_layer(w, m, v, g_mine, g_sibling, carried, layer, core, col_sharded, name):
    depth, rows, width = w.shape
    pr, pc = g_mine.shape
    tr, tc = _elem_tiles(pr, pc)
    n_half = pr // tr if col_sharded else pc // tc
    if carried is None:
        carried = tuple(lax.empty((depth, rows, width), F32) for _ in range(4))

    def body(sc_ref, w_ref, m_ref, v_ref, gm_ref, gs_ref, *rest):
        g_ref, d_ref, nm_ref, nv_ref = rest[-4:]
        pos = pl.program_id(0) if col_sharded else pl.program_id(1)
        gv = jnp.where(pos // n_half == sc_ref[0], gm_ref[...], gs_ref[...])
        g_ref[...] = gv
        d_ref[...], nm_ref[...], nv_ref[...] = _adamw_math(w_ref[...], gv, m_ref[...], v_ref[...])

    stacked = pl.BlockSpec((None, tr, tc), lambda i, j, sc: (layer, i, j))
    if col_sharded:
        half = pl.BlockSpec((tr, tc), lambda i, j, sc: (i % n_half, j))
    else:
        half = pl.BlockSpec((tr, tc), lambda i, j, sc: (i, j % n_half))
    return pl.pallas_call(
        body, name=name,
        grid_spec=_scalar_grid((rows // tr, width // tc), [stacked] * 3 + [half] * 2 + [ANY] * 4, [stacked] * 4),
        out_shape=[_sds((depth, rows, width), F32)] * 4, input_output_aliases={6: 0, 7: 1, 8: 2, 9: 3},
        compiler_params=_params(("parallel", "parallel")))(core, w, m, v, g_mine, g_sibling, *carried)


_SEM = pl.BlockSpec(memory_space=pltpu.SEMAPHORE)
_HBM = pl.BlockSpec(memory_space=pltpu.HBM)
_VMEM = pl.BlockSpec(memory_space=pltpu.VMEM)
_EFFECT = pltpu.SideEffectType.DATAFLOW_SIDE_EFFECTING


def _place():
    x, y, c = lax.axis_index("x"), lax.axis_index("y"), lax.axis_index("c")
    others = [(1 - x, y), (x, 1 - y), (1 - x, 1 - y)]
    return x, y, c, others


def _chip_index(px, py):
    return 2 * px + py


def _remote(src, dst, send_sems, recv_sems, k, to):
    return pltpu.make_async_remote_copy(src_ref=src, dst_ref=dst, send_sem=send_sems.at[k], recv_sem=recv_sems.at[k],
                                        device_id=to, device_id_type=MESH)


def _phase(name, bufs, waits, wait_fn, n_start, start_fn, deps):
    nb, nd = len(bufs), len(deps)

    def body(*refs):
        buf_refs = refs[:nb]
        pos = nb
        if waits is not None:
            wait_fn(buf_refs, refs[pos], refs[pos + 1])
            pos += 2
        pos += nd
        if n_start:
            start_fn(buf_refs, refs[pos], refs[pos + 1])
            pos += 2
        token = refs[pos + nb]
        token[...] = jnp.zeros_like(token)

    n_sem_out = 2 if n_start else 0
    if waits is None:
        bufs = [pltpu.with_memory_space_constraint(b, pltpu.HBM) for b in bufs]
    outs = pl.pallas_call(
        body, name=name,
        in_specs=[_HBM] * nb + ([_SEM] * 2 if waits is not None else []) + [ANY] * nd,
        out_specs=[_SEM] * n_sem_out + [_HBM] * nb + [_VMEM],
        out_shape=[pltpu.SemaphoreType.DMA((n_start,))] * n_sem_out + [pltpu.HBM(b.shape, b.dtype) for b in bufs] + [_sds((8, LANES), F32)],
        input_output_aliases={i: n_sem_out + i for i in range(nb)},
        compiler_params=pltpu.CompilerParams(has_side_effects=_EFFECT),
    )(*bufs, *(waits if waits is not None else ()), *deps)
    sems = tuple(outs[:2]) if n_start else None
    return sems, list(outs[n_sem_out:n_sem_out + nb]), outs[-1]


class _Chains:
    def __init__(self):
        self.active = []
        self.last = None
        self.dep = None
        self.pinned = []
        self.token = None
        self.at = None

    def phase(self, name, bufs, waits, wait_fn, n_start, start_fn):
        deps = [a for a in (self.last, self.dep) if a is not None] + self.pinned
        sems, thru, token = _phase(name, bufs, waits, wait_fn, n_start, start_fn, deps)
        self.last, self.dep, self.token, self.pinned = token, None, token, []
        return sems, thru

    def pin(self, result):
        self.pinned.append(result)

    def add(self, gen):
        self.active.append(gen)

    def tick(self, dep, at=None):
        self.at, self.token, self.dep = at, None, dep
        for gen in list(self.active):
            if next(gen, "done") == "done":
                self.active.remove(gen)
        return self.token


def _after(small, token):
    return small if token is None else small + token[0, 0]


def _shard_region(ref, col_sharded, chip, n_shard):
    start = pl.multiple_of(chip * n_shard, LANES if col_sharded else 8)
    if col_sharded:
        return ref.at[:, pl.ds(start, n_shard)]
    return ref.at[pl.ds(start, n_shard), :]


def _row_half(ref, half):
    n_rows = ref.shape[0]
    return ref.at[pl.ds(pl.multiple_of(half * (n_rows // 2), 8), n_rows // 2), :]


def _half_of(ref, col_sharded, half):
    rows, width = ref.shape
    if col_sharded:
        return ref.at[pl.ds(pl.multiple_of(half * (rows // 2), 8), rows // 2), :]
    return ref.at[:, pl.ds(pl.multiple_of(half * (width // 2), LANES), width // 2)]


def _gather_chain(chains, tag, group, bufs, out, wait_for):
    n_w = len(group)
    n = 3 * n_w

    def region(refs, a, chip, half):
        col = BIG[group[a]][1]
        n_shard = refs[a].shape[1] // N_CHIPS if col else refs[a].shape[0] // N_CHIPS
        return _row_half(_shard_region(refs[a], col, chip, n_shard), half)

    def start_ici(refs, send, recv):
        x, y, c, others = _place()
        for a in range(n_w):
            mine = region(refs, a, _chip_index(x, y), c)
            for j, (ox, oy) in enumerate(others):
                _remote(mine, mine, send, recv, 3 * a + j, (ox, oy, c)).start()

    def wait_ici(refs, send, recv):
        x, y, c, others = _place()
        for a in range(n_w):
            for j, (ox, oy) in enumerate(others):
                landed = region(refs, a, _chip_index(ox, oy), c)
                cp = _remote(landed, landed, send, recv, 3 * a + j, (x, y, 1 - c))
                cp.wait_recv()
                cp.wait_send()

    def start_d2d(refs, send, recv):
        x, y, c, others = _place()
        for a in range(n_w):
            for j, (ox, oy) in enumerate(others):
                landed = region(refs, a, _chip_index(ox, oy), c)
                _remote(landed, landed, send, recv, 3 * a + j, (x, y, 1 - c)).start()

    def wait_d2d(refs, send, recv):
        x, y, c, others = _place()
        for a in range(n_w):
            for j, (ox, oy) in enumerate(others):
                theirs = region(refs, a, _chip_index(ox, oy), 1 - c)
                cp = _remote(theirs, theirs, send, recv, 3 * a + j, (x, y, 1 - c))
                cp.wait_recv()
                cp.wait_send()

    sems, bufs = chains.phase(f"gather_ici_start_{tag}", bufs, None, None, n, start_ici)
    yield
    while wait_for is not None and chains.at != wait_for:
        yield
    sems, bufs = chains.phase(f"gather_forward_{tag}", bufs, sems, wait_ici, n, start_d2d)
    yield
    _, bufs = chains.phase(f"gather_done_{tag}", bufs, sems, wait_d2d, 0, None)
    for a, buf in zip(group, bufs):
        out[BIG[a][0]] = buf


def _grad_chain(chains, layer, group, grads, core, chip, w, mom, var, carried):
    n = len(group)
    tag = f"{layer}_{group[0]}"
    kinds = [BIG[a][1] for a in group]
    names = [BIG[a][0] for a in group]
    sibling_of = lambda x, y, c: (x, y, 1 - c)

    def pair_start(refs, send, recv):
        x, y, c, _ = _place()
        for i in range(n):
            _remote(_half_of(refs[i], kinds[i], 1 - c), refs[n + i], send, recv, i, sibling_of(x, y, c)).start()

    def pair_wait(refs, send, recv):
        x, y, c, _ = _place()
        for i in range(n):
            cp = _remote(_half_of(refs[i], kinds[i], 1 - c), refs[n + i], send, recv, i, sibling_of(x, y, c))
            cp.wait_recv()
            cp.wait_send()

    def piece(ref, col, chip_idx):
        return _shard_region(ref, col, chip_idx, ref.shape[1] // N_CHIPS if col else ref.shape[0] // N_CHIPS)

    def scatter_start(refs, send, recv):
        x, y, c, others = _place()
        for i in range(n):
            for j, (ox, oy) in enumerate(others):
                _remote(piece(refs[i], kinds[i], _chip_index(ox, oy)), refs[n + i].at[j], send, recv, 3 * i + j, (ox, oy, c)).start()

    def scatter_wait(refs, send, recv):
        x, y, c, others = _place()
        for i in range(n):
            for j, (ox, oy) in enumerate(others):
                cp = _remote(piece(refs[i], kinds[i], _chip_index(ox, oy)), refs[n + i].at[j], send, recv, 3 * i + j, (ox, oy, c))
                cp.wait_recv()
                cp.wait_send()

    def join_start(refs, send, recv):
        x, y, c, _ = _place()
        for i in range(n):
            _remote(refs[i], refs[n + i], send, recv, i, sibling_of(x, y, c)).start()

    def join_wait(refs, send, recv):
        x, y, c, _ = _place()
        for i in range(n):
            cp = _remote(refs[i], refs[n + i], send, recv, i, sibling_of(x, y, c))
            cp.wait_recv()
            cp.wait_send()

    half_shapes = [(g.shape[0] // 2, g.shape[1]) if col else (g.shape[0], g.shape[1] // 2) for col, g in zip(kinds, grads)]
    lands = [lax.empty(s, F32) for s in half_shapes]
    sems, bufs = chains.phase(f"pair_start_{tag}", list(grads) + lands, None, None, n, pair_start)
    yield
    _, bufs = chains.phase(f"pair_wait_{tag}", bufs, sems, pair_wait, 0, None)
    pair_sums = [_pair_add(bufs[i], bufs[n + i], core, kinds[i], "pair_add_" + names[i]) for i in range(n)]
    piece_shapes = [(s[0], s[1] // N_CHIPS) if col else (s[0] // N_CHIPS, s[1]) for col, s in zip(kinds, half_shapes)]
    slots = [lax.empty((N_CHIPS - 1, *s), BF16) for s in piece_shapes]
    sems, bufs = chains.phase(f"scatter_start_{tag}", pair_sums + slots, None, None, 3 * n, scatter_start)
    yield
    yield
    _, bufs = chains.phase(f"scatter_wait_{tag}", bufs, sems, scatter_wait, 0, None)
    reduced = [_sum_chips(bufs[i], bufs[n + i], chip, kinds[i], "sum_chips_" + names[i]) for i in range(n)]
    theirs = [lax.empty(s, F32) for s in piece_shapes]
    sems, bufs = chains.phase(f"join_start_{tag}", reduced + theirs, None, None, n, join_start)
    yield
    _, bufs = chains.phase(f"join_wait_{tag}", bufs, sems, join_wait, 0, None)
    for i in range(n):
        carried[names[i]] = _adamw_layer(w[names[i]], mom[names[i]], var[names[i]], bufs[i], bufs[n + i], carried.get(names[i]),
                                         layer, core, kinds[i], "adamw_" + names[i])
        chains.pin(carried[names[i]][0])


def _allreduce_small(vec, name):
    rows = vec.shape[0]
    masks = [(dx, dy, dc) for dx in (0, 1) for dy in (0, 1) for dc in (0, 1)][1:]

    def body(v_ref, o_ref, gather_ref, send_sems, recv_sems):
        x, y, c, _ = _place()
        me = 4 * x + 2 * y + c
        gather_ref[me] = v_ref[...]
        copies = []
        for k, (dx, dy, dc) in enumerate(masks):
            peer = (x ^ dx, y ^ dy, c ^ dc)
            copies.append(_remote(v_ref, gather_ref.at[me], send_sems, recv_sems, k, peer))
        for cp in copies:
            cp.start()
        for k, (dx, dy, dc) in enumerate(masks):
            slot = gather_ref.at[4 * (x ^ dx) + 2 * (y ^ dy) + (c ^ dc)]
            _remote(slot, slot, send_sems, recv_sems, k, (x, y, c)).wait_recv()
        for cp in copies:
            cp.wait_send()
        acc = gather_ref[0]
        for dev in range(1, N_DEV):
            acc = acc + gather_ref[dev]
        o_ref[...] = acc

    return pl.pallas_call(
        body, name=name, in_specs=[_VMEM], out_specs=_VMEM, out_shape=_sds((rows, LANES), F32),
        scratch_shapes=[pltpu.VMEM((N_DEV, rows, LANES), F32), pltpu.SemaphoreType.DMA((N_DEV - 1,)), pltpu.SemaphoreType.DMA((N_DEV - 1,))],
        compiler_params=pltpu.CompilerParams(has_side_effects=True, vmem_limit_bytes=VMEM_LIMIT_BYTES),
    )(vec)


def _relu2(acc):
    r = jnp.maximum(acc, 0.0)
    return acc, r * r


def _relu2_bwd(acc, up):
    return (acc * (2.0 * jnp.maximum(up.astype(F32), 0.0)),)


def _layer_fwd(x, h, w, cols, tables, tick, last, target=None):
    cos_t, sin_t = tables
    z = _mm(h, w("w_in"), "nn", [BF16], name="mm_in")
    a = _conv_fwd(z, w("conv_w"), cols)
    q, k, v = _rope_fwd(z, w("q_norm"), w("k_norm"), cos_t, sin_t, cols)
    o, lse = _attn_fwd(q, k, v)
    tick(o, 0)
    tick(None, 1)
    y_a = _mm(a, w("w_out_conv"), "nn", [F32], name="mm_out_conv")
    y_b = _mm(o, w("w_out_attn"), "nn", [F32], name="mm_out_attn")
    mix = _gate_fwd(z, w("gate_bias"), y_a, y_b, cols)
    mixed = _mm(mix, w("w_merge"), "nn", [F32], name="mm_merge")
    x1, h2 = _resid_norm(x, mixed, _after(w("norm_mix_post"), tick(mixed, 2)), w("norm_mlp_pre"))
    up, act = _mm(h2, w("w_up"), "nn", [BF16, BF16], name="mm_up", epilogue=_relu2)
    f = _mm(act, w("w_down"), "nn", [F32], name="mm_down")
    kept = dict(x=x, h=h, z=z, a=a, q=q, k=k, v=v, o=o, lse=lse, y_a=y_a, y_b=y_b, mix=mix, mixed=mixed, x1=x1, h2=h2, up=up, act=act, f=f)
    g_post = _after(w("norm_mlp_post"), tick(f, 3))
    if last:
        return _resid_norm_loss(x1, f, g_post, target), kept
    return _resid_norm(x1, f, g_post, w("norm_next")), kept


def _layer_bwd(dx_out, w, kept, cols, tables, tick, emit):
    cos_t, sin_t = tables
    t = kept
    df, d_norm_mlp_post = _norm_bwd(t["f"], _after(w["norm_mlp_post"], tick(dx_out, 0)), dx_out, None, BF16, "norm_bwd_mlp_post")
    dup = _mm(df, w["w_down"], "nt", [BF16], name="mm_d_act", epilogue=_relu2_bwd, extras=(t["up"],))
    g_w_down = _mm(t["act"], df, "tn", [F32], name="mm_g_down")
    g_w_up = _mm(t["h2"], dup, "tn", [F32], name="mm_g_up")
    emit((4, 5), [g_w_up, g_w_down])
    dh2 = _mm(dup, w["w_up"], "nt", [F32], name="mm_d_h2")
    dx1, d_norm_mlp_pre = _norm_bwd(t["x1"], _after(w["norm_mlp_pre"], tick(dh2, 1)), dh2, dx_out, F32, "norm_bwd_mlp_pre")
    dmixed, d_norm_mix_post = _norm_bwd(t["mixed"], w["norm_mix_post"], dx1, None, BF16, "norm_bwd_mix_post")
    dmix = _mm(dmixed, w["w_merge"], "nt", [F32], name="mm_d_mix")
    g_w_merge = _mm(t["mix"], dmixed, "tn", [F32], name="mm_g_merge")
    dy_a, dy_b, dz_ga, dz_gb, dbias_a, dbias_b = _gate_bwd(t["z"], _after(w["gate_bias"], tick(g_w_merge, 2)), t["y_a"], t["y_b"], dmix, cols)
    g_w_out_conv = _mm(t["a"], dy_a, "tn", [F32], name="mm_g_out_conv")
    da = _mm(dy_a, w["w_out_conv"], "nt", [F32], name="mm_d_a")
    g_w_out_attn = _mm(t["o"], dy_b, "tn", [F32], name="mm_g_out_attn")
    emit((1, 2, 3), [g_w_out_conv, g_w_out_attn, g_w_merge])
    do = _mm(dy_b, w["w_out_attn"], "nt", [BF16], name="mm_d_o")
    dz_cb, dz_cc, dz_ci, d_conv_w = _conv_bwd(t["z"], _after(w["conv_w"], tick(do, 3)), da, cols)
    dq, dk_t, dv_t = _attn_bwd(t["q"], t["k"], t["v"], do, t["lse"])
    dz_q, dz_k, dz_v, d_q_norm, d_k_norm = _rope_bwd(t["z"], _after(w["q_norm"], tick(dv_t, 4)), w["k_norm"], cos_t, sin_t, dq, dk_t, dv_t, cols)
    dz = jnp.concatenate([dz_cb, dz_cc, dz_ci, dz_q, dz_k, dz_v, dz_ga, dz_gb], axis=1)
    g_w_in = _mm(t["h"], dz, "tn", [F32], name="mm_g_in")
    emit((0,), [g_w_in])
    dh = _mm(dz, w["w_in"], "nt", [F32], name="mm_d_h", tk=3328)
    dx_in, d_norm_mix_pre = _norm_bwd(t["x"], _after(w["norm_mix_pre"], tick(dh, 5)), dh, dx1, F32, "norm_bwd_mix_pre")
    small = dict(norm_mix_pre=d_norm_mix_pre, gate_bias=jnp.concatenate([dbias_a, dbias_b], axis=1), conv_w=d_conv_w, q_norm=d_q_norm,
                 k_norm=d_k_norm, norm_mix_post=d_norm_mix_post, norm_mlp_pre=d_norm_mlp_pre, norm_mlp_post=d_norm_mlp_post)
    return dx_in, small


SMALL = ("norm_mix_pre", "gate_bias", "conv_w", "q_norm", "k_norm", "norm_mix_post", "norm_mlp_pre", "norm_mlp_post")
WEIGHTS = ("norm_mix_pre", "w_in", "gate_bias", "conv_w", "q_norm", "k_norm", "w_out_conv", "w_out_attn", "w_merge",
           "norm_mix_post", "norm_mlp_pre", "w_up", "w_down", "norm_mlp_post")


def _pack(parts):
    flat = jnp.concatenate([a.reshape(-1) for a in parts])
    rows = -(-flat.shape[0] // LANES)
    pad = (-rows) % 8
    flat = jnp.pad(flat, (0, (rows + pad) * LANES - flat.shape[0]))
    return flat.reshape(rows + pad, LANES)


def _unpack(packed, shapes):
    flat = packed.reshape(-1)
    out, off = [], 0
    for shp in shapes:
        n = math.prod(shp)
        out.append(flat[off:off + n].reshape(shp))
        off += n
    return out


def kernel(x, norm_mix_pre, w_in, gate_bias, conv_w, q_norm, k_norm, w_out_conv, w_out_attn, w_merge, norm_mix_post, norm_mlp_pre, w_up, w_down, norm_mlp_post, loss_target, m_norm_mix_pre, m_w_in, m_gate_bias, m_conv_w, m_q_norm, m_k_norm, m_w_out_conv, m_w_out_attn, m_w_merge, m_norm_mix_post, m_norm_mlp_pre, m_w_up, m_w_down, m_norm_mlp_post, v_norm_mix_pre, v_w_in, v_gate_bias, v_conv_w, v_q_norm, v_k_norm, v_w_out_conv, v_w_out_attn, v_w_merge, v_norm_mix_post, v_norm_mlp_pre, v_w_up, v_w_down, v_norm_mlp_post):
    w = dict(norm_mix_pre=norm_mix_pre, w_in=w_in, gate_bias=gate_bias, conv_w=conv_w, q_norm=q_norm, k_norm=k_norm, w_out_conv=w_out_conv,
             w_out_attn=w_out_attn, w_merge=w_merge, norm_mix_post=norm_mix_post, norm_mlp_pre=norm_mlp_pre, w_up=w_up, w_down=w_down,
             norm_mlp_post=norm_mlp_post)
    mom = dict(norm_mix_pre=m_norm_mix_pre, w_in=m_w_in, gate_bias=m_gate_bias, conv_w=m_conv_w, q_norm=m_q_norm, k_norm=m_k_norm,
               w_out_conv=m_w_out_conv, w_out_attn=m_w_out_attn, w_merge=m_w_merge, norm_mix_post=m_norm_mix_post, norm_mlp_pre=m_norm_mlp_pre,
               w_up=m_w_up, w_down=m_w_down, norm_mlp_post=m_norm_mlp_post)
    var = dict(norm_mix_pre=v_norm_mix_pre, w_in=v_w_in, gate_bias=v_gate_bias, conv_w=v_conv_w, q_norm=v_q_norm, k_norm=v_k_norm,
               w_out_conv=v_w_out_conv, w_out_attn=v_w_out_attn, w_merge=v_w_merge, norm_mix_post=v_norm_mix_post, norm_mlp_pre=v_norm_mlp_pre,
               w_up=v_w_up, w_down=v_w_down, norm_mlp_post=v_norm_mlp_post)
    depth = w_in.shape[0]
    _, s, d = x.shape
    cols = _Cols(d)
    x0 = x.reshape(s, d)
    target = loss_target.reshape(s, d)
    tables = _rope_tables(s)
    chip = (2 * lax.axis_index("x") + lax.axis_index("y")).astype(jnp.int32)
    core = lax.axis_index("c").astype(jnp.int32)
    chip_vec, core_vec = chip.reshape(1), core.reshape(1)

    n_conv = conv_w.shape[-1]
    placed = lax.dynamic_update_slice_in_dim(jnp.zeros((depth, conv_w.shape[1], n_conv * N_CHIPS), F32), conv_w, chip * n_conv, axis=2)
    conv_full = _unpack(_allreduce_small(_pack([jnp.where(core == 0, placed, 0.0)]), "gather_conv_w"), [placed.shape])[0]

    chains = _Chains()
    chains.pin(conv_full)
    full = [{} for _ in range(depth)]
    everything = tuple(range(N_BIG))
    gathers = []
    for l in range(depth):
        bufs = [_cast_place(w[name], l, chip_vec, col, "cast_place_" + name) for name, col in BIG]
        groups = ((0,), everything[1:]) if l == 0 else (everything,)
        for group in groups:
            first = l == 0 and group[0] == 0
            wait_for = None if first else ("fwd", max(l - 1, 0), 0 if l == 0 else 2)
            gathers.append(_gather_chain(chains, f"{l}_{group[0]}", group, [bufs[a] for a in group], full[l], wait_for))
            next(gathers[-1])
    next(gathers[0])
    next(gathers[0], None)
    for g in gathers[1:]:
        chains.add(g)

    def layer_params(l):
        p = dict(full[l])
        p["conv_w"] = conv_full[l]
        p["gate_bias"] = gate_bias[l].reshape(1, -1)
        for name in ("norm_mix_pre", "q_norm", "k_norm", "norm_mix_post", "norm_mlp_pre", "norm_mlp_post"):
            p[name] = w[name][l].reshape(1, -1)
        if l + 1 < depth:
            p["norm_next"] = w["norm_mix_pre"][l + 1].reshape(1, -1)
        return p

    kept = []
    xl, h = x0, _norm_first(x0, norm_mix_pre[0].reshape(1, -1))
    for l in range(depth):
        last = l == depth - 1
        (xl, h), t = _layer_fwd(xl, h, lambda name, l=l: layer_params(l)[name], cols, tables,
                                lambda dep, k, l=l: chains.tick(dep, ("fwd", l, k)), last, target if last else None)
        kept.append(t)
    dx, loss_local = xl, h

    carried = {}
    small = [None] * depth
    for l in reversed(range(depth)):
        def emit(group, grads, l=l):
            chains.add(_grad_chain(chains, l, group, grads, core_vec, chip_vec, w, mom, var, carried))

        dx, small[l] = _layer_bwd(dx, layer_params(l), kept[l], cols, tables, lambda dep, k: chains.tick(dep), emit)
    while chains.active:
        chains.tick(None)
    grads = {name: carried[name][0] for name, _ in BIG}
    delta = {name: carried[name][1] for name, _ in BIG}
    new_m = {name: carried[name][2] for name, _ in BIG}
    new_v = {name: carried[name][3] for name, _ in BIG}

    small_full_shapes = [(depth,) + small[0][name].shape for name in SMALL]
    packed = _pack([jnp.stack([small[l][name] for l in range(depth)]) for name in SMALL] + [jnp.broadcast_to(loss_local.reshape(1), (LANES,))])
    small_sum = _unpack(_allreduce_small(packed, "allreduce_small"), small_full_shapes + [(LANES,)])
    loss = small_sum[-1][0]
    for name, g in zip(SMALL, small_sum[:-1]):
        if name == "conv_w":
            g = lax.dynamic_slice_in_dim(g, chip * n_conv, n_conv, axis=2)
        grads[name] = g.reshape(w[name].shape)
    small_shapes = [w[name].shape for name in SMALL]
    packs = [_pack([src[name] for name in SMALL]) for src in (w, grads, mom, var)]
    for dst, out in zip((delta, new_m, new_v), _adamw(*packs, "adamw_small")):
        for name, val in zip(SMALL, _unpack(out, small_shapes)):
            dst[name] = val

    grad_x = dx.reshape(x.shape)
    return (loss, grad_x, *[grads[n] for n in WEIGHTS], *[delta[n] for n in WEIGHTS], *[new_m[n] for n in WEIGHTS], *[new_v[n] for n in WEIGHTS])
```

```python
import math

import jax
import jax.numpy as jnp
from jax import lax
from jax.experimental import pallas as pl
from jax.experimental.pallas import tpu as pltpu

F32 = jnp.float32
BF16 = jnp.bfloat16

HEAD_DIM = 128
GROUP = 4
GRID_W = 64
ROPE_THETA = 10000.0
RMS_EPS = 1e-6
ADAM_LR = 0.001
ADAM_B1 = 0.9
ADAM_B2 = 0.999
ADAM_EPS = 1e-08
ADAM_WD = 0.01
ADAM_STEP = 10

LANES = 128
N_CHIPS = 4
N_DEV = 8
VMEM_LIMIT_BYTES = 56 * 1024 * 1024
MESH = pl.DeviceIdType.MESH
ANY = pl.BlockSpec(memory_space=pl.ANY)


def _tile(dim, cap, mult):
    if dim <= cap:
        return dim
    t = (cap // mult) * mult
    while t >= mult:
        if dim % t == 0:
            return t
        t -= mult
    raise ValueError(f"no tile for {dim} under {cap} in multiples of {mult}")


def _params(sem=None):
    return pltpu.CompilerParams(dimension_semantics=sem, vmem_limit_bytes=VMEM_LIMIT_BYTES)


def _sds(shape, dtype):
    return jax.ShapeDtypeStruct(tuple(shape), dtype)


def _rstd(x):
    return lax.rsqrt(jnp.mean(x * x, axis=-1, keepdims=True) + RMS_EPS)


_DOT_DIMS = {"nn": ((1,), (0,)), "nt": ((1,), (1,)), "tn": ((0,), (0,))}


def _mm(a, b, mode, out_dtypes, *, name, epilogue=None, extras=(), tm=1024, tn=1024, tk=2048, half=None):
    if mode == "nn":
        (m, k), (k2, n) = a.shape, b.shape
    elif mode == "nt":
        (m, k), (n, k2) = a.shape, b.shape
    else:
        (k, m), (k2, n) = a.shape, b.shape
    assert k == k2, (a.shape, b.shape, mode)
    side = half[1] if half is not None else None
    if side == "m":
        m //= 2
    elif side == "n":
        n //= 2
    tm, tn, tk = _tile(m, tm, 8), _tile(n, tn, LANES), _tile(k, tk, LANES)
    nk = k // tk
    row = (lambda i, s: i + s[0][0] * (m // tm)) if side == "m" else (lambda i, s: i)
    col = (lambda j, s: j + s[0][0] * (n // tn)) if side == "n" else (lambda j, s: j)
    a_spec = pl.BlockSpec((tk, tm), lambda i, j, kk, *s: (kk, row(i, s))) if mode == "tn" else pl.BlockSpec((tm, tk), lambda i, j, kk, *s: (row(i, s), kk))
    b_spec = pl.BlockSpec((tn, tk), lambda i, j, kk, *s: (col(j, s), kk)) if mode == "nt" else pl.BlockSpec((tk, tn), lambda i, j, kk, *s: (kk, col(j, s)))
    tile_spec = pl.BlockSpec((tm, tn), lambda i, j, kk, *s: (i, j))
    n_extra, n_out = len(extras), len(out_dtypes)
    dims = (_DOT_DIMS[mode], ((), ()))

    def body(*refs):
        if half is not None:
            refs = refs[1:]
        a_ref, b_ref = refs[:2]
        extra_refs = refs[2:2 + n_extra]
        out_refs = refs[2 + n_extra:2 + n_extra + n_out]
        part = lax.dot_general(a_ref[...].astype(BF16), b_ref[...].astype(BF16), dims, preferred_element_type=F32)

        def finish(total):
            res = epilogue(total, *[e[...] for e in extra_refs]) if epilogue is not None else (total,)
            for o, r in zip(out_refs, res):
                o[...] = r.astype(o.dtype)

        if nk == 1:
            finish(part)
        else:
            acc = refs[-1]
            kk = pl.program_id(2)

            @pl.when(kk == 0)
            def _():
                acc[...] = part

            @pl.when(kk > 0)
            def _():
                acc[...] += part

            @pl.when(kk == nk - 1)
            def _():
                finish(acc[...])

    grid = (m // tm, n // tn, nk)
    in_specs, out_specs = [a_spec, b_spec] + [tile_spec] * n_extra, [tile_spec] * n_out
    scratch = [pltpu.VMEM((tm, tn), F32)] if nk > 1 else []
    if half is None:
        layout, lead = dict(grid=grid, in_specs=in_specs, out_specs=out_specs, scratch_shapes=scratch), ()
    else:
        layout = dict(grid_spec=pltpu.PrefetchScalarGridSpec(num_scalar_prefetch=1, grid=grid, in_specs=in_specs, out_specs=out_specs,
                                                             scratch_shapes=scratch))
        lead = (half[0],)
    outs = pl.pallas_call(body, name=name, out_shape=[_sds((m, n), d) for d in out_dtypes],
                          compiler_params=_params(("parallel", "parallel", "arbitrary")), **layout)(*lead, a, b, *extras)
    return outs if n_out > 1 else outs[0]


ROW_TILE = 256


def _norm_first(x, g):
    s, d = x.shape
    ts = _tile(s, ROW_TILE, 8)

    def body(x_ref, g_ref, h_ref):
        xv = x_ref[...]
        h_ref[...] = (xv * _rstd(xv) * g_ref[...]).astype(h_ref.dtype)

    row = pl.BlockSpec((ts, d), lambda i: (i, 0))
    vec = pl.BlockSpec((1, d), lambda i: (0, 0))
    return pl.pallas_call(body, name="norm_first", grid=(s // ts,), in_specs=[row, vec], out_specs=row,
                          out_shape=_sds((s, d), BF16), compiler_params=_params(("parallel",)))(x, g)


def _resid_norm(xres, y, g_post, g_next):
    s, d = xres.shape
    ts = _tile(s, ROW_TILE, 8)

    def body(x_ref, y_ref, gp_ref, gn_ref, xn_ref, hn_ref):
        yv = y_ref[...]
        xn = x_ref[...] + yv * _rstd(yv) * gp_ref[...]
        xn_ref[...] = xn
        hn_ref[...] = (xn * _rstd(xn) * gn_ref[...]).astype(hn_ref.dtype)

    row = pl.BlockSpec((ts, d), lambda i: (i, 0))
    vec = pl.BlockSpec((1, d), lambda i: (0, 0))
    return pl.pallas_call(body, name="resid_norm", grid=(s // ts,), in_specs=[row, row, vec, vec], out_specs=[row, row],
                          out_shape=[_sds((s, d), F32), _sds((s, d), BF16)], compiler_params=_params(("parallel",)))(xres, y, g_post, g_next)


def _resid_norm_loss(xres, y, g_post, target):
    s, d = xres.shape
    ts = _tile(s, ROW_TILE, 8)
    n_steps = s // ts

    def body(x_ref, y_ref, gp_ref, t_ref, dout_ref, loss_ref, acc_ref):
        i = pl.program_id(0)
        yv = y_ref[...]
        err = x_ref[...] + yv * _rstd(yv) * gp_ref[...] - t_ref[...]
        dout_ref[...] = err / d
        part = jnp.sum(err * err, axis=0, keepdims=True)

        @pl.when(i == 0)
        def _():
            acc_ref[...] = part

        @pl.when(i > 0)
        def _():
            acc_ref[...] += part

        @pl.when(i == n_steps - 1)
        def _():
            loss_ref[...] = 0.5 * jnp.sum(acc_ref[...], axis=1, keepdims=True) / d

    row = pl.BlockSpec((ts, d), lambda i: (i, 0))
    vec = pl.BlockSpec((1, d), lambda i: (0, 0))
    one = pl.BlockSpec((1, 1), lambda i: (0, 0))
    return pl.pallas_call(body, name="resid_norm_loss", grid=(n_steps,), in_specs=[row, row, vec, row], out_specs=[row, one],
                          out_shape=[_sds((s, d), F32), _sds((1, 1), F32)], scratch_shapes=[pltpu.VMEM((1, d), F32)],
                          compiler_params=_params(("arbitrary",)))(xres, y, g_post, target)


def _norm_bwd(xin, g, dout, dres, out_dtype, name):
    s, d = xin.shape
    ts = _tile(s, ROW_TILE, 8)
    has_res = dres is not None

    def body(*refs):
        x_ref, g_ref, do_ref = refs[:3]
        dx_ref, dg_ref = refs[-2:]
        i = pl.program_id(0)
        xv, dov = x_ref[...], do_ref[...]
        r = _rstd(xv)
        xhat = xv * r
        dg = jnp.sum(dov * xhat, axis=0, keepdims=True)
        dxh = dov * g_ref[...]
        dx = r * (dxh - xhat * jnp.mean(dxh * xhat, axis=-1, keepdims=True))
        if has_res:
            dx = dx + refs[3][...]
        dx_ref[...] = dx.astype(dx_ref.dtype)

        @pl.when(i == 0)
        def _():
            dg_ref[...] = dg

        @pl.when(i > 0)
        def _():
            dg_ref[...] += dg

    row = pl.BlockSpec((ts, d), lambda i: (i, 0))
    vec = pl.BlockSpec((1, d), lambda i: (0, 0))
    ops = [xin, g, dout] + ([dres] if has_res else [])
    return pl.pallas_call(body, name=name, grid=(s // ts,), in_specs=[row, vec, row] + ([row] if has_res else []),
                          out_specs=[row, vec], out_shape=[_sds((s, d), out_dtype), _sds((1, d), F32)],
                          compiler_params=_params(("arbitrary",)))(*ops)


class _Cols:
    def __init__(self, d):
        self.d = d
        self.kv = d // GROUP
        self.cb, self.cc, self.ci, self.q = 0, d, 2 * d, 3 * d
        self.k = 4 * d
        self.v = 4 * d + self.kv
        self.ga = 4 * d + 2 * self.kv
        self.gb = 5 * d + 2 * self.kv
        self.width = 6 * d + 2 * self.kv


CONV_COLS = 128


def _shift_rows(u, down):
    s = u.shape[0]
    rows = lax.broadcasted_iota(jnp.int32, u.shape, 0)
    if down:
        return jnp.where(rows == 0, 0.0, pltpu.roll(u, 1, 0))
    return jnp.where(rows == s - 1, 0.0, pltpu.roll(u, s - 1, 0))


def _conv_fwd(z, w, cols):
    s, d = z.shape[0], cols.d
    cw = CONV_COLS

    def body(cb_ref, cc_ref, ci_ref, w_ref, a_ref):
        u = cc_ref[...].astype(F32) * ci_ref[...].astype(F32)
        wv = w_ref[...]
        conv = wv[0:1] * _shift_rows(u, True) + wv[1:2] * u + wv[2:3] * _shift_rows(u, False)
        a_ref[...] = (cb_ref[...].astype(F32) * conv).astype(a_ref.dtype)

    def zspec(off):
        return pl.BlockSpec((s, cw), lambda j: (0, off // cw + j))

    return pl.pallas_call(body, name="conv_fwd", grid=(d // cw,),
                          in_specs=[zspec(cols.cb), zspec(cols.cc), zspec(cols.ci), pl.BlockSpec((3, cw), lambda j: (0, j))],
                          out_specs=pl.BlockSpec((s, cw), lambda j: (0, j)), out_shape=_sds((s, d), BF16),
                          compiler_params=_params(("parallel",)))(z, z, z, w)


def _conv_bwd(z, w, da, cols):
    s, d = z.shape[0], cols.d
    cw = CONV_COLS

    def body(cb_ref, cc_ref, ci_ref, w_ref, da_ref, dcb_ref, dcc_ref, dci_ref, dw_ref):
        cb, cc, ci, dav = cb_ref[...].astype(F32), cc_ref[...].astype(F32), ci_ref[...].astype(F32), da_ref[...]
        wv = w_ref[...]
        u = cc * ci
        um, up = _shift_rows(u, True), _shift_rows(u, False)
        conv = wv[0:1] * um + wv[1:2] * u + wv[2:3] * up
        dcb_ref[...] = (dav * conv).astype(dcb_ref.dtype)
        dconv = dav * cb
        dw_ref[0:1, :] = jnp.sum(dconv * um, axis=0, keepdims=True)
        dw_ref[1:2, :] = jnp.sum(dconv * u, axis=0, keepdims=True)
        dw_ref[2:3, :] = jnp.sum(dconv * up, axis=0, keepdims=True)
        du = wv[0:1] * _shift_rows(dconv, False) + wv[1:2] * dconv + wv[2:3] * _shift_rows(dconv, True)
        dcc_ref[...] = (du * ci).astype(dcc_ref.dtype)
        dci_ref[...] = (du * cc).astype(dci_ref.dtype)

    def zspec(off):
        return pl.BlockSpec((s, cw), lambda j: (0, off // cw + j))

    col = pl.BlockSpec((s, cw), lambda j: (0, j))
    wspec = pl.BlockSpec((3, cw), lambda j: (0, j))
    return pl.pallas_call(body, name="conv_bwd", grid=(d // cw,),
                          in_specs=[zspec(cols.cb), zspec(cols.cc), zspec(cols.ci), wspec, col],
                          out_specs=[col, col, col, wspec],
                          out_shape=[_sds((s, d), BF16)] * 3 + [_sds((3, d), F32)],
                          compiler_params=_params(("parallel",)))(z, z, z, w, da)


def _gate_fwd(z, bias, y_a, y_b, cols):
    s, d = y_a.shape
    ts, cw = _tile(s, ROW_TILE, 8), cols.kv
    nj = d // cw

    def body(ga_ref, gb_ref, ba_ref, bb_ref, ya_ref, yb_ref, o_ref):
        gate_a = jax.nn.sigmoid(ga_ref[...].astype(F32) + ba_ref[...])
        gate_b = jax.nn.sigmoid(gb_ref[...].astype(F32) + bb_ref[...])
        o_ref[...] = (gate_a * ya_ref[...] + gate_b * yb_ref[...]).astype(o_ref.dtype)

    tile = pl.BlockSpec((ts, cw), lambda i, j: (i, j))
    return pl.pallas_call(
        body, name="gate_fwd", grid=(s // ts, nj),
        in_specs=[pl.BlockSpec((ts, cw), lambda i, j: (i, cols.ga // cw + j)), pl.BlockSpec((ts, cw), lambda i, j: (i, cols.gb // cw + j)),
                  pl.BlockSpec((1, cw), lambda i, j: (0, j)), pl.BlockSpec((1, cw), lambda i, j: (0, nj + j)), tile, tile],
        out_specs=tile, out_shape=_sds((s, d), BF16), compiler_params=_params(("parallel", "parallel")))(z, z, bias, bias, y_a, y_b)


def _gate_bwd(z, bias, y_a, y_b, dmix, cols):
    s, d = y_a.shape
    ts, cw = _tile(s, ROW_TILE, 8), cols.kv
    nj = d // cw

    def body(ga_ref, gb_ref, ba_ref, bb_ref, ya_ref, yb_ref, dm_ref, dya_ref, dyb_ref, dga_ref, dgb_ref, dba_ref, dbb_ref):
        i = pl.program_id(1)
        gate_a = jax.nn.sigmoid(ga_ref[...].astype(F32) + ba_ref[...])
        gate_b = jax.nn.sigmoid(gb_ref[...].astype(F32) + bb_ref[...])
        dm = dm_ref[...]
        dya_ref[...] = (dm * gate_a).astype(dya_ref.dtype)
        dyb_ref[...] = (dm * gate_b).astype(dyb_ref.dtype)
        dga = dm * ya_ref[...] * (gate_a * (1.0 - gate_a))
        dgb = dm * yb_ref[...] * (gate_b * (1.0 - gate_b))
        dga_ref[...] = dga.astype(dga_ref.dtype)
        dgb_ref[...] = dgb.astype(dgb_ref.dtype)
        sa = jnp.sum(dga, axis=0, keepdims=True)
        sb = jnp.sum(dgb, axis=0, keepdims=True)

        @pl.when(i == 0)
        def _():
            dba_ref[...] = sa
            dbb_ref[...] = sb

        @pl.when(i > 0)
        def _():
            dba_ref[...] += sa
            dbb_ref[...] += sb

    tile = pl.BlockSpec((ts, cw), lambda j, i: (i, j))
    vec = pl.BlockSpec((1, cw), lambda j, i: (0, j))
    return pl.pallas_call(
        body, name="gate_bwd", grid=(nj, s // ts),
        in_specs=[pl.BlockSpec((ts, cw), lambda j, i: (i, cols.ga // cw + j)), pl.BlockSpec((ts, cw), lambda j, i: (i, cols.gb // cw + j)),
                  vec, pl.BlockSpec((1, cw), lambda j, i: (0, nj + j)), tile, tile, tile],
        out_specs=[tile, tile, tile, tile, vec, vec],
        out_shape=[_sds((s, d), BF16)] * 4 + [_sds((1, d), F32)] * 2,
        compiler_params=_params(("parallel", "arbitrary")))(z, z, bias, bias, y_a, y_b, dmix)


def _rope_tables(s):
    axis_dim = HEAD_DIM // 2
    n_freq = axis_dim // 2
    rows = s // GRID_W
    row_idx = jnp.repeat(jnp.arange(rows, dtype=jnp.int32), GRID_W)
    col_idx = jnp.tile(jnp.arange(GRID_W, dtype=jnp.int32), rows)
    inv_freq = ROPE_THETA ** (-jnp.arange(0, axis_dim, 2, dtype=F32) / axis_dim)
    ang = jnp.stack([row_idx.astype(F32)[:, None] * inv_freq, col_idx.astype(F32)[:, None] * inv_freq], axis=1)
    cos, sin = jnp.cos(ang), jnp.sin(ang)
    cos_t = jnp.stack([cos, cos], axis=2).reshape(s, HEAD_DIM)
    sin_t = jnp.stack([-sin, sin], axis=2).reshape(s, HEAD_DIM)
    return cos_t, sin_t


def _partner(x):
    n = x.shape[-1]
    lane = lax.broadcasted_iota(jnp.int32, x.shape, x.ndim - 1)
    quarter = HEAD_DIM // 4
    return jnp.where(lane % (2 * quarter) < quarter, pltpu.roll(x, n - quarter, x.ndim - 1), pltpu.roll(x, quarter, x.ndim - 1))


LOG2E = math.log2(math.e)
Q_SCALE = LOG2E / math.sqrt(HEAD_DIM)


def _rope_fwd(z, qn, kn, cos_t, sin_t, cols):
    s, d, kv = z.shape[0], cols.d, cols.kv
    ts = _tile(s, ROW_TILE, 8)
    scale = Q_SCALE

    def body(q_ref, k_ref, v_ref, qn_ref, kn_ref, c_ref, s_ref, qo_ref, ko_ref, vo_ref):
        c, sn = c_ref[...], s_ref[...]

        def head(xh, g):
            xn = xh * _rstd(xh) * g
            return xn * c + _partner(xn) * sn

        for h in range(d // HEAD_DIM):
            sl = slice(h * HEAD_DIM, (h + 1) * HEAD_DIM)
            qo_ref[:, sl] = (head(q_ref[:, sl].astype(F32), qn_ref[...]) * scale).astype(qo_ref.dtype)
        for h in range(kv // HEAD_DIM):
            sl = slice(h * HEAD_DIM, (h + 1) * HEAD_DIM)
            ko_ref[:, sl] = head(k_ref[:, sl].astype(F32), kn_ref[...]).astype(ko_ref.dtype)
        vo_ref[...] = v_ref[...].astype(vo_ref.dtype)

    vec = pl.BlockSpec((1, HEAD_DIM), lambda i: (0, 0))
    tab = pl.BlockSpec((ts, HEAD_DIM), lambda i: (i, 0))
    return pl.pallas_call(
        body, name="rope_fwd", grid=(s // ts,),
        in_specs=[pl.BlockSpec((ts, d), lambda i: (i, cols.q // d)), pl.BlockSpec((ts, kv), lambda i: (i, cols.k // kv)),
                  pl.BlockSpec((ts, kv), lambda i: (i, cols.v // kv)), vec, vec, tab, tab],
        out_specs=[pl.BlockSpec((ts, d), lambda i: (i, 0)), pl.BlockSpec((ts, kv), lambda i: (i, 0)), pl.BlockSpec((ts, kv), lambda i: (i, 0))],
        out_shape=[_sds((s, d), BF16), _sds((s, kv), BF16), _sds((s, kv), BF16)],
        compiler_params=_params(("parallel",)))(z, z, z, qn, kn, cos_t, sin_t)


def _rope_bwd(z, qn, kn, cos_t, sin_t, dq, dk_t, dv_t, cols):
    s, d, kv = z.shape[0], cols.d, cols.kv
    ts = _tile(s, ROW_TILE, 8)
    scale = 1.0 / math.sqrt(HEAD_DIM)

    def body(q_ref, k_ref, qn_ref, kn_ref, c_ref, s_ref, dq_ref, dkt_ref, dvt_ref, dzq_ref, dzk_ref, dzv_ref, dqn_ref, dkn_ref):
        i = pl.program_id(0)
        c, sn = c_ref[...], s_ref[...]
        dk_all = dkt_ref[...].T * (1.0 / LOG2E)

        def head_bwd(xh, g, drot):
            dxn = drot * c + _partner(drot * sn)
            r = _rstd(xh)
            xhat = xh * r
            dgain = jnp.sum(dxn * xhat, axis=0, keepdims=True)
            dxh = dxn * g
            return r * (dxh - xhat * jnp.mean(dxh * xhat, axis=-1, keepdims=True)), dgain

        dqn = jnp.zeros((1, HEAD_DIM), F32)
        for h in range(d // HEAD_DIM):
            sl = slice(h * HEAD_DIM, (h + 1) * HEAD_DIM)
            dx, dg = head_bwd(q_ref[:, sl].astype(F32), qn_ref[...], dq_ref[:, sl] * scale)
            dzq_ref[:, sl] = dx.astype(dzq_ref.dtype)
            dqn = dqn + dg
        dkn = jnp.zeros((1, HEAD_DIM), F32)
        for h in range(kv // HEAD_DIM):
            sl = slice(h * HEAD_DIM, (h + 1) * HEAD_DIM)
            dx, dg = head_bwd(k_ref[:, sl].astype(F32), kn_ref[...], dk_all[:, sl])
            dzk_ref[:, sl] = dx.astype(dzk_ref.dtype)
            dkn = dkn + dg
        dzv_ref[...] = dvt_ref[...].T.astype(dzv_ref.dtype)

        @pl.when(i == 0)
        def _():
            dqn_ref[...] = dqn
            dkn_ref[...] = dkn

        @pl.when(i > 0)
        def _():
            dqn_ref[...] += dqn
            dkn_ref[...] += dkn

    vec = pl.BlockSpec((1, HEAD_DIM), lambda i: (0, 0))
    tab = pl.BlockSpec((ts, HEAD_DIM), lambda i: (i, 0))
    qrow = pl.BlockSpec((ts, d), lambda i: (i, 0))
    krow = pl.BlockSpec((ts, kv), lambda i: (i, 0))
    kcol = pl.BlockSpec((kv, ts), lambda i: (0, i))
    return pl.pallas_call(
        body, name="rope_bwd", grid=(s // ts,),
        in_specs=[pl.BlockSpec((ts, d), lambda i: (i, cols.q // d)), pl.BlockSpec((ts, kv), lambda i: (i, cols.k // kv)),
                  vec, vec, tab, tab, qrow, kcol, kcol],
        out_specs=[qrow, krow, krow, vec, vec],
        out_shape=[_sds((s, d), BF16), _sds((s, kv), BF16), _sds((s, kv), BF16), _sds((1, HEAD_DIM), F32), _sds((1, HEAD_DIM), F32)],
        compiler_params=_params(("arbitrary",)))(z, z, qn, kn, cos_t, sin_t, dq, dk_t, dv_t)


Q_TILE = 256
_NT = (((1,), (1,)), ((), ()))
_NN = (((1,), (0,)), ((), ()))


def _attn_fwd(q, k, v):
    s, d = q.shape
    kvh = k.shape[1] // HEAD_DIM
    tq = _tile(s, Q_TILE, LANES)
    gw = GROUP * HEAD_DIM

    def body(q_ref, k_ref, v_ref, o_ref, lse_ref):
        kk, vv = k_ref[...], v_ref[...]
        for g in range(GROUP):
            sl = slice(g * HEAD_DIM, (g + 1) * HEAD_DIM)
            sc = lax.dot_general(q_ref[:, sl], kk, _NT, preferred_element_type=F32)
            mx = jnp.max(sc, axis=-1, keepdims=True)
            p = jnp.exp2(sc - mx)
            l = jnp.sum(p, axis=-1, keepdims=True)
            o = lax.dot_general(p.astype(BF16), vv, _NN, preferred_element_type=F32) * (1.0 / l)
            o_ref[:, sl] = o.astype(o_ref.dtype)
            lse_ref[:, g:g + 1] = mx + jnp.log(l) * LOG2E

    return pl.pallas_call(
        body, name="attn_fwd", grid=(kvh, s // tq),
        in_specs=[pl.BlockSpec((tq, gw), lambda j, i: (i, j)), pl.BlockSpec((s, HEAD_DIM), lambda j, i: (0, j)), pl.BlockSpec((s, HEAD_DIM), lambda j, i: (0, j))],
        out_specs=[pl.BlockSpec((tq, gw), lambda j, i: (i, j)), pl.BlockSpec((None, tq, GROUP), lambda j, i: (j, i, 0))],
        out_shape=[_sds((s, d), BF16), _sds((kvh, s, GROUP), F32)],
        compiler_params=_params(("parallel", "parallel")))(q, k, v)


_TN = (((0,), (0,)), ((), ()))


def _attn_bwd(q, k, v, do, lse):
    s, d = q.shape
    kv = k.shape[1]
    kvh = kv // HEAD_DIM
    tq = _tile(s, Q_TILE, LANES)
    gw = GROUP * HEAD_DIM

    def body(q_ref, k_ref, v_ref, do_ref, lse_ref, dq_ref, dkt_ref, dvt_ref):
        i = pl.program_id(1)
        kk, vv = k_ref[...], v_ref[...]

        @pl.when(i == 0)
        def _():
            dkt_ref[...] = jnp.zeros_like(dkt_ref)
            dvt_ref[...] = jnp.zeros_like(dvt_ref)

        for g in range(GROUP):
            sl = slice(g * HEAD_DIM, (g + 1) * HEAD_DIM)
            qg, dog = q_ref[:, sl], do_ref[:, sl]
            sc = lax.dot_general(qg, kk, _NT, preferred_element_type=F32)
            p = jnp.exp2(sc - lse_ref[:, g:g + 1])
            dp = lax.dot_general(dog, vv, _NT, preferred_element_type=F32)
            delta = jnp.sum(p * dp, axis=-1, keepdims=True)
            ds = (p * (dp - delta)).astype(BF16)
            dq_ref[:, sl] = lax.dot_general(ds, kk, _NN, preferred_element_type=F32)
            dkt_ref[...] += lax.dot_general(qg, ds, _TN, preferred_element_type=F32)
            dvt_ref[...] += lax.dot_general(dog, p.astype(BF16), _TN, preferred_element_type=F32)

    qspec = pl.BlockSpec((tq, gw), lambda j, i: (i, j))
    kspec = pl.BlockSpec((s, HEAD_DIM), lambda j, i: (0, j))
    stat = pl.BlockSpec((None, tq, GROUP), lambda j, i: (j, i, 0))
    tspec = pl.BlockSpec((HEAD_DIM, s), lambda j, i: (j, 0))
    return pl.pallas_call(
        body, name="attn_bwd", grid=(kvh, s // tq),
        in_specs=[qspec, kspec, kspec, qspec, stat], out_specs=[qspec, tspec, tspec],
        out_shape=[_sds((s, d), F32), _sds((kv, s), F32), _sds((kv, s), F32)],
        compiler_params=_params(("parallel", "arbitrary")))(q, k, v, do, lse)


ELEM_BLOCK_BYTES = 1 << 20

BIG = (("w_in", True), ("w_out_conv", False), ("w_out_attn", False), ("w_merge", False), ("w_up", True), ("w_down", False))
N_BIG = len(BIG)


def _elem_tiles(rows, width):
    tc = _tile(width, 2048, LANES)
    tr = _tile(rows, max(8, ELEM_BLOCK_BYTES // (4 * tc)), 8)
    return tr, tc


def _scalar_grid(grid, in_specs, out_specs):
    return pltpu.PrefetchScalarGridSpec(num_scalar_prefetch=1, grid=grid, in_specs=in_specs, out_specs=out_specs)


def _cast_place(w_stack, layer, chip, col_sharded, name):
    _, rows, width = w_stack.shape
    tr, tc = _elem_tiles(rows, width)
    nr, nc = rows // tr, width // tc

    def body(sc_ref, x_ref, o_ref):
        o_ref[...] = x_ref[...].astype(o_ref.dtype)

    if col_sharded:
        full, out_spec = (rows, width * N_CHIPS), pl.BlockSpec((tr, tc), lambda i, j, sc: (i, sc[0] * nc + j))
    else:
        full, out_spec = (rows * N_CHIPS, width), pl.BlockSpec((tr, tc), lambda i, j, sc: (sc[0] * nr + i, j))
    return pl.pallas_call(
        body, name=name,
        grid_spec=_scalar_grid((nr, nc), [pl.BlockSpec((None, tr, tc), lambda i, j, sc: (layer, i, j))], out_spec),
        out_shape=_sds(full, BF16), compiler_params=_params(("parallel", "parallel")))(chip, w_stack)


def _add_landed(acc, landed):
    return (acc + landed,)


def _slot_of_relation(rel):
    return jnp.where(rel == 2, 0, jnp.where(rel == 1, 1, 2))


def _sum_chips(pair_sum, landed, chip, col_sharded, name):
    _, rows, width = landed.shape
    tr, tc = _elem_tiles(rows, width)
    nr, nc = rows // tr, width // tc

    def body(chip_ref, own_ref, q_ref, o_ref):
        me = chip_ref[0]
        own = own_ref[...].astype(F32)
        acc = None
        for t in range(N_CHIPS):
            rel = me ^ t
            term = jnp.where(rel == 0, own, q_ref[_slot_of_relation(rel)].astype(F32))
            acc = term if acc is None else acc + term
        o_ref[...] = acc

    if col_sharded:
        own_spec = pl.BlockSpec((tr, tc), lambda i, j, c: (i, c[0] * nc + j))
    else:
        own_spec = pl.BlockSpec((tr, tc), lambda i, j, c: (c[0] * nr + i, j))
    return pl.pallas_call(
        body, name=name,
        grid_spec=_scalar_grid((nr, nc), [own_spec, pl.BlockSpec((N_CHIPS - 1, tr, tc), lambda i, j, c: (0, i, j))],
                               pl.BlockSpec((tr, tc), lambda i, j, c: (i, j))),
        out_shape=_sds((rows, width), F32), compiler_params=_params(("parallel", "parallel")))(chip, pair_sum, landed)


def _adamw_math(w, g, m, v):
    mn = ADAM_B1 * m + (1.0 - ADAM_B1) * g
    vn = ADAM_B2 * v + (1.0 - ADAM_B2) * jnp.square(g)
    m_hat = mn / (1.0 - ADAM_B1 ** ADAM_STEP)
    v_hat = vn / (1.0 - ADAM_B2 ** ADAM_STEP)
    return -ADAM_LR * (m_hat / (jnp.sqrt(v_hat) + ADAM_EPS) + ADAM_WD * w), mn, vn


def _adamw(w, g, m, v, name):
    shape = w.shape
    width = shape[-1]
    w2, g2, m2, v2 = (a.reshape(-1, width) for a in (w, g, m, v))
    rows = w2.shape[0]
    tr, tc = _elem_tiles(rows, width)

    def body(w_ref, g_ref, m_ref, v_ref, d_ref, nm_ref, nv_ref):
        d_ref[...], nm_ref[...], nv_ref[...] = _adamw_math(w_ref[...], g_ref[...], m_ref[...], v_ref[...])

    tile = pl.BlockSpec((tr, tc), lambda i, j: (i, j))
    outs = pl.pallas_call(body, name=name, grid=(rows // tr, width // tc), in_specs=[tile] * 4, out_specs=[tile] * 3,
                          out_shape=[_sds((rows, width), F32)] * 3, compiler_params=_params(("parallel", "parallel")))(w2, g2, m2, v2)
    return tuple(o.reshape(shape) for o in outs)


def _adamw_layer(w, m, v, g_mine, g_sibling, carried, layer, core, col_sharded, name):
    depth, rows, width = w.shape
    pr, pc = g_mine.shape
    tr, tc = _elem_tiles(pr, pc)
    n_half = pr // tr if col_sharded else pc // tc
    if carried is None:
        carried = tuple(lax.empty((depth, rows, width), F32) for _ in range(4))

    def body(sc_ref, w_ref, m_ref, v_ref, gm_ref, gs_ref, *rest):
        g_ref, d_ref, nm_ref, nv_ref = rest[-4:]
        pos = pl.program_id(0) if col_sharded else pl.program_id(1)
        gv = jnp.where(pos // n_half == sc_ref[0], gm_ref[...], gs_ref[...])
        g_ref[...] = gv
        d_ref[...], nm_ref[...], nv_ref[...] = _adamw_math(w_ref[...], gv, m_ref[...], v_ref[...])

    stacked = pl.BlockSpec((None, tr, tc), lambda i, j, sc: (layer, i, j))
    if col_sharded:
        half = pl.BlockSpec((tr, tc), lambda i, j, sc: (i % n_half, j))
    else:
        half = pl.BlockSpec((tr, tc), lambda i, j, sc: (i, j % n_half))
    return pl.pallas_call(
        body, name=name,
        grid_spec=_scalar_grid((rows // tr, width // tc), [stacked] * 3 + [half] * 2 + [ANY] * 4, [stacked] * 4),
        out_shape=[_sds((depth, rows, width), F32)] * 4, input_output_aliases={6: 0, 7: 1, 8: 2, 9: 3},
        compiler_params=_params(("parallel", "parallel")))(core, w, m, v, g_mine, g_sibling, *carried)


_SEM = pl.BlockSpec(memory_space=pltpu.SEMAPHORE)
_HBM = pl.BlockSpec(memory_space=pltpu.HBM)
_VMEM = pl.BlockSpec(memory_space=pltpu.VMEM)
_EFFECT = pltpu.SideEffectType.DATAFLOW_SIDE_EFFECTING


def _place():
    x, y, c = lax.axis_index("x"), lax.axis_index("y"), lax.axis_index("c")
    others = [(1 - x, y), (x, 1 - y), (1 - x, 1 - y)]
    return x, y, c, others


def _chip_index(px, py):
    return 2 * px + py


def _remote(src, dst, send_sems, recv_sems, k, to):
    return pltpu.make_async_remote_copy(src_ref=src, dst_ref=dst, send_sem=send_sems.at[k], recv_sem=recv_sems.at[k],
                                        device_id=to, device_id_type=MESH)


def _phase(name, bufs, waits, wait_fn, n_start, start_fn, deps):
    nb, nd = len(bufs), len(deps)

    def body(*refs):
        buf_refs = refs[:nb]
        pos = nb
        if waits is not None:
            wait_fn(buf_refs, refs[pos], refs[pos + 1])
            pos += 2
        pos += nd
        if n_start:
            start_fn(buf_refs, refs[pos], refs[pos + 1])
            pos += 2
        token = refs[pos + nb]
        token[...] = jnp.zeros_like(token)

    n_sem_out = 2 if n_start else 0
    if waits is None:
        bufs = [pltpu.with_memory_space_constraint(b, pltpu.HBM) for b in bufs]
    outs = pl.pallas_call(
        body, name=name,
        in_specs=[_HBM] * nb + ([_SEM] * 2 if waits is not None else []) + [ANY] * nd,
        out_specs=[_SEM] * n_sem_out + [_HBM] * nb + [_VMEM],
        out_shape=[pltpu.SemaphoreType.DMA((n_start,))] * n_sem_out + [pltpu.HBM(b.shape, b.dtype) for b in bufs] + [_sds((8, LANES), F32)],
        input_output_aliases={i: n_sem_out + i for i in range(nb)},
        compiler_params=pltpu.CompilerParams(has_side_effects=_EFFECT),
    )(*bufs, *(waits if waits is not None else ()), *deps)
    sems = tuple(outs[:2]) if n_start else None
    return sems, list(outs[n_sem_out:n_sem_out + nb]), outs[-1]


class _Chains:
    def __init__(self):
        self.active = []
        self.last = None
        self.dep = None
        self.pinned = []
        self.token = None
        self.at = None

    def phase(self, name, bufs, waits, wait_fn, n_start, start_fn):
        deps = [a for a in (self.last, self.dep) if a is not None] + self.pinned
        sems, thru, token = _phase(name, bufs, waits, wait_fn, n_start, start_fn, deps)
        self.last, self.dep, self.token, self.pinned = token, None, token, []
        return sems, thru

    def pin(self, result):
        self.pinned.append(result)

    def add(self, gen):
        self.active.append(gen)

    def tick(self, dep, at=None):
        self.at, self.token, self.dep = at, None, dep
        for gen in list(self.active):
            if next(gen, "done") == "done":
                self.active.remove(gen)
        return self.token


def _after(small, token):
    return small if token is None else small + token[0, 0]


def _shard_region(ref, col_sharded, chip, n_shard):
    start = pl.multiple_of(chip * n_shard, LANES if col_sharded else 8)
    if col_sharded:
        return ref.at[:, pl.ds(start, n_shard)]
    return ref.at[pl.ds(start, n_shard), :]


def _row_half(ref, half):
    n_rows = ref.shape[0]
    return ref.at[pl.ds(pl.multiple_of(half * (n_rows // 2), 8), n_rows // 2), :]


def _gather_chain(chains, tag, group, bufs, out, wait_for):
    n_w = len(group)
    n = 3 * n_w

    def region(refs, a, chip, half):
        col = BIG[group[a]][1]
        n_shard = refs[a].shape[1] // N_CHIPS if col else refs[a].shape[0] // N_CHIPS
        return _row_half(_shard_region(refs[a], col, chip, n_shard), half)

    def start_ici(refs, send, recv):
        x, y, c, others = _place()
        for a in range(n_w):
            mine = region(refs, a, _chip_index(x, y), c)
            for j, (ox, oy) in enumerate(others):
                _remote(mine, mine, send, recv, 3 * a + j, (ox, oy, c)).start()

    def wait_ici(refs, send, recv):
        x, y, c, others = _place()
        for a in range(n_w):
            for j, (ox, oy) in enumerate(others):
                landed = region(refs, a, _chip_index(ox, oy), c)
                cp = _remote(landed, landed, send, recv, 3 * a + j, (x, y, 1 - c))
                cp.wait_recv()
                cp.wait_send()

    def start_d2d(refs, send, recv):
        x, y, c, others = _place()
        for a in range(n_w):
            for j, (ox, oy) in enumerate(others):
                landed = region(refs, a, _chip_index(ox, oy), c)
                _remote(landed, landed, send, recv, 3 * a + j, (x, y, 1 - c)).start()

    def wait_d2d(refs, send, recv):
        x, y, c, others = _place()
        for a in range(n_w):
            for j, (ox, oy) in enumerate(others):
                theirs = region(refs, a, _chip_index(ox, oy), 1 - c)
                cp = _remote(theirs, theirs, send, recv, 3 * a + j, (x, y, 1 - c))
                cp.wait_recv()
                cp.wait_send()

    sems, bufs = chains.phase(f"gather_ici_start_{tag}", bufs, None, None, n, start_ici)
    yield
    while wait_for is not None and chains.at != wait_for:
        yield
    sems, bufs = chains.phase(f"gather_forward_{tag}", bufs, sems, wait_ici, n, start_d2d)
    yield
    _, bufs = chains.phase(f"gather_done_{tag}", bufs, sems, wait_d2d, 0, None)
    for a, buf in zip(group, bufs):
        out[BIG[a][0]] = buf


def _grad_chain(chains, layer, group, pairs, core, other_core, chip, w, mom, var, carried):
    n = len(group)
    tag = f"{layer}_{group[0]}"
    kinds = [BIG[a][1] for a in group]
    names = [BIG[a][0] for a in group]
    sibling_of = lambda x, y, c: (x, y, 1 - c)

    def pair_start(refs, send, recv):
        x, y, c, _ = _place()
        for i in range(n):
            _remote(refs[i], refs[n + i], send, recv, i, sibling_of(x, y, c)).start()

    def pair_wait(refs, send, recv):
        x, y, c, _ = _place()
        for i in range(n):
            cp = _remote(refs[i], refs[n + i], send, recv, i, sibling_of(x, y, c))
            cp.wait_recv()
            cp.wait_send()

    def piece(ref, col, chip_idx):
        return _shard_region(ref, col, chip_idx, ref.shape[1] // N_CHIPS if col else ref.shape[0] // N_CHIPS)

    def scatter_start(refs, send, recv):
        x, y, c, others = _place()
        for i in range(n):
            for j, (ox, oy) in enumerate(others):
                _remote(piece(refs[i], kinds[i], _chip_index(ox, oy)), refs[n + i].at[j], send, recv, 3 * i + j, (ox, oy, c)).start()

    def scatter_wait(refs, send, recv):
        x, y, c, others = _place()
        for i in range(n):
            for j, (ox, oy) in enumerate(others):
                cp = _remote(piece(refs[i], kinds[i], _chip_index(ox, oy)), refs[n + i].at[j], send, recv, 3 * i + j, (ox, oy, c))
                cp.wait_recv()
                cp.wait_send()

    def join_start(refs, send, recv):
        x, y, c, _ = _place()
        for i in range(n):
            _remote(refs[i], refs[n + i], send, recv, i, sibling_of(x, y, c)).start()

    def join_wait(refs, send, recv):
        x, y, c, _ = _place()
        for i in range(n):
            cp = _remote(refs[i], refs[n + i], send, recv, i, sibling_of(x, y, c))
            cp.wait_recv()
            cp.wait_send()

    sides = ["m" if col else "n" for col in kinds]
    sends = [_mm(a, b, "tn", [F32], name="mm_g_send_" + names[i], half=(other_core, sides[i])) for i, (a, b) in enumerate(pairs)]
    half_shapes = [g.shape for g in sends]
    lands = [lax.empty(s, F32) for s in half_shapes]
    sems, bufs = chains.phase(f"pair_start_{tag}", sends + lands, None, None, n, pair_start)
    yield
    _, bufs = chains.phase(f"pair_wait_{tag}", bufs, sems, pair_wait, 0, None)
    pair_sums = [_mm(a, b, "tn", [BF16], name="mm_g_keep_" + names[i], half=(core, sides[i]), epilogue=_add_landed, extras=(bufs[n + i],))
                 for i, (a, b) in enumerate(pairs)]
    piece_shapes = [(s[0], s[1] // N_CHIPS) if col else (s[0] // N_CHIPS, s[1]) for col, s in zip(kinds, half_shapes)]
    slots = [lax.empty((N_CHIPS - 1, *s), BF16) for s in piece_shapes]
    sems, bufs = chains.phase(f"scatter_start_{tag}", pair_sums + slots, None, None, 3 * n, scatter_start)
    yield
    yield
    _, bufs = chains.phase(f"scatter_wait_{tag}", bufs, sems, scatter_wait, 0, None)
    reduced = [_sum_chips(bufs[i], bufs[n + i], chip, kinds[i], "sum_chips_" + names[i]) for i in range(n)]
    theirs = [lax.empty(s, F32) for s in piece_shapes]
    sems, bufs = chains.phase(f"join_start_{tag}", reduced + theirs, None, None, n, join_start)
    yield
    _, bufs = chains.phase(f"join_wait_{tag}", bufs, sems, join_wait, 0, None)
    for i in range(n):
        carried[names[i]] = _adamw_layer(w[names[i]], mom[names[i]], var[names[i]], bufs[i], bufs[n + i], carried.get(names[i]),
                                         layer, core, kinds[i], "adamw_" + names[i])
        chains.pin(carried[names[i]][0])


def _allreduce_small(vec, name):
    rows = vec.shape[0]
    masks = [(dx, dy, dc) for dx in (0, 1) for dy in (0, 1) for dc in (0, 1)][1:]

    def body(v_ref, o_ref, gather_ref, send_sems, recv_sems):
        x, y, c, _ = _place()
        me = 4 * x + 2 * y + c
        gather_ref[me] = v_ref[...]
        copies = []
        for k, (dx, dy, dc) in enumerate(masks):
            peer = (x ^ dx, y ^ dy, c ^ dc)
            copies.append(_remote(v_ref, gather_ref.at[me], send_sems, recv_sems, k, peer))
        for cp in copies:
            cp.start()
        for k, (dx, dy, dc) in enumerate(masks):
            slot = gather_ref.at[4 * (x ^ dx) + 2 * (y ^ dy) + (c ^ dc)]
            _remote(slot, slot, send_sems, recv_sems, k, (x, y, c)).wait_recv()
        for cp in copies:
            cp.wait_send()
        acc = gather_ref[0]
        for dev in range(1, N_DEV):
            acc = acc + gather_ref[dev]
        o_ref[...] = acc

    return pl.pallas_call(
        body, name=name, in_specs=[_VMEM], out_specs=_VMEM, out_shape=_sds((rows, LANES), F32),
        scratch_shapes=[pltpu.VMEM((N_DEV, rows, LANES), F32), pltpu.SemaphoreType.DMA((N_DEV - 1,)), pltpu.SemaphoreType.DMA((N_DEV - 1,))],
        compiler_params=pltpu.CompilerParams(has_side_effects=True, vmem_limit_bytes=VMEM_LIMIT_BYTES),
    )(vec)


def _relu2(acc):
    r = jnp.maximum(acc, 0.0)
    return acc, r * r


def _relu2_bwd(acc, up):
    return (acc * (2.0 * jnp.maximum(up.astype(F32), 0.0)),)


def _layer_fwd(x, h, w, cols, tables, tick, last, target=None):
    cos_t, sin_t = tables
    z = _mm(h, w("w_in"), "nn", [BF16], name="mm_in")
    a = _conv_fwd(z, w("conv_w"), cols)
    q, k, v = _rope_fwd(z, _after(w("q_norm"), tick(a, 0)), w("k_norm"), cos_t, sin_t, cols)
    o, lse = _attn_fwd(q, k, v)
    tick(o, 1)
    tick(None, 2)
    y_a = _mm(a, w("w_out_conv"), "nn", [F32], name="mm_out_conv")
    y_b = _mm(o, w("w_out_attn"), "nn", [F32], name="mm_out_attn")
    mix = _gate_fwd(z, w("gate_bias"), y_a, y_b, cols)
    mixed = _mm(mix, w("w_merge"), "nn", [F32], name="mm_merge")
    x1, h2 = _resid_norm(x, mixed, _after(w("norm_mix_post"), tick(mixed, 3)), w("norm_mlp_pre"))
    up, act = _mm(h2, w("w_up"), "nn", [BF16, BF16], name="mm_up", epilogue=_relu2)
    f = _mm(act, w("w_down"), "nn", [F32], name="mm_down")
    kept = dict(x=x, h=h, z=z, a=a, q=q, k=k, v=v, o=o, lse=lse, y_a=y_a, y_b=y_b, mix=mix, mixed=mixed, x1=x1, h2=h2, up=up, act=act, f=f)
    g_post = _after(w("norm_mlp_post"), tick(f, 4))
    if last:
        return _resid_norm_loss(x1, f, g_post, target), kept
    return _resid_norm(x1, f, g_post, w("norm_next")), kept


def _layer_bwd(dx_out, w, kept, cols, tables, tick, emit):
    cos_t, sin_t = tables
    t = kept
    df, d_norm_mlp_post = _norm_bwd(t["f"], _after(w["norm_mlp_post"], tick(dx_out, 0)), dx_out, None, BF16, "norm_bwd_mlp_post")
    dup = _mm(df, w["w_down"], "nt", [BF16], name="mm_d_act", epilogue=_relu2_bwd, extras=(t["up"],))
    emit((4, 5), [(t["h2"], dup), (t["act"], df)])
    dh2 = _mm(dup, w["w_up"], "nt", [F32], name="mm_d_h2")
    dx1, d_norm_mlp_pre = _norm_bwd(t["x1"], _after(w["norm_mlp_pre"], tick(dh2, 1)), dh2, dx_out, F32, "norm_bwd_mlp_pre")
    dmixed, d_norm_mix_post = _norm_bwd(t["mixed"], w["norm_mix_post"], dx1, None, BF16, "norm_bwd_mix_post")
    dmix = _mm(dmixed, w["w_merge"], "nt", [F32], name="mm_d_mix")
    dy_a, dy_b, dz_ga, dz_gb, dbias_a, dbias_b = _gate_bwd(t["z"], _after(w["gate_bias"], tick(dmix, 2)), t["y_a"], t["y_b"], dmix, cols)
    da = _mm(dy_a, w["w_out_conv"], "nt", [F32], name="mm_d_a")
    emit((1, 2, 3), [(t["a"], dy_a), (t["o"], dy_b), (t["mix"], dmixed)])
    do = _mm(dy_b, w["w_out_attn"], "nt", [BF16], name="mm_d_o")
    dz_cb, dz_cc, dz_ci, d_conv_w = _conv_bwd(t["z"], _after(w["conv_w"], tick(do, 3)), da, cols)
    dq, dk_t, dv_t = _attn_bwd(t["q"], t["k"], t["v"], do, t["lse"])
    dz_q, dz_k, dz_v, d_q_norm, d_k_norm = _rope_bwd(t["z"], _after(w["q_norm"], tick(dv_t, 4)), w["k_norm"], cos_t, sin_t, dq, dk_t, dv_t, cols)
    dz = jnp.concatenate([dz_cb, dz_cc, dz_ci, dz_q, dz_k, dz_v, dz_ga, dz_gb], axis=1)
    emit((0,), [(t["h"], dz)])
    tick(dz, 5)
    dh = _mm(dz, w["w_in"], "nt", [F32], name="mm_d_h", tk=3328)
    dx_in, d_norm_mix_pre = _norm_bwd(t["x"], _after(w["norm_mix_pre"], tick(dh, 6)), dh, dx1, F32, "norm_bwd_mix_pre")
    small = dict(norm_mix_pre=d_norm_mix_pre, gate_bias=jnp.concatenate([dbias_a, dbias_b], axis=1), conv_w=d_conv_w, q_norm=d_q_norm,
                 k_norm=d_k_norm, norm_mix_post=d_norm_mix_post, norm_mlp_pre=d_norm_mlp_pre, norm_mlp_post=d_norm_mlp_post)
    return dx_in, small


SMALL = ("norm_mix_pre", "gate_bias", "conv_w", "q_norm", "k_norm", "norm_mix_post", "norm_mlp_pre", "norm_mlp_post")
WEIGHTS = ("norm_mix_pre", "w_in", "gate_bias", "conv_w", "q_norm", "k_norm", "w_out_conv", "w_out_attn", "w_merge",
           "norm_mix_post", "norm_mlp_pre", "w_up", "w_down", "norm_mlp_post")


def _pack(parts):
    flat = jnp.concatenate([a.reshape(-1) for a in parts])
    rows = -(-flat.shape[0] // LANES)
    pad = (-rows) % 8
    flat = jnp.pad(flat, (0, (rows + pad) * LANES - flat.shape[0]))
    return flat.reshape(rows + pad, LANES)


def _unpack(packed, shapes):
    flat = packed.reshape(-1)
    out, off = [], 0
    for shp in shapes:
        n = math.prod(shp)
        out.append(flat[off:off + n].reshape(shp))
        off += n
    return out


def kernel(x, norm_mix_pre, w_in, gate_bias, conv_w, q_norm, k_norm, w_out_conv, w_out_attn, w_merge, norm_mix_post, norm_mlp_pre, w_up, w_down, norm_mlp_post, loss_target, m_norm_mix_pre, m_w_in, m_gate_bias, m_conv_w, m_q_norm, m_k_norm, m_w_out_conv, m_w_out_attn, m_w_merge, m_norm_mix_post, m_norm_mlp_pre, m_w_up, m_w_down, m_norm_mlp_post, v_norm_mix_pre, v_w_in, v_gate_bias, v_conv_w, v_q_norm, v_k_norm, v_w_out_conv, v_w_out_attn, v_w_merge, v_norm_mix_post, v_norm_mlp_pre, v_w_up, v_w_down, v_norm_mlp_post):
    w = dict(norm_mix_pre=norm_mix_pre, w_in=w_in, gate_bias=gate_bias, conv_w=conv_w, q_norm=q_norm, k_norm=k_norm, w_out_conv=w_out_conv,
             w_out_attn=w_out_attn, w_merge=w_merge, norm_mix_post=norm_mix_post, norm_mlp_pre=norm_mlp_pre, w_up=w_up, w_down=w_down,
             norm_mlp_post=norm_mlp_post)
    mom = dict(norm_mix_pre=m_norm_mix_pre, w_in=m_w_in, gate_bias=m_gate_bias, conv_w=m_conv_w, q_norm=m_q_norm, k_norm=m_k_norm,
               w_out_conv=m_w_out_conv, w_out_attn=m_w_out_attn, w_merge=m_w_merge, norm_mix_post=m_norm_mix_post, norm_mlp_pre=m_norm_mlp_pre,
               w_up=m_w_up, w_down=m_w_down, norm_mlp_post=m_norm_mlp_post)
    var = dict(norm_mix_pre=v_norm_mix_pre, w_in=v_w_in, gate_bias=v_gate_bias, conv_w=v_conv_w, q_norm=v_q_norm, k_norm=v_k_norm,
               w_out_conv=v_w_out_conv, w_out_attn=v_w_out_attn, w_merge=v_w_merge, norm_mix_post=v_norm_mix_post, norm_mlp_pre=v_norm_mlp_pre,
               w_up=v_w_up, w_down=v_w_down, norm_mlp_post=v_norm_mlp_post)
    depth = w_in.shape[0]
    _, s, d = x.shape
    cols = _Cols(d)
    x0 = x.reshape(s, d)
    target = loss_target.reshape(s, d)
    tables = _rope_tables(s)
    chip = (2 * lax.axis_index("x") + lax.axis_index("y")).astype(jnp.int32)
    core = lax.axis_index("c").astype(jnp.int32)
    chip_vec, core_vec, other_core_vec = chip.reshape(1), core.reshape(1), (1 - core).reshape(1)

    n_conv = conv_w.shape[-1]
    placed = lax.dynamic_update_slice_in_dim(jnp.zeros((depth, conv_w.shape[1], n_conv * N_CHIPS), F32), conv_w, chip * n_conv, axis=2)
    conv_full = _unpack(_allreduce_small(_pack([jnp.where(core == 0, placed, 0.0)]), "gather_conv_w"), [placed.shape])[0]

    chains = _Chains()
    chains.pin(conv_full)
    full = [{} for _ in range(depth)]
    everything = tuple(range(N_BIG))
    gathers = []
    for l in range(depth):
        bufs = [_cast_place(w[name], l, chip_vec, col, "cast_place_" + name) for name, col in BIG]
        for group in ((0,), everything[1:]):
            if group[0] == 0:
                wait_for = ("fwd", l - 1, 3) if l else None
            else:
                wait_for = ("fwd", l, 0 if l else 1)
            gathers.append(_gather_chain(chains, f"{l}_{group[0]}", group, [bufs[a] for a in group], full[l], wait_for))
            next(gathers[-1])
    next(gathers[0])
    next(gathers[0], None)
    for g in gathers[1:]:
        chains.add(g)

    def layer_params(l):
        p = dict(full[l])
        p["conv_w"] = conv_full[l]
        p["gate_bias"] = gate_bias[l].reshape(1, -1)
        for name in ("norm_mix_pre", "q_norm", "k_norm", "norm_mix_post", "norm_mlp_pre", "norm_mlp_post"):
            p[name] = w[name][l].reshape(1, -1)
        if l + 1 < depth:
            p["norm_next"] = w["norm_mix_pre"][l + 1].reshape(1, -1)
        return p

    kept = []
    xl, h = x0, _norm_first(x0, norm_mix_pre[0].reshape(1, -1))
    for l in range(depth):
        last = l == depth - 1
        (xl, h), t = _layer_fwd(xl, h, lambda name, l=l: layer_params(l)[name], cols, tables,
                                lambda dep, k, l=l: chains.tick(dep, ("fwd", l, k)), last, target if last else None)
        kept.append(t)
    dx, loss_local = xl, h

    carried = {}
    small = [None] * depth
    for l in reversed(range(depth)):
        def emit(group, pairs, l=l):
            chains.add(_grad_chain(chains, l, group, pairs, core_vec, other_core_vec, chip_vec, w, mom, var, carried))

        dx, small[l] = _layer_bwd(dx, layer_params(l), kept[l], cols, tables, lambda dep, k: chains.tick(dep), emit)
    while chains.active:
        chains.tick(None)
    grads = {name: carried[name][0] for name, _ in BIG}
    delta = {name: carried[name][1] for name, _ in BIG}
    new_m = {name: carried[name][2] for name, _ in BIG}
    new_v = {name: carried[name][3] for name, _ in BIG}

    small_full_shapes = [(depth,) + small[0][name].shape for name in SMALL]
    packed = _pack([jnp.stack([small[l][name] for l in range(depth)]) for name in SMALL] + [jnp.broadcast_to(loss_local.reshape(1), (LANES,))])
    small_sum = _unpack(_allreduce_small(packed, "allreduce_small"), small_full_shapes + [(LANES,)])
    loss = small_sum[-1][0]
    for name, g in zip(SMALL, small_sum[:-1]):
        if name == "conv_w":
            g = lax.dynamic_slice_in_dim(g, chip * n_conv, n_conv, axis=2)
        grads[name] = g.reshape(w[name].shape)
    small_shapes = [w[name].shape for name in SMALL]
    packs = [_pack([src[name] for name in SMALL]) for src in (w, grads, mom, var)]
    for dst, out in zip((delta, new_m, new_v), _adamw(*packs, "adamw_small")):
        for name, val in zip(SMALL, _unpack(out, small_shapes)):
            dst[name] = val

    grad_x = dx.reshape(x.shape)
    return (loss, grad_x, *[grads[n] for n in WEIGHTS], *[delta[n] for n in WEIGHTS], *[new_m[n] for n in WEIGHTS], *[new_v[n] for n in WEIGHTS])
```

```python
import math

import jax
import jax.numpy as jnp
from jax import lax
from jax.experimental import pallas as pl
from jax.experimental.pallas import tpu as pltpu

F32 = jnp.float32
BF16 = jnp.bfloat16

HEAD_DIM = 128
GROUP = 4
GRID_W = 64
ROPE_THETA = 10000.0
RMS_EPS = 1e-6
ADAM_LR = 0.001
ADAM_B1 = 0.9
ADAM_B2 = 0.999
ADAM_EPS = 1e-08
ADAM_WD = 0.01
ADAM_STEP = 10

LANES = 128
N_CHIPS = 4
N_DEV = 8
VMEM_LIMIT_BYTES = 56 * 1024 * 1024
MESH = pl.DeviceIdType.MESH
ANY = pl.BlockSpec(memory_space=pl.ANY)


def _tile(dim, cap, mult):
    if dim <= cap:
        return dim
    t = (cap // mult) * mult
    while t >= mult:
        if dim % t == 0:
            return t
        t -= mult
    raise ValueError(f"no tile for {dim} under {cap} in multiples of {mult}")


def _params(sem=None):
    return pltpu.CompilerParams(dimension_semantics=sem, vmem_limit_bytes=VMEM_LIMIT_BYTES)


def _sds(shape, dtype):
    return jax.ShapeDtypeStruct(tuple(shape), dtype)


def _rstd(x):
    return lax.rsqrt(jnp.mean(x * x, axis=-1, keepdims=True) + RMS_EPS)


_DOT_DIMS = {"nn": ((1,), (0,)), "nt": ((1,), (1,)), "tn": ((0,), (0,))}


def _mm(a, b, mode, out_dtypes, *, name, epilogue=None, extras=(), tm=1024, tn=1024, tk=2048, half=None):
    if mode == "nn":
        (m, k), (k2, n) = a.shape, b.shape
    elif mode == "nt":
        (m, k), (n, k2) = a.shape, b.shape
    else:
        (k, m), (k2, n) = a.shape, b.shape
    assert k == k2, (a.shape, b.shape, mode)
    side = half[1] if half is not None else None
    if side == "m":
        m //= 2
    elif side == "n":
        n //= 2
    tm, tn, tk = _tile(m, tm, 8), _tile(n, tn, LANES), _tile(k, tk, LANES)
    nk = k // tk
    row = (lambda i, s: i + s[0][0] * (m // tm)) if side == "m" else (lambda i, s: i)
    col = (lambda j, s: j + s[0][0] * (n // tn)) if side == "n" else (lambda j, s: j)
    a_spec = pl.BlockSpec((tk, tm), lambda i, j, kk, *s: (kk, row(i, s))) if mode == "tn" else pl.BlockSpec((tm, tk), lambda i, j, kk, *s: (row(i, s), kk))
    b_spec = pl.BlockSpec((tn, tk), lambda i, j, kk, *s: (col(j, s), kk)) if mode == "nt" else pl.BlockSpec((tk, tn), lambda i, j, kk, *s: (kk, col(j, s)))
    tile_spec = pl.BlockSpec((tm, tn), lambda i, j, kk, *s: (i, j))
    n_extra, n_out = len(extras), len(out_dtypes)
    dims = (_DOT_DIMS[mode], ((), ()))

    def body(*refs):
        if half is not None:
            refs = refs[1:]
        a_ref, b_ref = refs[:2]
        extra_refs = refs[2:2 + n_extra]
        out_refs = refs[2 + n_extra:2 + n_extra + n_out]
        part = lax.dot_general(a_ref[...].astype(BF16), b_ref[...].astype(BF16), dims, preferred_element_type=F32)

        def finish(total):
            res = epilogue(total, *[e[...] for e in extra_refs]) if epilogue is not None else (total,)
            for o, r in zip(out_refs, res):
                o[...] = r.astype(o.dtype)

        if nk == 1:
            finish(part)
        else:
            acc = refs[-1]
            kk = pl.program_id(2)

            @pl.when(kk == 0)
            def _():
                acc[...] = part

            @pl.when(kk > 0)
            def _():
                acc[...] += part

            @pl.when(kk == nk - 1)
            def _():
                finish(acc[...])

    grid = (m // tm, n // tn, nk)
    in_specs, out_specs = [a_spec, b_spec] + [tile_spec] * n_extra, [tile_spec] * n_out
    scratch = [pltpu.VMEM((tm, tn), F32)] if nk > 1 else []
    if half is None:
        layout, lead = dict(grid=grid, in_specs=in_specs, out_specs=out_specs, scratch_shapes=scratch), ()
    else:
        layout = dict(grid_spec=pltpu.PrefetchScalarGridSpec(num_scalar_prefetch=1, grid=grid, in_specs=in_specs, out_specs=out_specs,
                                                             scratch_shapes=scratch))
        lead = (half[0],)
    outs = pl.pallas_call(body, name=name, out_shape=[_sds((m, n), d) for d in out_dtypes],
                          compiler_params=_params(("parallel", "parallel", "arbitrary")), **layout)(*lead, a, b, *extras)
    return outs if n_out > 1 else outs[0]


ROW_TILE = 256


def _norm_first(x, g):
    s, d = x.shape
    ts = _tile(s, ROW_TILE, 8)

    def body(x_ref, g_ref, h_ref):
        xv = x_ref[...]
        h_ref[...] = (xv * _rstd(xv) * g_ref[...]).astype(h_ref.dtype)

    row = pl.BlockSpec((ts, d), lambda i: (i, 0))
    vec = pl.BlockSpec((1, d), lambda i: (0, 0))
    return pl.pallas_call(body, name="norm_first", grid=(s // ts,), in_specs=[row, vec], out_specs=row,
                          out_shape=_sds((s, d), BF16), compiler_params=_params(("parallel",)))(x, g)


def _resid_norm(xres, y, g_post, g_next):
    s, d = xres.shape
    ts = _tile(s, ROW_TILE, 8)

    def body(x_ref, y_ref, gp_ref, gn_ref, xn_ref, hn_ref):
        yv = y_ref[...].astype(F32)
        xn = x_ref[...] + yv * _rstd(yv) * gp_ref[...]
        xn_ref[...] = xn
        hn_ref[...] = (xn * _rstd(xn) * gn_ref[...]).astype(hn_ref.dtype)

    row = pl.BlockSpec((ts, d), lambda i: (i, 0))
    vec = pl.BlockSpec((1, d), lambda i: (0, 0))
    return pl.pallas_call(body, name="resid_norm", grid=(s // ts,), in_specs=[row, row, vec, vec], out_specs=[row, row],
                          out_shape=[_sds((s, d), F32), _sds((s, d), BF16)], compiler_params=_params(("parallel",)))(xres, y, g_post, g_next)


def _resid_norm_loss(xres, y, g_post, target):
    s, d = xres.shape
    ts = _tile(s, ROW_TILE, 8)
    n_steps = s // ts

    def body(x_ref, y_ref, gp_ref, t_ref, dout_ref, loss_ref, acc_ref):
        i = pl.program_id(0)
        yv = y_ref[...].astype(F32)
        err = x_ref[...] + yv * _rstd(yv) * gp_ref[...] - t_ref[...]
        dout_ref[...] = err / d
        part = jnp.sum(err * err, axis=0, keepdims=True)

        @pl.when(i == 0)
        def _():
            acc_ref[...] = part

        @pl.when(i > 0)
        def _():
            acc_ref[...] += part

        @pl.when(i == n_steps - 1)
        def _():
            loss_ref[...] = 0.5 * jnp.sum(acc_ref[...], axis=1, keepdims=True) / d

    row = pl.BlockSpec((ts, d), lambda i: (i, 0))
    vec = pl.BlockSpec((1, d), lambda i: (0, 0))
    one = pl.BlockSpec((1, 1), lambda i: (0, 0))
    return pl.pallas_call(body, name="resid_norm_loss", grid=(n_steps,), in_specs=[row, row, vec, row], out_specs=[row, one],
                          out_shape=[_sds((s, d), F32), _sds((1, 1), F32)], scratch_shapes=[pltpu.VMEM((1, d), F32)],
                          compiler_params=_params(("arbitrary",)))(xres, y, g_post, target)


def _norm_bwd(xin, g, dout, dres, out_dtype, name):
    s, d = xin.shape
    ts = _tile(s, ROW_TILE, 8)
    has_res = dres is not None

    def body(*refs):
        x_ref, g_ref, do_ref = refs[:3]
        dx_ref, dg_ref = refs[-2:]
        i = pl.program_id(0)
        xv, dov = x_ref[...].astype(F32), do_ref[...].astype(F32)
        r = _rstd(xv)
        xhat = xv * r
        dg = jnp.sum(dov * xhat, axis=0, keepdims=True)
        dxh = dov * g_ref[...]
        dx = r * (dxh - xhat * jnp.mean(dxh * xhat, axis=-1, keepdims=True))
        if has_res:
            dx = dx + refs[3][...]
        dx_ref[...] = dx.astype(dx_ref.dtype)

        @pl.when(i == 0)
        def _():
            dg_ref[...] = dg

        @pl.when(i > 0)
        def _():
            dg_ref[...] += dg

    row = pl.BlockSpec((ts, d), lambda i: (i, 0))
    vec = pl.BlockSpec((1, d), lambda i: (0, 0))
    ops = [xin, g, dout] + ([dres] if has_res else [])
    return pl.pallas_call(body, name=name, grid=(s // ts,), in_specs=[row, vec, row] + ([row] if has_res else []),
                          out_specs=[row, vec], out_shape=[_sds((s, d), out_dtype), _sds((1, d), F32)],
                          compiler_params=_params(("arbitrary",)))(*ops)


class _Cols:
    def __init__(self, d):
        self.d = d
        self.kv = d // GROUP
        self.cb, self.cc, self.ci, self.q = 0, d, 2 * d, 3 * d
        self.k = 4 * d
        self.v = 4 * d + self.kv
        self.ga = 4 * d + 2 * self.kv
        self.gb = 5 * d + 2 * self.kv
        self.width = 6 * d + 2 * self.kv


CONV_COLS = 128


def _shift_rows(u, down):
    s = u.shape[0]
    rows = lax.broadcasted_iota(jnp.int32, u.shape, 0)
    if down:
        return jnp.where(rows == 0, 0.0, pltpu.roll(u, 1, 0))
    return jnp.where(rows == s - 1, 0.0, pltpu.roll(u, s - 1, 0))


def _conv_fwd(z, w, cols):
    s, d = z.shape[0], cols.d
    cw = CONV_COLS

    def body(cb_ref, cc_ref, ci_ref, w_ref, a_ref):
        u = cc_ref[...].astype(F32) * ci_ref[...].astype(F32)
        wv = w_ref[...]
        conv = wv[0:1] * _shift_rows(u, True) + wv[1:2] * u + wv[2:3] * _shift_rows(u, False)
        a_ref[...] = (cb_ref[...].astype(F32) * conv).astype(a_ref.dtype)

    def zspec(off):
        return pl.BlockSpec((s, cw), lambda j: (0, off // cw + j))

    return pl.pallas_call(body, name="conv_fwd", grid=(d // cw,),
                          in_specs=[zspec(cols.cb), zspec(cols.cc), zspec(cols.ci), pl.BlockSpec((3, cw), lambda j: (0, j))],
                          out_specs=pl.BlockSpec((s, cw), lambda j: (0, j)), out_shape=_sds((s, d), BF16),
                          compiler_params=_params(("parallel",)))(z, z, z, w)


def _conv_bwd(z, w, da, cols):
    s, d = z.shape[0], cols.d
    cw = CONV_COLS

    def body(cb_ref, cc_ref, ci_ref, w_ref, da_ref, dcb_ref, dcc_ref, dci_ref, dw_ref):
        cb, cc, ci, dav = cb_ref[...].astype(F32), cc_ref[...].astype(F32), ci_ref[...].astype(F32), da_ref[...].astype(F32)
        wv = w_ref[...]
        u = cc * ci
        um, up = _shift_rows(u, True), _shift_rows(u, False)
        conv = wv[0:1] * um + wv[1:2] * u + wv[2:3] * up
        dcb_ref[...] = (dav * conv).astype(dcb_ref.dtype)
        dconv = dav * cb
        dw_ref[0:1, :] = jnp.sum(dconv * um, axis=0, keepdims=True)
        dw_ref[1:2, :] = jnp.sum(dconv * u, axis=0, keepdims=True)
        dw_ref[2:3, :] = jnp.sum(dconv * up, axis=0, keepdims=True)
        du = wv[0:1] * _shift_rows(dconv, False) + wv[1:2] * dconv + wv[2:3] * _shift_rows(dconv, True)
        dcc_ref[...] = (du * ci).astype(dcc_ref.dtype)
        dci_ref[...] = (du * cc).astype(dci_ref.dtype)

    def zspec(off):
        return pl.BlockSpec((s, cw), lambda j: (0, off // cw + j))

    col = pl.BlockSpec((s, cw), lambda j: (0, j))
    wspec = pl.BlockSpec((3, cw), lambda j: (0, j))
    return pl.pallas_call(body, name="conv_bwd", grid=(d // cw,),
                          in_specs=[zspec(cols.cb), zspec(cols.cc), zspec(cols.ci), wspec, col],
                          out_specs=[col, col, col, wspec],
                          out_shape=[_sds((s, d), BF16)] * 3 + [_sds((3, d), F32)],
                          compiler_params=_params(("parallel",)))(z, z, z, w, da)


def _gate_fwd(z, bias, y_a, y_b, cols):
    s, d = y_a.shape
    ts, cw = _tile(s, ROW_TILE, 8), cols.kv
    nj = d // cw

    def body(ga_ref, gb_ref, ba_ref, bb_ref, ya_ref, yb_ref, o_ref):
        gate_a = jax.nn.sigmoid(ga_ref[...].astype(F32) + ba_ref[...])
        gate_b = jax.nn.sigmoid(gb_ref[...].astype(F32) + bb_ref[...])
        o_ref[...] = (gate_a * ya_ref[...].astype(F32) + gate_b * yb_ref[...].astype(F32)).astype(o_ref.dtype)

    tile = pl.BlockSpec((ts, cw), lambda i, j: (i, j))
    return pl.pallas_call(
        body, name="gate_fwd", grid=(s // ts, nj),
        in_specs=[pl.BlockSpec((ts, cw), lambda i, j: (i, cols.ga // cw + j)), pl.BlockSpec((ts, cw), lambda i, j: (i, cols.gb // cw + j)),
                  pl.BlockSpec((1, cw), lambda i, j: (0, j)), pl.BlockSpec((1, cw), lambda i, j: (0, nj + j)), tile, tile],
        out_specs=tile, out_shape=_sds((s, d), BF16), compiler_params=_params(("parallel", "parallel")))(z, z, bias, bias, y_a, y_b)


def _gate_bwd(z, bias, y_a, y_b, dmix, cols):
    s, d = y_a.shape
    ts, cw = _tile(s, ROW_TILE, 8), cols.kv
    nj = d // cw

    def body(ga_ref, gb_ref, ba_ref, bb_ref, ya_ref, yb_ref, dm_ref, dya_ref, dyb_ref, dga_ref, dgb_ref, dba_ref, dbb_ref):
        i = pl.program_id(1)
        gate_a = jax.nn.sigmoid(ga_ref[...].astype(F32) + ba_ref[...])
        gate_b = jax.nn.sigmoid(gb_ref[...].astype(F32) + bb_ref[...])
        dm = dm_ref[...].astype(F32)
        dya_ref[...] = (dm * gate_a).astype(dya_ref.dtype)
        dyb_ref[...] = (dm * gate_b).astype(dyb_ref.dtype)
        dga = dm * ya_ref[...].astype(F32) * (gate_a * (1.0 - gate_a))
        dgb = dm * yb_ref[...].astype(F32) * (gate_b * (1.0 - gate_b))
        dga_ref[...] = dga.astype(dga_ref.dtype)
        dgb_ref[...] = dgb.astype(dgb_ref.dtype)
        sa = jnp.sum(dga, axis=0, keepdims=True)
        sb = jnp.sum(dgb, axis=0, keepdims=True)

        @pl.when(i == 0)
        def _():
            dba_ref[...] = sa
            dbb_ref[...] = sb

        @pl.when(i > 0)
        def _():
            dba_ref[...] += sa
            dbb_ref[...] += sb

    tile = pl.BlockSpec((ts, cw), lambda j, i: (i, j))
    vec = pl.BlockSpec((1, cw), lambda j, i: (0, j))
    return pl.pallas_call(
        body, name="gate_bwd", grid=(nj, s // ts),
        in_specs=[pl.BlockSpec((ts, cw), lambda j, i: (i, cols.ga // cw + j)), pl.BlockSpec((ts, cw), lambda j, i: (i, cols.gb // cw + j)),
                  vec, pl.BlockSpec((1, cw), lambda j, i: (0, nj + j)), tile, tile, tile],
        out_specs=[tile, tile, tile, tile, vec, vec],
        out_shape=[_sds((s, d), BF16)] * 4 + [_sds((1, d), F32)] * 2,
        compiler_params=_params(("parallel", "arbitrary")))(z, z, bias, bias, y_a, y_b, dmix)


def _rope_tables(s):
    axis_dim = HEAD_DIM // 2
    n_freq = axis_dim // 2
    rows = s // GRID_W
    row_idx = jnp.repeat(jnp.arange(rows, dtype=jnp.int32), GRID_W)
    col_idx = jnp.tile(jnp.arange(GRID_W, dtype=jnp.int32), rows)
    inv_freq = ROPE_THETA ** (-jnp.arange(0, axis_dim, 2, dtype=F32) / axis_dim)
    ang = jnp.stack([row_idx.astype(F32)[:, None] * inv_freq, col_idx.astype(F32)[:, None] * inv_freq], axis=1)
    cos, sin = jnp.cos(ang), jnp.sin(ang)
    cos_t = jnp.stack([cos, cos], axis=2).reshape(s, HEAD_DIM)
    sin_t = jnp.stack([-sin, sin], axis=2).reshape(s, HEAD_DIM)
    return cos_t, sin_t


def _partner(x):
    n = x.shape[-1]
    lane = lax.broadcasted_iota(jnp.int32, x.shape, x.ndim - 1)
    quarter = HEAD_DIM // 4
    return jnp.where(lane % (2 * quarter) < quarter, pltpu.roll(x, n - quarter, x.ndim - 1), pltpu.roll(x, quarter, x.ndim - 1))


LOG2E = math.log2(math.e)
Q_SCALE = LOG2E / math.sqrt(HEAD_DIM)


def _rope_fwd(z, qn, kn, cos_t, sin_t, cols):
    s, d, kv = z.shape[0], cols.d, cols.kv
    ts = _tile(s, ROW_TILE, 8)
    scale = Q_SCALE

    def body(q_ref, k_ref, v_ref, qn_ref, kn_ref, c_ref, s_ref, qo_ref, ko_ref, vo_ref):
        c, sn = c_ref[...], s_ref[...]

        def head(xh, g):
            xn = xh * _rstd(xh) * g
            return xn * c + _partner(xn) * sn

        for h in range(d // HEAD_DIM):
            sl = slice(h * HEAD_DIM, (h + 1) * HEAD_DIM)
            qo_ref[:, sl] = (head(q_ref[:, sl].astype(F32), qn_ref[...]) * scale).astype(qo_ref.dtype)
        for h in range(kv // HEAD_DIM):
            sl = slice(h * HEAD_DIM, (h + 1) * HEAD_DIM)
            ko_ref[:, sl] = head(k_ref[:, sl].astype(F32), kn_ref[...]).astype(ko_ref.dtype)
        vo_ref[...] = v_ref[...].astype(vo_ref.dtype)

    vec = pl.BlockSpec((1, HEAD_DIM), lambda i: (0, 0))
    tab = pl.BlockSpec((ts, HEAD_DIM), lambda i: (i, 0))
    return pl.pallas_call(
        body, name="rope_fwd", grid=(s // ts,),
        in_specs=[pl.BlockSpec((ts, d), lambda i: (i, cols.q // d)), pl.BlockSpec((ts, kv), lambda i: (i, cols.k // kv)),
                  pl.BlockSpec((ts, kv), lambda i: (i, cols.v // kv)), vec, vec, tab, tab],
        out_specs=[pl.BlockSpec((ts, d), lambda i: (i, 0)), pl.BlockSpec((ts, kv), lambda i: (i, 0)), pl.BlockSpec((ts, kv), lambda i: (i, 0))],
        out_shape=[_sds((s, d), BF16), _sds((s, kv), BF16), _sds((s, kv), BF16)],
        compiler_params=_params(("parallel",)))(z, z, z, qn, kn, cos_t, sin_t)


def _rope_bwd(z, qn, kn, cos_t, sin_t, dq, dk_t, dv_t, cols):
    s, d, kv = z.shape[0], cols.d, cols.kv
    ts = _tile(s, ROW_TILE, 8)
    scale = 1.0 / math.sqrt(HEAD_DIM)

    def body(q_ref, k_ref, qn_ref, kn_ref, c_ref, s_ref, dq_ref, dkt_ref, dvt_ref, dzq_ref, dzk_ref, dzv_ref, dqn_ref, dkn_ref):
        i = pl.program_id(0)
        c, sn = c_ref[...], s_ref[...]
        dk_all = dkt_ref[...].T * (1.0 / LOG2E)

        def head_bwd(xh, g, drot):
            dxn = drot * c + _partner(drot * sn)
            r = _rstd(xh)
            xhat = xh * r
            dgain = jnp.sum(dxn * xhat, axis=0, keepdims=True)
            dxh = dxn * g
            return r * (dxh - xhat * jnp.mean(dxh * xhat, axis=-1, keepdims=True)), dgain

        dqn = jnp.zeros((1, HEAD_DIM), F32)
        for h in range(d // HEAD_DIM):
            sl = slice(h * HEAD_DIM, (h + 1) * HEAD_DIM)
            dx, dg = head_bwd(q_ref[:, sl].astype(F32), qn_ref[...], dq_ref[:, sl].astype(F32) * scale)
            dzq_ref[:, sl] = dx.astype(dzq_ref.dtype)
            dqn = dqn + dg
        dkn = jnp.zeros((1, HEAD_DIM), F32)
        for h in range(kv // HEAD_DIM):
            sl = slice(h * HEAD_DIM, (h + 1) * HEAD_DIM)
            dx, dg = head_bwd(k_ref[:, sl].astype(F32), kn_ref[...], dk_all[:, sl])
            dzk_ref[:, sl] = dx.astype(dzk_ref.dtype)
            dkn = dkn + dg
        dzv_ref[...] = dvt_ref[...].T.astype(dzv_ref.dtype)

        @pl.when(i == 0)
        def _():
            dqn_ref[...] = dqn
            dkn_ref[...] = dkn

        @pl.when(i > 0)
        def _():
            dqn_ref[...] += dqn
            dkn_ref[...] += dkn

    vec = pl.BlockSpec((1, HEAD_DIM), lambda i: (0, 0))
    tab = pl.BlockSpec((ts, HEAD_DIM), lambda i: (i, 0))
    qrow = pl.BlockSpec((ts, d), lambda i: (i, 0))
    krow = pl.BlockSpec((ts, kv), lambda i: (i, 0))
    kcol = pl.BlockSpec((kv, ts), lambda i: (0, i))
    return pl.pallas_call(
        body, name="rope_bwd", grid=(s // ts,),
        in_specs=[pl.BlockSpec((ts, d), lambda i: (i, cols.q // d)), pl.BlockSpec((ts, kv), lambda i: (i, cols.k // kv)),
                  vec, vec, tab, tab, qrow, kcol, kcol],
        out_specs=[qrow, krow, krow, vec, vec],
        out_shape=[_sds((s, d), BF16), _sds((s, kv), BF16), _sds((s, kv), BF16), _sds((1, HEAD_DIM), F32), _sds((1, HEAD_DIM), F32)],
        compiler_params=_params(("arbitrary",)))(z, z, qn, kn, cos_t, sin_t, dq, dk_t, dv_t)


Q_TILE = 256
_NT = (((1,), (1,)), ((), ()))
_NN = (((1,), (0,)), ((), ()))


def _attn_fwd(q, k, v):
    s, d = q.shape
    kvh = k.shape[1] // HEAD_DIM
    tq = _tile(s, Q_TILE, LANES)
    gw = GROUP * HEAD_DIM

    def body(q_ref, k_ref, v_ref, o_ref, lse_ref):
        kk, vv = k_ref[...], v_ref[...]
        for g in range(GROUP):
            sl = slice(g * HEAD_DIM, (g + 1) * HEAD_DIM)
            sc = lax.dot_general(q_ref[:, sl], kk, _NT, preferred_element_type=F32)
            mx = jnp.max(sc, axis=-1, keepdims=True)
            p = jnp.exp2(sc - mx)
            l = jnp.sum(p, axis=-1, keepdims=True)
            o = lax.dot_general(p.astype(BF16), vv, _NN, preferred_element_type=F32) * (1.0 / l)
            o_ref[:, sl] = o.astype(o_ref.dtype)
            lse_ref[:, g:g + 1] = mx + jnp.log(l) * LOG2E

    return pl.pallas_call(
        body, name="attn_fwd", grid=(kvh, s // tq),
        in_specs=[pl.BlockSpec((tq, gw), lambda j, i: (i, j)), pl.BlockSpec((s, HEAD_DIM), lambda j, i: (0, j)), pl.BlockSpec((s, HEAD_DIM), lambda j, i: (0, j))],
        out_specs=[pl.BlockSpec((tq, gw), lambda j, i: (i, j)), pl.BlockSpec((None, tq, GROUP), lambda j, i: (j, i, 0))],
        out_shape=[_sds((s, d), BF16), _sds((kvh, s, GROUP), F32)],
        compiler_params=_params(("parallel", "parallel")))(q, k, v)


_TN = (((0,), (0,)), ((), ()))


def _attn_bwd(q, k, v, do, lse):
    s, d = q.shape
    kv = k.shape[1]
    kvh = kv // HEAD_DIM
    tq = _tile(s, Q_TILE, LANES)
    gw = GROUP * HEAD_DIM

    def body(q_ref, k_ref, v_ref, do_ref, lse_ref, dq_ref, dkt_ref, dvt_ref):
        i = pl.program_id(1)
        kk, vv = k_ref[...], v_ref[...]

        @pl.when(i == 0)
        def _():
            dkt_ref[...] = jnp.zeros_like(dkt_ref)
            dvt_ref[...] = jnp.zeros_like(dvt_ref)

        for g in range(GROUP):
            sl = slice(g * HEAD_DIM, (g + 1) * HEAD_DIM)
            qg, dog = q_ref[:, sl], do_ref[:, sl]
            sc = lax.dot_general(qg, kk, _NT, preferred_element_type=F32)
            p = jnp.exp2(sc - lse_ref[:, g:g + 1])
            dp = lax.dot_general(dog, vv, _NT, preferred_element_type=F32)
            delta = jnp.sum(p * dp, axis=-1, keepdims=True)
            ds = (p * (dp - delta)).astype(BF16)
            dq_ref[:, sl] = lax.dot_general(ds, kk, _NN, preferred_element_type=F32).astype(dq_ref.dtype)
            dkt_ref[...] += lax.dot_general(qg, ds, _TN, preferred_element_type=F32)
            dvt_ref[...] += lax.dot_general(dog, p.astype(BF16), _TN, preferred_element_type=F32)

    qspec = pl.BlockSpec((tq, gw), lambda j, i: (i, j))
    kspec = pl.BlockSpec((s, HEAD_DIM), lambda j, i: (0, j))
    stat = pl.BlockSpec((None, tq, GROUP), lambda j, i: (j, i, 0))
    tspec = pl.BlockSpec((HEAD_DIM, s), lambda j, i: (j, 0))
    return pl.pallas_call(
        body, name="attn_bwd", grid=(kvh, s // tq),
        in_specs=[qspec, kspec, kspec, qspec, stat], out_specs=[qspec, tspec, tspec],
        out_shape=[_sds((s, d), BF16), _sds((kv, s), F32), _sds((kv, s), F32)],
        compiler_params=_params(("parallel", "arbitrary")))(q, k, v, do, lse)


ELEM_BLOCK_BYTES = 1 << 20

BIG = (("w_in", True), ("w_out_conv", False), ("w_out_attn", False), ("w_merge", False), ("w_up", True), ("w_down", False))
N_BIG = len(BIG)


def _elem_tiles(rows, width):
    tc = _tile(width, 2048, LANES)
    tr = _tile(rows, max(8, ELEM_BLOCK_BYTES // (4 * tc)), 8)
    return tr, tc


def _scalar_grid(grid, in_specs, out_specs):
    return pltpu.PrefetchScalarGridSpec(num_scalar_prefetch=1, grid=grid, in_specs=in_specs, out_specs=out_specs)


def _cast_place(w_stack, layer, chip, col_sharded, name):
    _, rows, width = w_stack.shape
    tr, tc = _elem_tiles(rows, width)
    nr, nc = rows // tr, width // tc

    def body(sc_ref, x_ref, o_ref):
        o_ref[...] = x_ref[...].astype(o_ref.dtype)

    if col_sharded:
        full, out_spec = (rows, width * N_CHIPS), pl.BlockSpec((tr, tc), lambda i, j, sc: (i, sc[0] * nc + j))
    else:
        full, out_spec = (rows * N_CHIPS, width), pl.BlockSpec((tr, tc), lambda i, j, sc: (sc[0] * nr + i, j))
    return pl.pallas_call(
        body, name=name,
        grid_spec=_scalar_grid((nr, nc), [pl.BlockSpec((None, tr, tc), lambda i, j, sc: (layer, i, j))], out_spec),
        out_shape=_sds(full, BF16), compiler_params=_params(("parallel", "parallel")))(chip, w_stack)


def _add_landed(acc, landed):
    return (acc + landed,)


def _slot_of_relation(rel):
    return jnp.where(rel == 2, 0, jnp.where(rel == 1, 1, 2))


def _sum_chips(pair_sum, landed, chip, col_sharded, name):
    _, rows, width = landed.shape
    tr, tc = _elem_tiles(rows, width)
    nr, nc = rows // tr, width // tc

    def body(chip_ref, own_ref, q_ref, o_ref):
        me = chip_ref[0]
        own = own_ref[...].astype(F32)
        acc = None
        for t in range(N_CHIPS):
            rel = me ^ t
            term = jnp.where(rel == 0, own, q_ref[_slot_of_relation(rel)].astype(F32))
            acc = term if acc is None else acc + term
        o_ref[...] = acc

    if col_sharded:
        own_spec = pl.BlockSpec((tr, tc), lambda i, j, c: (i, c[0] * nc + j))
    else:
        own_spec = pl.BlockSpec((tr, tc), lambda i, j, c: (c[0] * nr + i, j))
    return pl.pallas_call(
        body, name=name,
        grid_spec=_scalar_grid((nr, nc), [own_spec, pl.BlockSpec((N_CHIPS - 1, tr, tc), lambda i, j, c: (0, i, j))],
                               pl.BlockSpec((tr, tc), lambda i, j, c: (i, j))),
        out_shape=_sds((rows, width), F32), compiler_params=_params(("parallel", "parallel")))(chip, pair_sum, landed)


def _adamw_math(w, g, m, v):
    mn = ADAM_B1 * m + (1.0 - ADAM_B1) * g
    vn = ADAM_B2 * v + (1.0 - ADAM_B2) * jnp.square(g)
    m_hat = mn / (1.0 - ADAM_B1 ** ADAM_STEP)
    v_hat = vn / (1.0 - ADAM_B2 ** ADAM_STEP)
    return -ADAM_LR * (m_hat / (jnp.sqrt(v_hat) + ADAM_EPS) + ADAM_WD * w), mn, vn


def _adamw(w, g, m, v, name):
    shape = w.shape
    width = shape[-1]
    w2, g2, m2, v2 = (a.reshape(-1, width) for a in (w, g, m, v))
    rows = w2.shape[0]
    tr, tc = _elem_tiles(rows, width)

    def body(w_ref, g_ref, m_ref, v_ref, d_ref, nm_ref, nv_ref):
        d_ref[...], nm_ref[...], nv_ref[...] = _adamw_math(w_ref[...], g_ref[...], m_ref[...], v_ref[...])

    tile = pl.BlockSpec((tr, tc), lambda i, j: (i, j))
    outs = pl.pallas_call(body, name=name, grid=(rows // tr, width // tc), in_specs=[tile] * 4, out_specs=[tile] * 3,
                          out_shape=[_sds((rows, width), F32)] * 3, compiler_params=_params(("parallel", "parallel")))(w2, g2, m2, v2)
    return tuple(o.reshape(shape) for o in outs)


def _adamw_layer(w, m, v, g_mine, g_sibling, carried, layer, core, col_sharded, name):
    depth, rows, width = w.shape
    pr, pc = g_mine.shape
    tr, tc = _elem_tiles(pr, pc)
    n_half = pr // tr if col_sharded else pc // tc
    if carried is None:
        carried = tuple(lax.empty((depth, rows, width), F32) for _ in range(4))

    def body(sc_ref, w_ref, m_ref, v_ref, gm_ref, gs_ref, *rest):
        g_ref, d_ref, nm_ref, nv_ref = rest[-4:]
        pos = pl.program_id(0) if col_sharded else pl.program_id(1)
        gv = jnp.where(pos // n_half == sc_ref[0], gm_ref[...], gs_ref[...])
        g_ref[...] = gv
        d_ref[...], nm_ref[...], nv_ref[...] = _adamw_math(w_ref[...], gv, m_ref[...], v_ref[...])

    stacked = pl.BlockSpec((None, tr, tc), lambda i, j, sc: (layer, i, j))
    n_cols = width // tc

    def half_spec(mine):
        def index(i, j, sc):
            own = sc[0] if mine else 1 - sc[0]
            pos = i if col_sharded else j
            used, before = pos // n_half == own, pos // n_half < own
            within = jnp.where(used, pos % n_half, jnp.where(before, 0, n_half - 1))
            if col_sharded:
                return within, jnp.where(used, j, jnp.where(before, 0, n_cols - 1))
            return i, within
        return pl.BlockSpec((tr, tc), index)

    return pl.pallas_call(
        body, name=name,
        grid_spec=_scalar_grid((rows // tr, width // tc), [stacked] * 3 + [half_spec(True), half_spec(False)] + [ANY] * 4, [stacked] * 4),
        out_shape=[_sds((depth, rows, width), F32)] * 4, input_output_aliases={6: 0, 7: 1, 8: 2, 9: 3},
        compiler_params=_params(("parallel", "parallel")))(core, w, m, v, g_mine, g_sibling, *carried)


_SEM = pl.BlockSpec(memory_space=pltpu.SEMAPHORE)
_HBM = pl.BlockSpec(memory_space=pltpu.HBM)
_VMEM = pl.BlockSpec(memory_space=pltpu.VMEM)
_EFFECT = pltpu.SideEffectType.DATAFLOW_SIDE_EFFECTING


def _place():
    x, y, c = lax.axis_index("x"), lax.axis_index("y"), lax.axis_index("c")
    others = [(1 - x, y), (x, 1 - y), (1 - x, 1 - y)]
    return x, y, c, others


def _chip_index(px, py):
    return 2 * px + py


def _remote(src, dst, send_sems, recv_sems, k, to):
    return pltpu.make_async_remote_copy(src_ref=src, dst_ref=dst, send_sem=send_sems.at[k], recv_sem=recv_sems.at[k],
                                        device_id=to, device_id_type=MESH)


def _phase(name, bufs, waits, wait_fn, n_start, start_fn, deps):
    nb, nd = len(bufs), len(deps)

    def body(*refs):
        buf_refs = refs[:nb]
        pos = nb
        if waits is not None:
            wait_fn(buf_refs, refs[pos], refs[pos + 1])
            pos += 2
        pos += nd
        if n_start:
            start_fn(buf_refs, refs[pos], refs[pos + 1])
            pos += 2
        token = refs[pos + nb]
        token[...] = jnp.zeros_like(token)

    n_sem_out = 2 if n_start else 0
    if waits is None:
        bufs = [pltpu.with_memory_space_constraint(b, pltpu.HBM) for b in bufs]
    outs = pl.pallas_call(
        body, name=name,
        in_specs=[_HBM] * nb + ([_SEM] * 2 if waits is not None else []) + [ANY] * nd,
        out_specs=[_SEM] * n_sem_out + [_HBM] * nb + [_VMEM],
        out_shape=[pltpu.SemaphoreType.DMA((n_start,))] * n_sem_out + [pltpu.HBM(b.shape, b.dtype) for b in bufs] + [_sds((8, LANES), F32)],
        input_output_aliases={i: n_sem_out + i for i in range(nb)},
        compiler_params=pltpu.CompilerParams(has_side_effects=_EFFECT),
    )(*bufs, *(waits if waits is not None else ()), *deps)
    sems = tuple(outs[:2]) if n_start else None
    return sems, list(outs[n_sem_out:n_sem_out + nb]), outs[-1]


class _Chains:
    def __init__(self):
        self.active = []
        self.last = None
        self.dep = None
        self.pinned = []
        self.token = None
        self.at = None

    def phase(self, name, bufs, waits, wait_fn, n_start, start_fn):
        deps = [a for a in (self.last, self.dep) if a is not None] + self.pinned
        sems, thru, token = _phase(name, bufs, waits, wait_fn, n_start, start_fn, deps)
        self.last, self.dep, self.token, self.pinned = token, None, token, []
        return sems, thru

    def pin(self, result):
        self.pinned.append(result)

    def add(self, gen):
        self.active.append(gen)

    def tick(self, dep, at=None):
        self.at, self.token, self.dep = at, None, dep
        for gen in list(self.active):
            if next(gen, "done") == "done":
                self.active.remove(gen)
        return self.token


def _after(small, token):
    return small if token is None else small + token[0, 0]


def _shard_region(ref, col_sharded, chip, n_shard):
    start = pl.multiple_of(chip * n_shard, LANES if col_sharded else 8)
    if col_sharded:
        return ref.at[:, pl.ds(start, n_shard)]
    return ref.at[pl.ds(start, n_shard), :]


def _row_half(ref, half):
    n_rows = ref.shape[0]
    return ref.at[pl.ds(pl.multiple_of(half * (n_rows // 2), 8), n_rows // 2), :]


def _gather_chain(chains, tag, group, bufs, out, wait_for):
    n_w = len(group)
    n = 3 * n_w

    def region(refs, a, chip, half):
        col = BIG[group[a]][1]
        n_shard = refs[a].shape[1] // N_CHIPS if col else refs[a].shape[0] // N_CHIPS
        return _row_half(_shard_region(refs[a], col, chip, n_shard), half)

    def start_ici(refs, send, recv):
        x, y, c, others = _place()
        for a in range(n_w):
            mine = region(refs, a, _chip_index(x, y), c)
            for j, (ox, oy) in enumerate(others):
                _remote(mine, mine, send, recv, 3 * a + j, (ox, oy, c)).start()

    def wait_ici(refs, send, recv):
        x, y, c, others = _place()
        for a in range(n_w):
            for j, (ox, oy) in enumerate(others):
                landed = region(refs, a, _chip_index(ox, oy), c)
                cp = _remote(landed, landed, send, recv, 3 * a + j, (x, y, 1 - c))
                cp.wait_recv()
                cp.wait_send()

    def start_d2d(refs, send, recv):
        x, y, c, others = _place()
        for a in range(n_w):
            for j, (ox, oy) in enumerate(others):
                landed = region(refs, a, _chip_index(ox, oy), c)
                _remote(landed, landed, send, recv, 3 * a + j, (x, y, 1 - c)).start()

    def wait_d2d(refs, send, recv):
        x, y, c, others = _place()
        for a in range(n_w):
            for j, (ox, oy) in enumerate(others):
                theirs = region(refs, a, _chip_index(ox, oy), 1 - c)
                cp = _remote(theirs, theirs, send, recv, 3 * a + j, (x, y, 1 - c))
                cp.wait_recv()
                cp.wait_send()

    sems, bufs = chains.phase(f"gather_ici_start_{tag}", bufs, None, None, n, start_ici)
    yield
    while wait_for is not None and chains.at != wait_for:
        yield
    sems, bufs = chains.phase(f"gather_forward_{tag}", bufs, sems, wait_ici, n, start_d2d)
    yield
    _, bufs = chains.phase(f"gather_done_{tag}", bufs, sems, wait_d2d, 0, None)
    for a, buf in zip(group, bufs):
        out[BIG[a][0]] = buf


def _grad_chain(chains, layer, group, pairs, core, other_core, chip, w, mom, var, carried):
    n = len(group)
    tag = f"{layer}_{group[0]}"
    kinds = [BIG[a][1] for a in group]
    names = [BIG[a][0] for a in group]
    sibling_of = lambda x, y, c: (x, y, 1 - c)

    def pair_start(refs, send, recv):
        x, y, c, _ = _place()
        for i in range(n):
            _remote(refs[i], refs[n + i], send, recv, i, sibling_of(x, y, c)).start()

    def pair_wait(refs, send, recv):
        x, y, c, _ = _place()
        for i in range(n):
            cp = _remote(refs[i], refs[n + i], send, recv, i, sibling_of(x, y, c))
            cp.wait_recv()
            cp.wait_send()

    def piece(ref, col, chip_idx):
        return _shard_region(ref, col, chip_idx, ref.shape[1] // N_CHIPS if col else ref.shape[0] // N_CHIPS)

    def scatter_start(refs, send, recv):
        x, y, c, others = _place()
        for i in range(n):
            for j, (ox, oy) in enumerate(others):
                _remote(piece(refs[i], kinds[i], _chip_index(ox, oy)), refs[n + i].at[j], send, recv, 3 * i + j, (ox, oy, c)).start()

    def scatter_wait(refs, send, recv):
        x, y, c, others = _place()
        for i in range(n):
            for j, (ox, oy) in enumerate(others):
                cp = _remote(piece(refs[i], kinds[i], _chip_index(ox, oy)), refs[n + i].at[j], send, recv, 3 * i + j, (ox, oy, c))
                cp.wait_recv()
                cp.wait_send()

    def join_start(refs, send, recv):
        x, y, c, _ = _place()
        for i in range(n):
            _remote(refs[i], refs[n + i], send, recv, i, sibling_of(x, y, c)).start()

    def join_wait(refs, send, recv):
        x, y, c, _ = _place()
        for i in range(n):
            cp = _remote(refs[i], refs[n + i], send, recv, i, sibling_of(x, y, c))
            cp.wait_recv()
            cp.wait_send()

    sides = ["m" if col else "n" for col in kinds]
    sends = [_mm(a, b, "tn", [F32], name="mm_g_send_" + names[i], half=(other_core, sides[i])) for i, (a, b) in enumerate(pairs)]
    half_shapes = [g.shape for g in sends]
    lands = [lax.empty(s, F32) for s in half_shapes]
    sems, bufs = chains.phase(f"pair_start_{tag}", sends + lands, None, None, n, pair_start)
    yield
    _, bufs = chains.phase(f"pair_wait_{tag}", bufs, sems, pair_wait, 0, None)
    pair_sums = [_mm(a, b, "tn", [BF16], name="mm_g_keep_" + names[i], half=(core, sides[i]), epilogue=_add_landed, extras=(bufs[n + i],))
                 for i, (a, b) in enumerate(pairs)]
    piece_shapes = [(s[0], s[1] // N_CHIPS) if col else (s[0] // N_CHIPS, s[1]) for col, s in zip(kinds, half_shapes)]
    slots = [lax.empty((N_CHIPS - 1, *s), BF16) for s in piece_shapes]
    sems, bufs = chains.phase(f"scatter_start_{tag}", pair_sums + slots, None, None, 3 * n, scatter_start)
    yield
    yield
    _, bufs = chains.phase(f"scatter_wait_{tag}", bufs, sems, scatter_wait, 0, None)
    reduced = [_sum_chips(bufs[i], bufs[n + i], chip, kinds[i], "sum_chips_" + names[i]) for i in range(n)]
    theirs = [lax.empty(s, F32) for s in piece_shapes]
    sems, bufs = chains.phase(f"join_start_{tag}", reduced + theirs, None, None, n, join_start)
    yield
    _, bufs = chains.phase(f"join_wait_{tag}", bufs, sems, join_wait, 0, None)
    for i in range(n):
        carried[names[i]] = _adamw_layer(w[names[i]], mom[names[i]], var[names[i]], bufs[i], bufs[n + i], carried.get(names[i]),
                                         layer, core, kinds[i], "adamw_" + names[i])
        chains.pin(carried[names[i]][0])


def _allreduce_small(vec, name):
    rows = vec.shape[0]
    masks = [(dx, dy, dc) for dx in (0, 1) for dy in (0, 1) for dc in (0, 1)][1:]

    def body(v_ref, o_ref, gather_ref, send_sems, recv_sems):
        x, y, c, _ = _place()
        me = 4 * x + 2 * y + c
        gather_ref[me] = v_ref[...]
        copies = []
        for k, (dx, dy, dc) in enumerate(masks):
            peer = (x ^ dx, y ^ dy, c ^ dc)
            copies.append(_remote(v_ref, gather_ref.at[me], send_sems, recv_sems, k, peer))
        for cp in copies:
            cp.start()
        for k, (dx, dy, dc) in enumerate(masks):
            slot = gather_ref.at[4 * (x ^ dx) + 2 * (y ^ dy) + (c ^ dc)]
            _remote(slot, slot, send_sems, recv_sems, k, (x, y, c)).wait_recv()
        for cp in copies:
            cp.wait_send()
        acc = gather_ref[0]
        for dev in range(1, N_DEV):
            acc = acc + gather_ref[dev]
        o_ref[...] = acc

    return pl.pallas_call(
        body, name=name, in_specs=[_VMEM], out_specs=_VMEM, out_shape=_sds((rows, LANES), F32),
        scratch_shapes=[pltpu.VMEM((N_DEV, rows, LANES), F32), pltpu.SemaphoreType.DMA((N_DEV - 1,)), pltpu.SemaphoreType.DMA((N_DEV - 1,))],
        compiler_params=pltpu.CompilerParams(has_side_effects=True, vmem_limit_bytes=VMEM_LIMIT_BYTES),
    )(vec)


def _relu2(acc):
    r = jnp.maximum(acc, 0.0)
    return acc, r * r


def _relu2_bwd(acc, up):
    return (acc * (2.0 * jnp.maximum(up.astype(F32), 0.0)),)


def _layer_fwd(x, h, w, cols, tables, tick, last, target=None):
    cos_t, sin_t = tables
    z = _mm(h, w("w_in"), "nn", [BF16], name="mm_in")
    a = _conv_fwd(z, w("conv_w"), cols)
    q, k, v = _rope_fwd(z, _after(w("q_norm"), tick(a, 0)), w("k_norm"), cos_t, sin_t, cols)
    o, lse = _attn_fwd(q, k, v)
    tick(o, 1)
    tick(None, 2)
    y_a = _mm(a, w("w_out_conv"), "nn", [BF16], name="mm_out_conv")
    y_b = _mm(o, w("w_out_attn"), "nn", [BF16], name="mm_out_attn")
    mix = _gate_fwd(z, w("gate_bias"), y_a, y_b, cols)
    mixed = _mm(mix, w("w_merge"), "nn", [BF16], name="mm_merge")
    x1, h2 = _resid_norm(x, mixed, _after(w("norm_mix_post"), tick(mixed, 3)), w("norm_mlp_pre"))
    up, act = _mm(h2, w("w_up"), "nn", [BF16, BF16], name="mm_up", epilogue=_relu2)
    f = _mm(act, w("w_down"), "nn", [BF16], name="mm_down")
    kept = dict(x=x, h=h, z=z, a=a, q=q, k=k, v=v, o=o, lse=lse, y_a=y_a, y_b=y_b, mix=mix, mixed=mixed, x1=x1, h2=h2, up=up, act=act, f=f)
    g_post = _after(w("norm_mlp_post"), tick(f, 4))
    if last:
        return _resid_norm_loss(x1, f, g_post, target), kept
    return _resid_norm(x1, f, g_post, w("norm_next")), kept


def _layer_bwd(dx_out, w, kept, cols, tables, tick, emit):
    cos_t, sin_t = tables
    t = kept
    df, d_norm_mlp_post = _norm_bwd(t["f"], _after(w["norm_mlp_post"], tick(dx_out, 0)), dx_out, None, BF16, "norm_bwd_mlp_post")
    dup = _mm(df, w["w_down"], "nt", [BF16], name="mm_d_act", epilogue=_relu2_bwd, extras=(t["up"],))
    emit((4, 5), [(t["h2"], dup), (t["act"], df)])
    dh2 = _mm(dup, w["w_up"], "nt", [BF16], name="mm_d_h2")
    dx1, d_norm_mlp_pre = _norm_bwd(t["x1"], _after(w["norm_mlp_pre"], tick(dh2, 1)), dh2, dx_out, F32, "norm_bwd_mlp_pre")
    dmixed, d_norm_mix_post = _norm_bwd(t["mixed"], w["norm_mix_post"], dx1, None, BF16, "norm_bwd_mix_post")
    dmix = _mm(dmixed, w["w_merge"], "nt", [BF16], name="mm_d_mix")
    dy_a, dy_b, dz_ga, dz_gb, dbias_a, dbias_b = _gate_bwd(t["z"], _after(w["gate_bias"], tick(dmix, 2)), t["y_a"], t["y_b"], dmix, cols)
    da = _mm(dy_a, w["w_out_conv"], "nt", [BF16], name="mm_d_a")
    emit((1, 2, 3), [(t["a"], dy_a), (t["o"], dy_b), (t["mix"], dmixed)])
    do = _mm(dy_b, w["w_out_attn"], "nt", [BF16], name="mm_d_o")
    dz_cb, dz_cc, dz_ci, d_conv_w = _conv_bwd(t["z"], _after(w["conv_w"], tick(do, 3)), da, cols)
    dq, dk_t, dv_t = _attn_bwd(t["q"], t["k"], t["v"], do, t["lse"])
    dz_q, dz_k, dz_v, d_q_norm, d_k_norm = _rope_bwd(t["z"], _after(w["q_norm"], tick(dv_t, 4)), w["k_norm"], cos_t, sin_t, dq, dk_t, dv_t, cols)
    dz = jnp.concatenate([dz_cb, dz_cc, dz_ci, dz_q, dz_k, dz_v, dz_ga, dz_gb], axis=1)
    emit((0,), [(t["h"], dz)])
    tick(dz, 5)
    dh = _mm(dz, w["w_in"], "nt", [BF16], name="mm_d_h", tk=3328)
    dx_in, d_norm_mix_pre = _norm_bwd(t["x"], _after(w["norm_mix_pre"], tick(dh, 6)), dh, dx1, F32, "norm_bwd_mix_pre")
    small = dict(norm_mix_pre=d_norm_mix_pre, gate_bias=jnp.concatenate([dbias_a, dbias_b], axis=1), conv_w=d_conv_w, q_norm=d_q_norm,
                 k_norm=d_k_norm, norm_mix_post=d_norm_mix_post, norm_mlp_pre=d_norm_mlp_pre, norm_mlp_post=d_norm_mlp_post)
    return dx_in, small


SMALL = ("norm_mix_pre", "gate_bias", "conv_w", "q_norm", "k_norm", "norm_mix_post", "norm_mlp_pre", "norm_mlp_post")
WEIGHTS = ("norm_mix_pre", "w_in", "gate_bias", "conv_w", "q_norm", "k_norm", "w_out_conv", "w_out_attn", "w_merge",
           "norm_mix_post", "norm_mlp_pre", "w_up", "w_down", "norm_mlp_post")


def _pack(parts):
    flat = jnp.concatenate([a.reshape(-1) for a in parts])
    rows = -(-flat.shape[0] // LANES)
    pad = (-rows) % 8
    flat = jnp.pad(flat, (0, (rows + pad) * LANES - flat.shape[0]))
    return flat.reshape(rows + pad, LANES)


def _unpack(packed, shapes):
    flat = packed.reshape(-1)
    out, off = [], 0
    for shp in shapes:
        n = math.prod(shp)
        out.append(flat[off:off + n].reshape(shp))
        off += n
    return out


def kernel(x, norm_mix_pre, w_in, gate_bias, conv_w, q_norm, k_norm, w_out_conv, w_out_attn, w_merge, norm_mix_post, norm_mlp_pre, w_up, w_down, norm_mlp_post, loss_target, m_norm_mix_pre, m_w_in, m_gate_bias, m_conv_w, m_q_norm, m_k_norm, m_w_out_conv, m_w_out_attn, m_w_merge, m_norm_mix_post, m_norm_mlp_pre, m_w_up, m_w_down, m_norm_mlp_post, v_norm_mix_pre, v_w_in, v_gate_bias, v_conv_w, v_q_norm, v_k_norm, v_w_out_conv, v_w_out_attn, v_w_merge, v_norm_mix_post, v_norm_mlp_pre, v_w_up, v_w_down, v_norm_mlp_post):
    w = dict(norm_mix_pre=norm_mix_pre, w_in=w_in, gate_bias=gate_bias, conv_w=conv_w, q_norm=q_norm, k_norm=k_norm, w_out_conv=w_out_conv,
             w_out_attn=w_out_attn, w_merge=w_merge, norm_mix_post=norm_mix_post, norm_mlp_pre=norm_mlp_pre, w_up=w_up, w_down=w_down,
             norm_mlp_post=norm_mlp_post)
    mom = dict(norm_mix_pre=m_norm_mix_pre, w_in=m_w_in, gate_bias=m_gate_bias, conv_w=m_conv_w, q_norm=m_q_norm, k_norm=m_k_norm,
               w_out_conv=m_w_out_conv, w_out_attn=m_w_out_attn, w_merge=m_w_merge, norm_mix_post=m_norm_mix_post, norm_mlp_pre=m_norm_mlp_pre,
               w_up=m_w_up, w_down=m_w_down, norm_mlp_post=m_norm_mlp_post)
    var = dict(norm_mix_pre=v_norm_mix_pre, w_in=v_w_in, gate_bias=v_gate_bias, conv_w=v_conv_w, q_norm=v_q_norm, k_norm=v_k_norm,
               w_out_conv=v_w_out_conv, w_out_attn=v_w_out_attn, w_merge=v_w_merge, norm_mix_post=v_norm_mix_post, norm_mlp_pre=v_norm_mlp_pre,
               w_up=v_w_up, w_down=v_w_down, norm_mlp_post=v_norm_mlp_post)
    depth = w_in.shape[0]
    _, s, d = x.shape
    cols = _Cols(d)
    x0 = x.reshape(s, d)
    target = loss_target.reshape(s, d)
    tables = _rope_tables(s)
    chip = (2 * lax.axis_index("x") + lax.axis_index("y")).astype(jnp.int32)
    core = lax.axis_index("c").astype(jnp.int32)
    chip_vec, core_vec, other_core_vec = chip.reshape(1), core.reshape(1), (1 - core).reshape(1)

    n_conv = conv_w.shape[-1]
    placed = lax.dynamic_update_slice_in_dim(jnp.zeros((depth, conv_w.shape[1], n_conv * N_CHIPS), F32), conv_w, chip * n_conv, axis=2)
    conv_full = _unpack(_allreduce_small(_pack([jnp.where(core == 0, placed, 0.0)]), "gather_conv_w"), [placed.shape])[0]

    chains = _Chains()
    chains.pin(conv_full)
    full = [{} for _ in range(depth)]
    everything = tuple(range(N_BIG))
    gathers = []
    for l in range(depth):
        bufs = [_cast_place(w[name], l, chip_vec, col, "cast_place_" + name) for name, col in BIG]
        for group in ((0,), everything[1:]):
            if group[0] == 0:
                wait_for = ("fwd", l - 1, 3) if l else None
            else:
                wait_for = ("fwd", l, 0 if l else 1)
            gathers.append(_gather_chain(chains, f"{l}_{group[0]}", group, [bufs[a] for a in group], full[l], wait_for))
            next(gathers[-1])
    next(gathers[0])
    next(gathers[0], None)
    for g in gathers[1:]:
        chains.add(g)

    def layer_params(l):
        p = dict(full[l])
        p["conv_w"] = conv_full[l]
        p["gate_bias"] = gate_bias[l].reshape(1, -1)
        for name in ("norm_mix_pre", "q_norm", "k_norm", "norm_mix_post", "norm_mlp_pre", "norm_mlp_post"):
            p[name] = w[name][l].reshape(1, -1)
        if l + 1 < depth:
            p["norm_next"] = w["norm_mix_pre"][l + 1].reshape(1, -1)
        return p

    kept = []
    xl, h = x0, _norm_first(x0, norm_mix_pre[0].reshape(1, -1))
    for l in range(depth):
        last = l == depth - 1
        (xl, h), t = _layer_fwd(xl, h, lambda name, l=l: layer_params(l)[name], cols, tables,
                                lambda dep, k, l=l: chains.tick(dep, ("fwd", l, k)), last, target if last else None)
        kept.append(t)
    dx, loss_local = xl, h

    carried = {}
    small = [None] * depth
    for l in reversed(range(depth)):
        def emit(group, pairs, l=l):
            chains.add(_grad_chain(chains, l, group, pairs, core_vec, other_core_vec, chip_vec, w, mom, var, carried))

        dx, small[l] = _layer_bwd(dx, layer_params(l), kept[l], cols, tables, lambda dep, k: chains.tick(dep), emit)
    while chains.active:
        chains.tick(None)
    grads = {name: carried[name][0] for name, _ in BIG}
    delta = {name: carried[name][1] for name, _ in BIG}
    new_m = {name: carried[name][2] for name, _ in BIG}
    new_v = {name: carried[name][3] for name, _ in BIG}

    small_full_shapes = [(depth,) + small[0][name].shape for name in SMALL]
    packed = _pack([jnp.stack([small[l][name] for l in range(depth)]) for name in SMALL] + [jnp.broadcast_to(loss_local.reshape(1), (LANES,))])
    small_sum = _unpack(_allreduce_small(packed, "allreduce_small"), small_full_shapes + [(LANES,)])
    loss = small_sum[-1][0]
    for name, g in zip(SMALL, small_sum[:-1]):
        if name == "conv_w":
            g = lax.dynamic_slice_in_dim(g, chip * n_conv, n_conv, axis=2)
        grads[name] = g.reshape(w[name].shape)
    small_shapes = [w[name].shape for name in SMALL]
    packs = [_pack([src[name] for name in SMALL]) for src in (w, grads, mom, var)]
    for dst, out in zip((delta, new_m, new_v), _adamw(*packs, "adamw_small")):
        for name, val in zip(SMALL, _unpack(out, small_shapes)):
            dst[name] = val

    grad_x = dx.reshape(x.shape)
    return (loss, grad_x, *[grads[n] for n in WEIGHTS], *[delta[n] for n in WEIGHTS], *[new_m[n] for n in WEIGHTS], *[new_v[n] for n in WEIGHTS])
```

```python
import math

import jax
import jax.numpy as jnp
from jax import lax
from jax.experimental import pallas as pl
from jax.experimental.pallas import tpu as pltpu

F32 = jnp.float32
BF16 = jnp.bfloat16

HEAD_DIM = 128
GROUP = 4
GRID_W = 64
ROPE_THETA = 10000.0
RMS_EPS = 1e-6
ADAM_LR = 0.001
ADAM_B1 = 0.9
ADAM_B2 = 0.999
ADAM_EPS = 1e-08
ADAM_WD = 0.01
ADAM_STEP = 10

LANES = 128
N_CHIPS = 4
N_DEV = 8
VMEM_LIMIT_BYTES = 56 * 1024 * 1024
MESH = pl.DeviceIdType.MESH
ANY = pl.BlockSpec(memory_space=pl.ANY)


def _tile(dim, cap, mult):
    if dim <= cap:
        return dim
    t = (cap // mult) * mult
    while t >= mult:
        if dim % t == 0:
            return t
        t -= mult
    raise ValueError(f"no tile for {dim} under {cap} in multiples of {mult}")


def _params(sem=None):
    return pltpu.CompilerParams(dimension_semantics=sem, vmem_limit_bytes=VMEM_LIMIT_BYTES)


def _sds(shape, dtype):
    return jax.ShapeDtypeStruct(tuple(shape), dtype)


def _rstd(x):
    return lax.rsqrt(jnp.mean(x * x, axis=-1, keepdims=True) + RMS_EPS)


_DOT_DIMS = {"nn": ((1,), (0,)), "nt": ((1,), (1,)), "tn": ((0,), (0,))}


def _mm(a, b, mode, out_dtypes, *, name, epilogue=None, extras=(), tm=1024, tn=1024, tk=2048, half=None):
    if mode == "nn":
        (m, k), (k2, n) = a.shape, b.shape
    elif mode == "nt":
        (m, k), (n, k2) = a.shape, b.shape
    else:
        (k, m), (k2, n) = a.shape, b.shape
    assert k == k2, (a.shape, b.shape, mode)
    side = half[1] if half is not None else None
    if side == "m":
        m //= 2
    elif side == "n":
        n //= 2
    tm, tn, tk = _tile(m, tm, 8), _tile(n, tn, LANES), _tile(k, tk, LANES)
    nk = k // tk
    row = (lambda i, s: i + s[0][0] * (m // tm)) if side == "m" else (lambda i, s: i)
    col = (lambda j, s: j + s[0][0] * (n // tn)) if side == "n" else (lambda j, s: j)
    a_spec = pl.BlockSpec((tk, tm), lambda i, j, kk, *s: (kk, row(i, s))) if mode == "tn" else pl.BlockSpec((tm, tk), lambda i, j, kk, *s: (row(i, s), kk))
    b_spec = pl.BlockSpec((tn, tk), lambda i, j, kk, *s: (col(j, s), kk)) if mode == "nt" else pl.BlockSpec((tk, tn), lambda i, j, kk, *s: (kk, col(j, s)))
    tile_spec = pl.BlockSpec((tm, tn), lambda i, j, kk, *s: (i, j))
    n_extra, n_out = len(extras), len(out_dtypes)
    dims = (_DOT_DIMS[mode], ((), ()))

    def body(*refs):
        if half is not None:
            refs = refs[1:]
        a_ref, b_ref = refs[:2]
        extra_refs = refs[2:2 + n_extra]
        out_refs = refs[2 + n_extra:2 + n_extra + n_out]
        part = lax.dot_general(a_ref[...].astype(BF16), b_ref[...].astype(BF16), dims, preferred_element_type=F32)

        def finish(total):
            res = epilogue(total, *[e[...] for e in extra_refs]) if epilogue is not None else (total,)
            for o, r in zip(out_refs, res):
                o[...] = r.astype(o.dtype)

        if nk == 1:
            finish(part)
        else:
            acc = refs[-1]
            kk = pl.program_id(2)

            @pl.when(kk == 0)
            def _():
                acc[...] = part

            @pl.when(kk > 0)
            def _():
                acc[...] += part

            @pl.when(kk == nk - 1)
            def _():
                finish(acc[...])

    grid = (m // tm, n // tn, nk)
    in_specs, out_specs = [a_spec, b_spec] + [tile_spec] * n_extra, [tile_spec] * n_out
    scratch = [pltpu.VMEM((tm, tn), F32)] if nk > 1 else []
    if half is None:
        layout, lead = dict(grid=grid, in_specs=in_specs, out_specs=out_specs, scratch_shapes=scratch), ()
    else:
        layout = dict(grid_spec=pltpu.PrefetchScalarGridSpec(num_scalar_prefetch=1, grid=grid, in_specs=in_specs, out_specs=out_specs,
                                                             scratch_shapes=scratch))
        lead = (half[0],)
    outs = pl.pallas_call(body, name=name, out_shape=[_sds((m, n), d) for d in out_dtypes],
                          compiler_params=_params(("parallel", "parallel", "arbitrary")), **layout)(*lead, a, b, *extras)
    return outs if n_out > 1 else outs[0]


ROW_TILE = 512
GATE_ROW_TILE = 1024


def _norm_first(x, g):
    s, d = x.shape
    ts = _tile(s, ROW_TILE, 8)

    def body(x_ref, g_ref, h_ref):
        xv = x_ref[...]
        h_ref[...] = (xv * _rstd(xv) * g_ref[...]).astype(h_ref.dtype)

    row = pl.BlockSpec((ts, d), lambda i: (i, 0))
    vec = pl.BlockSpec((1, d), lambda i: (0, 0))
    return pl.pallas_call(body, name="norm_first", grid=(s // ts,), in_specs=[row, vec], out_specs=row,
                          out_shape=_sds((s, d), BF16), compiler_params=_params(("parallel",)))(x, g)


def _resid_norm(xres, y, g_post, g_next):
    s, d = xres.shape
    ts = _tile(s, ROW_TILE, 8)

    def body(x_ref, y_ref, gp_ref, gn_ref, xn_ref, hn_ref):
        yv = y_ref[...].astype(F32)
        xn = x_ref[...] + yv * _rstd(yv) * gp_ref[...]
        xn_ref[...] = xn
        hn_ref[...] = (xn * _rstd(xn) * gn_ref[...]).astype(hn_ref.dtype)

    row = pl.BlockSpec((ts, d), lambda i: (i, 0))
    vec = pl.BlockSpec((1, d), lambda i: (0, 0))
    return pl.pallas_call(body, name="resid_norm", grid=(s // ts,), in_specs=[row, row, vec, vec], out_specs=[row, row],
                          out_shape=[_sds((s, d), F32), _sds((s, d), BF16)], compiler_params=_params(("parallel",)))(xres, y, g_post, g_next)


def _resid_norm_loss(xres, y, g_post, target):
    s, d = xres.shape
    ts = _tile(s, ROW_TILE, 8)
    n_steps = s // ts

    def body(x_ref, y_ref, gp_ref, t_ref, dout_ref, loss_ref, acc_ref):
        i = pl.program_id(0)
        yv = y_ref[...].astype(F32)
        err = x_ref[...] + yv * _rstd(yv) * gp_ref[...] - t_ref[...]
        dout_ref[...] = err / d
        part = jnp.sum(err * err, axis=0, keepdims=True)

        @pl.when(i == 0)
        def _():
            acc_ref[...] = part

        @pl.when(i > 0)
        def _():
            acc_ref[...] += part

        @pl.when(i == n_steps - 1)
        def _():
            loss_ref[...] = 0.5 * jnp.sum(acc_ref[...], axis=1, keepdims=True) / d

    row = pl.BlockSpec((ts, d), lambda i: (i, 0))
    vec = pl.BlockSpec((1, d), lambda i: (0, 0))
    one = pl.BlockSpec((1, 1), lambda i: (0, 0))
    return pl.pallas_call(body, name="resid_norm_loss", grid=(n_steps,), in_specs=[row, row, vec, row], out_specs=[row, one],
                          out_shape=[_sds((s, d), F32), _sds((1, 1), F32)], scratch_shapes=[pltpu.VMEM((1, d), F32)],
                          compiler_params=_params(("arbitrary",)))(xres, y, g_post, target)


def _norm_bwd(xin, g, dout, dres, out_dtype, name):
    s, d = xin.shape
    ts = _tile(s, ROW_TILE, 8)
    has_res = dres is not None

    def body(*refs):
        x_ref, g_ref, do_ref = refs[:3]
        dx_ref, dg_ref = refs[-2:]
        i = pl.program_id(0)
        xv, dov = x_ref[...].astype(F32), do_ref[...].astype(F32)
        r = _rstd(xv)
        xhat = xv * r
        dg = jnp.sum(dov * xhat, axis=0, keepdims=True)
        dxh = dov * g_ref[...]
        dx = r * (dxh - xhat * jnp.mean(dxh * xhat, axis=-1, keepdims=True))
        if has_res:
            dx = dx + refs[3][...]
        dx_ref[...] = dx.astype(dx_ref.dtype)

        @pl.when(i == 0)
        def _():
            dg_ref[...] = dg

        @pl.when(i > 0)
        def _():
            dg_ref[...] += dg

    row = pl.BlockSpec((ts, d), lambda i: (i, 0))
    vec = pl.BlockSpec((1, d), lambda i: (0, 0))
    ops = [xin, g, dout] + ([dres] if has_res else [])
    return pl.pallas_call(body, name=name, grid=(s // ts,), in_specs=[row, vec, row] + ([row] if has_res else []),
                          out_specs=[row, vec], out_shape=[_sds((s, d), out_dtype), _sds((1, d), F32)],
                          compiler_params=_params(("arbitrary",)))(*ops)


class _Cols:
    def __init__(self, d):
        self.d = d
        self.kv = d // GROUP
        self.cb, self.cc, self.ci, self.q = 0, d, 2 * d, 3 * d
        self.k = 4 * d
        self.v = 4 * d + self.kv
        self.ga = 4 * d + 2 * self.kv
        self.gb = 5 * d + 2 * self.kv
        self.width = 6 * d + 2 * self.kv


CONV_COLS = 128


def _shift_rows(u, down):
    s = u.shape[0]
    rows = lax.broadcasted_iota(jnp.int32, u.shape, 0)
    if down:
        return jnp.where(rows == 0, 0.0, pltpu.roll(u, 1, 0))
    return jnp.where(rows == s - 1, 0.0, pltpu.roll(u, s - 1, 0))


def _conv_fwd(z, w, cols):
    s, d = z.shape[0], cols.d
    cw = CONV_COLS

    def body(cb_ref, cc_ref, ci_ref, w_ref, a_ref):
        u = cc_ref[...].astype(F32) * ci_ref[...].astype(F32)
        wv = w_ref[...]
        conv = wv[0:1] * _shift_rows(u, True) + wv[1:2] * u + wv[2:3] * _shift_rows(u, False)
        a_ref[...] = (cb_ref[...].astype(F32) * conv).astype(a_ref.dtype)

    def zspec(off):
        return pl.BlockSpec((s, cw), lambda j: (0, off // cw + j))

    return pl.pallas_call(body, name="conv_fwd", grid=(d // cw,),
                          in_specs=[zspec(cols.cb), zspec(cols.cc), zspec(cols.ci), pl.BlockSpec((3, cw), lambda j: (0, j))],
                          out_specs=pl.BlockSpec((s, cw), lambda j: (0, j)), out_shape=_sds((s, d), BF16),
                          compiler_params=_params(("parallel",)))(z, z, z, w)


def _conv_bwd(z, w, da, cols):
    s, d = z.shape[0], cols.d
    cw = CONV_COLS

    def body(cb_ref, cc_ref, ci_ref, w_ref, da_ref, dcb_ref, dcc_ref, dci_ref, dw_ref):
        cb, cc, ci, dav = cb_ref[...].astype(F32), cc_ref[...].astype(F32), ci_ref[...].astype(F32), da_ref[...].astype(F32)
        wv = w_ref[...]
        u = cc * ci
        um, up = _shift_rows(u, True), _shift_rows(u, False)
        conv = wv[0:1] * um + wv[1:2] * u + wv[2:3] * up
        dcb_ref[...] = (dav * conv).astype(dcb_ref.dtype)
        dconv = dav * cb
        dw_ref[0:1, :] = jnp.sum(dconv * um, axis=0, keepdims=True)
        dw_ref[1:2, :] = jnp.sum(dconv * u, axis=0, keepdims=True)
        dw_ref[2:3, :] = jnp.sum(dconv * up, axis=0, keepdims=True)
        du = wv[0:1] * _shift_rows(dconv, False) + wv[1:2] * dconv + wv[2:3] * _shift_rows(dconv, True)
        dcc_ref[...] = (du * ci).astype(dcc_ref.dtype)
        dci_ref[...] = (du * cc).astype(dci_ref.dtype)

    def zspec(off):
        return pl.BlockSpec((s, cw), lambda j: (0, off // cw + j))

    col = pl.BlockSpec((s, cw), lambda j: (0, j))
    wspec = pl.BlockSpec((3, cw), lambda j: (0, j))
    return pl.pallas_call(body, name="conv_bwd", grid=(d // cw,),
                          in_specs=[zspec(cols.cb), zspec(cols.cc), zspec(cols.ci), wspec, col],
                          out_specs=[col, col, col, wspec],
                          out_shape=[_sds((s, d), BF16)] * 3 + [_sds((3, d), F32)],
                          compiler_params=_params(("parallel",)))(z, z, z, w, da)


def _gate_fwd(z, bias, y_a, y_b, cols):
    s, d = y_a.shape
    ts, cw = _tile(s, GATE_ROW_TILE, 8), cols.kv
    nj = d // cw

    def body(ga_ref, gb_ref, ba_ref, bb_ref, ya_ref, yb_ref, o_ref):
        gate_a = jax.nn.sigmoid(ga_ref[...].astype(F32) + ba_ref[...])
        gate_b = jax.nn.sigmoid(gb_ref[...].astype(F32) + bb_ref[...])
        o_ref[...] = (gate_a * ya_ref[...].astype(F32) + gate_b * yb_ref[...].astype(F32)).astype(o_ref.dtype)

    tile = pl.BlockSpec((ts, cw), lambda i, j: (i, j))
    return pl.pallas_call(
        body, name="gate_fwd", grid=(s // ts, nj),
        in_specs=[pl.BlockSpec((ts, cw), lambda i, j: (i, cols.ga // cw + j)), pl.BlockSpec((ts, cw), lambda i, j: (i, cols.gb // cw + j)),
                  pl.BlockSpec((1, cw), lambda i, j: (0, j)), pl.BlockSpec((1, cw), lambda i, j: (0, nj + j)), tile, tile],
        out_specs=tile, out_shape=_sds((s, d), BF16), compiler_params=_params(("parallel", "parallel")))(z, z, bias, bias, y_a, y_b)


def _gate_bwd(z, bias, y_a, y_b, dmix, cols):
    s, d = y_a.shape
    ts, cw = _tile(s, GATE_ROW_TILE, 8), cols.kv
    nj = d // cw

    def body(ga_ref, gb_ref, ba_ref, bb_ref, ya_ref, yb_ref, dm_ref, dya_ref, dyb_ref, dga_ref, dgb_ref, dba_ref, dbb_ref):
        i = pl.program_id(1)
        gate_a = jax.nn.sigmoid(ga_ref[...].astype(F32) + ba_ref[...])
        gate_b = jax.nn.sigmoid(gb_ref[...].astype(F32) + bb_ref[...])
        dm = dm_ref[...].astype(F32)
        dya_ref[...] = (dm * gate_a).astype(dya_ref.dtype)
        dyb_ref[...] = (dm * gate_b).astype(dyb_ref.dtype)
        dga = dm * ya_ref[...].astype(F32) * (gate_a * (1.0 - gate_a))
        dgb = dm * yb_ref[...].astype(F32) * (gate_b * (1.0 - gate_b))
        dga_ref[...] = dga.astype(dga_ref.dtype)
        dgb_ref[...] = dgb.astype(dgb_ref.dtype)
        sa = jnp.sum(dga, axis=0, keepdims=True)
        sb = jnp.sum(dgb, axis=0, keepdims=True)

        @pl.when(i == 0)
        def _():
            dba_ref[...] = sa
            dbb_ref[...] = sb

        @pl.when(i > 0)
        def _():
            dba_ref[...] += sa
            dbb_ref[...] += sb

    tile = pl.BlockSpec((ts, cw), lambda j, i: (i, j))
    vec = pl.BlockSpec((1, cw), lambda j, i: (0, j))
    return pl.pallas_call(
        body, name="gate_bwd", grid=(nj, s // ts),
        in_specs=[pl.BlockSpec((ts, cw), lambda j, i: (i, cols.ga // cw + j)), pl.BlockSpec((ts, cw), lambda j, i: (i, cols.gb // cw + j)),
                  vec, pl.BlockSpec((1, cw), lambda j, i: (0, nj + j)), tile, tile, tile],
        out_specs=[tile, tile, tile, tile, vec, vec],
        out_shape=[_sds((s, d), BF16)] * 4 + [_sds((1, d), F32)] * 2,
        compiler_params=_params(("parallel", "arbitrary")))(z, z, bias, bias, y_a, y_b, dmix)


def _rope_tables(s):
    axis_dim = HEAD_DIM // 2
    n_freq = axis_dim // 2
    rows = s // GRID_W
    row_idx = jnp.repeat(jnp.arange(rows, dtype=jnp.int32), GRID_W)
    col_idx = jnp.tile(jnp.arange(GRID_W, dtype=jnp.int32), rows)
    inv_freq = ROPE_THETA ** (-jnp.arange(0, axis_dim, 2, dtype=F32) / axis_dim)
    ang = jnp.stack([row_idx.astype(F32)[:, None] * inv_freq, col_idx.astype(F32)[:, None] * inv_freq], axis=1)
    cos, sin = jnp.cos(ang), jnp.sin(ang)
    cos_t = jnp.stack([cos, cos], axis=2).reshape(s, HEAD_DIM)
    sin_t = jnp.stack([-sin, sin], axis=2).reshape(s, HEAD_DIM)
    return cos_t, sin_t


def _partner(x):
    n = x.shape[-1]
    lane = lax.broadcasted_iota(jnp.int32, x.shape, x.ndim - 1)
    quarter = HEAD_DIM // 4
    return jnp.where(lane % (2 * quarter) < quarter, pltpu.roll(x, n - quarter, x.ndim - 1), pltpu.roll(x, quarter, x.ndim - 1))


LOG2E = math.log2(math.e)
Q_SCALE = LOG2E / math.sqrt(HEAD_DIM)


def _rope_fwd(z, qn, kn, cos_t, sin_t, cols):
    s, d, kv = z.shape[0], cols.d, cols.kv
    ts = _tile(s, ROW_TILE, 8)
    scale = Q_SCALE

    def body(q_ref, k_ref, v_ref, qn_ref, kn_ref, c_ref, s_ref, qo_ref, ko_ref, vo_ref):
        c, sn = c_ref[...], s_ref[...]

        def head(xh, g):
            xn = xh * _rstd(xh) * g
            return xn * c + _partner(xn) * sn

        for h in range(d // HEAD_DIM):
            sl = slice(h * HEAD_DIM, (h + 1) * HEAD_DIM)
            qo_ref[:, sl] = (head(q_ref[:, sl].astype(F32), qn_ref[...]) * scale).astype(qo_ref.dtype)
        for h in range(kv // HEAD_DIM):
            sl = slice(h * HEAD_DIM, (h + 1) * HEAD_DIM)
            ko_ref[:, sl] = head(k_ref[:, sl].astype(F32), kn_ref[...]).astype(ko_ref.dtype)
        vo_ref[...] = v_ref[...].astype(vo_ref.dtype)

    vec = pl.BlockSpec((1, HEAD_DIM), lambda i: (0, 0))
    tab = pl.BlockSpec((ts, HEAD_DIM), lambda i: (i, 0))
    return pl.pallas_call(
        body, name="rope_fwd", grid=(s // ts,),
        in_specs=[pl.BlockSpec((ts, d), lambda i: (i, cols.q // d)), pl.BlockSpec((ts, kv), lambda i: (i, cols.k // kv)),
                  pl.BlockSpec((ts, kv), lambda i: (i, cols.v // kv)), vec, vec, tab, tab],
        out_specs=[pl.BlockSpec((ts, d), lambda i: (i, 0)), pl.BlockSpec((ts, kv), lambda i: (i, 0)), pl.BlockSpec((ts, kv), lambda i: (i, 0))],
        out_shape=[_sds((s, d), BF16), _sds((s, kv), BF16), _sds((s, kv), BF16)],
        compiler_params=_params(("parallel",)))(z, z, z, qn, kn, cos_t, sin_t)


def _rope_bwd(z, qn, kn, cos_t, sin_t, dq, dk_t, dv_t, cols):
    s, d, kv = z.shape[0], cols.d, cols.kv
    ts = _tile(s, ROW_TILE, 8)
    scale = 1.0 / math.sqrt(HEAD_DIM)

    def body(q_ref, k_ref, qn_ref, kn_ref, c_ref, s_ref, dq_ref, dkt_ref, dvt_ref, dzq_ref, dzk_ref, dzv_ref, dqn_ref, dkn_ref):
        i = pl.program_id(0)
        c, sn = c_ref[...], s_ref[...]
        dk_all = dkt_ref[...].T * (1.0 / LOG2E)

        def head_bwd(xh, g, drot):
            dxn = drot * c + _partner(drot * sn)
            r = _rstd(xh)
            xhat = xh * r
            dgain = jnp.sum(dxn * xhat, axis=0, keepdims=True)
            dxh = dxn * g
            return r * (dxh - xhat * jnp.mean(dxh * xhat, axis=-1, keepdims=True)), dgain

        dqn = jnp.zeros((1, HEAD_DIM), F32)
        for h in range(d // HEAD_DIM):
            sl = slice(h * HEAD_DIM, (h + 1) * HEAD_DIM)
            dx, dg = head_bwd(q_ref[:, sl].astype(F32), qn_ref[...], dq_ref[:, sl].astype(F32) * scale)
            dzq_ref[:, sl] = dx.astype(dzq_ref.dtype)
            dqn = dqn + dg
        dkn = jnp.zeros((1, HEAD_DIM), F32)
        for h in range(kv // HEAD_DIM):
            sl = slice(h * HEAD_DIM, (h + 1) * HEAD_DIM)
            dx, dg = head_bwd(k_ref[:, sl].astype(F32), kn_ref[...], dk_all[:, sl])
            dzk_ref[:, sl] = dx.astype(dzk_ref.dtype)
            dkn = dkn + dg
        dzv_ref[...] = dvt_ref[...].T.astype(dzv_ref.dtype)

        @pl.when(i == 0)
        def _():
            dqn_ref[...] = dqn
            dkn_ref[...] = dkn

        @pl.when(i > 0)
        def _():
            dqn_ref[...] += dqn
            dkn_ref[...] += dkn

    vec = pl.BlockSpec((1, HEAD_DIM), lambda i: (0, 0))
    tab = pl.BlockSpec((ts, HEAD_DIM), lambda i: (i, 0))
    qrow = pl.BlockSpec((ts, d), lambda i: (i, 0))
    krow = pl.BlockSpec((ts, kv), lambda i: (i, 0))
    kcol = pl.BlockSpec((kv, ts), lambda i: (0, i))
    return pl.pallas_call(
        body, name="rope_bwd", grid=(s // ts,),
        in_specs=[pl.BlockSpec((ts, d), lambda i: (i, cols.q // d)), pl.BlockSpec((ts, kv), lambda i: (i, cols.k // kv)),
                  vec, vec, tab, tab, qrow, kcol, kcol],
        out_specs=[qrow, krow, krow, vec, vec],
        out_shape=[_sds((s, d), BF16), _sds((s, kv), BF16), _sds((s, kv), BF16), _sds((1, HEAD_DIM), F32), _sds((1, HEAD_DIM), F32)],
        compiler_params=_params(("arbitrary",)))(z, z, qn, kn, cos_t, sin_t, dq, dk_t, dv_t)


Q_TILE = 256
_NT = (((1,), (1,)), ((), ()))
_NN = (((1,), (0,)), ((), ()))


def _attn_fwd(q, k, v):
    s, d = q.shape
    kvh = k.shape[1] // HEAD_DIM
    tq = _tile(s, Q_TILE, LANES)
    gw = GROUP * HEAD_DIM

    def body(q_ref, k_ref, v_ref, o_ref, lse_ref):
        kk, vv = k_ref[...], v_ref[...]
        for g in range(GROUP):
            sl = slice(g * HEAD_DIM, (g + 1) * HEAD_DIM)
            sc = lax.dot_general(q_ref[:, sl], kk, _NT, preferred_element_type=F32)
            mx = jnp.max(sc, axis=-1, keepdims=True)
            p = jnp.exp2(sc - mx)
            l = jnp.sum(p, axis=-1, keepdims=True)
            o = lax.dot_general(p.astype(BF16), vv, _NN, preferred_element_type=F32) * (1.0 / l)
            o_ref[:, sl] = o.astype(o_ref.dtype)
            lse_ref[:, g:g + 1] = mx + jnp.log(l) * LOG2E

    return pl.pallas_call(
        body, name="attn_fwd", grid=(kvh, s // tq),
        in_specs=[pl.BlockSpec((tq, gw), lambda j, i: (i, j)), pl.BlockSpec((s, HEAD_DIM), lambda j, i: (0, j)), pl.BlockSpec((s, HEAD_DIM), lambda j, i: (0, j))],
        out_specs=[pl.BlockSpec((tq, gw), lambda j, i: (i, j)), pl.BlockSpec((None, tq, GROUP), lambda j, i: (j, i, 0))],
        out_shape=[_sds((s, d), BF16), _sds((kvh, s, GROUP), F32)],
        compiler_params=_params(("parallel", "parallel")))(q, k, v)


_TN = (((0,), (0,)), ((), ()))


def _attn_bwd(q, k, v, do, lse):
    s, d = q.shape
    kv = k.shape[1]
    kvh = kv // HEAD_DIM
    tq = _tile(s, Q_TILE, LANES)
    gw = GROUP * HEAD_DIM

    def body(q_ref, k_ref, v_ref, do_ref, lse_ref, dq_ref, dkt_ref, dvt_ref):
        i = pl.program_id(1)
        kk, vv = k_ref[...], v_ref[...]

        @pl.when(i == 0)
        def _():
            dkt_ref[...] = jnp.zeros_like(dkt_ref)
            dvt_ref[...] = jnp.zeros_like(dvt_ref)

        for g in range(GROUP):
            sl = slice(g * HEAD_DIM, (g + 1) * HEAD_DIM)
            qg, dog = q_ref[:, sl], do_ref[:, sl]
            sc = lax.dot_general(qg, kk, _NT, preferred_element_type=F32)
            p = jnp.exp2(sc - lse_ref[:, g:g + 1])
            dp = lax.dot_general(dog, vv, _NT, preferred_element_type=F32)
            delta = jnp.sum(p * dp, axis=-1, keepdims=True)
            ds = (p * (dp - delta)).astype(BF16)
            dq_ref[:, sl] = lax.dot_general(ds, kk, _NN, preferred_element_type=F32).astype(dq_ref.dtype)
            dkt_ref[...] += lax.dot_general(qg, ds, _TN, preferred_element_type=F32)
            dvt_ref[...] += lax.dot_general(dog, p.astype(BF16), _TN, preferred_element_type=F32)

    qspec = pl.BlockSpec((tq, gw), lambda j, i: (i, j))
    kspec = pl.BlockSpec((s, HEAD_DIM), lambda j, i: (0, j))
    stat = pl.BlockSpec((None, tq, GROUP), lambda j, i: (j, i, 0))
    tspec = pl.BlockSpec((HEAD_DIM, s), lambda j, i: (j, 0))
    return pl.pallas_call(
        body, name="attn_bwd", grid=(kvh, s // tq),
        in_specs=[qspec, kspec, kspec, qspec, stat], out_specs=[qspec, tspec, tspec],
        out_shape=[_sds((s, d), BF16), _sds((kv, s), F32), _sds((kv, s), F32)],
        compiler_params=_params(("parallel", "arbitrary")))(q, k, v, do, lse)


ELEM_BLOCK_BYTES = 2 << 20

BIG = (("w_in", True), ("w_out_conv", False), ("w_out_attn", False), ("w_merge", False), ("w_up", True), ("w_down", False))
N_BIG = len(BIG)


def _elem_tiles(rows, width):
    tc = _tile(width, 2048, LANES)
    tr = _tile(rows, max(8, ELEM_BLOCK_BYTES // (4 * tc)), 8)
    return tr, tc


def _scalar_grid(grid, in_specs, out_specs):
    return pltpu.PrefetchScalarGridSpec(num_scalar_prefetch=1, grid=grid, in_specs=in_specs, out_specs=out_specs)


def _cast_place(w_stack, layer, chip, col_sharded, name):
    _, rows, width = w_stack.shape
    tr, tc = _elem_tiles(rows, width)
    nr, nc = rows // tr, width // tc

    def body(sc_ref, x_ref, o_ref):
        o_ref[...] = x_ref[...].astype(o_ref.dtype)

    if col_sharded:
        full, out_spec = (rows, width * N_CHIPS), pl.BlockSpec((tr, tc), lambda i, j, sc: (i, sc[0] * nc + j))
    else:
        full, out_spec = (rows * N_CHIPS, width), pl.BlockSpec((tr, tc), lambda i, j, sc: (sc[0] * nr + i, j))
    return pl.pallas_call(
        body, name=name,
        grid_spec=_scalar_grid((nr, nc), [pl.BlockSpec((None, tr, tc), lambda i, j, sc: (layer, i, j))], out_spec),
        out_shape=_sds(full, BF16), compiler_params=_params(("parallel", "parallel")))(chip, w_stack)


def _add_landed(acc, landed):
    return (acc + landed,)


def _slot_of_relation(rel):
    return jnp.where(rel == 2, 0, jnp.where(rel == 1, 1, 2))


def _sum_chips(pair_sum, landed, chip, col_sharded, name):
    _, rows, width = landed.shape
    tr, tc = _elem_tiles(rows, width)
    nr, nc = rows // tr, width // tc

    def body(chip_ref, own_ref, q_ref, o_ref):
        me = chip_ref[0]
        own = own_ref[...].astype(F32)
        acc = None
        for t in range(N_CHIPS):
            rel = me ^ t
            term = jnp.where(rel == 0, own, q_ref[_slot_of_relation(rel)].astype(F32))
            acc = term if acc is None else acc + term
        o_ref[...] = acc

    if col_sharded:
        own_spec = pl.BlockSpec((tr, tc), lambda i, j, c: (i, c[0] * nc + j))
    else:
        own_spec = pl.BlockSpec((tr, tc), lambda i, j, c: (c[0] * nr + i, j))
    return pl.pallas_call(
        body, name=name,
        grid_spec=_scalar_grid((nr, nc), [own_spec, pl.BlockSpec((N_CHIPS - 1, tr, tc), lambda i, j, c: (0, i, j))],
                               pl.BlockSpec((tr, tc), lambda i, j, c: (i, j))),
        out_shape=_sds((rows, width), F32), compiler_params=_params(("parallel", "parallel")))(chip, pair_sum, landed)


def _adamw_math(w, g, m, v):
    mn = ADAM_B1 * m + (1.0 - ADAM_B1) * g
    vn = ADAM_B2 * v + (1.0 - ADAM_B2) * jnp.square(g)
    m_hat = mn / (1.0 - ADAM_B1 ** ADAM_STEP)
    v_hat = vn / (1.0 - ADAM_B2 ** ADAM_STEP)
    return -ADAM_LR * (m_hat / (jnp.sqrt(v_hat) + ADAM_EPS) + ADAM_WD * w), mn, vn


def _adamw(w, g, m, v, name):
    shape = w.shape
    width = shape[-1]
    w2, g2, m2, v2 = (a.reshape(-1, width) for a in (w, g, m, v))
    rows = w2.shape[0]
    tr, tc = _elem_tiles(rows, width)

    def body(w_ref, g_ref, m_ref, v_ref, d_ref, nm_ref, nv_ref):
        d_ref[...], nm_ref[...], nv_ref[...] = _adamw_math(w_ref[...], g_ref[...], m_ref[...], v_ref[...])

    tile = pl.BlockSpec((tr, tc), lambda i, j: (i, j))
    outs = pl.pallas_call(body, name=name, grid=(rows // tr, width // tc), in_specs=[tile] * 4, out_specs=[tile] * 3,
                          out_shape=[_sds((rows, width), F32)] * 3, compiler_params=_params(("parallel", "parallel")))(w2, g2, m2, v2)
    return tuple(o.reshape(shape) for o in outs)


def _adamw_layer(w, m, v, g_mine, g_sibling, carried, layer, core, col_sharded, name):
    depth, rows, width = w.shape
    pr, pc = g_mine.shape
    tr, tc = _elem_tiles(pr, pc)
    n_half = pr // tr if col_sharded else pc // tc
    if carried is None:
        carried = tuple(lax.empty((depth, rows, width), F32) for _ in range(4))

    def body(sc_ref, w_ref, m_ref, v_ref, gm_ref, gs_ref, *rest):
        g_ref, d_ref, nm_ref, nv_ref = rest[-4:]
        pos = pl.program_id(0) if col_sharded else pl.program_id(1)
        gv = jnp.where(pos // n_half == sc_ref[0], gm_ref[...], gs_ref[...])
        g_ref[...] = gv
        d_ref[...], nm_ref[...], nv_ref[...] = _adamw_math(w_ref[...], gv, m_ref[...], v_ref[...])

    stacked = pl.BlockSpec((None, tr, tc), lambda i, j, sc: (layer, i, j))
    n_cols = width // tc

    def half_spec(mine):
        def index(i, j, sc):
            own = sc[0] if mine else 1 - sc[0]
            pos = i if col_sharded else j
            used, before = pos // n_half == own, pos // n_half < own
            within = jnp.where(used, pos % n_half, jnp.where(before, 0, n_half - 1))
            if col_sharded:
                return within, jnp.where(used, j, jnp.where(before, 0, n_cols - 1))
            return i, within
        return pl.BlockSpec((tr, tc), index)

    return pl.pallas_call(
        body, name=name,
        grid_spec=_scalar_grid((rows // tr, width // tc), [stacked] * 3 + [half_spec(True), half_spec(False)] + [ANY] * 4, [stacked] * 4),
        out_shape=[_sds((depth, rows, width), F32)] * 4, input_output_aliases={6: 0, 7: 1, 8: 2, 9: 3},
        compiler_params=_params(("parallel", "parallel")))(core, w, m, v, g_mine, g_sibling, *carried)


_SEM = pl.BlockSpec(memory_space=pltpu.SEMAPHORE)
_HBM = pl.BlockSpec(memory_space=pltpu.HBM)
_VMEM = pl.BlockSpec(memory_space=pltpu.VMEM)
_EFFECT = pltpu.SideEffectType.DATAFLOW_SIDE_EFFECTING


def _place():
    x, y, c = lax.axis_index("x"), lax.axis_index("y"), lax.axis_index("c")
    others = [(1 - x, y), (x, 1 - y), (1 - x, 1 - y)]
    return x, y, c, others


def _chip_index(px, py):
    return 2 * px + py


def _remote(src, dst, send_sems, recv_sems, k, to):
    return pltpu.make_async_remote_copy(src_ref=src, dst_ref=dst, send_sem=send_sems.at[k], recv_sem=recv_sems.at[k],
                                        device_id=to, device_id_type=MESH)


def _phase(name, bufs, waits, wait_fn, n_start, start_fn, deps):
    nb, nd = len(bufs), len(deps)

    def body(*refs):
        buf_refs = refs[:nb]
        pos = nb
        if waits is not None:
            wait_fn(buf_refs, refs[pos], refs[pos + 1])
            pos += 2
        pos += nd
        if n_start:
            start_fn(buf_refs, refs[pos], refs[pos + 1])
            pos += 2
        token = refs[pos + nb]
        token[...] = jnp.zeros_like(token)

    n_sem_out = 2 if n_start else 0
    if waits is None:
        bufs = [pltpu.with_memory_space_constraint(b, pltpu.HBM) for b in bufs]
    outs = pl.pallas_call(
        body, name=name,
        in_specs=[_HBM] * nb + ([_SEM] * 2 if waits is not None else []) + [ANY] * nd,
        out_specs=[_SEM] * n_sem_out + [_HBM] * nb + [_VMEM],
        out_shape=[pltpu.SemaphoreType.DMA((n_start,))] * n_sem_out + [pltpu.HBM(b.shape, b.dtype) for b in bufs] + [_sds((8, LANES), F32)],
        input_output_aliases={i: n_sem_out + i for i in range(nb)},
        compiler_params=pltpu.CompilerParams(has_side_effects=_EFFECT),
    )(*bufs, *(waits if waits is not None else ()), *deps)
    sems = tuple(outs[:2]) if n_start else None
    return sems, list(outs[n_sem_out:n_sem_out + nb]), outs[-1]


class _Chains:
    def __init__(self):
        self.active = []
        self.last = None
        self.dep = None
        self.pinned = []
        self.token = None
        self.at = None

    def phase(self, name, bufs, waits, wait_fn, n_start, start_fn):
        deps = [a for a in (self.last, self.dep) if a is not None] + self.pinned
        sems, thru, token = _phase(name, bufs, waits, wait_fn, n_start, start_fn, deps)
        self.last, self.dep, self.token, self.pinned = token, None, token, []
        return sems, thru

    def pin(self, result):
        self.pinned.append(result)

    def add(self, gen):
        self.active.append(gen)

    def tick(self, dep, at=None):
        self.at, self.token, self.dep = at, None, dep
        for gen in list(self.active):
            if next(gen, "done") == "done":
                self.active.remove(gen)
        return self.token


def _after(small, token):
    return small if token is None else small + token[0, 0]


def _shard_region(ref, col_sharded, chip, n_shard):
    start = pl.multiple_of(chip * n_shard, LANES if col_sharded else 8)
    if col_sharded:
        return ref.at[:, pl.ds(start, n_shard)]
    return ref.at[pl.ds(start, n_shard), :]


def _row_half(ref, half):
    n_rows = ref.shape[0]
    return ref.at[pl.ds(pl.multiple_of(half * (n_rows // 2), 8), n_rows // 2), :]


def _gather_chain(chains, tag, group, bufs, out, wait_for):
    n_w = len(group)
    n = 3 * n_w

    def region(refs, a, chip, half):
        col = BIG[group[a]][1]
        n_shard = refs[a].shape[1] // N_CHIPS if col else refs[a].shape[0] // N_CHIPS
        return _row_half(_shard_region(refs[a], col, chip, n_shard), half)

    def start_ici(refs, send, recv):
        x, y, c, others = _place()
        for a in range(n_w):
            mine = region(refs, a, _chip_index(x, y), c)
            for j, (ox, oy) in enumerate(others):
                _remote(mine, mine, send, recv, 3 * a + j, (ox, oy, c)).start()

    def wait_ici(refs, send, recv):
        x, y, c, others = _place()
        for a in range(n_w):
            for j, (ox, oy) in enumerate(others):
                landed = region(refs, a, _chip_index(ox, oy), c)
                cp = _remote(landed, landed, send, recv, 3 * a + j, (x, y, 1 - c))
                cp.wait_recv()
                cp.wait_send()

    def start_d2d(refs, send, recv):
        x, y, c, others = _place()
        for a in range(n_w):
            for j, (ox, oy) in enumerate(others):
                landed = region(refs, a, _chip_index(ox, oy), c)
                _remote(landed, landed, send, recv, 3 * a + j, (x, y, 1 - c)).start()

    def wait_d2d(refs, send, recv):
        x, y, c, others = _place()
        for a in range(n_w):
            for j, (ox, oy) in enumerate(others):
                theirs = region(refs, a, _chip_index(ox, oy), 1 - c)
                cp = _remote(theirs, theirs, send, recv, 3 * a + j, (x, y, 1 - c))
                cp.wait_recv()
                cp.wait_send()

    sems, bufs = chains.phase(f"gather_ici_start_{tag}", bufs, None, None, n, start_ici)
    yield
    while wait_for is not None and chains.at != wait_for:
        yield
    sems, bufs = chains.phase(f"gather_forward_{tag}", bufs, sems, wait_ici, n, start_d2d)
    yield
    _, bufs = chains.phase(f"gather_done_{tag}", bufs, sems, wait_d2d, 0, None)
    for a, buf in zip(group, bufs):
        out[BIG[a][0]] = buf


def _grad_chain(chains, layer, group, pairs, core, other_core, chip, w, mom, var, carried):
    n = len(group)
    tag = f"{layer}_{group[0]}"
    kinds = [BIG[a][1] for a in group]
    names = [BIG[a][0] for a in group]
    sibling_of = lambda x, y, c: (x, y, 1 - c)

    def pair_start(refs, send, recv):
        x, y, c, _ = _place()
        for i in range(n):
            _remote(refs[i], refs[n + i], send, recv, i, sibling_of(x, y, c)).start()

    def pair_wait(refs, send, recv):
        x, y, c, _ = _place()
        for i in range(n):
            cp = _remote(refs[i], refs[n + i], send, recv, i, sibling_of(x, y, c))
            cp.wait_recv()
            cp.wait_send()

    def piece(ref, col, chip_idx):
        return _shard_region(ref, col, chip_idx, ref.shape[1] // N_CHIPS if col else ref.shape[0] // N_CHIPS)

    def scatter_start(refs, send, recv):
        x, y, c, others = _place()
        for i in range(n):
            for j, (ox, oy) in enumerate(others):
                _remote(piece(refs[i], kinds[i], _chip_index(ox, oy)), refs[n + i].at[j], send, recv, 3 * i + j, (ox, oy, c)).start()

    def scatter_wait(refs, send, recv):
        x, y, c, others = _place()
        for i in range(n):
            for j, (ox, oy) in enumerate(others):
                cp = _remote(piece(refs[i], kinds[i], _chip_index(ox, oy)), refs[n + i].at[j], send, recv, 3 * i + j, (ox, oy, c))
                cp.wait_recv()
                cp.wait_send()

    def join_start(refs, send, recv):
        x, y, c, _ = _place()
        for i in range(n):
            _remote(refs[i], refs[n + i], send, recv, i, sibling_of(x, y, c)).start()

    def join_wait(refs, send, recv):
        x, y, c, _ = _place()
        for i in range(n):
            cp = _remote(refs[i], refs[n + i], send, recv, i, sibling_of(x, y, c))
            cp.wait_recv()
            cp.wait_send()

    sides = ["m" if col else "n" for col in kinds]
    sends = [_mm(a, b, "tn", [F32], name="mm_g_send_" + names[i], half=(other_core, sides[i])) for i, (a, b) in enumerate(pairs)]
    half_shapes = [g.shape for g in sends]
    lands = [lax.empty(s, F32) for s in half_shapes]
    sems, bufs = chains.phase(f"pair_start_{tag}", sends + lands, None, None, n, pair_start)
    yield
    _, bufs = chains.phase(f"pair_wait_{tag}", bufs, sems, pair_wait, 0, None)
    pair_sums = [_mm(a, b, "tn", [BF16], name="mm_g_keep_" + names[i], half=(core, sides[i]), epilogue=_add_landed, extras=(bufs[n + i],))
                 for i, (a, b) in enumerate(pairs)]
    piece_shapes = [(s[0], s[1] // N_CHIPS) if col else (s[0] // N_CHIPS, s[1]) for col, s in zip(kinds, half_shapes)]
    slots = [lax.empty((N_CHIPS - 1, *s), BF16) for s in piece_shapes]
    sems, bufs = chains.phase(f"scatter_start_{tag}", pair_sums + slots, None, None, 3 * n, scatter_start)
    yield
    yield
    _, bufs = chains.phase(f"scatter_wait_{tag}", bufs, sems, scatter_wait, 0, None)
    reduced = [_sum_chips(bufs[i], bufs[n + i], chip, kinds[i], "sum_chips_" + names[i]) for i in range(n)]
    theirs = [lax.empty(s, F32) for s in piece_shapes]
    sems, bufs = chains.phase(f"join_start_{tag}", reduced + theirs, None, None, n, join_start)
    yield
    _, bufs = chains.phase(f"join_wait_{tag}", bufs, sems, join_wait, 0, None)
    for i in range(n):
        carried[names[i]] = _adamw_layer(w[names[i]], mom[names[i]], var[names[i]], bufs[i], bufs[n + i], carried.get(names[i]),
                                         layer, core, kinds[i], "adamw_" + names[i])
        chains.pin(carried[names[i]][0])


def _allreduce_small(vec, name):
    rows = vec.shape[0]
    masks = [(dx, dy, dc) for dx in (0, 1) for dy in (0, 1) for dc in (0, 1)][1:]

    def body(v_ref, o_ref, gather_ref, send_sems, recv_sems):
        x, y, c, _ = _place()
        me = 4 * x + 2 * y + c
        gather_ref[me] = v_ref[...]
        copies = []
        for k, (dx, dy, dc) in enumerate(masks):
            peer = (x ^ dx, y ^ dy, c ^ dc)
            copies.append(_remote(v_ref, gather_ref.at[me], send_sems, recv_sems, k, peer))
        for cp in copies:
            cp.start()
        for k, (dx, dy, dc) in enumerate(masks):
            slot = gather_ref.at[4 * (x ^ dx) + 2 * (y ^ dy) + (c ^ dc)]
            _remote(slot, slot, send_sems, recv_sems, k, (x, y, c)).wait_recv()
        for cp in copies:
            cp.wait_send()
        acc = gather_ref[0]
        for dev in range(1, N_DEV):
            acc = acc + gather_ref[dev]
        o_ref[...] = acc

    return pl.pallas_call(
        body, name=name, in_specs=[_VMEM], out_specs=_VMEM, out_shape=_sds((rows, LANES), F32),
        scratch_shapes=[pltpu.VMEM((N_DEV, rows, LANES), F32), pltpu.SemaphoreType.DMA((N_DEV - 1,)), pltpu.SemaphoreType.DMA((N_DEV - 1,))],
        compiler_params=pltpu.CompilerParams(has_side_effects=True, vmem_limit_bytes=VMEM_LIMIT_BYTES),
    )(vec)


def _relu2(acc):
    r = jnp.maximum(acc, 0.0)
    return acc, r * r


def _relu2_bwd(acc, up):
    return (acc * (2.0 * jnp.maximum(up.astype(F32), 0.0)),)


def _layer_fwd(x, h, w, cols, tables, tick, last, target=None):
    cos_t, sin_t = tables
    z = _mm(h, w("w_in"), "nn", [BF16], name="mm_in")
    a = _conv_fwd(z, w("conv_w"), cols)
    q, k, v = _rope_fwd(z, _after(w("q_norm"), tick(a, 0)), w("k_norm"), cos_t, sin_t, cols)
    o, lse = _attn_fwd(q, k, v)
    tick(o, 1)
    tick(None, 2)
    y_a = _mm(a, w("w_out_conv"), "nn", [BF16], name="mm_out_conv")
    y_b = _mm(o, w("w_out_attn"), "nn", [BF16], name="mm_out_attn")
    mix = _gate_fwd(z, w("gate_bias"), y_a, y_b, cols)
    mixed = _mm(mix, w("w_merge"), "nn", [BF16], name="mm_merge")
    x1, h2 = _resid_norm(x, mixed, _after(w("norm_mix_post"), tick(mixed, 3)), w("norm_mlp_pre"))
    up, act = _mm(h2, w("w_up"), "nn", [BF16, BF16], name="mm_up", epilogue=_relu2)
    f = _mm(act, w("w_down"), "nn", [BF16], name="mm_down")
    kept = dict(x=x, h=h, z=z, a=a, q=q, k=k, v=v, o=o, lse=lse, y_a=y_a, y_b=y_b, mix=mix, mixed=mixed, x1=x1, h2=h2, up=up, act=act, f=f)
    g_post = _after(w("norm_mlp_post"), tick(f, 4))
    if last:
        return _resid_norm_loss(x1, f, g_post, target), kept
    return _resid_norm(x1, f, g_post, w("norm_next")), kept


def _layer_bwd(dx_out, w, kept, cols, tables, tick, emit):
    cos_t, sin_t = tables
    t = kept
    df, d_norm_mlp_post = _norm_bwd(t["f"], _after(w["norm_mlp_post"], tick(dx_out, 0)), dx_out, None, BF16, "norm_bwd_mlp_post")
    dup = _mm(df, w["w_down"], "nt", [BF16], name="mm_d_act", epilogue=_relu2_bwd, extras=(t["up"],))
    emit((4, 5), [(t["h2"], dup), (t["act"], df)])
    dh2 = _mm(dup, w["w_up"], "nt", [BF16], name="mm_d_h2")
    dx1, d_norm_mlp_pre = _norm_bwd(t["x1"], _after(w["norm_mlp_pre"], tick(dh2, 1)), dh2, dx_out, F32, "norm_bwd_mlp_pre")
    dmixed, d_norm_mix_post = _norm_bwd(t["mixed"], w["norm_mix_post"], dx1, None, BF16, "norm_bwd_mix_post")
    dmix = _mm(dmixed, w["w_merge"], "nt", [BF16], name="mm_d_mix")
    dy_a, dy_b, dz_ga, dz_gb, dbias_a, dbias_b = _gate_bwd(t["z"], _after(w["gate_bias"], tick(dmix, 2)), t["y_a"], t["y_b"], dmix, cols)
    da = _mm(dy_a, w["w_out_conv"], "nt", [BF16], name="mm_d_a")
    emit((1, 2, 3), [(t["a"], dy_a), (t["o"], dy_b), (t["mix"], dmixed)])
    do = _mm(dy_b, w["w_out_attn"], "nt", [BF16], name="mm_d_o")
    dz_cb, dz_cc, dz_ci, d_conv_w = _conv_bwd(t["z"], _after(w["conv_w"], tick(do, 3)), da, cols)
    dq, dk_t, dv_t = _attn_bwd(t["q"], t["k"], t["v"], do, t["lse"])
    dz_q, dz_k, dz_v, d_q_norm, d_k_norm = _rope_bwd(t["z"], _after(w["q_norm"], tick(dv_t, 4)), w["k_norm"], cos_t, sin_t, dq, dk_t, dv_t, cols)
    dz = jnp.concatenate([dz_cb, dz_cc, dz_ci, dz_q, dz_k, dz_v, dz_ga, dz_gb], axis=1)
    emit((0,), [(t["h"], dz)])
    tick(dz, 5)
    dh = _mm(dz, w["w_in"], "nt", [BF16], name="mm_d_h", tk=3328)
    dx_in, d_norm_mix_pre = _norm_bwd(t["x"], _after(w["norm_mix_pre"], tick(dh, 6)), dh, dx1, F32, "norm_bwd_mix_pre")
    small = dict(norm_mix_pre=d_norm_mix_pre, gate_bias=jnp.concatenate([dbias_a, dbias_b], axis=1), conv_w=d_conv_w, q_norm=d_q_norm,
                 k_norm=d_k_norm, norm_mix_post=d_norm_mix_post, norm_mlp_pre=d_norm_mlp_pre, norm_mlp_post=d_norm_mlp_post)
    return dx_in, small


SMALL = ("norm_mix_pre", "gate_bias", "conv_w", "q_norm", "k_norm", "norm_mix_post", "norm_mlp_pre", "norm_mlp_post")
WEIGHTS = ("norm_mix_pre", "w_in", "gate_bias", "conv_w", "q_norm", "k_norm", "w_out_conv", "w_out_attn", "w_merge",
           "norm_mix_post", "norm_mlp_pre", "w_up", "w_down", "norm_mlp_post")


def _pack(parts):
    flat = jnp.concatenate([a.reshape(-1) for a in parts])
    rows = -(-flat.shape[0] // LANES)
    pad = (-rows) % 8
    flat = jnp.pad(flat, (0, (rows + pad) * LANES - flat.shape[0]))
    return flat.reshape(rows + pad, LANES)


def _unpack(packed, shapes):
    flat = packed.reshape(-1)
    out, off = [], 0
    for shp in shapes:
        n = math.prod(shp)
        out.append(flat[off:off + n].reshape(shp))
        off += n
    return out


def kernel(x, norm_mix_pre, w_in, gate_bias, conv_w, q_norm, k_norm, w_out_conv, w_out_attn, w_merge, norm_mix_post, norm_mlp_pre, w_up, w_down, norm_mlp_post, loss_target, m_norm_mix_pre, m_w_in, m_gate_bias, m_conv_w, m_q_norm, m_k_norm, m_w_out_conv, m_w_out_attn, m_w_merge, m_norm_mix_post, m_norm_mlp_pre, m_w_up, m_w_down, m_norm_mlp_post, v_norm_mix_pre, v_w_in, v_gate_bias, v_conv_w, v_q_norm, v_k_norm, v_w_out_conv, v_w_out_attn, v_w_merge, v_norm_mix_post, v_norm_mlp_pre, v_w_up, v_w_down, v_norm_mlp_post):
    w = dict(norm_mix_pre=norm_mix_pre, w_in=w_in, gate_bias=gate_bias, conv_w=conv_w, q_norm=q_norm, k_norm=k_norm, w_out_conv=w_out_conv,
             w_out_attn=w_out_attn, w_merge=w_merge, norm_mix_post=norm_mix_post, norm_mlp_pre=norm_mlp_pre, w_up=w_up, w_down=w_down,
             norm_mlp_post=norm_mlp_post)
    mom = dict(norm_mix_pre=m_norm_mix_pre, w_in=m_w_in, gate_bias=m_gate_bias, conv_w=m_conv_w, q_norm=m_q_norm, k_norm=m_k_norm,
               w_out_conv=m_w_out_conv, w_out_attn=m_w_out_attn, w_merge=m_w_merge, norm_mix_post=m_norm_mix_post, norm_mlp_pre=m_norm_mlp_pre,
               w_up=m_w_up, w_down=m_w_down, norm_mlp_post=m_norm_mlp_post)
    var = dict(norm_mix_pre=v_norm_mix_pre, w_in=v_w_in, gate_bias=v_gate_bias, conv_w=v_conv_w, q_norm=v_q_norm, k_norm=v_k_norm,
               w_out_conv=v_w_out_conv, w_out_attn=v_w_out_attn, w_merge=v_w_merge, norm_mix_post=v_norm_mix_post, norm_mlp_pre=v_norm_mlp_pre,
               w_up=v_w_up, w_down=v_w_down, norm_mlp_post=v_norm_mlp_post)
    depth = w_in.shape[0]
    _, s, d = x.shape
    cols = _Cols(d)
    x0 = x.reshape(s, d)
    target = loss_target.reshape(s, d)
    tables = _rope_tables(s)
    chip = (2 * lax.axis_index("x") + lax.axis_index("y")).astype(jnp.int32)
    core = lax.axis_index("c").astype(jnp.int32)
    chip_vec, core_vec, other_core_vec = chip.reshape(1), core.reshape(1), (1 - core).reshape(1)

    n_conv = conv_w.shape[-1]
    placed = lax.dynamic_update_slice_in_dim(jnp.zeros((depth, conv_w.shape[1], n_conv * N_CHIPS), F32), conv_w, chip * n_conv, axis=2)
    conv_full = _unpack(_allreduce_small(_pack([jnp.where(core == 0, placed, 0.0)]), "gather_conv_w"), [placed.shape])[0]

    chains = _Chains()
    chains.pin(conv_full)
    full = [{} for _ in range(depth)]
    everything = tuple(range(N_BIG))
    gathers = []
    for l in range(depth):
        bufs = [_cast_place(w[name], l, chip_vec, col, "cast_place_" + name) for name, col in BIG]
        for group in ((0,), everything[1:]):
            if group[0] == 0:
                wait_for = ("fwd", l - 1, 3) if l else None
            else:
                wait_for = ("fwd", l, 0 if l else 1)
            gathers.append(_gather_chain(chains, f"{l}_{group[0]}", group, [bufs[a] for a in group], full[l], wait_for))
            next(gathers[-1])
    next(gathers[0])
    next(gathers[0], None)
    for g in gathers[1:]:
        chains.add(g)

    def layer_params(l):
        p = dict(full[l])
        p["conv_w"] = conv_full[l]
        p["gate_bias"] = gate_bias[l].reshape(1, -1)
        for name in ("norm_mix_pre", "q_norm", "k_norm", "norm_mix_post", "norm_mlp_pre", "norm_mlp_post"):
            p[name] = w[name][l].reshape(1, -1)
        if l + 1 < depth:
            p["norm_next"] = w["norm_mix_pre"][l + 1].reshape(1, -1)
        return p

    kept = []
    xl, h = x0, _norm_first(x0, norm_mix_pre[0].reshape(1, -1))
    for l in range(depth):
        last = l == depth - 1
        (xl, h), t = _layer_fwd(xl, h, lambda name, l=l: layer_params(l)[name], cols, tables,
                                lambda dep, k, l=l: chains.tick(dep, ("fwd", l, k)), last, target if last else None)
        kept.append(t)
    dx, loss_local = xl, h

    carried = {}
    small = [None] * depth
    for l in reversed(range(depth)):
        def emit(group, pairs, l=l):
            chains.add(_grad_chain(chains, l, group, pairs, core_vec, other_core_vec, chip_vec, w, mom, var, carried))

        dx, small[l] = _layer_bwd(dx, layer_params(l), kept[l], cols, tables, lambda dep, k: chains.tick(dep), emit)
    while chains.active:
        chains.tick(None)
    grads = {name: carried[name][0] for name, _ in BIG}
    delta = {name: carried[name][1] for name, _ in BIG}
    new_m = {name: carried[name][2] for name, _ in BIG}
    new_v = {name: carried[name][3] for name, _ in BIG}

    small_full_shapes = [(depth,) + small[0][name].shape for name in SMALL]
    packed = _pack([jnp.stack([small[l][name] for l in range(depth)]) for name in SMALL] + [jnp.broadcast_to(loss_local.reshape(1), (LANES,))])
    small_sum = _unpack(_allreduce_small(packed, "allreduce_small"), small_full_shapes + [(LANES,)])
    loss = small_sum[-1][0]
    for name, g in zip(SMALL, small_sum[:-1]):
        if name == "conv_w":
            g = lax.dynamic_slice_in_dim(g, chip * n_conv, n_conv, axis=2)
        grads[name] = g.reshape(w[name].shape)
    small_shapes = [w[name].shape for name in SMALL]
    packs = [_pack([src[name] for name in SMALL]) for src in (w, grads, mom, var)]
    for dst, out in zip((delta, new_m, new_v), _adamw(*packs, "adamw_small")):
        for name, val in zip(SMALL, _unpack(out, small_shapes)):
            dst[name] = val

    grad_x = dx.reshape(x.shape)
    return (loss, grad_x, *[grads[n] for n in WEIGHTS], *[delta[n] for n in WEIGHTS], *[new_m[n] for n in WEIGHTS], *[new_v[n] for n in WEIGHTS])
```

```python
import math

import jax
import jax.numpy as jnp
from jax import lax
from jax.experimental import pallas as pl
from jax.experimental.pallas import tpu as pltpu

F32 = jnp.float32
BF16 = jnp.bfloat16

HEAD_DIM = 128
GROUP = 4
GRID_W = 64
ROPE_THETA = 10000.0
RMS_EPS = 1e-6
ADAM_LR = 0.001
ADAM_B1 = 0.9
ADAM_B2 = 0.999
ADAM_EPS = 1e-08
ADAM_WD = 0.01
ADAM_STEP = 10

LANES = 128
N_CHIPS = 4
N_DEV = 8
VMEM_LIMIT_BYTES = 56 * 1024 * 1024
MESH = pl.DeviceIdType.MESH
ANY = pl.BlockSpec(memory_space=pl.ANY)


def _tile(dim, cap, mult):
    if dim <= cap:
        return dim
    t = (cap // mult) * mult
    while t >= mult:
        if dim % t == 0:
            return t
        t -= mult
    raise ValueError(f"no tile for {dim} under {cap} in multiples of {mult}")


def _params(sem=None):
    return pltpu.CompilerParams(dimension_semantics=sem, vmem_limit_bytes=VMEM_LIMIT_BYTES)


def _sds(shape, dtype):
    return jax.ShapeDtypeStruct(tuple(shape), dtype)


def _rstd(x):
    return lax.rsqrt(jnp.mean(x * x, axis=-1, keepdims=True) + RMS_EPS)


_DOT_DIMS = {"nn": ((1,), (0,)), "nt": ((1,), (1,)), "tn": ((0,), (0,))}


def _mm(a, b, mode, out_dtypes, *, name, epilogue=None, extras=(), tm=1024, tn=1024, tk=2048, half=None):
    if mode == "nn":
        (m, k), (k2, n) = a.shape, b.shape
    elif mode == "nt":
        (m, k), (n, k2) = a.shape, b.shape
    else:
        (k, m), (k2, n) = a.shape, b.shape
    assert k == k2, (a.shape, b.shape, mode)
    side = half[1] if half is not None else None
    if side == "m":
        m //= 2
    elif side == "n":
        n //= 2
    tm, tn, tk = _tile(m, tm, 8), _tile(n, tn, LANES), _tile(k, tk, LANES)
    nk = k // tk
    row = (lambda i, s: i + s[0][0] * (m // tm)) if side == "m" else (lambda i, s: i)
    col = (lambda j, s: j + s[0][0] * (n // tn)) if side == "n" else (lambda j, s: j)
    a_spec = pl.BlockSpec((tk, tm), lambda i, j, kk, *s: (kk, row(i, s))) if mode == "tn" else pl.BlockSpec((tm, tk), lambda i, j, kk, *s: (row(i, s), kk))
    b_spec = pl.BlockSpec((tn, tk), lambda i, j, kk, *s: (col(j, s), kk)) if mode == "nt" else pl.BlockSpec((tk, tn), lambda i, j, kk, *s: (kk, col(j, s)))
    tile_spec = pl.BlockSpec((tm, tn), lambda i, j, kk, *s: (i, j))
    n_extra, n_out = len(extras), len(out_dtypes)
    dims = (_DOT_DIMS[mode], ((), ()))
    keep_t = mode == "tn"

    def body(*refs):
        if half is not None:
            refs = refs[1:]
        a_ref, b_ref = refs[:2]
        extra_refs = refs[2:2 + n_extra]
        out_refs = refs[2 + n_extra:2 + n_extra + n_out]
        if keep_t:
            at_ref, kt = refs[-1], pl.program_id(2)

            @pl.when(pl.program_id(1) == 0)
            def _():
                at_ref[kt] = a_ref[...].astype(BF16).T

            part = lax.dot_general(at_ref[kt], b_ref[...].astype(BF16), (_DOT_DIMS["nn"], ((), ())), preferred_element_type=F32)
        else:
            part = lax.dot_general(a_ref[...].astype(BF16), b_ref[...].astype(BF16), dims, preferred_element_type=F32)

        def finish(total):
            res = epilogue(total, *[e[...] for e in extra_refs]) if epilogue is not None else (total,)
            for o, r in zip(out_refs, res):
                o[...] = r.astype(o.dtype)

        if nk == 1:
            finish(part)
        else:
            acc = refs[-2] if keep_t else refs[-1]
            kk = pl.program_id(2)

            @pl.when(kk == 0)
            def _():
                acc[...] = part

            @pl.when(kk > 0)
            def _():
                acc[...] += part

            @pl.when(kk == nk - 1)
            def _():
                finish(acc[...])

    grid = (m // tm, n // tn, nk)
    in_specs, out_specs = [a_spec, b_spec] + [tile_spec] * n_extra, [tile_spec] * n_out
    scratch = ([pltpu.VMEM((tm, tn), F32)] if nk > 1 else []) + ([pltpu.VMEM((nk, tm, tk), BF16)] if keep_t else [])
    if half is None:
        layout, lead = dict(grid=grid, in_specs=in_specs, out_specs=out_specs, scratch_shapes=scratch), ()
    else:
        layout = dict(grid_spec=pltpu.PrefetchScalarGridSpec(num_scalar_prefetch=1, grid=grid, in_specs=in_specs, out_specs=out_specs,
                                                             scratch_shapes=scratch))
        lead = (half[0],)
    outs = pl.pallas_call(body, name=name, out_shape=[_sds((m, n), d) for d in out_dtypes],
                          compiler_params=_params(("parallel", "arbitrary" if keep_t else "parallel", "arbitrary")), **layout)(*lead, a, b, *extras)
    return outs if n_out > 1 else outs[0]


ROW_TILE = 512
GATE_ROW_TILE = 1024


def _norm_first(x, g):
    s, d = x.shape
    ts = _tile(s, ROW_TILE, 8)

    def body(x_ref, g_ref, h_ref):
        xv = x_ref[...]
        h_ref[...] = (xv * _rstd(xv) * g_ref[...]).astype(h_ref.dtype)

    row = pl.BlockSpec((ts, d), lambda i: (i, 0))
    vec = pl.BlockSpec((1, d), lambda i: (0, 0))
    return pl.pallas_call(body, name="norm_first", grid=(s // ts,), in_specs=[row, vec], out_specs=row,
                          out_shape=_sds((s, d), BF16), compiler_params=_params(("parallel",)))(x, g)


def _resid_norm(xres, y, g_post, g_next):
    s, d = xres.shape
    ts = _tile(s, ROW_TILE, 8)

    def body(x_ref, y_ref, gp_ref, gn_ref, xn_ref, hn_ref):
        yv = y_ref[...].astype(F32)
        xn = x_ref[...] + yv * _rstd(yv) * gp_ref[...]
        xn_ref[...] = xn
        hn_ref[...] = (xn * _rstd(xn) * gn_ref[...]).astype(hn_ref.dtype)

    row = pl.BlockSpec((ts, d), lambda i: (i, 0))
    vec = pl.BlockSpec((1, d), lambda i: (0, 0))
    return pl.pallas_call(body, name="resid_norm", grid=(s // ts,), in_specs=[row, row, vec, vec], out_specs=[row, row],
                          out_shape=[_sds((s, d), F32), _sds((s, d), BF16)], compiler_params=_params(("parallel",)))(xres, y, g_post, g_next)


def _resid_norm_loss(xres, y, g_post, target):
    s, d = xres.shape
    ts = _tile(s, ROW_TILE, 8)
    n_steps = s // ts

    def body(x_ref, y_ref, gp_ref, t_ref, dout_ref, loss_ref, acc_ref):
        i = pl.program_id(0)
        yv = y_ref[...].astype(F32)
        err = x_ref[...] + yv * _rstd(yv) * gp_ref[...] - t_ref[...]
        dout_ref[...] = err / d
        part = jnp.sum(err * err, axis=0, keepdims=True)

        @pl.when(i == 0)
        def _():
            acc_ref[...] = part

        @pl.when(i > 0)
        def _():
            acc_ref[...] += part

        @pl.when(i == n_steps - 1)
        def _():
            loss_ref[...] = 0.5 * jnp.sum(acc_ref[...], axis=1, keepdims=True) / d

    row = pl.BlockSpec((ts, d), lambda i: (i, 0))
    vec = pl.BlockSpec((1, d), lambda i: (0, 0))
    one = pl.BlockSpec((1, 1), lambda i: (0, 0))
    return pl.pallas_call(body, name="resid_norm_loss", grid=(n_steps,), in_specs=[row, row, vec, row], out_specs=[row, one],
                          out_shape=[_sds((s, d), F32), _sds((1, 1), F32)], scratch_shapes=[pltpu.VMEM((1, d), F32)],
                          compiler_params=_params(("arbitrary",)))(xres, y, g_post, target)


def _norm_bwd(xin, g, dout, dres, out_dtype, name):
    s, d = xin.shape
    ts = _tile(s, ROW_TILE, 8)
    has_res = dres is not None

    def body(*refs):
        x_ref, g_ref, do_ref = refs[:3]
        dx_ref, dg_ref = refs[-2:]
        i = pl.program_id(0)
        xv, dov = x_ref[...].astype(F32), do_ref[...].astype(F32)
        r = _rstd(xv)
        xhat = xv * r
        dg = jnp.sum(dov * xhat, axis=0, keepdims=True)
        dxh = dov * g_ref[...]
        dx = r * (dxh - xhat * jnp.mean(dxh * xhat, axis=-1, keepdims=True))
        if has_res:
            dx = dx + refs[3][...]
        dx_ref[...] = dx.astype(dx_ref.dtype)

        @pl.when(i == 0)
        def _():
            dg_ref[...] = dg

        @pl.when(i > 0)
        def _():
            dg_ref[...] += dg

    row = pl.BlockSpec((ts, d), lambda i: (i, 0))
    vec = pl.BlockSpec((1, d), lambda i: (0, 0))
    ops = [xin, g, dout] + ([dres] if has_res else [])
    return pl.pallas_call(body, name=name, grid=(s // ts,), in_specs=[row, vec, row] + ([row] if has_res else []),
                          out_specs=[row, vec], out_shape=[_sds((s, d), out_dtype), _sds((1, d), F32)],
                          compiler_params=_params(("arbitrary",)))(*ops)


class _Cols:
    def __init__(self, d):
        self.d = d
        self.kv = d // GROUP
        self.cb, self.cc, self.ci, self.q = 0, d, 2 * d, 3 * d
        self.k = 4 * d
        self.v = 4 * d + self.kv
        self.ga = 4 * d + 2 * self.kv
        self.gb = 5 * d + 2 * self.kv
        self.width = 6 * d + 2 * self.kv


CONV_COLS = 128


def _shift_rows(u, down):
    s = u.shape[0]
    rows = lax.broadcasted_iota(jnp.int32, u.shape, 0)
    if down:
        return jnp.where(rows == 0, 0.0, pltpu.roll(u, 1, 0))
    return jnp.where(rows == s - 1, 0.0, pltpu.roll(u, s - 1, 0))


def _conv_fwd(z, w, cols):
    s, d = z.shape[0], cols.d
    cw = CONV_COLS

    def body(cb_ref, cc_ref, ci_ref, w_ref, a_ref):
        u = cc_ref[...].astype(F32) * ci_ref[...].astype(F32)
        wv = w_ref[...]
        conv = wv[0:1] * _shift_rows(u, True) + wv[1:2] * u + wv[2:3] * _shift_rows(u, False)
        a_ref[...] = (cb_ref[...].astype(F32) * conv).astype(a_ref.dtype)

    def zspec(off):
        return pl.BlockSpec((s, cw), lambda j: (0, off // cw + j))

    return pl.pallas_call(body, name="conv_fwd", grid=(d // cw,),
                          in_specs=[zspec(cols.cb), zspec(cols.cc), zspec(cols.ci), pl.BlockSpec((3, cw), lambda j: (0, j))],
                          out_specs=pl.BlockSpec((s, cw), lambda j: (0, j)), out_shape=_sds((s, d), BF16),
                          compiler_params=_params(("parallel",)))(z, z, z, w)


def _conv_bwd(z, w, da, cols):
    s, d = z.shape[0], cols.d
    cw = CONV_COLS

    def body(cb_ref, cc_ref, ci_ref, w_ref, da_ref, dcb_ref, dcc_ref, dci_ref, dw_ref):
        cb, cc, ci, dav = cb_ref[...].astype(F32), cc_ref[...].astype(F32), ci_ref[...].astype(F32), da_ref[...].astype(F32)
        wv = w_ref[...]
        u = cc * ci
        um, up = _shift_rows(u, True), _shift_rows(u, False)
        conv = wv[0:1] * um + wv[1:2] * u + wv[2:3] * up
        dcb_ref[...] = (dav * conv).astype(dcb_ref.dtype)
        dconv = dav * cb
        dw_ref[0:1, :] = jnp.sum(dconv * um, axis=0, keepdims=True)
        dw_ref[1:2, :] = jnp.sum(dconv * u, axis=0, keepdims=True)
        dw_ref[2:3, :] = jnp.sum(dconv * up, axis=0, keepdims=True)
        du = wv[0:1] * _shift_rows(dconv, False) + wv[1:2] * dconv + wv[2:3] * _shift_rows(dconv, True)
        dcc_ref[...] = (du * ci).astype(dcc_ref.dtype)
        dci_ref[...] = (du * cc).astype(dci_ref.dtype)

    def zspec(off):
        return pl.BlockSpec((s, cw), lambda j: (0, off // cw + j))

    col = pl.BlockSpec((s, cw), lambda j: (0, j))
    wspec = pl.BlockSpec((3, cw), lambda j: (0, j))
    return pl.pallas_call(body, name="conv_bwd", grid=(d // cw,),
                          in_specs=[zspec(cols.cb), zspec(cols.cc), zspec(cols.ci), wspec, col],
                          out_specs=[col, col, col, wspec],
                          out_shape=[_sds((s, d), BF16)] * 3 + [_sds((3, d), F32)],
                          compiler_params=_params(("parallel",)))(z, z, z, w, da)


def _gate_fwd(z, bias, y_a, y_b, cols):
    s, d = y_a.shape
    ts, cw = _tile(s, GATE_ROW_TILE, 8), cols.kv
    nj = d // cw

    def body(ga_ref, gb_ref, ba_ref, bb_ref, ya_ref, yb_ref, o_ref):
        gate_a = jax.nn.sigmoid(ga_ref[...].astype(F32) + ba_ref[...])
        gate_b = jax.nn.sigmoid(gb_ref[...].astype(F32) + bb_ref[...])
        o_ref[...] = (gate_a * ya_ref[...].astype(F32) + gate_b * yb_ref[...].astype(F32)).astype(o_ref.dtype)

    tile = pl.BlockSpec((ts, cw), lambda i, j: (i, j))
    return pl.pallas_call(
        body, name="gate_fwd", grid=(s // ts, nj),
        in_specs=[pl.BlockSpec((ts, cw), lambda i, j: (i, cols.ga // cw + j)), pl.BlockSpec((ts, cw), lambda i, j: (i, cols.gb // cw + j)),
                  pl.BlockSpec((1, cw), lambda i, j: (0, j)), pl.BlockSpec((1, cw), lambda i, j: (0, nj + j)), tile, tile],
        out_specs=tile, out_shape=_sds((s, d), BF16), compiler_params=_params(("parallel", "parallel")))(z, z, bias, bias, y_a, y_b)


def _gate_bwd(z, bias, y_a, y_b, dmix, cols):
    s, d = y_a.shape
    ts, cw = _tile(s, GATE_ROW_TILE, 8), cols.kv
    nj = d // cw

    def body(ga_ref, gb_ref, ba_ref, bb_ref, ya_ref, yb_ref, dm_ref, dya_ref, dyb_ref, dga_ref, dgb_ref, dba_ref, dbb_ref):
        i = pl.program_id(1)
        gate_a = jax.nn.sigmoid(ga_ref[...].astype(F32) + ba_ref[...])
        gate_b = jax.nn.sigmoid(gb_ref[...].astype(F32) + bb_ref[...])
        dm = dm_ref[...].astype(F32)
        dya_ref[...] = (dm * gate_a).astype(dya_ref.dtype)
        dyb_ref[...] = (dm * gate_b).astype(dyb_ref.dtype)
        dga = dm * ya_ref[...].astype(F32) * (gate_a * (1.0 - gate_a))
        dgb = dm * yb_ref[...].astype(F32) * (gate_b * (1.0 - gate_b))
        dga_ref[...] = dga.astype(dga_ref.dtype)
        dgb_ref[...] = dgb.astype(dgb_ref.dtype)
        sa = jnp.sum(dga, axis=0, keepdims=True)
        sb = jnp.sum(dgb, axis=0, keepdims=True)

        @pl.when(i == 0)
        def _():
            dba_ref[...] = sa
            dbb_ref[...] = sb

        @pl.when(i > 0)
        def _():
            dba_ref[...] += sa
            dbb_ref[...] += sb

    tile = pl.BlockSpec((ts, cw), lambda j, i: (i, j))
    vec = pl.BlockSpec((1, cw), lambda j, i: (0, j))
    return pl.pallas_call(
        body, name="gate_bwd", grid=(nj, s // ts),
        in_specs=[pl.BlockSpec((ts, cw), lambda j, i: (i, cols.ga // cw + j)), pl.BlockSpec((ts, cw), lambda j, i: (i, cols.gb // cw + j)),
                  vec, pl.BlockSpec((1, cw), lambda j, i: (0, nj + j)), tile, tile, tile],
        out_specs=[tile, tile, tile, tile, vec, vec],
        out_shape=[_sds((s, d), BF16)] * 4 + [_sds((1, d), F32)] * 2,
        compiler_params=_params(("parallel", "arbitrary")))(z, z, bias, bias, y_a, y_b, dmix)


def _rope_tables(s):
    axis_dim = HEAD_DIM // 2
    n_freq = axis_dim // 2
    rows = s // GRID_W
    row_idx = jnp.repeat(jnp.arange(rows, dtype=jnp.int32), GRID_W)
    col_idx = jnp.tile(jnp.arange(GRID_W, dtype=jnp.int32), rows)
    inv_freq = ROPE_THETA ** (-jnp.arange(0, axis_dim, 2, dtype=F32) / axis_dim)
    ang = jnp.stack([row_idx.astype(F32)[:, None] * inv_freq, col_idx.astype(F32)[:, None] * inv_freq], axis=1)
    cos, sin = jnp.cos(ang), jnp.sin(ang)
    cos_t = jnp.stack([cos, cos], axis=2).reshape(s, HEAD_DIM)
    sin_t = jnp.stack([-sin, sin], axis=2).reshape(s, HEAD_DIM)
    return cos_t, sin_t


def _partner(x):
    n = x.shape[-1]
    lane = lax.broadcasted_iota(jnp.int32, x.shape, x.ndim - 1)
    quarter = HEAD_DIM // 4
    return jnp.where(lane % (2 * quarter) < quarter, pltpu.roll(x, n - quarter, x.ndim - 1), pltpu.roll(x, quarter, x.ndim - 1))


LOG2E = math.log2(math.e)
Q_SCALE = LOG2E / math.sqrt(HEAD_DIM)


def _rope_fwd(z, qn, kn, cos_t, sin_t, cols):
    s, d, kv = z.shape[0], cols.d, cols.kv
    ts = _tile(s, ROW_TILE, 8)
    scale = Q_SCALE

    def body(q_ref, k_ref, v_ref, qn_ref, kn_ref, c_ref, s_ref, qo_ref, ko_ref, vo_ref):
        c, sn = c_ref[...], s_ref[...]

        def head(xh, g):
            xn = xh * _rstd(xh) * g
            return xn * c + _partner(xn) * sn

        for h in range(d // HEAD_DIM):
            sl = slice(h * HEAD_DIM, (h + 1) * HEAD_DIM)
            qo_ref[:, sl] = (head(q_ref[:, sl].astype(F32), qn_ref[...]) * scale).astype(qo_ref.dtype)
        for h in range(kv // HEAD_DIM):
            sl = slice(h * HEAD_DIM, (h + 1) * HEAD_DIM)
            ko_ref[:, sl] = head(k_ref[:, sl].astype(F32), kn_ref[...]).astype(ko_ref.dtype)
        vo_ref[...] = v_ref[...].astype(vo_ref.dtype)

    vec = pl.BlockSpec((1, HEAD_DIM), lambda i: (0, 0))
    tab = pl.BlockSpec((ts, HEAD_DIM), lambda i: (i, 0))
    return pl.pallas_call(
        body, name="rope_fwd", grid=(s // ts,),
        in_specs=[pl.BlockSpec((ts, d), lambda i: (i, cols.q // d)), pl.BlockSpec((ts, kv), lambda i: (i, cols.k // kv)),
                  pl.BlockSpec((ts, kv), lambda i: (i, cols.v // kv)), vec, vec, tab, tab],
        out_specs=[pl.BlockSpec((ts, d), lambda i: (i, 0)), pl.BlockSpec((ts, kv), lambda i: (i, 0)), pl.BlockSpec((ts, kv), lambda i: (i, 0))],
        out_shape=[_sds((s, d), BF16), _sds((s, kv), BF16), _sds((s, kv), BF16)],
        compiler_params=_params(("parallel",)))(z, z, z, qn, kn, cos_t, sin_t)


def _rope_bwd(z, qn, kn, cos_t, sin_t, dq, dk_t, dv_t, cols):
    s, d, kv = z.shape[0], cols.d, cols.kv
    ts = _tile(s, ROW_TILE, 8)
    scale = 1.0 / math.sqrt(HEAD_DIM)

    def body(q_ref, k_ref, qn_ref, kn_ref, c_ref, s_ref, dq_ref, dkt_ref, dvt_ref, dzq_ref, dzk_ref, dzv_ref, dqn_ref, dkn_ref):
        i = pl.program_id(0)
        c, sn = c_ref[...], s_ref[...]
        dk_all = dkt_ref[...].T * (1.0 / LOG2E)

        def head_bwd(xh, g, drot):
            dxn = drot * c + _partner(drot * sn)
            r = _rstd(xh)
            xhat = xh * r
            dgain = jnp.sum(dxn * xhat, axis=0, keepdims=True)
            dxh = dxn * g
            return r * (dxh - xhat * jnp.mean(dxh * xhat, axis=-1, keepdims=True)), dgain

        dqn = jnp.zeros((1, HEAD_DIM), F32)
        for h in range(d // HEAD_DIM):
            sl = slice(h * HEAD_DIM, (h + 1) * HEAD_DIM)
            dx, dg = head_bwd(q_ref[:, sl].astype(F32), qn_ref[...], dq_ref[:, sl].astype(F32) * scale)
            dzq_ref[:, sl] = dx.astype(dzq_ref.dtype)
            dqn = dqn + dg
        dkn = jnp.zeros((1, HEAD_DIM), F32)
        for h in range(kv // HEAD_DIM):
            sl = slice(h * HEAD_DIM, (h + 1) * HEAD_DIM)
            dx, dg = head_bwd(k_ref[:, sl].astype(F32), kn_ref[...], dk_all[:, sl])
            dzk_ref[:, sl] = dx.astype(dzk_ref.dtype)
            dkn = dkn + dg
        dzv_ref[...] = dvt_ref[...].T.astype(dzv_ref.dtype)

        @pl.when(i == 0)
        def _():
            dqn_ref[...] = dqn
            dkn_ref[...] = dkn

        @pl.when(i > 0)
        def _():
            dqn_ref[...] += dqn
            dkn_ref[...] += dkn

    vec = pl.BlockSpec((1, HEAD_DIM), lambda i: (0, 0))
    tab = pl.BlockSpec((ts, HEAD_DIM), lambda i: (i, 0))
    qrow = pl.BlockSpec((ts, d), lambda i: (i, 0))
    krow = pl.BlockSpec((ts, kv), lambda i: (i, 0))
    kcol = pl.BlockSpec((kv, ts), lambda i: (0, i))
    return pl.pallas_call(
        body, name="rope_bwd", grid=(s // ts,),
        in_specs=[pl.BlockSpec((ts, d), lambda i: (i, cols.q // d)), pl.BlockSpec((ts, kv), lambda i: (i, cols.k // kv)),
                  vec, vec, tab, tab, qrow, kcol, kcol],
        out_specs=[qrow, krow, krow, vec, vec],
        out_shape=[_sds((s, d), BF16), _sds((s, kv), BF16), _sds((s, kv), BF16), _sds((1, HEAD_DIM), F32), _sds((1, HEAD_DIM), F32)],
        compiler_params=_params(("arbitrary",)))(z, z, qn, kn, cos_t, sin_t, dq, dk_t, dv_t)


Q_TILE = 256
_NT = (((1,), (1,)), ((), ()))
_NN = (((1,), (0,)), ((), ()))


def _attn_fwd(q, k, v):
    s, d = q.shape
    kvh = k.shape[1] // HEAD_DIM
    tq = _tile(s, Q_TILE, LANES)
    gw = GROUP * HEAD_DIM

    def body(q_ref, k_ref, v_ref, o_ref, lse_ref):
        kk, vv = k_ref[...], v_ref[...]
        for g in range(GROUP):
            sl = slice(g * HEAD_DIM, (g + 1) * HEAD_DIM)
            sc = lax.dot_general(q_ref[:, sl], kk, _NT, preferred_element_type=F32)
            mx = jnp.max(sc, axis=-1, keepdims=True)
            p = jnp.exp2(sc - mx)
            l = jnp.sum(p, axis=-1, keepdims=True)
            o = lax.dot_general(p.astype(BF16), vv, _NN, preferred_element_type=F32) * (1.0 / l)
            o_ref[:, sl] = o.astype(o_ref.dtype)
            lse_ref[:, g:g + 1] = mx + jnp.log(l) * LOG2E

    return pl.pallas_call(
        body, name="attn_fwd", grid=(kvh, s // tq),
        in_specs=[pl.BlockSpec((tq, gw), lambda j, i: (i, j)), pl.BlockSpec((s, HEAD_DIM), lambda j, i: (0, j)), pl.BlockSpec((s, HEAD_DIM), lambda j, i: (0, j))],
        out_specs=[pl.BlockSpec((tq, gw), lambda j, i: (i, j)), pl.BlockSpec((None, tq, GROUP), lambda j, i: (j, i, 0))],
        out_shape=[_sds((s, d), BF16), _sds((kvh, s, GROUP), F32)],
        compiler_params=_params(("parallel", "parallel")))(q, k, v)


_TN = (((0,), (0,)), ((), ()))


def _attn_bwd(q, k, v, do, lse):
    s, d = q.shape
    kv = k.shape[1]
    kvh = kv // HEAD_DIM
    tq = _tile(s, Q_TILE, LANES)
    gw = GROUP * HEAD_DIM

    def body(q_ref, k_ref, v_ref, do_ref, lse_ref, dq_ref, dkt_ref, dvt_ref):
        i = pl.program_id(1)
        kk, vv = k_ref[...], v_ref[...]

        @pl.when(i == 0)
        def _():
            dkt_ref[...] = jnp.zeros_like(dkt_ref)
            dvt_ref[...] = jnp.zeros_like(dvt_ref)

        for g in range(GROUP):
            sl = slice(g * HEAD_DIM, (g + 1) * HEAD_DIM)
            qg, dog = q_ref[:, sl], do_ref[:, sl]
            sc = lax.dot_general(qg, kk, _NT, preferred_element_type=F32)
            p = jnp.exp2(sc - lse_ref[:, g:g + 1])
            dp = lax.dot_general(dog, vv, _NT, preferred_element_type=F32)
            delta = jnp.sum(p * dp, axis=-1, keepdims=True)
            ds = (p * (dp - delta)).astype(BF16)
            dq_ref[:, sl] = lax.dot_general(ds, kk, _NN, preferred_element_type=F32).astype(dq_ref.dtype)
            dkt_ref[...] += lax.dot_general(qg, ds, _TN, preferred_element_type=F32)
            dvt_ref[...] += lax.dot_general(dog, p.astype(BF16), _TN, preferred_element_type=F32)

    qspec = pl.BlockSpec((tq, gw), lambda j, i: (i, j))
    kspec = pl.BlockSpec((s, HEAD_DIM), lambda j, i: (0, j))
    stat = pl.BlockSpec((None, tq, GROUP), lambda j, i: (j, i, 0))
    tspec = pl.BlockSpec((HEAD_DIM, s), lambda j, i: (j, 0))
    return pl.pallas_call(
        body, name="attn_bwd", grid=(kvh, s // tq),
        in_specs=[qspec, kspec, kspec, qspec, stat], out_specs=[qspec, tspec, tspec],
        out_shape=[_sds((s, d), BF16), _sds((kv, s), F32), _sds((kv, s), F32)],
        compiler_params=_params(("parallel", "arbitrary")))(q, k, v, do, lse)


ELEM_BLOCK_BYTES = 2 << 20

BIG = (("w_in", True), ("w_out_conv", False), ("w_out_attn", False), ("w_merge", False), ("w_up", True), ("w_down", False))
N_BIG = len(BIG)


def _elem_tiles(rows, width):
    tc = _tile(width, 2048, LANES)
    tr = _tile(rows, max(8, ELEM_BLOCK_BYTES // (4 * tc)), 8)
    return tr, tc


def _scalar_grid(grid, in_specs, out_specs):
    return pltpu.PrefetchScalarGridSpec(num_scalar_prefetch=1, grid=grid, in_specs=in_specs, out_specs=out_specs)


def _cast_place(w_stack, layer, chip, col_sharded, name):
    _, rows, width = w_stack.shape
    tr, tc = _elem_tiles(rows, width)
    nr, nc = rows // tr, width // tc

    def body(sc_ref, x_ref, o_ref):
        o_ref[...] = x_ref[...].astype(o_ref.dtype)

    if col_sharded:
        full, out_spec = (rows, width * N_CHIPS), pl.BlockSpec((tr, tc), lambda i, j, sc: (i, sc[0] * nc + j))
    else:
        full, out_spec = (rows * N_CHIPS, width), pl.BlockSpec((tr, tc), lambda i, j, sc: (sc[0] * nr + i, j))
    return pl.pallas_call(
        body, name=name,
        grid_spec=_scalar_grid((nr, nc), [pl.BlockSpec((None, tr, tc), lambda i, j, sc: (layer, i, j))], out_spec),
        out_shape=_sds(full, BF16), compiler_params=_params(("parallel", "parallel")))(chip, w_stack)


def _add_landed(acc, landed):
    return (acc + landed,)


def _slot_of_relation(rel):
    return jnp.where(rel == 2, 0, jnp.where(rel == 1, 1, 2))


def _sum_chips(pair_sum, landed, chip, col_sharded, name):
    _, rows, width = landed.shape
    tr, tc = _elem_tiles(rows, width)
    nr, nc = rows // tr, width // tc

    def body(chip_ref, own_ref, q_ref, o_ref):
        me = chip_ref[0]
        own = own_ref[...].astype(F32)
        acc = None
        for t in range(N_CHIPS):
            rel = me ^ t
            term = jnp.where(rel == 0, own, q_ref[_slot_of_relation(rel)].astype(F32))
            acc = term if acc is None else acc + term
        o_ref[...] = acc

    if col_sharded:
        own_spec = pl.BlockSpec((tr, tc), lambda i, j, c: (i, c[0] * nc + j))
    else:
        own_spec = pl.BlockSpec((tr, tc), lambda i, j, c: (c[0] * nr + i, j))
    return pl.pallas_call(
        body, name=name,
        grid_spec=_scalar_grid((nr, nc), [own_spec, pl.BlockSpec((N_CHIPS - 1, tr, tc), lambda i, j, c: (0, i, j))],
                               pl.BlockSpec((tr, tc), lambda i, j, c: (i, j))),
        out_shape=_sds((rows, width), F32), compiler_params=_params(("parallel", "parallel")))(chip, pair_sum, landed)


def _adamw_math(w, g, m, v):
    mn = ADAM_B1 * m + (1.0 - ADAM_B1) * g
    vn = ADAM_B2 * v + (1.0 - ADAM_B2) * jnp.square(g)
    m_hat = mn / (1.0 - ADAM_B1 ** ADAM_STEP)
    v_hat = vn / (1.0 - ADAM_B2 ** ADAM_STEP)
    return -ADAM_LR * (m_hat / (jnp.sqrt(v_hat) + ADAM_EPS) + ADAM_WD * w), mn, vn


def _adamw(w, g, m, v, name):
    shape = w.shape
    width = shape[-1]
    w2, g2, m2, v2 = (a.reshape(-1, width) for a in (w, g, m, v))
    rows = w2.shape[0]
    tr, tc = _elem_tiles(rows, width)

    def body(w_ref, g_ref, m_ref, v_ref, d_ref, nm_ref, nv_ref):
        d_ref[...], nm_ref[...], nv_ref[...] = _adamw_math(w_ref[...], g_ref[...], m_ref[...], v_ref[...])

    tile = pl.BlockSpec((tr, tc), lambda i, j: (i, j))
    outs = pl.pallas_call(body, name=name, grid=(rows // tr, width // tc), in_specs=[tile] * 4, out_specs=[tile] * 3,
                          out_shape=[_sds((rows, width), F32)] * 3, compiler_params=_params(("parallel", "parallel")))(w2, g2, m2, v2)
    return tuple(o.reshape(shape) for o in outs)


def _adamw_layer(w, m, v, g_mine, g_sibling, carried, layer, core, col_sharded, name):
    depth, rows, width = w.shape
    pr, pc = g_mine.shape
    tr, tc = _elem_tiles(pr, pc)
    n_half = pr // tr if col_sharded else pc // tc
    if carried is None:
        carried = tuple(lax.empty((depth, rows, width), F32) for _ in range(4))

    def body(sc_ref, w_ref, m_ref, v_ref, gm_ref, gs_ref, *rest):
        g_ref, d_ref, nm_ref, nv_ref = rest[-4:]
        pos = pl.program_id(0) if col_sharded else pl.program_id(1)
        gv = jnp.where(pos // n_half == sc_ref[0], gm_ref[...], gs_ref[...])
        g_ref[...] = gv
        d_ref[...], nm_ref[...], nv_ref[...] = _adamw_math(w_ref[...], gv, m_ref[...], v_ref[...])

    stacked = pl.BlockSpec((None, tr, tc), lambda i, j, sc: (layer, i, j))
    n_cols = width // tc

    def half_spec(mine):
        def index(i, j, sc):
            own = sc[0] if mine else 1 - sc[0]
            pos = i if col_sharded else j
            used, before = pos // n_half == own, pos // n_half < own
            within = jnp.where(used, pos % n_half, jnp.where(before, 0, n_half - 1))
            if col_sharded:
                return within, jnp.where(used, j, jnp.where(before, 0, n_cols - 1))
            return i, within
        return pl.BlockSpec((tr, tc), index)

    return pl.pallas_call(
        body, name=name,
        grid_spec=_scalar_grid((rows // tr, width // tc), [stacked] * 3 + [half_spec(True), half_spec(False)] + [ANY] * 4, [stacked] * 4),
        out_shape=[_sds((depth, rows, width), F32)] * 4, input_output_aliases={6: 0, 7: 1, 8: 2, 9: 3},
        compiler_params=_params(("parallel", "parallel")))(core, w, m, v, g_mine, g_sibling, *carried)


_SEM = pl.BlockSpec(memory_space=pltpu.SEMAPHORE)
_HBM = pl.BlockSpec(memory_space=pltpu.HBM)
_VMEM = pl.BlockSpec(memory_space=pltpu.VMEM)
_EFFECT = pltpu.SideEffectType.DATAFLOW_SIDE_EFFECTING


def _place():
    x, y, c = lax.axis_index("x"), lax.axis_index("y"), lax.axis_index("c")
    others = [(1 - x, y), (x, 1 - y), (1 - x, 1 - y)]
    return x, y, c, others


def _chip_index(px, py):
    return 2 * px + py


def _remote(src, dst, send_sems, recv_sems, k, to):
    return pltpu.make_async_remote_copy(src_ref=src, dst_ref=dst, send_sem=send_sems.at[k], recv_sem=recv_sems.at[k],
                                        device_id=to, device_id_type=MESH)


def _phase(name, bufs, waits, wait_fn, n_start, start_fn, deps):
    nb, nd = len(bufs), len(deps)

    def body(*refs):
        buf_refs = refs[:nb]
        pos = nb
        if waits is not None:
            wait_fn(buf_refs, refs[pos], refs[pos + 1])
            pos += 2
        pos += nd
        if n_start:
            start_fn(buf_refs, refs[pos], refs[pos + 1])
            pos += 2
        token = refs[pos + nb]
        token[...] = jnp.zeros_like(token)

    n_sem_out = 2 if n_start else 0
    if waits is None:
        bufs = [pltpu.with_memory_space_constraint(b, pltpu.HBM) for b in bufs]
    outs = pl.pallas_call(
        body, name=name,
        in_specs=[_HBM] * nb + ([_SEM] * 2 if waits is not None else []) + [ANY] * nd,
        out_specs=[_SEM] * n_sem_out + [_HBM] * nb + [_VMEM],
        out_shape=[pltpu.SemaphoreType.DMA((n_start,))] * n_sem_out + [pltpu.HBM(b.shape, b.dtype) for b in bufs] + [_sds((8, LANES), F32)],
        input_output_aliases={i: n_sem_out + i for i in range(nb)},
        compiler_params=pltpu.CompilerParams(has_side_effects=_EFFECT),
    )(*bufs, *(waits if waits is not None else ()), *deps)
    sems = tuple(outs[:2]) if n_start else None
    return sems, list(outs[n_sem_out:n_sem_out + nb]), outs[-1]


class _Chains:
    def __init__(self):
        self.active = []
        self.last = None
        self.dep = None
        self.pinned = []
        self.token = None
        self.at = None

    def phase(self, name, bufs, waits, wait_fn, n_start, start_fn):
        deps = [a for a in (self.last, self.dep) if a is not None] + self.pinned
        sems, thru, token = _phase(name, bufs, waits, wait_fn, n_start, start_fn, deps)
        self.last, self.dep, self.token, self.pinned = token, None, token, []
        return sems, thru

    def pin(self, result):
        self.pinned.append(result)

    def add(self, gen):
        self.active.append(gen)

    def tick(self, dep, at=None):
        self.at, self.token, self.dep = at, None, dep
        for gen in list(self.active):
            if next(gen, "done") == "done":
                self.active.remove(gen)
        return self.token


def _after(small, token):
    return small if token is None else small + token[0, 0]


def _shard_region(ref, col_sharded, chip, n_shard):
    start = pl.multiple_of(chip * n_shard, LANES if col_sharded else 8)
    if col_sharded:
        return ref.at[:, pl.ds(start, n_shard)]
    return ref.at[pl.ds(start, n_shard), :]


def _row_half(ref, half):
    n_rows = ref.shape[0]
    return ref.at[pl.ds(pl.multiple_of(half * (n_rows // 2), 8), n_rows // 2), :]


def _gather_chain(chains, tag, group, bufs, out, wait_for):
    n_w = len(group)
    n = 3 * n_w

    def region(refs, a, chip, half):
        col = BIG[group[a]][1]
        n_shard = refs[a].shape[1] // N_CHIPS if col else refs[a].shape[0] // N_CHIPS
        return _row_half(_shard_region(refs[a], col, chip, n_shard), half)

    def start_ici(refs, send, recv):
        x, y, c, others = _place()
        for a in range(n_w):
            mine = region(refs, a, _chip_index(x, y), c)
            for j, (ox, oy) in enumerate(others):
                _remote(mine, mine, send, recv, 3 * a + j, (ox, oy, c)).start()

    def wait_ici(refs, send, recv):
        x, y, c, others = _place()
        for a in range(n_w):
            for j, (ox, oy) in enumerate(others):
                landed = region(refs, a, _chip_index(ox, oy), c)
                cp = _remote(landed, landed, send, recv, 3 * a + j, (x, y, 1 - c))
                cp.wait_recv()
                cp.wait_send()

    def start_d2d(refs, send, recv):
        x, y, c, others = _place()
        for a in range(n_w):
            for j, (ox, oy) in enumerate(others):
                landed = region(refs, a, _chip_index(ox, oy), c)
                _remote(landed, landed, send, recv, 3 * a + j, (x, y, 1 - c)).start()

    def wait_d2d(refs, send, recv):
        x, y, c, others = _place()
        for a in range(n_w):
            for j, (ox, oy) in enumerate(others):
                theirs = region(refs, a, _chip_index(ox, oy), 1 - c)
                cp = _remote(theirs, theirs, send, recv, 3 * a + j, (x, y, 1 - c))
                cp.wait_recv()
                cp.wait_send()

    sems, bufs = chains.phase(f"gather_ici_start_{tag}", bufs, None, None, n, start_ici)
    yield
    while wait_for is not None and chains.at != wait_for:
        yield
    sems, bufs = chains.phase(f"gather_forward_{tag}", bufs, sems, wait_ici, n, start_d2d)
    yield
    _, bufs = chains.phase(f"gather_done_{tag}", bufs, sems, wait_d2d, 0, None)
    for a, buf in zip(group, bufs):
        out[BIG[a][0]] = buf


def _grad_chain(chains, layer, group, pairs, core, other_core, chip, w, mom, var, carried):
    n = len(group)
    tag = f"{layer}_{group[0]}"
    kinds = [BIG[a][1] for a in group]
    names = [BIG[a][0] for a in group]
    sibling_of = lambda x, y, c: (x, y, 1 - c)

    def pair_start(refs, send, recv):
        x, y, c, _ = _place()
        for i in range(n):
            _remote(refs[i], refs[n + i], send, recv, i, sibling_of(x, y, c)).start()

    def pair_wait(refs, send, recv):
        x, y, c, _ = _place()
        for i in range(n):
            cp = _remote(refs[i], refs[n + i], send, recv, i, sibling_of(x, y, c))
            cp.wait_recv()
            cp.wait_send()

    def piece(ref, col, chip_idx):
        return _shard_region(ref, col, chip_idx, ref.shape[1] // N_CHIPS if col else ref.shape[0] // N_CHIPS)

    def scatter_start(refs, send, recv):
        x, y, c, others = _place()
        for i in range(n):
            for j, (ox, oy) in enumerate(others):
                _remote(piece(refs[i], kinds[i], _chip_index(ox, oy)), refs[n + i].at[j], send, recv, 3 * i + j, (ox, oy, c)).start()

    def scatter_wait(refs, send, recv):
        x, y, c, others = _place()
        for i in range(n):
            for j, (ox, oy) in enumerate(others):
                cp = _remote(piece(refs[i], kinds[i], _chip_index(ox, oy)), refs[n + i].at[j], send, recv, 3 * i + j, (ox, oy, c))
                cp.wait_recv()
                cp.wait_send()

    def join_start(refs, send, recv):
        x, y, c, _ = _place()
        for i in range(n):
            _remote(refs[i], refs[n + i], send, recv, i, sibling_of(x, y, c)).start()

    def join_wait(refs, send, recv):
        x, y, c, _ = _place()
        for i in range(n):
            cp = _remote(refs[i], refs[n + i], send, recv, i, sibling_of(x, y, c))
            cp.wait_recv()
            cp.wait_send()

    sides = ["m" if col else "n" for col in kinds]
    sends = [_mm(a, b, "tn", [F32], name="mm_g_send_" + names[i], half=(other_core, sides[i])) for i, (a, b) in enumerate(pairs)]
    half_shapes = [g.shape for g in sends]
    lands = [lax.empty(s, F32) for s in half_shapes]
    sems, bufs = chains.phase(f"pair_start_{tag}", sends + lands, None, None, n, pair_start)
    yield
    _, bufs = chains.phase(f"pair_wait_{tag}", bufs, sems, pair_wait, 0, None)
    pair_sums = [_mm(a, b, "tn", [BF16], name="mm_g_keep_" + names[i], half=(core, sides[i]), epilogue=_add_landed, extras=(bufs[n + i],))
                 for i, (a, b) in enumerate(pairs)]
    piece_shapes = [(s[0], s[1] // N_CHIPS) if col else (s[0] // N_CHIPS, s[1]) for col, s in zip(kinds, half_shapes)]
    slots = [lax.empty((N_CHIPS - 1, *s), BF16) for s in piece_shapes]
    sems, bufs = chains.phase(f"scatter_start_{tag}", pair_sums + slots, None, None, 3 * n, scatter_start)
    yield
    yield
    _, bufs = chains.phase(f"scatter_wait_{tag}", bufs, sems, scatter_wait, 0, None)
    reduced = [_sum_chips(bufs[i], bufs[n + i], chip, kinds[i], "sum_chips_" + names[i]) for i in range(n)]
    theirs = [lax.empty(s, F32) for s in piece_shapes]
    sems, bufs = chains.phase(f"join_start_{tag}", reduced + theirs, None, None, n, join_start)
    yield
    _, bufs = chains.phase(f"join_wait_{tag}", bufs, sems, join_wait, 0, None)
    for i in range(n):
        carried[names[i]] = _adamw_layer(w[names[i]], mom[names[i]], var[names[i]], bufs[i], bufs[n + i], carried.get(names[i]),
                                         layer, core, kinds[i], "adamw_" + names[i])
        chains.pin(carried[names[i]][0])


def _allreduce_small(vec, name):
    rows = vec.shape[0]
    masks = [(dx, dy, dc) for dx in (0, 1) for dy in (0, 1) for dc in (0, 1)][1:]

    def body(v_ref, o_ref, gather_ref, send_sems, recv_sems):
        x, y, c, _ = _place()
        me = 4 * x + 2 * y + c
        gather_ref[me] = v_ref[...]
        copies = []
        for k, (dx, dy, dc) in enumerate(masks):
            peer = (x ^ dx, y ^ dy, c ^ dc)
            copies.append(_remote(v_ref, gather_ref.at[me], send_sems, recv_sems, k, peer))
        for cp in copies:
            cp.start()
        for k, (dx, dy, dc) in enumerate(masks):
            slot = gather_ref.at[4 * (x ^ dx) + 2 * (y ^ dy) + (c ^ dc)]
            _remote(slot, slot, send_sems, recv_sems, k, (x, y, c)).wait_recv()
        for cp in copies:
            cp.wait_send()
        acc = gather_ref[0]
        for dev in range(1, N_DEV):
            acc = acc + gather_ref[dev]
        o_ref[...] = acc

    return pl.pallas_call(
        body, name=name, in_specs=[_VMEM], out_specs=_VMEM, out_shape=_sds((rows, LANES), F32),
        scratch_shapes=[pltpu.VMEM((N_DEV, rows, LANES), F32), pltpu.SemaphoreType.DMA((N_DEV - 1,)), pltpu.SemaphoreType.DMA((N_DEV - 1,))],
        compiler_params=pltpu.CompilerParams(has_side_effects=True, vmem_limit_bytes=VMEM_LIMIT_BYTES),
    )(vec)


def _relu2(acc):
    r = jnp.maximum(acc, 0.0)
    return acc, r * r


def _relu2_bwd(acc, up):
    return (acc * (2.0 * jnp.maximum(up.astype(F32), 0.0)),)


def _layer_fwd(x, h, w, cols, tables, tick, last, target=None):
    cos_t, sin_t = tables
    z = _mm(h, w("w_in"), "nn", [BF16], name="mm_in")
    a = _conv_fwd(z, w("conv_w"), cols)
    q, k, v = _rope_fwd(z, _after(w("q_norm"), tick(a, 0)), w("k_norm"), cos_t, sin_t, cols)
    o, lse = _attn_fwd(q, k, v)
    tick(o, 1)
    tick(None, 2)
    y_a = _mm(a, w("w_out_conv"), "nn", [BF16], name="mm_out_conv")
    y_b = _mm(o, w("w_out_attn"), "nn", [BF16], name="mm_out_attn")
    mix = _gate_fwd(z, w("gate_bias"), y_a, y_b, cols)
    mixed = _mm(mix, w("w_merge"), "nn", [BF16], name="mm_merge")
    x1, h2 = _resid_norm(x, mixed, _after(w("norm_mix_post"), tick(mixed, 3)), w("norm_mlp_pre"))
    up, act = _mm(h2, w("w_up"), "nn", [BF16, BF16], name="mm_up", epilogue=_relu2)
    f = _mm(act, w("w_down"), "nn", [BF16], name="mm_down")
    kept = dict(x=x, h=h, z=z, a=a, q=q, k=k, v=v, o=o, lse=lse, y_a=y_a, y_b=y_b, mix=mix, mixed=mixed, x1=x1, h2=h2, up=up, act=act, f=f)
    g_post = _after(w("norm_mlp_post"), tick(f, 4))
    if last:
        return _resid_norm_loss(x1, f, g_post, target), kept
    return _resid_norm(x1, f, g_post, w("norm_next")), kept


def _layer_bwd(dx_out, w, kept, cols, tables, tick, emit):
    cos_t, sin_t = tables
    t = kept
    df, d_norm_mlp_post = _norm_bwd(t["f"], _after(w["norm_mlp_post"], tick(dx_out, 0)), dx_out, None, BF16, "norm_bwd_mlp_post")
    dup = _mm(df, w["w_down"], "nt", [BF16], name="mm_d_act", epilogue=_relu2_bwd, extras=(t["up"],))
    emit((4, 5), [(t["h2"], dup), (t["act"], df)])
    dh2 = _mm(dup, w["w_up"], "nt", [BF16], name="mm_d_h2")
    dx1, d_norm_mlp_pre = _norm_bwd(t["x1"], _after(w["norm_mlp_pre"], tick(dh2, 1)), dh2, dx_out, F32, "norm_bwd_mlp_pre")
    dmixed, d_norm_mix_post = _norm_bwd(t["mixed"], w["norm_mix_post"], dx1, None, BF16, "norm_bwd_mix_post")
    dmix = _mm(dmixed, w["w_merge"], "nt", [BF16], name="mm_d_mix")
    dy_a, dy_b, dz_ga, dz_gb, dbias_a, dbias_b = _gate_bwd(t["z"], _after(w["gate_bias"], tick(dmix, 2)), t["y_a"], t["y_b"], dmix, cols)
    da = _mm(dy_a, w["w_out_conv"], "nt", [BF16], name="mm_d_a")
    emit((1, 2, 3), [(t["a"], dy_a), (t["o"], dy_b), (t["mix"], dmixed)])
    do = _mm(dy_b, w["w_out_attn"], "nt", [BF16], name="mm_d_o")
    dz_cb, dz_cc, dz_ci, d_conv_w = _conv_bwd(t["z"], _after(w["conv_w"], tick(do, 3)), da, cols)
    dq, dk_t, dv_t = _attn_bwd(t["q"], t["k"], t["v"], do, t["lse"])
    dz_q, dz_k, dz_v, d_q_norm, d_k_norm = _rope_bwd(t["z"], _after(w["q_norm"], tick(dv_t, 4)), w["k_norm"], cos_t, sin_t, dq, dk_t, dv_t, cols)
    dz = jnp.concatenate([dz_cb, dz_cc, dz_ci, dz_q, dz_k, dz_v, dz_ga, dz_gb], axis=1)
    emit((0,), [(t["h"], dz)])
    tick(dz, 5)
    dh = _mm(dz, w["w_in"], "nt", [BF16], name="mm_d_h", tk=3328)
    dx_in, d_norm_mix_pre = _norm_bwd(t["x"], _after(w["norm_mix_pre"], tick(dh, 6)), dh, dx1, F32, "norm_bwd_mix_pre")
    small = dict(norm_mix_pre=d_norm_mix_pre, gate_bias=jnp.concatenate([dbias_a, dbias_b], axis=1), conv_w=d_conv_w, q_norm=d_q_norm,
                 k_norm=d_k_norm, norm_mix_post=d_norm_mix_post, norm_mlp_pre=d_norm_mlp_pre, norm_mlp_post=d_norm_mlp_post)
    return dx_in, small


SMALL = ("norm_mix_pre", "gate_bias", "conv_w", "q_norm", "k_norm", "norm_mix_post", "norm_mlp_pre", "norm_mlp_post")
WEIGHTS = ("norm_mix_pre", "w_in", "gate_bias", "conv_w", "q_norm", "k_norm", "w_out_conv", "w_out_attn", "w_merge",
           "norm_mix_post", "norm_mlp_pre", "w_up", "w_down", "norm_mlp_post")


def _pack(parts):
    flat = jnp.concatenate([a.reshape(-1) for a in parts])
    rows = -(-flat.shape[0] // LANES)
    pad = (-rows) % 8
    flat = jnp.pad(flat, (0, (rows + pad) * LANES - flat.shape[0]))
    return flat.reshape(rows + pad, LANES)


def _unpack(packed, shapes):
    flat = packed.reshape(-1)
    out, off = [], 0
    for shp in shapes:
        n = math.prod(shp)
        out.append(flat[off:off + n].reshape(shp))
        off += n
    return out


def kernel(x, norm_mix_pre, w_in, gate_bias, conv_w, q_norm, k_norm, w_out_conv, w_out_attn, w_merge, norm_mix_post, norm_mlp_pre, w_up, w_down, norm_mlp_post, loss_target, m_norm_mix_pre, m_w_in, m_gate_bias, m_conv_w, m_q_norm, m_k_norm, m_w_out_conv, m_w_out_attn, m_w_merge, m_norm_mix_post, m_norm_mlp_pre, m_w_up, m_w_down, m_norm_mlp_post, v_norm_mix_pre, v_w_in, v_gate_bias, v_conv_w, v_q_norm, v_k_norm, v_w_out_conv, v_w_out_attn, v_w_merge, v_norm_mix_post, v_norm_mlp_pre, v_w_up, v_w_down, v_norm_mlp_post):
    w = dict(norm_mix_pre=norm_mix_pre, w_in=w_in, gate_bias=gate_bias, conv_w=conv_w, q_norm=q_norm, k_norm=k_norm, w_out_conv=w_out_conv,
             w_out_attn=w_out_attn, w_merge=w_merge, norm_mix_post=norm_mix_post, norm_mlp_pre=norm_mlp_pre, w_up=w_up, w_down=w_down,
             norm_mlp_post=norm_mlp_post)
    mom = dict(norm_mix_pre=m_norm_mix_pre, w_in=m_w_in, gate_bias=m_gate_bias, conv_w=m_conv_w, q_norm=m_q_norm, k_norm=m_k_norm,
               w_out_conv=m_w_out_conv, w_out_attn=m_w_out_attn, w_merge=m_w_merge, norm_mix_post=m_norm_mix_post, norm_mlp_pre=m_norm_mlp_pre,
               w_up=m_w_up, w_down=m_w_down, norm_mlp_post=m_norm_mlp_post)
    var = dict(norm_mix_pre=v_norm_mix_pre, w_in=v_w_in, gate_bias=v_gate_bias, conv_w=v_conv_w, q_norm=v_q_norm, k_norm=v_k_norm,
               w_out_conv=v_w_out_conv, w_out_attn=v_w_out_attn, w_merge=v_w_merge, norm_mix_post=v_norm_mix_post, norm_mlp_pre=v_norm_mlp_pre,
               w_up=v_w_up, w_down=v_w_down, norm_mlp_post=v_norm_mlp_post)
    depth = w_in.shape[0]
    _, s, d = x.shape
    cols = _Cols(d)
    x0 = x.reshape(s, d)
    target = loss_target.reshape(s, d)
    tables = _rope_tables(s)
    chip = (2 * lax.axis_index("x") + lax.axis_index("y")).astype(jnp.int32)
    core = lax.axis_index("c").astype(jnp.int32)
    chip_vec, core_vec, other_core_vec = chip.reshape(1), core.reshape(1), (1 - core).reshape(1)

    n_conv = conv_w.shape[-1]
    placed = lax.dynamic_update_slice_in_dim(jnp.zeros((depth, conv_w.shape[1], n_conv * N_CHIPS), F32), conv_w, chip * n_conv, axis=2)
    conv_full = _unpack(_allreduce_small(_pack([jnp.where(core == 0, placed, 0.0)]), "gather_conv_w"), [placed.shape])[0]

    chains = _Chains()
    chains.pin(conv_full)
    full = [{} for _ in range(depth)]
    everything = tuple(range(N_BIG))
    gathers = []
    for l in range(depth):
        bufs = [_cast_place(w[name], l, chip_vec, col, "cast_place_" + name) for name, col in BIG]
        for group in ((0,), everything[1:]):
            if group[0] == 0:
                wait_for = ("fwd", l - 1, 3) if l else None
            else:
                wait_for = ("fwd", l, 0 if l else 1)
            gathers.append(_gather_chain(chains, f"{l}_{group[0]}", group, [bufs[a] for a in group], full[l], wait_for))
            next(gathers[-1])
    next(gathers[0])
    next(gathers[0], None)
    for g in gathers[1:]:
        chains.add(g)

    def layer_params(l):
        p = dict(full[l])
        p["conv_w"] = conv_full[l]
        p["gate_bias"] = gate_bias[l].reshape(1, -1)
        for name in ("norm_mix_pre", "q_norm", "k_norm", "norm_mix_post", "norm_mlp_pre", "norm_mlp_post"):
            p[name] = w[name][l].reshape(1, -1)
        if l + 1 < depth:
            p["norm_next"] = w["norm_mix_pre"][l + 1].reshape(1, -1)
        return p

    kept = []
    xl, h = x0, _norm_first(x0, norm_mix_pre[0].reshape(1, -1))
    for l in range(depth):
        last = l == depth - 1
        (xl, h), t = _layer_fwd(xl, h, lambda name, l=l: layer_params(l)[name], cols, tables,
                                lambda dep, k, l=l: chains.tick(dep, ("fwd", l, k)), last, target if last else None)
        kept.append(t)
    dx, loss_local = xl, h

    carried = {}
    small = [None] * depth
    for l in reversed(range(depth)):
        def emit(group, pairs, l=l):
            chains.add(_grad_chain(chains, l, group, pairs, core_vec, other_core_vec, chip_vec, w, mom, var, carried))

        dx, small[l] = _layer_bwd(dx, layer_params(l), kept[l], cols, tables, lambda dep, k: chains.tick(dep), emit)
    while chains.active:
        chains.tick(None)
    grads = {name: carried[name][0] for name, _ in BIG}
    delta = {name: carried[name][1] for name, _ in BIG}
    new_m = {name: carried[name][2] for name, _ in BIG}
    new_v = {name: carried[name][3] for name, _ in BIG}

    small_full_shapes = [(depth,) + small[0][name].shape for name in SMALL]
    packed = _pack([jnp.stack([small[l][name] for l in range(depth)]) for name in SMALL] + [jnp.broadcast_to(loss_local.reshape(1), (LANES,))])
    small_sum = _unpack(_allreduce_small(packed, "allreduce_small"), small_full_shapes + [(LANES,)])
    loss = small_sum[-1][0]
    for name, g in zip(SMALL, small_sum[:-1]):
        if name == "conv_w":
            g = lax.dynamic_slice_in_dim(g, chip * n_conv, n_conv, axis=2)
        grads[name] = g.reshape(w[name].shape)
    small_shapes = [w[name].shape for name in SMALL]
    packs = [_pack([src[name] for name in SMALL]) for src in (w, grads, mom, var)]
    for dst, out in zip((delta, new_m, new_v), _adamw(*packs, "adamw_small")):
        for name, val in zip(SMALL, _unpack(out, small_shapes)):
            dst[name] = val

    grad_x = dx.reshape(x.shape)
    return (loss, grad_x, *[grads[n] for n in WEIGHTS], *[delta[n] for n in WEIGHTS], *[new_m[n] for n in WEIGHTS], *[new_v[n] for n in WEIGHTS])
```

```python
import math

import jax
import jax.numpy as jnp
from jax import lax
from jax.experimental import pallas as pl
from jax.experimental.pallas import tpu as pltpu

F32 = jnp.float32
BF16 = jnp.bfloat16

HEAD_DIM = 128
GROUP = 4
GRID_W = 64
ROPE_THETA = 10000.0
RMS_EPS = 1e-6
ADAM_LR = 0.001
ADAM_B1 = 0.9
ADAM_B2 = 0.999
ADAM_EPS = 1e-08
ADAM_WD = 0.01
ADAM_STEP = 10

LANES = 128
N_CHIPS = 4
N_DEV = 8
VMEM_LIMIT_BYTES = 56 * 1024 * 1024
MESH = pl.DeviceIdType.MESH
ANY = pl.BlockSpec(memory_space=pl.ANY)


def _tile(dim, cap, mult):
    if dim <= cap:
        return dim
    t = (cap // mult) * mult
    while t >= mult:
        if dim % t == 0:
            return t
        t -= mult
    raise ValueError(f"no tile for {dim} under {cap} in multiples of {mult}")


def _params(sem=None):
    return pltpu.CompilerParams(dimension_semantics=sem, vmem_limit_bytes=VMEM_LIMIT_BYTES)


def _sds(shape, dtype):
    return jax.ShapeDtypeStruct(tuple(shape), dtype)


def _rstd(x):
    return lax.rsqrt(jnp.mean(x * x, axis=-1, keepdims=True) + RMS_EPS)


_DOT_DIMS = {"nn": ((1,), (0,)), "nt": ((1,), (1,)), "tn": ((0,), (0,))}


def _mm(a, b, mode, out_dtypes, *, name, epilogue=None, extras=(), tm=1024, tn=1024, tk=2048, half=None):
    if mode == "nn":
        (m, k), (k2, n) = a.shape, b.shape
    elif mode == "nt":
        (m, k), (n, k2) = a.shape, b.shape
    else:
        (k, m), (k2, n) = a.shape, b.shape
    assert k == k2, (a.shape, b.shape, mode)
    side = half[1] if half is not None else None
    if side == "m":
        m //= 2
    elif side == "n":
        n //= 2
    tm, tn, tk = _tile(m, tm, 8), _tile(n, tn, LANES), _tile(k, tk, LANES)
    nk = k // tk
    row = (lambda i, s: i + s[0][0] * (m // tm)) if side == "m" else (lambda i, s: i)
    col = (lambda j, s: j + s[0][0] * (n // tn)) if side == "n" else (lambda j, s: j)
    a_spec = (pl.BlockSpec((tk, tm), lambda i, j, kk, *s: (jnp.where(j == 0, kk, nk - 1), row(i, s))) if mode == "tn"
              else pl.BlockSpec((tm, tk), lambda i, j, kk, *s: (row(i, s), kk)))
    b_spec = pl.BlockSpec((tn, tk), lambda i, j, kk, *s: (col(j, s), kk)) if mode == "nt" else pl.BlockSpec((tk, tn), lambda i, j, kk, *s: (kk, col(j, s)))
    tile_spec = pl.BlockSpec((tm, tn), lambda i, j, kk, *s: (i, j))
    n_extra, n_out = len(extras), len(out_dtypes)
    dims = (_DOT_DIMS[mode], ((), ()))
    keep_t = mode == "tn"

    def body(*refs):
        if half is not None:
            refs = refs[1:]
        a_ref, b_ref = refs[:2]
        extra_refs = refs[2:2 + n_extra]
        out_refs = refs[2 + n_extra:2 + n_extra + n_out]
        if keep_t:
            at_ref, kt = refs[-1], pl.program_id(2)

            @pl.when(pl.program_id(1) == 0)
            def _():
                at_ref[kt] = a_ref[...].astype(BF16).T

            part = lax.dot_general(at_ref[kt], b_ref[...].astype(BF16), (_DOT_DIMS["nn"], ((), ())), preferred_element_type=F32)
        else:
            part = lax.dot_general(a_ref[...].astype(BF16), b_ref[...].astype(BF16), dims, preferred_element_type=F32)

        def finish(total):
            res = epilogue(total, *[e[...] for e in extra_refs]) if epilogue is not None else (total,)
            for o, r in zip(out_refs, res):
                o[...] = r.astype(o.dtype)

        if nk == 1:
            finish(part)
        else:
            acc = refs[-2] if keep_t else refs[-1]
            kk = pl.program_id(2)

            @pl.when(kk == 0)
            def _():
                acc[...] = part

            @pl.when(kk > 0)
            def _():
                acc[...] += part

            @pl.when(kk == nk - 1)
            def _():
                finish(acc[...])

    grid = (m // tm, n // tn, nk)
    in_specs, out_specs = [a_spec, b_spec] + [tile_spec] * n_extra, [tile_spec] * n_out
    scratch = ([pltpu.VMEM((tm, tn), F32)] if nk > 1 else []) + ([pltpu.VMEM((nk, tm, tk), BF16)] if keep_t else [])
    if half is None:
        layout, lead = dict(grid=grid, in_specs=in_specs, out_specs=out_specs, scratch_shapes=scratch), ()
    else:
        layout = dict(grid_spec=pltpu.PrefetchScalarGridSpec(num_scalar_prefetch=1, grid=grid, in_specs=in_specs, out_specs=out_specs,
                                                             scratch_shapes=scratch))
        lead = (half[0],)
    outs = pl.pallas_call(body, name=name, out_shape=[_sds((m, n), d) for d in out_dtypes],
                          compiler_params=_params(("parallel", "arbitrary" if keep_t else "parallel", "arbitrary")), **layout)(*lead, a, b, *extras)
    return outs if n_out > 1 else outs[0]


ROW_TILE = 512
GATE_ROW_TILE = 1024


def _norm_first(x, g):
    s, d = x.shape
    ts = _tile(s, ROW_TILE, 8)

    def body(x_ref, g_ref, h_ref):
        xv = x_ref[...]
        h_ref[...] = (xv * _rstd(xv) * g_ref[...]).astype(h_ref.dtype)

    row = pl.BlockSpec((ts, d), lambda i: (i, 0))
    vec = pl.BlockSpec((1, d), lambda i: (0, 0))
    return pl.pallas_call(body, name="norm_first", grid=(s // ts,), in_specs=[row, vec], out_specs=row,
                          out_shape=_sds((s, d), BF16), compiler_params=_params(("parallel",)))(x, g)


def _resid_norm(xres, y, g_post, g_next):
    s, d = xres.shape
    ts = _tile(s, ROW_TILE, 8)

    def body(x_ref, y_ref, gp_ref, gn_ref, xn_ref, hn_ref):
        yv = y_ref[...].astype(F32)
        xn = x_ref[...] + yv * _rstd(yv) * gp_ref[...]
        xn_ref[...] = xn
        hn_ref[...] = (xn * _rstd(xn) * gn_ref[...]).astype(hn_ref.dtype)

    row = pl.BlockSpec((ts, d), lambda i: (i, 0))
    vec = pl.BlockSpec((1, d), lambda i: (0, 0))
    return pl.pallas_call(body, name="resid_norm", grid=(s // ts,), in_specs=[row, row, vec, vec], out_specs=[row, row],
                          out_shape=[_sds((s, d), F32), _sds((s, d), BF16)], compiler_params=_params(("parallel",)))(xres, y, g_post, g_next)


def _resid_norm_loss(xres, y, g_post, target):
    s, d = xres.shape
    ts = _tile(s, ROW_TILE, 8)
    n_steps = s // ts

    def body(x_ref, y_ref, gp_ref, t_ref, dout_ref, loss_ref, acc_ref):
        i = pl.program_id(0)
        yv = y_ref[...].astype(F32)
        err = x_ref[...] + yv * _rstd(yv) * gp_ref[...] - t_ref[...]
        dout_ref[...] = err / d
        part = jnp.sum(err * err, axis=0, keepdims=True)

        @pl.when(i == 0)
        def _():
            acc_ref[...] = part

        @pl.when(i > 0)
        def _():
            acc_ref[...] += part

        @pl.when(i == n_steps - 1)
        def _():
            loss_ref[...] = 0.5 * jnp.sum(acc_ref[...], axis=1, keepdims=True) / d

    row = pl.BlockSpec((ts, d), lambda i: (i, 0))
    vec = pl.BlockSpec((1, d), lambda i: (0, 0))
    one = pl.BlockSpec((1, 1), lambda i: (0, 0))
    return pl.pallas_call(body, name="resid_norm_loss", grid=(n_steps,), in_specs=[row, row, vec, row], out_specs=[row, one],
                          out_shape=[_sds((s, d), F32), _sds((1, 1), F32)], scratch_shapes=[pltpu.VMEM((1, d), F32)],
                          compiler_params=_params(("arbitrary",)))(xres, y, g_post, target)


def _norm_bwd(xin, g, dout, dres, out_dtype, name):
    s, d = xin.shape
    ts = _tile(s, ROW_TILE, 8)
    has_res = dres is not None

    def body(*refs):
        x_ref, g_ref, do_ref = refs[:3]
        dx_ref, dg_ref = refs[-2:]
        i = pl.program_id(0)
        xv, dov = x_ref[...].astype(F32), do_ref[...].astype(F32)
        r = _rstd(xv)
        xhat = xv * r
        dg = jnp.sum(dov * xhat, axis=0, keepdims=True)
        dxh = dov * g_ref[...]
        dx = r * (dxh - xhat * jnp.mean(dxh * xhat, axis=-1, keepdims=True))
        if has_res:
            dx = dx + refs[3][...]
        dx_ref[...] = dx.astype(dx_ref.dtype)

        @pl.when(i == 0)
        def _():
            dg_ref[...] = dg

        @pl.when(i > 0)
        def _():
            dg_ref[...] += dg

    row = pl.BlockSpec((ts, d), lambda i: (i, 0))
    vec = pl.BlockSpec((1, d), lambda i: (0, 0))
    ops = [xin, g, dout] + ([dres] if has_res else [])
    return pl.pallas_call(body, name=name, grid=(s // ts,), in_specs=[row, vec, row] + ([row] if has_res else []),
                          out_specs=[row, vec], out_shape=[_sds((s, d), out_dtype), _sds((1, d), F32)],
                          compiler_params=_params(("arbitrary",)))(*ops)


class _Cols:
    def __init__(self, d):
        self.d = d
        self.kv = d // GROUP
        self.cb, self.cc, self.ci, self.q = 0, d, 2 * d, 3 * d
        self.k = 4 * d
        self.v = 4 * d + self.kv
        self.ga = 4 * d + 2 * self.kv
        self.gb = 5 * d + 2 * self.kv
        self.width = 6 * d + 2 * self.kv


CONV_COLS = 128


def _shift_rows(u, down):
    s = u.shape[0]
    rows = lax.broadcasted_iota(jnp.int32, u.shape, 0)
    if down:
        return jnp.where(rows == 0, 0.0, pltpu.roll(u, 1, 0))
    return jnp.where(rows == s - 1, 0.0, pltpu.roll(u, s - 1, 0))


def _conv_fwd(z, w, cols):
    s, d = z.shape[0], cols.d
    cw = CONV_COLS

    def body(cb_ref, cc_ref, ci_ref, w_ref, a_ref):
        u = cc_ref[...].astype(F32) * ci_ref[...].astype(F32)
        wv = w_ref[...]
        conv = wv[0:1] * _shift_rows(u, True) + wv[1:2] * u + wv[2:3] * _shift_rows(u, False)
        a_ref[...] = (cb_ref[...].astype(F32) * conv).astype(a_ref.dtype)

    def zspec(off):
        return pl.BlockSpec((s, cw), lambda j: (0, off // cw + j))

    return pl.pallas_call(body, name="conv_fwd", grid=(d // cw,),
                          in_specs=[zspec(cols.cb), zspec(cols.cc), zspec(cols.ci), pl.BlockSpec((3, cw), lambda j: (0, j))],
                          out_specs=pl.BlockSpec((s, cw), lambda j: (0, j)), out_shape=_sds((s, d), BF16),
                          compiler_params=_params(("parallel",)))(z, z, z, w)


def _conv_bwd(z, w, da, cols):
    s, d = z.shape[0], cols.d
    cw = CONV_COLS

    def body(cb_ref, cc_ref, ci_ref, w_ref, da_ref, dcb_ref, dcc_ref, dci_ref, dw_ref):
        cb, cc, ci, dav = cb_ref[...].astype(F32), cc_ref[...].astype(F32), ci_ref[...].astype(F32), da_ref[...].astype(F32)
        wv = w_ref[...]
        u = cc * ci
        um, up = _shift_rows(u, True), _shift_rows(u, False)
        conv = wv[0:1] * um + wv[1:2] * u + wv[2:3] * up
        dcb_ref[...] = (dav * conv).astype(dcb_ref.dtype)
        dconv = dav * cb
        dw_ref[0:1, :] = jnp.sum(dconv * um, axis=0, keepdims=True)
        dw_ref[1:2, :] = jnp.sum(dconv * u, axis=0, keepdims=True)
        dw_ref[2:3, :] = jnp.sum(dconv * up, axis=0, keepdims=True)
        du = wv[0:1] * _shift_rows(dconv, False) + wv[1:2] * dconv + wv[2:3] * _shift_rows(dconv, True)
        dcc_ref[...] = (du * ci).astype(dcc_ref.dtype)
        dci_ref[...] = (du * cc).astype(dci_ref.dtype)

    def zspec(off):
        return pl.BlockSpec((s, cw), lambda j: (0, off // cw + j))

    col = pl.BlockSpec((s, cw), lambda j: (0, j))
    wspec = pl.BlockSpec((3, cw), lambda j: (0, j))
    return pl.pallas_call(body, name="conv_bwd", grid=(d // cw,),
                          in_specs=[zspec(cols.cb), zspec(cols.cc), zspec(cols.ci), wspec, col],
                          out_specs=[col, col, col, wspec],
                          out_shape=[_sds((s, d), BF16)] * 3 + [_sds((3, d), F32)],
                          compiler_params=_params(("parallel",)))(z, z, z, w, da)


def _gate_fwd(z, bias, y_a, y_b, cols):
    s, d = y_a.shape
    ts, cw = _tile(s, GATE_ROW_TILE, 8), cols.kv
    nj = d // cw

    def body(ga_ref, gb_ref, ba_ref, bb_ref, ya_ref, yb_ref, o_ref):
        gate_a = jax.nn.sigmoid(ga_ref[...].astype(F32) + ba_ref[...])
        gate_b = jax.nn.sigmoid(gb_ref[...].astype(F32) + bb_ref[...])
        o_ref[...] = (gate_a * ya_ref[...].astype(F32) + gate_b * yb_ref[...].astype(F32)).astype(o_ref.dtype)

    tile = pl.BlockSpec((ts, cw), lambda i, j: (i, j))
    return pl.pallas_call(
        body, name="gate_fwd", grid=(s // ts, nj),
        in_specs=[pl.BlockSpec((ts, cw), lambda i, j: (i, cols.ga // cw + j)), pl.BlockSpec((ts, cw), lambda i, j: (i, cols.gb // cw + j)),
                  pl.BlockSpec((1, cw), lambda i, j: (0, j)), pl.BlockSpec((1, cw), lambda i, j: (0, nj + j)), tile, tile],
        out_specs=tile, out_shape=_sds((s, d), BF16), compiler_params=_params(("parallel", "parallel")))(z, z, bias, bias, y_a, y_b)


def _gate_bwd(z, bias, y_a, y_b, dmix, cols):
    s, d = y_a.shape
    ts, cw = _tile(s, GATE_ROW_TILE, 8), cols.kv
    nj = d // cw

    def body(ga_ref, gb_ref, ba_ref, bb_ref, ya_ref, yb_ref, dm_ref, dya_ref, dyb_ref, dga_ref, dgb_ref, dba_ref, dbb_ref):
        i = pl.program_id(1)
        gate_a = jax.nn.sigmoid(ga_ref[...].astype(F32) + ba_ref[...])
        gate_b = jax.nn.sigmoid(gb_ref[...].astype(F32) + bb_ref[...])
        dm = dm_ref[...].astype(F32)
        dya_ref[...] = (dm * gate_a).astype(dya_ref.dtype)
        dyb_ref[...] = (dm * gate_b).astype(dyb_ref.dtype)
        dga = dm * ya_ref[...].astype(F32) * (gate_a * (1.0 - gate_a))
        dgb = dm * yb_ref[...].astype(F32) * (gate_b * (1.0 - gate_b))
        dga_ref[...] = dga.astype(dga_ref.dtype)
        dgb_ref[...] = dgb.astype(dgb_ref.dtype)
        sa = jnp.sum(dga, axis=0, keepdims=True)
        sb = jnp.sum(dgb, axis=0, keepdims=True)

        @pl.when(i == 0)
        def _():
            dba_ref[...] = sa
            dbb_ref[...] = sb

        @pl.when(i > 0)
        def _():
            dba_ref[...] += sa
            dbb_ref[...] += sb

    tile = pl.BlockSpec((ts, cw), lambda j, i: (i, j))
    vec = pl.BlockSpec((1, cw), lambda j, i: (0, j))
    return pl.pallas_call(
        body, name="gate_bwd", grid=(nj, s // ts),
        in_specs=[pl.BlockSpec((ts, cw), lambda j, i: (i, cols.ga // cw + j)), pl.BlockSpec((ts, cw), lambda j, i: (i, cols.gb // cw + j)),
                  vec, pl.BlockSpec((1, cw), lambda j, i: (0, nj + j)), tile, tile, tile],
        out_specs=[tile, tile, tile, tile, vec, vec],
        out_shape=[_sds((s, d), BF16)] * 4 + [_sds((1, d), F32)] * 2,
        compiler_params=_params(("parallel", "arbitrary")))(z, z, bias, bias, y_a, y_b, dmix)


def _rope_tables(s):
    axis_dim = HEAD_DIM // 2
    n_freq = axis_dim // 2
    rows = s // GRID_W
    row_idx = jnp.repeat(jnp.arange(rows, dtype=jnp.int32), GRID_W)
    col_idx = jnp.tile(jnp.arange(GRID_W, dtype=jnp.int32), rows)
    inv_freq = ROPE_THETA ** (-jnp.arange(0, axis_dim, 2, dtype=F32) / axis_dim)
    ang = jnp.stack([row_idx.astype(F32)[:, None] * inv_freq, col_idx.astype(F32)[:, None] * inv_freq], axis=1)
    cos, sin = jnp.cos(ang), jnp.sin(ang)
    cos_t = jnp.stack([cos, cos], axis=2).reshape(s, HEAD_DIM)
    sin_t = jnp.stack([-sin, sin], axis=2).reshape(s, HEAD_DIM)
    return cos_t, sin_t


def _partner(x):
    n = x.shape[-1]
    lane = lax.broadcasted_iota(jnp.int32, x.shape, x.ndim - 1)
    quarter = HEAD_DIM // 4
    return jnp.where(lane % (2 * quarter) < quarter, pltpu.roll(x, n - quarter, x.ndim - 1), pltpu.roll(x, quarter, x.ndim - 1))


LOG2E = math.log2(math.e)
Q_SCALE = LOG2E / math.sqrt(HEAD_DIM)


def _rope_fwd(z, qn, kn, cos_t, sin_t, cols):
    s, d, kv = z.shape[0], cols.d, cols.kv
    ts = _tile(s, ROW_TILE, 8)
    scale = Q_SCALE

    def body(q_ref, k_ref, v_ref, qn_ref, kn_ref, c_ref, s_ref, qo_ref, ko_ref, vo_ref):
        c, sn = c_ref[...], s_ref[...]

        def head(xh, g):
            xn = xh * _rstd(xh) * g
            return xn * c + _partner(xn) * sn

        for h in range(d // HEAD_DIM):
            sl = slice(h * HEAD_DIM, (h + 1) * HEAD_DIM)
            qo_ref[:, sl] = (head(q_ref[:, sl].astype(F32), qn_ref[...]) * scale).astype(qo_ref.dtype)
        for h in range(kv // HEAD_DIM):
            sl = slice(h * HEAD_DIM, (h + 1) * HEAD_DIM)
            ko_ref[:, sl] = head(k_ref[:, sl].astype(F32), kn_ref[...]).astype(ko_ref.dtype)
        vo_ref[...] = v_ref[...].astype(vo_ref.dtype)

    vec = pl.BlockSpec((1, HEAD_DIM), lambda i: (0, 0))
    tab = pl.BlockSpec((ts, HEAD_DIM), lambda i: (i, 0))
    return pl.pallas_call(
        body, name="rope_fwd", grid=(s // ts,),
        in_specs=[pl.BlockSpec((ts, d), lambda i: (i, cols.q // d)), pl.BlockSpec((ts, kv), lambda i: (i, cols.k // kv)),
                  pl.BlockSpec((ts, kv), lambda i: (i, cols.v // kv)), vec, vec, tab, tab],
        out_specs=[pl.BlockSpec((ts, d), lambda i: (i, 0)), pl.BlockSpec((ts, kv), lambda i: (i, 0)), pl.BlockSpec((ts, kv), lambda i: (i, 0))],
        out_shape=[_sds((s, d), BF16), _sds((s, kv), BF16), _sds((s, kv), BF16)],
        compiler_params=_params(("parallel",)))(z, z, z, qn, kn, cos_t, sin_t)


def _rope_bwd(z, qn, kn, cos_t, sin_t, dq, dk_t, dv_t, cols):
    s, d, kv = z.shape[0], cols.d, cols.kv
    ts = _tile(s, ROW_TILE, 8)
    scale = 1.0 / math.sqrt(HEAD_DIM)

    def body(q_ref, k_ref, qn_ref, kn_ref, c_ref, s_ref, dq_ref, dkt_ref, dvt_ref, dzq_ref, dzk_ref, dzv_ref, dqn_ref, dkn_ref):
        i = pl.program_id(0)
        c, sn = c_ref[...], s_ref[...]
        dk_all = dkt_ref[...].T * (1.0 / LOG2E)

        def head_bwd(xh, g, drot):
            dxn = drot * c + _partner(drot * sn)
            r = _rstd(xh)
            xhat = xh * r
            dgain = jnp.sum(dxn * xhat, axis=0, keepdims=True)
            dxh = dxn * g
            return r * (dxh - xhat * jnp.mean(dxh * xhat, axis=-1, keepdims=True)), dgain

        dqn = jnp.zeros((1, HEAD_DIM), F32)
        for h in range(d // HEAD_DIM):
            sl = slice(h * HEAD_DIM, (h + 1) * HEAD_DIM)
            dx, dg = head_bwd(q_ref[:, sl].astype(F32), qn_ref[...], dq_ref[:, sl].astype(F32) * scale)
            dzq_ref[:, sl] = dx.astype(dzq_ref.dtype)
            dqn = dqn + dg
        dkn = jnp.zeros((1, HEAD_DIM), F32)
        for h in range(kv // HEAD_DIM):
            sl = slice(h * HEAD_DIM, (h + 1) * HEAD_DIM)
            dx, dg = head_bwd(k_ref[:, sl].astype(F32), kn_ref[...], dk_all[:, sl])
            dzk_ref[:, sl] = dx.astype(dzk_ref.dtype)
            dkn = dkn + dg
        dzv_ref[...] = dvt_ref[...].T.astype(dzv_ref.dtype)

        @pl.when(i == 0)
        def _():
            dqn_ref[...] = dqn
            dkn_ref[...] = dkn

        @pl.when(i > 0)
        def _():
            dqn_ref[...] += dqn
            dkn_ref[...] += dkn

    vec = pl.BlockSpec((1, HEAD_DIM), lambda i: (0, 0))
    tab = pl.BlockSpec((ts, HEAD_DIM), lambda i: (i, 0))
    qrow = pl.BlockSpec((ts, d), lambda i: (i, 0))
    krow = pl.BlockSpec((ts, kv), lambda i: (i, 0))
    kcol = pl.BlockSpec((kv, ts), lambda i: (0, i))
    return pl.pallas_call(
        body, name="rope_bwd", grid=(s // ts,),
        in_specs=[pl.BlockSpec((ts, d), lambda i: (i, cols.q // d)), pl.BlockSpec((ts, kv), lambda i: (i, cols.k // kv)),
                  vec, vec, tab, tab, qrow, kcol, kcol],
        out_specs=[qrow, krow, krow, vec, vec],
        out_shape=[_sds((s, d), BF16), _sds((s, kv), BF16), _sds((s, kv), BF16), _sds((1, HEAD_DIM), F32), _sds((1, HEAD_DIM), F32)],
        compiler_params=_params(("arbitrary",)))(z, z, qn, kn, cos_t, sin_t, dq, dk_t, dv_t)


Q_TILE = 256
_NT = (((1,), (1,)), ((), ()))
_NN = (((1,), (0,)), ((), ()))


def _attn_fwd(q, k, v):
    s, d = q.shape
    kvh = k.shape[1] // HEAD_DIM
    tq = _tile(s, Q_TILE, LANES)
    gw = GROUP * HEAD_DIM

    def body(q_ref, k_ref, v_ref, o_ref, lse_ref):
        kk, vv = k_ref[...], v_ref[...]
        for g in range(GROUP):
            sl = slice(g * HEAD_DIM, (g + 1) * HEAD_DIM)
            sc = lax.dot_general(q_ref[:, sl], kk, _NT, preferred_element_type=F32)
            mx = jnp.max(sc, axis=-1, keepdims=True)
            p = jnp.exp2(sc - mx)
            l = jnp.sum(p, axis=-1, keepdims=True)
            o = lax.dot_general(p.astype(BF16), vv, _NN, preferred_element_type=F32) * (1.0 / l)
            o_ref[:, sl] = o.astype(o_ref.dtype)
            lse_ref[:, g:g + 1] = mx + jnp.log(l) * LOG2E

    return pl.pallas_call(
        body, name="attn_fwd", grid=(kvh, s // tq),
        in_specs=[pl.BlockSpec((tq, gw), lambda j, i: (i, j)), pl.BlockSpec((s, HEAD_DIM), lambda j, i: (0, j)), pl.BlockSpec((s, HEAD_DIM), lambda j, i: (0, j))],
        out_specs=[pl.BlockSpec((tq, gw), lambda j, i: (i, j)), pl.BlockSpec((None, tq, GROUP), lambda j, i: (j, i, 0))],
        out_shape=[_sds((s, d), BF16), _sds((kvh, s, GROUP), F32)],
        compiler_params=_params(("parallel", "parallel")))(q, k, v)


_TN = (((0,), (0,)), ((), ()))


def _attn_bwd(q, k, v, do, lse):
    s, d = q.shape
    kv = k.shape[1]
    kvh = kv // HEAD_DIM
    tq = _tile(s, Q_TILE, LANES)
    gw = GROUP * HEAD_DIM

    def body(q_ref, k_ref, v_ref, do_ref, lse_ref, dq_ref, dkt_ref, dvt_ref):
        i = pl.program_id(1)
        kk, vv = k_ref[...], v_ref[...]

        @pl.when(i == 0)
        def _():
            dkt_ref[...] = jnp.zeros_like(dkt_ref)
            dvt_ref[...] = jnp.zeros_like(dvt_ref)

        for g in range(GROUP):
            sl = slice(g * HEAD_DIM, (g + 1) * HEAD_DIM)
            qg, dog = q_ref[:, sl], do_ref[:, sl]
            sc = lax.dot_general(qg, kk, _NT, preferred_element_type=F32)
            p = jnp.exp2(sc - lse_ref[:, g:g + 1])
            dp = lax.dot_general(dog, vv, _NT, preferred_element_type=F32)
            delta = jnp.sum(p * dp, axis=-1, keepdims=True)
            ds = (p * (dp - delta)).astype(BF16)
            dq_ref[:, sl] = lax.dot_general(ds, kk, _NN, preferred_element_type=F32).astype(dq_ref.dtype)
            dkt_ref[...] += lax.dot_general(qg, ds, _TN, preferred_element_type=F32)
            dvt_ref[...] += lax.dot_general(dog, p.astype(BF16), _TN, preferred_element_type=F32)

    qspec = pl.BlockSpec((tq, gw), lambda j, i: (i, j))
    kspec = pl.BlockSpec((s, HEAD_DIM), lambda j, i: (0, j))
    stat = pl.BlockSpec((None, tq, GROUP), lambda j, i: (j, i, 0))
    tspec = pl.BlockSpec((HEAD_DIM, s), lambda j, i: (j, 0))
    return pl.pallas_call(
        body, name="attn_bwd", grid=(kvh, s // tq),
        in_specs=[qspec, kspec, kspec, qspec, stat], out_specs=[qspec, tspec, tspec],
        out_shape=[_sds((s, d), BF16), _sds((kv, s), F32), _sds((kv, s), F32)],
        compiler_params=_params(("parallel", "arbitrary")))(q, k, v, do, lse)


ELEM_BLOCK_BYTES = 2 << 20

BIG = (("w_in", True), ("w_out_conv", False), ("w_out_attn", False), ("w_merge", False), ("w_up", True), ("w_down", False))
N_BIG = len(BIG)


def _elem_tiles(rows, width):
    tc = _tile(width, 2048, LANES)
    tr = _tile(rows, max(8, ELEM_BLOCK_BYTES // (4 * tc)), 8)
    return tr, tc


def _scalar_grid(grid, in_specs, out_specs):
    return pltpu.PrefetchScalarGridSpec(num_scalar_prefetch=1, grid=grid, in_specs=in_specs, out_specs=out_specs)


def _cast_place(w_stack, layer, chip, col_sharded, name):
    _, rows, width = w_stack.shape
    tr, tc = _elem_tiles(rows, width)
    nr, nc = rows // tr, width // tc

    def body(sc_ref, x_ref, o_ref):
        o_ref[...] = x_ref[...].astype(o_ref.dtype)

    if col_sharded:
        full, out_spec = (rows, width * N_CHIPS), pl.BlockSpec((tr, tc), lambda i, j, sc: (i, sc[0] * nc + j))
    else:
        full, out_spec = (rows * N_CHIPS, width), pl.BlockSpec((tr, tc), lambda i, j, sc: (sc[0] * nr + i, j))
    return pl.pallas_call(
        body, name=name,
        grid_spec=_scalar_grid((nr, nc), [pl.BlockSpec((None, tr, tc), lambda i, j, sc: (layer, i, j))], out_spec),
        out_shape=_sds(full, BF16), compiler_params=_params(("parallel", "parallel")))(chip, w_stack)


def _add_landed(acc, landed):
    return (acc + landed,)


def _slot_of_relation(rel):
    return jnp.where(rel == 2, 0, jnp.where(rel == 1, 1, 2))


def _sum_chips(pair_sum, landed, chip, col_sharded, name):
    _, rows, width = landed.shape
    tr, tc = _elem_tiles(rows, width)
    nr, nc = rows // tr, width // tc

    def body(chip_ref, own_ref, q_ref, o_ref):
        me = chip_ref[0]
        own = own_ref[...].astype(F32)
        acc = None
        for t in range(N_CHIPS):
            rel = me ^ t
            term = jnp.where(rel == 0, own, q_ref[_slot_of_relation(rel)].astype(F32))
            acc = term if acc is None else acc + term
        o_ref[...] = acc

    if col_sharded:
        own_spec = pl.BlockSpec((tr, tc), lambda i, j, c: (i, c[0] * nc + j))
    else:
        own_spec = pl.BlockSpec((tr, tc), lambda i, j, c: (c[0] * nr + i, j))
    return pl.pallas_call(
        body, name=name,
        grid_spec=_scalar_grid((nr, nc), [own_spec, pl.BlockSpec((N_CHIPS - 1, tr, tc), lambda i, j, c: (0, i, j))],
                               pl.BlockSpec((tr, tc), lambda i, j, c: (i, j))),
        out_shape=_sds((rows, width), F32), compiler_params=_params(("parallel", "parallel")))(chip, pair_sum, landed)


def _adamw_math(w, g, m, v):
    mn = ADAM_B1 * m + (1.0 - ADAM_B1) * g
    vn = ADAM_B2 * v + (1.0 - ADAM_B2) * jnp.square(g)
    m_hat = mn / (1.0 - ADAM_B1 ** ADAM_STEP)
    v_hat = vn / (1.0 - ADAM_B2 ** ADAM_STEP)
    return -ADAM_LR * (m_hat / (jnp.sqrt(v_hat) + ADAM_EPS) + ADAM_WD * w), mn, vn


def _adamw(w, g, m, v, name):
    shape = w.shape
    width = shape[-1]
    w2, g2, m2, v2 = (a.reshape(-1, width) for a in (w, g, m, v))
    rows = w2.shape[0]
    tr, tc = _elem_tiles(rows, width)

    def body(w_ref, g_ref, m_ref, v_ref, d_ref, nm_ref, nv_ref):
        d_ref[...], nm_ref[...], nv_ref[...] = _adamw_math(w_ref[...], g_ref[...], m_ref[...], v_ref[...])

    tile = pl.BlockSpec((tr, tc), lambda i, j: (i, j))
    outs = pl.pallas_call(body, name=name, grid=(rows // tr, width // tc), in_specs=[tile] * 4, out_specs=[tile] * 3,
                          out_shape=[_sds((rows, width), F32)] * 3, compiler_params=_params(("parallel", "parallel")))(w2, g2, m2, v2)
    return tuple(o.reshape(shape) for o in outs)


def _adamw_layer(w, m, v, g_mine, g_sibling, carried, layer, core, col_sharded, name):
    depth, rows, width = w.shape
    pr, pc = g_mine.shape
    tr, tc = _elem_tiles(pr, pc)
    n_half = pr // tr if col_sharded else pc // tc
    if carried is None:
        carried = tuple(lax.empty((depth, rows, width), F32) for _ in range(4))

    def body(sc_ref, w_ref, m_ref, v_ref, gm_ref, gs_ref, *rest):
        g_ref, d_ref, nm_ref, nv_ref = rest[-4:]
        pos = pl.program_id(0) if col_sharded else pl.program_id(1)
        gv = jnp.where(pos // n_half == sc_ref[0], gm_ref[...], gs_ref[...])
        g_ref[...] = gv
        d_ref[...], nm_ref[...], nv_ref[...] = _adamw_math(w_ref[...], gv, m_ref[...], v_ref[...])

    stacked = pl.BlockSpec((None, tr, tc), lambda i, j, sc: (layer, i, j))
    n_cols = width // tc

    def half_spec(mine):
        def index(i, j, sc):
            own = sc[0] if mine else 1 - sc[0]
            pos = i if col_sharded else j
            used, before = pos // n_half == own, pos // n_half < own
            within = jnp.where(used, pos % n_half, jnp.where(before, 0, n_half - 1))
            if col_sharded:
                return within, jnp.where(used, j, jnp.where(before, 0, n_cols - 1))
            return i, within
        return pl.BlockSpec((tr, tc), index)

    return pl.pallas_call(
        body, name=name,
        grid_spec=_scalar_grid((rows // tr, width // tc), [stacked] * 3 + [half_spec(True), half_spec(False)] + [ANY] * 4, [stacked] * 4),
        out_shape=[_sds((depth, rows, width), F32)] * 4, input_output_aliases={6: 0, 7: 1, 8: 2, 9: 3},
        compiler_params=_params(("parallel", "parallel")))(core, w, m, v, g_mine, g_sibling, *carried)


_SEM = pl.BlockSpec(memory_space=pltpu.SEMAPHORE)
_HBM = pl.BlockSpec(memory_space=pltpu.HBM)
_VMEM = pl.BlockSpec(memory_space=pltpu.VMEM)
_EFFECT = pltpu.SideEffectType.DATAFLOW_SIDE_EFFECTING


def _place():
    x, y, c = lax.axis_index("x"), lax.axis_index("y"), lax.axis_index("c")
    others = [(1 - x, y), (x, 1 - y), (1 - x, 1 - y)]
    return x, y, c, others


def _chip_index(px, py):
    return 2 * px + py


def _remote(src, dst, send_sems, recv_sems, k, to):
    return pltpu.make_async_remote_copy(src_ref=src, dst_ref=dst, send_sem=send_sems.at[k], recv_sem=recv_sems.at[k],
                                        device_id=to, device_id_type=MESH)


def _phase(name, bufs, waits, wait_fn, n_start, start_fn, deps):
    nb, nd = len(bufs), len(deps)

    def body(*refs):
        buf_refs = refs[:nb]
        pos = nb
        if waits is not None:
            wait_fn(buf_refs, refs[pos], refs[pos + 1])
            pos += 2
        pos += nd
        if n_start:
            start_fn(buf_refs, refs[pos], refs[pos + 1])
            pos += 2
        token = refs[pos + nb]
        token[...] = jnp.zeros_like(token)

    n_sem_out = 2 if n_start else 0
    if waits is None:
        bufs = [pltpu.with_memory_space_constraint(b, pltpu.HBM) for b in bufs]
    outs = pl.pallas_call(
        body, name=name,
        in_specs=[_HBM] * nb + ([_SEM] * 2 if waits is not None else []) + [ANY] * nd,
        out_specs=[_SEM] * n_sem_out + [_HBM] * nb + [_VMEM],
        out_shape=[pltpu.SemaphoreType.DMA((n_start,))] * n_sem_out + [pltpu.HBM(b.shape, b.dtype) for b in bufs] + [_sds((8, LANES), F32)],
        input_output_aliases={i: n_sem_out + i for i in range(nb)},
        compiler_params=pltpu.CompilerParams(has_side_effects=_EFFECT),
    )(*bufs, *(waits if waits is not None else ()), *deps)
    sems = tuple(outs[:2]) if n_start else None
    return sems, list(outs[n_sem_out:n_sem_out + nb]), outs[-1]


class _Chains:
    def __init__(self):
        self.active = []
        self.last = None
        self.dep = None
        self.pinned = []
        self.token = None
        self.at = None

    def phase(self, name, bufs, waits, wait_fn, n_start, start_fn):
        deps = [a for a in (self.last, self.dep) if a is not None] + self.pinned
        sems, thru, token = _phase(name, bufs, waits, wait_fn, n_start, start_fn, deps)
        self.last, self.dep, self.token, self.pinned = token, None, token, []
        return sems, thru

    def pin(self, result):
        self.pinned.append(result)

    def add(self, gen):
        self.active.append(gen)

    def tick(self, dep, at=None):
        self.at, self.token, self.dep = at, None, dep
        for gen in list(self.active):
            if next(gen, "done") == "done":
                self.active.remove(gen)
        return self.token


def _after(small, token):
    return small if token is None else small + token[0, 0]


def _shard_region(ref, col_sharded, chip, n_shard):
    start = pl.multiple_of(chip * n_shard, LANES if col_sharded else 8)
    if col_sharded:
        return ref.at[:, pl.ds(start, n_shard)]
    return ref.at[pl.ds(start, n_shard), :]


def _row_half(ref, half):
    n_rows = ref.shape[0]
    return ref.at[pl.ds(pl.multiple_of(half * (n_rows // 2), 8), n_rows // 2), :]


def _gather_chain(chains, tag, group, bufs, out, wait_for):
    n_w = len(group)
    n = 3 * n_w

    def region(refs, a, chip, half):
        col = BIG[group[a]][1]
        n_shard = refs[a].shape[1] // N_CHIPS if col else refs[a].shape[0] // N_CHIPS
        return _row_half(_shard_region(refs[a], col, chip, n_shard), half)

    def start_ici(refs, send, recv):
        x, y, c, others = _place()
        for a in range(n_w):
            mine = region(refs, a, _chip_index(x, y), c)
            for j, (ox, oy) in enumerate(others):
                _remote(mine, mine, send, recv, 3 * a + j, (ox, oy, c)).start()

    def wait_ici(refs, send, recv):
        x, y, c, others = _place()
        for a in range(n_w):
            for j, (ox, oy) in enumerate(others):
                landed = region(refs, a, _chip_index(ox, oy), c)
                cp = _remote(landed, landed, send, recv, 3 * a + j, (x, y, 1 - c))
                cp.wait_recv()
                cp.wait_send()

    def start_d2d(refs, send, recv):
        x, y, c, others = _place()
        for a in range(n_w):
            for j, (ox, oy) in enumerate(others):
                landed = region(refs, a, _chip_index(ox, oy), c)
                _remote(landed, landed, send, recv, 3 * a + j, (x, y, 1 - c)).start()

    def wait_d2d(refs, send, recv):
        x, y, c, others = _place()
        for a in range(n_w):
            for j, (ox, oy) in enumerate(others):
                theirs = region(refs, a, _chip_index(ox, oy), 1 - c)
                cp = _remote(theirs, theirs, send, recv, 3 * a + j, (x, y, 1 - c))
                cp.wait_recv()
                cp.wait_send()

    sems, bufs = chains.phase(f"gather_ici_start_{tag}", bufs, None, None, n, start_ici)
    yield
    while wait_for is not None and chains.at != wait_for:
        yield
    sems, bufs = chains.phase(f"gather_forward_{tag}", bufs, sems, wait_ici, n, start_d2d)
    yield
    _, bufs = chains.phase(f"gather_done_{tag}", bufs, sems, wait_d2d, 0, None)
    for a, buf in zip(group, bufs):
        out[BIG[a][0]] = buf


def _grad_chain(chains, layer, group, pairs, core, other_core, chip, w, mom, var, carried):
    n = len(group)
    tag = f"{layer}_{group[0]}"
    kinds = [BIG[a][1] for a in group]
    names = [BIG[a][0] for a in group]
    sibling_of = lambda x, y, c: (x, y, 1 - c)

    def pair_start(refs, send, recv):
        x, y, c, _ = _place()
        for i in range(n):
            _remote(refs[i], refs[n + i], send, recv, i, sibling_of(x, y, c)).start()

    def pair_wait(refs, send, recv):
        x, y, c, _ = _place()
        for i in range(n):
            cp = _remote(refs[i], refs[n + i], send, recv, i, sibling_of(x, y, c))
            cp.wait_recv()
            cp.wait_send()

    def piece(ref, col, chip_idx):
        return _shard_region(ref, col, chip_idx, ref.shape[1] // N_CHIPS if col else ref.shape[0] // N_CHIPS)

    def scatter_start(refs, send, recv):
        x, y, c, others = _place()
        for i in range(n):
            for j, (ox, oy) in enumerate(others):
                _remote(piece(refs[i], kinds[i], _chip_index(ox, oy)), refs[n + i].at[j], send, recv, 3 * i + j, (ox, oy, c)).start()

    def scatter_wait(refs, send, recv):
        x, y, c, others = _place()
        for i in range(n):
            for j, (ox, oy) in enumerate(others):
                cp = _remote(piece(refs[i], kinds[i], _chip_index(ox, oy)), refs[n + i].at[j], send, recv, 3 * i + j, (ox, oy, c))
                cp.wait_recv()
                cp.wait_send()

    def join_start(refs, send, recv):
        x, y, c, _ = _place()
        for i in range(n):
            _remote(refs[i], refs[n + i], send, recv, i, sibling_of(x, y, c)).start()

    def join_wait(refs, send, recv):
        x, y, c, _ = _place()
        for i in range(n):
            cp = _remote(refs[i], refs[n + i], send, recv, i, sibling_of(x, y, c))
            cp.wait_recv()
            cp.wait_send()

    sides = ["m" if col else "n" for col in kinds]
    sends = [_mm(a, b, "tn", [F32], name="mm_g_send_" + names[i], half=(other_core, sides[i])) for i, (a, b) in enumerate(pairs)]
    half_shapes = [g.shape for g in sends]
    lands = [lax.empty(s, F32) for s in half_shapes]
    sems, bufs = chains.phase(f"pair_start_{tag}", sends + lands, None, None, n, pair_start)
    yield
    _, bufs = chains.phase(f"pair_wait_{tag}", bufs, sems, pair_wait, 0, None)
    pair_sums = [_mm(a, b, "tn", [BF16], name="mm_g_keep_" + names[i], half=(core, sides[i]), epilogue=_add_landed, extras=(bufs[n + i],))
                 for i, (a, b) in enumerate(pairs)]
    piece_shapes = [(s[0], s[1] // N_CHIPS) if col else (s[0] // N_CHIPS, s[1]) for col, s in zip(kinds, half_shapes)]
    slots = [lax.empty((N_CHIPS - 1, *s), BF16) for s in piece_shapes]
    sems, bufs = chains.phase(f"scatter_start_{tag}", pair_sums + slots, None, None, 3 * n, scatter_start)
    yield
    yield
    _, bufs = chains.phase(f"scatter_wait_{tag}", bufs, sems, scatter_wait, 0, None)
    reduced = [_sum_chips(bufs[i], bufs[n + i], chip, kinds[i], "sum_chips_" + names[i]) for i in range(n)]
    theirs = [lax.empty(s, F32) for s in piece_shapes]
    sems, bufs = chains.phase(f"join_start_{tag}", reduced + theirs, None, None, n, join_start)
    yield
    _, bufs = chains.phase(f"join_wait_{tag}", bufs, sems, join_wait, 0, None)
    for i in range(n):
        carried[names[i]] = _adamw_layer(w[names[i]], mom[names[i]], var[names[i]], bufs[i], bufs[n + i], carried.get(names[i]),
                                         layer, core, kinds[i], "adamw_" + names[i])
        chains.pin(carried[names[i]][0])


def _allreduce_small(vec, name):
    rows = vec.shape[0]
    masks = [(dx, dy, dc) for dx in (0, 1) for dy in (0, 1) for dc in (0, 1)][1:]

    def body(v_ref, o_ref, gather_ref, send_sems, recv_sems):
        x, y, c, _ = _place()
        me = 4 * x + 2 * y + c
        gather_ref[me] = v_ref[...]
        copies = []
        for k, (dx, dy, dc) in enumerate(masks):
            peer = (x ^ dx, y ^ dy, c ^ dc)
            copies.append(_remote(v_ref, gather_ref.at[me], send_sems, recv_sems, k, peer))
        for cp in copies:
            cp.start()
        for k, (dx, dy, dc) in enumerate(masks):
            slot = gather_ref.at[4 * (x ^ dx) + 2 * (y ^ dy) + (c ^ dc)]
            _remote(slot, slot, send_sems, recv_sems, k, (x, y, c)).wait_recv()
        for cp in copies:
            cp.wait_send()
        acc = gather_ref[0]
        for dev in range(1, N_DEV):
            acc = acc + gather_ref[dev]
        o_ref[...] = acc

    return pl.pallas_call(
        body, name=name, in_specs=[_VMEM], out_specs=_VMEM, out_shape=_sds((rows, LANES), F32),
        scratch_shapes=[pltpu.VMEM((N_DEV, rows, LANES), F32), pltpu.SemaphoreType.DMA((N_DEV - 1,)), pltpu.SemaphoreType.DMA((N_DEV - 1,))],
        compiler_params=pltpu.CompilerParams(has_side_effects=True, vmem_limit_bytes=VMEM_LIMIT_BYTES),
    )(vec)


def _relu2(acc):
    r = jnp.maximum(acc, 0.0)
    return acc, r * r


def _relu2_bwd(acc, up):
    return (acc * (2.0 * jnp.maximum(up.astype(F32), 0.0)),)


def _layer_fwd(x, h, w, cols, tables, tick, last, target=None):
    cos_t, sin_t = tables
    z = _mm(h, w("w_in"), "nn", [BF16], name="mm_in")
    a = _conv_fwd(z, w("conv_w"), cols)
    q, k, v = _rope_fwd(z, _after(w("q_norm"), tick(a, 0)), w("k_norm"), cos_t, sin_t, cols)
    o, lse = _attn_fwd(q, k, v)
    tick(o, 1)
    tick(None, 2)
    y_a = _mm(a, w("w_out_conv"), "nn", [BF16], name="mm_out_conv")
    y_b = _mm(o, w("w_out_attn"), "nn", [BF16], name="mm_out_attn")
    mix = _gate_fwd(z, w("gate_bias"), y_a, y_b, cols)
    mixed = _mm(mix, w("w_merge"), "nn", [BF16], name="mm_merge")
    x1, h2 = _resid_norm(x, mixed, _after(w("norm_mix_post"), tick(mixed, 3)), w("norm_mlp_pre"))
    up, act = _mm(h2, w("w_up"), "nn", [BF16, BF16], name="mm_up", epilogue=_relu2)
    f = _mm(act, w("w_down"), "nn", [BF16], name="mm_down")
    kept = dict(x=x, h=h, z=z, a=a, q=q, k=k, v=v, o=o, lse=lse, y_a=y_a, y_b=y_b, mix=mix, mixed=mixed, x1=x1, h2=h2, up=up, act=act, f=f)
    g_post = _after(w("norm_mlp_post"), tick(f, 4))
    if last:
        return _resid_norm_loss(x1, f, g_post, target), kept
    return _resid_norm(x1, f, g_post, w("norm_next")), kept


def _layer_bwd(dx_out, w, kept, cols, tables, tick, emit):
    cos_t, sin_t = tables
    t = kept
    df, d_norm_mlp_post = _norm_bwd(t["f"], _after(w["norm_mlp_post"], tick(dx_out, 0)), dx_out, None, BF16, "norm_bwd_mlp_post")
    dup = _mm(df, w["w_down"], "nt", [BF16], name="mm_d_act", epilogue=_relu2_bwd, extras=(t["up"],))
    emit((4, 5), [(t["h2"], dup), (t["act"], df)])
    dh2 = _mm(dup, w["w_up"], "nt", [BF16], name="mm_d_h2")
    dx1, d_norm_mlp_pre = _norm_bwd(t["x1"], _after(w["norm_mlp_pre"], tick(dh2, 1)), dh2, dx_out, F32, "norm_bwd_mlp_pre")
    dmixed, d_norm_mix_post = _norm_bwd(t["mixed"], w["norm_mix_post"], dx1, None, BF16, "norm_bwd_mix_post")
    dmix = _mm(dmixed, w["w_merge"], "nt", [BF16], name="mm_d_mix")
    dy_a, dy_b, dz_ga, dz_gb, dbias_a, dbias_b = _gate_bwd(t["z"], _after(w["gate_bias"], tick(dmix, 2)), t["y_a"], t["y_b"], dmix, cols)
    da = _mm(dy_a, w["w_out_conv"], "nt", [BF16], name="mm_d_a")
    emit((1, 2, 3), [(t["a"], dy_a), (t["o"], dy_b), (t["mix"], dmixed)])
    do = _mm(dy_b, w["w_out_attn"], "nt", [BF16], name="mm_d_o")
    dz_cb, dz_cc, dz_ci, d_conv_w = _conv_bwd(t["z"], _after(w["conv_w"], tick(do, 3)), da, cols)
    dq, dk_t, dv_t = _attn_bwd(t["q"], t["k"], t["v"], do, t["lse"])
    dz_q, dz_k, dz_v, d_q_norm, d_k_norm = _rope_bwd(t["z"], _after(w["q_norm"], tick(dv_t, 4)), w["k_norm"], cos_t, sin_t, dq, dk_t, dv_t, cols)
    dz = jnp.concatenate([dz_cb, dz_cc, dz_ci, dz_q, dz_k, dz_v, dz_ga, dz_gb], axis=1)
    emit((0,), [(t["h"], dz)])
    tick(dz, 5)
    dh = _mm(dz, w["w_in"], "nt", [BF16], name="mm_d_h", tk=3328)
    dx_in, d_norm_mix_pre = _norm_bwd(t["x"], _after(w["norm_mix_pre"], tick(dh, 6)), dh, dx1, F32, "norm_bwd_mix_pre")
    small = dict(norm_mix_pre=d_norm_mix_pre, gate_bias=jnp.concatenate([dbias_a, dbias_b], axis=1), conv_w=d_conv_w, q_norm=d_q_norm,
                 k_norm=d_k_norm, norm_mix_post=d_norm_mix_post, norm_mlp_pre=d_norm_mlp_pre, norm_mlp_post=d_norm_mlp_post)
    return dx_in, small


SMALL = ("norm_mix_pre", "gate_bias", "conv_w", "q_norm", "k_norm", "norm_mix_post", "norm_mlp_pre", "norm_mlp_post")
WEIGHTS = ("norm_mix_pre", "w_in", "gate_bias", "conv_w", "q_norm", "k_norm", "w_out_conv", "w_out_attn", "w_merge",
           "norm_mix_post", "norm_mlp_pre", "w_up", "w_down", "norm_mlp_post")


def _pack(parts):
    flat = jnp.concatenate([a.reshape(-1) for a in parts])
    rows = -(-flat.shape[0] // LANES)
    pad = (-rows) % 8
    flat = jnp.pad(flat, (0, (rows + pad) * LANES - flat.shape[0]))
    return flat.reshape(rows + pad, LANES)


def _unpack(packed, shapes):
    flat = packed.reshape(-1)
    out, off = [], 0
    for shp in shapes:
        n = math.prod(shp)
        out.append(flat[off:off + n].reshape(shp))
        off += n
    return out


def kernel(x, norm_mix_pre, w_in, gate_bias, conv_w, q_norm, k_norm, w_out_conv, w_out_attn, w_merge, norm_mix_post, norm_mlp_pre, w_up, w_down, norm_mlp_post, loss_target, m_norm_mix_pre, m_w_in, m_gate_bias, m_conv_w, m_q_norm, m_k_norm, m_w_out_conv, m_w_out_attn, m_w_merge, m_norm_mix_post, m_norm_mlp_pre, m_w_up, m_w_down, m_norm_mlp_post, v_norm_mix_pre, v_w_in, v_gate_bias, v_conv_w, v_q_norm, v_k_norm, v_w_out_conv, v_w_out_attn, v_w_merge, v_norm_mix_post, v_norm_mlp_pre, v_w_up, v_w_down, v_norm_mlp_post):
    w = dict(norm_mix_pre=norm_mix_pre, w_in=w_in, gate_bias=gate_bias, conv_w=conv_w, q_norm=q_norm, k_norm=k_norm, w_out_conv=w_out_conv,
             w_out_attn=w_out_attn, w_merge=w_merge, norm_mix_post=norm_mix_post, norm_mlp_pre=norm_mlp_pre, w_up=w_up, w_down=w_down,
             norm_mlp_post=norm_mlp_post)
    mom = dict(norm_mix_pre=m_norm_mix_pre, w_in=m_w_in, gate_bias=m_gate_bias, conv_w=m_conv_w, q_norm=m_q_norm, k_norm=m_k_norm,
               w_out_conv=m_w_out_conv, w_out_attn=m_w_out_attn, w_merge=m_w_merge, norm_mix_post=m_norm_mix_post, norm_mlp_pre=m_norm_mlp_pre,
               w_up=m_w_up, w_down=m_w_down, norm_mlp_post=m_norm_mlp_post)
    var = dict(norm_mix_pre=v_norm_mix_pre, w_in=v_w_in, gate_bias=v_gate_bias, conv_w=v_conv_w, q_norm=v_q_norm, k_norm=v_k_norm,
               w_out_conv=v_w_out_conv, w_out_attn=v_w_out_attn, w_merge=v_w_merge, norm_mix_post=v_norm_mix_post, norm_mlp_pre=v_norm_mlp_pre,
               w_up=v_w_up, w_down=v_w_down, norm_mlp_post=v_norm_mlp_post)
    depth = w_in.shape[0]
    _, s, d = x.shape
    cols = _Cols(d)
    x0 = x.reshape(s, d)
    target = loss_target.reshape(s, d)
    tables = _rope_tables(s)
    chip = (2 * lax.axis_index("x") + lax.axis_index("y")).astype(jnp.int32)
    core = lax.axis_index("c").astype(jnp.int32)
    chip_vec, core_vec, other_core_vec = chip.reshape(1), core.reshape(1), (1 - core).reshape(1)

    n_conv = conv_w.shape[-1]
    placed = lax.dynamic_update_slice_in_dim(jnp.zeros((depth, conv_w.shape[1], n_conv * N_CHIPS), F32), conv_w, chip * n_conv, axis=2)
    conv_full = _unpack(_allreduce_small(_pack([jnp.where(core == 0, placed, 0.0)]), "gather_conv_w"), [placed.shape])[0]

    chains = _Chains()
    chains.pin(conv_full)
    full = [{} for _ in range(depth)]
    everything = tuple(range(N_BIG))
    gathers = []
    for l in range(depth):
        bufs = [_cast_place(w[name], l, chip_vec, col, "cast_place_" + name) for name, col in BIG]
        for group in ((0,), everything[1:]):
            if group[0] == 0:
                wait_for = ("fwd", l - 1, 3) if l else None
            else:
                wait_for = ("fwd", l, 0 if l else 1)
            gathers.append(_gather_chain(chains, f"{l}_{group[0]}", group, [bufs[a] for a in group], full[l], wait_for))
            next(gathers[-1])
    next(gathers[0])
    next(gathers[0], None)
    for g in gathers[1:]:
        chains.add(g)

    def layer_params(l):
        p = dict(full[l])
        p["conv_w"] = conv_full[l]
        p["gate_bias"] = gate_bias[l].reshape(1, -1)
        for name in ("norm_mix_pre", "q_norm", "k_norm", "norm_mix_post", "norm_mlp_pre", "norm_mlp_post"):
            p[name] = w[name][l].reshape(1, -1)
        if l + 1 < depth:
            p["norm_next"] = w["norm_mix_pre"][l + 1].reshape(1, -1)
        return p

    kept = []
    xl, h = x0, _norm_first(x0, norm_mix_pre[0].reshape(1, -1))
    for l in range(depth):
        last = l == depth - 1
        (xl, h), t = _layer_fwd(xl, h, lambda name, l=l: layer_params(l)[name], cols, tables,
                                lambda dep, k, l=l: chains.tick(dep, ("fwd", l, k)), last, target if last else None)
        kept.append(t)
    dx, loss_local = xl, h

    carried = {}
    small = [None] * depth
    for l in reversed(range(depth)):
        def emit(group, pairs, l=l):
            chains.add(_grad_chain(chains, l, group, pairs, core_vec, other_core_vec, chip_vec, w, mom, var, carried))

        dx, small[l] = _layer_bwd(dx, layer_params(l), kept[l], cols, tables, lambda dep, k: chains.tick(dep), emit)
    while chains.active:
        chains.tick(None)
    grads = {name: carried[name][0] for name, _ in BIG}
    delta = {name: carried[name][1] for name, _ in BIG}
    new_m = {name: carried[name][2] for name, _ in BIG}
    new_v = {name: carried[name][3] for name, _ in BIG}

    small_full_shapes = [(depth,) + small[0][name].shape for name in SMALL]
    packed = _pack([jnp.stack([small[l][name] for l in range(depth)]) for name in SMALL] + [jnp.broadcast_to(loss_local.reshape(1), (LANES,))])
    small_sum = _unpack(_allreduce_small(packed, "allreduce_small"), small_full_shapes + [(LANES,)])
    loss = small_sum[-1][0]
    for name, g in zip(SMALL, small_sum[:-1]):
        if name == "conv_w":
            g = lax.dynamic_slice_in_dim(g, chip * n_conv, n_conv, axis=2)
        grads[name] = g.reshape(w[name].shape)
    small_shapes = [w[name].shape for name in SMALL]
    packs = [_pack([src[name] for name in SMALL]) for src in (w, grads, mom, var)]
    for dst, out in zip((delta, new_m, new_v), _adamw(*packs, "adamw_small")):
        for name, val in zip(SMALL, _unpack(out, small_shapes)):
            dst[name] = val

    grad_x = dx.reshape(x.shape)
    return (loss, grad_x, *[grads[n] for n in WEIGHTS], *[delta[n] for n in WEIGHTS], *[new_m[n] for n in WEIGHTS], *[new_v[n] for n in WEIGHTS])
```

```python
import math

import jax
import jax.numpy as jnp
from jax import lax
from jax.experimental import pallas as pl
from jax.experimental.pallas import tpu as pltpu

F32 = jnp.float32
BF16 = jnp.bfloat16

HEAD_DIM = 128
GROUP = 4
GRID_W = 64
ROPE_THETA = 10000.0
RMS_EPS = 1e-6
ADAM_LR = 0.001
ADAM_B1 = 0.9
ADAM_B2 = 0.999
ADAM_EPS = 1e-08
ADAM_WD = 0.01
ADAM_STEP = 10

LANES = 128
N_CHIPS = 4
N_DEV = 8
VMEM_LIMIT_BYTES = 56 * 1024 * 1024
MESH = pl.DeviceIdType.MESH
ANY = pl.BlockSpec(memory_space=pl.ANY)


def _tile(dim, cap, mult):
    if dim <= cap:
        return dim
    t = (cap // mult) * mult
    while t >= mult:
        if dim % t == 0:
            return t
        t -= mult
    raise ValueError(f"no tile for {dim} under {cap} in multiples of {mult}")


def _params(sem=None):
    return pltpu.CompilerParams(dimension_semantics=sem, vmem_limit_bytes=VMEM_LIMIT_BYTES)


def _sds(shape, dtype):
    return jax.ShapeDtypeStruct(tuple(shape), dtype)


def _rstd(x):
    return lax.rsqrt(jnp.mean(x * x, axis=-1, keepdims=True) + RMS_EPS)


_DOT_DIMS = {"nn": ((1,), (0,)), "nt": ((1,), (1,)), "tn": ((0,), (0,))}


def _mm(a, b, mode, out_dtypes, *, name, epilogue=None, extras=(), tm=1024, tn=1024, tk=2048, half=None):
    if mode == "nn":
        (m, k), (k2, n) = a.shape, b.shape
    elif mode == "nt":
        (m, k), (n, k2) = a.shape, b.shape
    else:
        (k, m), (k2, n) = a.shape, b.shape
    assert k == k2, (a.shape, b.shape, mode)
    side = half[1] if half is not None else None
    if side == "m":
        m //= 2
    elif side == "n":
        n //= 2
    tm, tn, tk = _tile(m, tm, 8), _tile(n, tn, LANES), _tile(k, tk, LANES)
    nk = k // tk
    row = (lambda i, s: i + s[0][0] * (m // tm)) if side == "m" else (lambda i, s: i)
    col = (lambda j, s: j + s[0][0] * (n // tn)) if side == "n" else (lambda j, s: j)
    a_spec = pl.BlockSpec((tk, tm), lambda i, j, kk, *s: (kk, row(i, s))) if mode == "tn" else pl.BlockSpec((tm, tk), lambda i, j, kk, *s: (row(i, s), kk))
    b_spec = pl.BlockSpec((tn, tk), lambda i, j, kk, *s: (col(j, s), kk)) if mode == "nt" else pl.BlockSpec((tk, tn), lambda i, j, kk, *s: (kk, col(j, s)))
    tile_spec = pl.BlockSpec((tm, tn), lambda i, j, kk, *s: (i, j))
    n_extra, n_out = len(extras), len(out_dtypes)
    dims = (_DOT_DIMS[mode], ((), ()))

    def body(*refs):
        if half is not None:
            refs = refs[1:]
        a_ref, b_ref = refs[:2]
        extra_refs = refs[2:2 + n_extra]
        out_refs = refs[2 + n_extra:2 + n_extra + n_out]
        part = lax.dot_general(a_ref[...].astype(BF16), b_ref[...].astype(BF16), dims, preferred_element_type=F32)

        def finish(total):
            res = epilogue(total, *[e[...] for e in extra_refs]) if epilogue is not None else (total,)
            for o, r in zip(out_refs, res):
                o[...] = r.astype(o.dtype)

        if nk == 1:
            finish(part)
        else:
            acc = refs[-1]
            kk = pl.program_id(2)

            @pl.when(kk == 0)
            def _():
                acc[...] = part

            @pl.when(kk > 0)
            def _():
                acc[...] += part

            @pl.when(kk == nk - 1)
            def _():
                finish(acc[...])

    grid = (m // tm, n // tn, nk)
    in_specs, out_specs = [a_spec, b_spec] + [tile_spec] * n_extra, [tile_spec] * n_out
    scratch = [pltpu.VMEM((tm, tn), F32)] if nk > 1 else []
    if half is None:
        layout, lead = dict(grid=grid, in_specs=in_specs, out_specs=out_specs, scratch_shapes=scratch), ()
    else:
        layout = dict(grid_spec=pltpu.PrefetchScalarGridSpec(num_scalar_prefetch=1, grid=grid, in_specs=in_specs, out_specs=out_specs,
                                                             scratch_shapes=scratch))
        lead = (half[0],)
    outs = pl.pallas_call(body, name=name, out_shape=[_sds((m, n), d) for d in out_dtypes],
                          compiler_params=_params(("parallel", "parallel", "arbitrary")), **layout)(*lead, a, b, *extras)
    return outs if n_out > 1 else outs[0]


ROW_TILE = 512
GATE_ROW_TILE = 1024


def _norm_first(x, g):
    s, d = x.shape
    ts = _tile(s, ROW_TILE, 8)

    def body(x_ref, g_ref, h_ref):
        xv = x_ref[...]
        h_ref[...] = (xv * _rstd(xv) * g_ref[...]).astype(h_ref.dtype)

    row = pl.BlockSpec((ts, d), lambda i: (i, 0))
    vec = pl.BlockSpec((1, d), lambda i: (0, 0))
    return pl.pallas_call(body, name="norm_first", grid=(s // ts,), in_specs=[row, vec], out_specs=row,
                          out_shape=_sds((s, d), BF16), compiler_params=_params(("parallel",)))(x, g)


def _resid_norm(xres, y, g_post, g_next):
    s, d = xres.shape
    ts = _tile(s, ROW_TILE, 8)

    def body(x_ref, y_ref, gp_ref, gn_ref, xn_ref, hn_ref):
        yv = y_ref[...].astype(F32)
        xn = x_ref[...] + yv * _rstd(yv) * gp_ref[...]
        xn_ref[...] = xn
        hn_ref[...] = (xn * _rstd(xn) * gn_ref[...]).astype(hn_ref.dtype)

    row = pl.BlockSpec((ts, d), lambda i: (i, 0))
    vec = pl.BlockSpec((1, d), lambda i: (0, 0))
    return pl.pallas_call(body, name="resid_norm", grid=(s // ts,), in_specs=[row, row, vec, vec], out_specs=[row, row],
                          out_shape=[_sds((s, d), F32), _sds((s, d), BF16)], compiler_params=_params(("parallel",)))(xres, y, g_post, g_next)


def _resid_norm_loss(xres, y, g_post, target):
    s, d = xres.shape
    ts = _tile(s, ROW_TILE, 8)
    n_steps = s // ts

    def body(x_ref, y_ref, gp_ref, t_ref, dout_ref, loss_ref, acc_ref):
        i = pl.program_id(0)
        yv = y_ref[...].astype(F32)
        err = x_ref[...] + yv * _rstd(yv) * gp_ref[...] - t_ref[...]
        dout_ref[...] = err / d
        part = jnp.sum(err * err, axis=0, keepdims=True)

        @pl.when(i == 0)
        def _():
            acc_ref[...] = part

        @pl.when(i > 0)
        def _():
            acc_ref[...] += part

        @pl.when(i == n_steps - 1)
        def _():
            loss_ref[...] = 0.5 * jnp.sum(acc_ref[...], axis=1, keepdims=True) / d

    row = pl.BlockSpec((ts, d), lambda i: (i, 0))
    vec = pl.BlockSpec((1, d), lambda i: (0, 0))
    one = pl.BlockSpec((1, 1), lambda i: (0, 0))
    return pl.pallas_call(body, name="resid_norm_loss", grid=(n_steps,), in_specs=[row, row, vec, row], out_specs=[row, one],
                          out_shape=[_sds((s, d), F32), _sds((1, 1), F32)], scratch_shapes=[pltpu.VMEM((1, d), F32)],
                          compiler_params=_params(("arbitrary",)))(xres, y, g_post, target)


def _norm_bwd(xin, g, dout, dres, out_dtype, name):
    s, d = xin.shape
    ts = _tile(s, ROW_TILE, 8)
    has_res = dres is not None

    def body(*refs):
        x_ref, g_ref, do_ref = refs[:3]
        dx_ref, dg_ref = refs[-2:]
        i = pl.program_id(0)
        xv, dov = x_ref[...].astype(F32), do_ref[...].astype(F32)
        r = _rstd(xv)
        xhat = xv * r
        dg = jnp.sum(dov * xhat, axis=0, keepdims=True)
        dxh = dov * g_ref[...]
        dx = r * (dxh - xhat * jnp.mean(dxh * xhat, axis=-1, keepdims=True))
        if has_res:
            dx = dx + refs[3][...]
        dx_ref[...] = dx.astype(dx_ref.dtype)

        @pl.when(i == 0)
        def _():
            dg_ref[...] = dg

        @pl.when(i > 0)
        def _():
            dg_ref[...] += dg

    row = pl.BlockSpec((ts, d), lambda i: (i, 0))
    vec = pl.BlockSpec((1, d), lambda i: (0, 0))
    ops = [xin, g, dout] + ([dres] if has_res else [])
    return pl.pallas_call(body, name=name, grid=(s // ts,), in_specs=[row, vec, row] + ([row] if has_res else []),
                          out_specs=[row, vec], out_shape=[_sds((s, d), out_dtype), _sds((1, d), F32)],
                          compiler_params=_params(("arbitrary",)))(*ops)


class _Cols:
    def __init__(self, d):
        self.d = d
        self.kv = d // GROUP
        self.cb, self.cc, self.ci, self.q = 0, d, 2 * d, 3 * d
        self.k = 4 * d
        self.v = 4 * d + self.kv
        self.ga = 4 * d + 2 * self.kv
        self.gb = 5 * d + 2 * self.kv
        self.width = 6 * d + 2 * self.kv


CONV_COLS = 128


def _shift_rows(u, down):
    s = u.shape[0]
    rows = lax.broadcasted_iota(jnp.int32, u.shape, 0)
    if down:
        return jnp.where(rows == 0, 0.0, pltpu.roll(u, 1, 0))
    return jnp.where(rows == s - 1, 0.0, pltpu.roll(u, s - 1, 0))


def _conv_fwd(z, w, cols):
    s, d = z.shape[0], cols.d
    cw = CONV_COLS

    def body(cb_ref, cc_ref, ci_ref, w_ref, a_ref):
        u = cc_ref[...].astype(F32) * ci_ref[...].astype(F32)
        wv = w_ref[...]
        conv = wv[0:1] * _shift_rows(u, True) + wv[1:2] * u + wv[2:3] * _shift_rows(u, False)
        a_ref[...] = (cb_ref[...].astype(F32) * conv).astype(a_ref.dtype)

    def zspec(off):
        return pl.BlockSpec((s, cw), lambda j: (0, off // cw + j))

    return pl.pallas_call(body, name="conv_fwd", grid=(d // cw,),
                          in_specs=[zspec(cols.cb), zspec(cols.cc), zspec(cols.ci), pl.BlockSpec((3, cw), lambda j: (0, j))],
                          out_specs=pl.BlockSpec((s, cw), lambda j: (0, j)), out_shape=_sds((s, d), BF16),
                          compiler_params=_params(("parallel",)))(z, z, z, w)


def _conv_bwd(z, w, da, cols):
    s, d = z.shape[0], cols.d
    cw = CONV_COLS

    def body(cb_ref, cc_ref, ci_ref, w_ref, da_ref, dcb_ref, dcc_ref, dci_ref, dw_ref):
        cb, cc, ci, dav = cb_ref[...].astype(F32), cc_ref[...].astype(F32), ci_ref[...].astype(F32), da_ref[...].astype(F32)
        wv = w_ref[...]
        u = cc * ci
        um, up = _shift_rows(u, True), _shift_rows(u, False)
        conv = wv[0:1] * um + wv[1:2] * u + wv[2:3] * up
        dcb_ref[...] = (dav * conv).astype(dcb_ref.dtype)
        dconv = dav * cb
        dw_ref[0:1, :] = jnp.sum(dconv * um, axis=0, keepdims=True)
        dw_ref[1:2, :] = jnp.sum(dconv * u, axis=0, keepdims=True)
        dw_ref[2:3, :] = jnp.sum(dconv * up, axis=0, keepdims=True)
        du = wv[0:1] * _shift_rows(dconv, False) + wv[1:2] * dconv + wv[2:3] * _shift_rows(dconv, True)
        dcc_ref[...] = (du * ci).astype(dcc_ref.dtype)
        dci_ref[...] = (du * cc).astype(dci_ref.dtype)

    def zspec(off):
        return pl.BlockSpec((s, cw), lambda j: (0, off // cw + j))

    col = pl.BlockSpec((s, cw), lambda j: (0, j))
    wspec = pl.BlockSpec((3, cw), lambda j: (0, j))
    return pl.pallas_call(body, name="conv_bwd", grid=(d // cw,),
                          in_specs=[zspec(cols.cb), zspec(cols.cc), zspec(cols.ci), wspec, col],
                          out_specs=[col, col, col, wspec],
                          out_shape=[_sds((s, d), BF16)] * 3 + [_sds((3, d), F32)],
                          compiler_params=_params(("parallel",)))(z, z, z, w, da)


def _gate_fwd(z, bias, y_a, y_b, cols):
    s, d = y_a.shape
    ts, cw = _tile(s, GATE_ROW_TILE, 8), cols.kv
    nj = d // cw

    def body(ga_ref, gb_ref, ba_ref, bb_ref, ya_ref, yb_ref, o_ref):
        gate_a = jax.nn.sigmoid(ga_ref[...].astype(F32) + ba_ref[...])
        gate_b = jax.nn.sigmoid(gb_ref[...].astype(F32) + bb_ref[...])
        o_ref[...] = (gate_a * ya_ref[...].astype(F32) + gate_b * yb_ref[...].astype(F32)).astype(o_ref.dtype)

    tile = pl.BlockSpec((ts, cw), lambda i, j: (i, j))
    return pl.pallas_call(
        body, name="gate_fwd", grid=(s // ts, nj),
        in_specs=[pl.BlockSpec((ts, cw), lambda i, j: (i, cols.ga // cw + j)), pl.BlockSpec((ts, cw), lambda i, j: (i, cols.gb // cw + j)),
                  pl.BlockSpec((1, cw), lambda i, j: (0, j)), pl.BlockSpec((1, cw), lambda i, j: (0, nj + j)), tile, tile],
        out_specs=tile, out_shape=_sds((s, d), BF16), compiler_params=_params(("parallel", "parallel")))(z, z, bias, bias, y_a, y_b)


def _gate_bwd(z, bias, y_a, y_b, dmix, cols):
    s, d = y_a.shape
    ts, cw = _tile(s, GATE_ROW_TILE, 8), cols.kv
    nj = d // cw

    def body(ga_ref, gb_ref, ba_ref, bb_ref, ya_ref, yb_ref, dm_ref, dya_ref, dyb_ref, dga_ref, dgb_ref, dba_ref, dbb_ref):
        i = pl.program_id(1)
        gate_a = jax.nn.sigmoid(ga_ref[...].astype(F32) + ba_ref[...])
        gate_b = jax.nn.sigmoid(gb_ref[...].astype(F32) + bb_ref[...])
        dm = dm_ref[...].astype(F32)
        dya_ref[...] = (dm * gate_a).astype(dya_ref.dtype)
        dyb_ref[...] = (dm * gate_b).astype(dyb_ref.dtype)
        dga = dm * ya_ref[...].astype(F32) * (gate_a * (1.0 - gate_a))
        dgb = dm * yb_ref[...].astype(F32) * (gate_b * (1.0 - gate_b))
        dga_ref[...] = dga.astype(dga_ref.dtype)
        dgb_ref[...] = dgb.astype(dgb_ref.dtype)
        sa = jnp.sum(dga, axis=0, keepdims=True)
        sb = jnp.sum(dgb, axis=0, keepdims=True)

        @pl.when(i == 0)
        def _():
            dba_ref[...] = sa
            dbb_ref[...] = sb

        @pl.when(i > 0)
        def _():
            dba_ref[...] += sa
            dbb_ref[...] += sb

    tile = pl.BlockSpec((ts, cw), lambda j, i: (i, j))
    vec = pl.BlockSpec((1, cw), lambda j, i: (0, j))
    return pl.pallas_call(
        body, name="gate_bwd", grid=(nj, s // ts),
        in_specs=[pl.BlockSpec((ts, cw), lambda j, i: (i, cols.ga // cw + j)), pl.BlockSpec((ts, cw), lambda j, i: (i, cols.gb // cw + j)),
                  vec, pl.BlockSpec((1, cw), lambda j, i: (0, nj + j)), tile, tile, tile],
        out_specs=[tile, tile, tile, tile, vec, vec],
        out_shape=[_sds((s, d), BF16)] * 4 + [_sds((1, d), F32)] * 2,
        compiler_params=_params(("parallel", "arbitrary")))(z, z, bias, bias, y_a, y_b, dmix)


def _rope_tables(s):
    axis_dim = HEAD_DIM // 2
    n_freq = axis_dim // 2
    rows = s // GRID_W
    row_idx = jnp.repeat(jnp.arange(rows, dtype=jnp.int32), GRID_W)
    col_idx = jnp.tile(jnp.arange(GRID_W, dtype=jnp.int32), rows)
    inv_freq = ROPE_THETA ** (-jnp.arange(0, axis_dim, 2, dtype=F32) / axis_dim)
    ang = jnp.stack([row_idx.astype(F32)[:, None] * inv_freq, col_idx.astype(F32)[:, None] * inv_freq], axis=1)
    cos, sin = jnp.cos(ang), jnp.sin(ang)
    cos_t = jnp.stack([cos, cos], axis=2).reshape(s, HEAD_DIM)
    sin_t = jnp.stack([-sin, sin], axis=2).reshape(s, HEAD_DIM)
    return cos_t, sin_t


def _partner(x):
    n = x.shape[-1]
    lane = lax.broadcasted_iota(jnp.int32, x.shape, x.ndim - 1)
    quarter = HEAD_DIM // 4
    return jnp.where(lane % (2 * quarter) < quarter, pltpu.roll(x, n - quarter, x.ndim - 1), pltpu.roll(x, quarter, x.ndim - 1))


LOG2E = math.log2(math.e)
Q_SCALE = LOG2E / math.sqrt(HEAD_DIM)


def _rope_fwd(z, qn, kn, cos_t, sin_t, cols):
    s, d, kv = z.shape[0], cols.d, cols.kv
    ts = _tile(s, ROW_TILE, 8)
    scale = Q_SCALE

    def body(q_ref, k_ref, v_ref, qn_ref, kn_ref, c_ref, s_ref, qo_ref, ko_ref, vo_ref):
        c, sn = c_ref[...], s_ref[...]

        def head(xh, g):
            xn = xh * _rstd(xh) * g
            return xn * c + _partner(xn) * sn

        for h in range(d // HEAD_DIM):
            sl = slice(h * HEAD_DIM, (h + 1) * HEAD_DIM)
            qo_ref[:, sl] = (head(q_ref[:, sl].astype(F32), qn_ref[...]) * scale).astype(qo_ref.dtype)
        for h in range(kv // HEAD_DIM):
            sl = slice(h * HEAD_DIM, (h + 1) * HEAD_DIM)
            ko_ref[:, sl] = head(k_ref[:, sl].astype(F32), kn_ref[...]).astype(ko_ref.dtype)
        vo_ref[...] = v_ref[...].astype(vo_ref.dtype)

    vec = pl.BlockSpec((1, HEAD_DIM), lambda i: (0, 0))
    tab = pl.BlockSpec((ts, HEAD_DIM), lambda i: (i, 0))
    return pl.pallas_call(
        body, name="rope_fwd", grid=(s // ts,),
        in_specs=[pl.BlockSpec((ts, d), lambda i: (i, cols.q // d)), pl.BlockSpec((ts, kv), lambda i: (i, cols.k // kv)),
                  pl.BlockSpec((ts, kv), lambda i: (i, cols.v // kv)), vec, vec, tab, tab],
        out_specs=[pl.BlockSpec((ts, d), lambda i: (i, 0)), pl.BlockSpec((ts, kv), lambda i: (i, 0)), pl.BlockSpec((ts, kv), lambda i: (i, 0))],
        out_shape=[_sds((s, d), BF16), _sds((s, kv), BF16), _sds((s, kv), BF16)],
        compiler_params=_params(("parallel",)))(z, z, z, qn, kn, cos_t, sin_t)


def _rope_bwd(z, qn, kn, cos_t, sin_t, dq, dk_t, dv_t, cols):
    s, d, kv = z.shape[0], cols.d, cols.kv
    ts = _tile(s, ROW_TILE, 8)
    scale = 1.0 / math.sqrt(HEAD_DIM)

    def body(q_ref, k_ref, qn_ref, kn_ref, c_ref, s_ref, dq_ref, dkt_ref, dvt_ref, dzq_ref, dzk_ref, dzv_ref, dqn_ref, dkn_ref):
        i = pl.program_id(0)
        c, sn = c_ref[...], s_ref[...]
        dk_all = dkt_ref[...].T * (1.0 / LOG2E)

        def head_bwd(xh, g, drot):
            dxn = drot * c + _partner(drot * sn)
            r = _rstd(xh)
            xhat = xh * r
            dgain = jnp.sum(dxn * xhat, axis=0, keepdims=True)
            dxh = dxn * g
            return r * (dxh - xhat * jnp.mean(dxh * xhat, axis=-1, keepdims=True)), dgain

        dqn = jnp.zeros((1, HEAD_DIM), F32)
        for h in range(d // HEAD_DIM):
            sl = slice(h * HEAD_DIM, (h + 1) * HEAD_DIM)
            dx, dg = head_bwd(q_ref[:, sl].astype(F32), qn_ref[...], dq_ref[:, sl].astype(F32) * scale)
            dzq_ref[:, sl] = dx.astype(dzq_ref.dtype)
            dqn = dqn + dg
        dkn = jnp.zeros((1, HEAD_DIM), F32)
        for h in range(kv // HEAD_DIM):
            sl = slice(h * HEAD_DIM, (h + 1) * HEAD_DIM)
            dx, dg = head_bwd(k_ref[:, sl].astype(F32), kn_ref[...], dk_all[:, sl])
            dzk_ref[:, sl] = dx.astype(dzk_ref.dtype)
            dkn = dkn + dg
        dzv_ref[...] = dvt_ref[...].T.astype(dzv_ref.dtype)

        @pl.when(i == 0)
        def _():
            dqn_ref[...] = dqn
            dkn_ref[...] = dkn

        @pl.when(i > 0)
        def _():
            dqn_ref[...] += dqn
            dkn_ref[...] += dkn

    vec = pl.BlockSpec((1, HEAD_DIM), lambda i: (0, 0))
    tab = pl.BlockSpec((ts, HEAD_DIM), lambda i: (i, 0))
    qrow = pl.BlockSpec((ts, d), lambda i: (i, 0))
    krow = pl.BlockSpec((ts, kv), lambda i: (i, 0))
    kcol = pl.BlockSpec((kv, ts), lambda i: (0, i))
    return pl.pallas_call(
        body, name="rope_bwd", grid=(s // ts,),
        in_specs=[pl.BlockSpec((ts, d), lambda i: (i, cols.q // d)), pl.BlockSpec((ts, kv), lambda i: (i, cols.k // kv)),
                  vec, vec, tab, tab, qrow, kcol, kcol],
        out_specs=[qrow, krow, krow, vec, vec],
        out_shape=[_sds((s, d), BF16), _sds((s, kv), BF16), _sds((s, kv), BF16), _sds((1, HEAD_DIM), F32), _sds((1, HEAD_DIM), F32)],
        compiler_params=_params(("arbitrary",)))(z, z, qn, kn, cos_t, sin_t, dq, dk_t, dv_t)


Q_TILE = 256
_NT = (((1,), (1,)), ((), ()))
_NN = (((1,), (0,)), ((), ()))


def _attn_fwd(q, k, v):
    s, d = q.shape
    kvh = k.shape[1] // HEAD_DIM
    tq = _tile(s, Q_TILE, LANES)
    gw = GROUP * HEAD_DIM

    def body(q_ref, k_ref, v_ref, o_ref, lse_ref):
        kk, vv = k_ref[...], v_ref[...]
        for g in range(GROUP):
            sl = slice(g * HEAD_DIM, (g + 1) * HEAD_DIM)
            sc = lax.dot_general(q_ref[:, sl], kk, _NT, preferred_element_type=F32)
            mx = jnp.max(sc, axis=-1, keepdims=True)
            p = jnp.exp2(sc - mx)
            l = jnp.sum(p, axis=-1, keepdims=True)
            o = lax.dot_general(p.astype(BF16), vv, _NN, preferred_element_type=F32) * (1.0 / l)
            o_ref[:, sl] = o.astype(o_ref.dtype)
            lse_ref[:, g:g + 1] = mx + jnp.log(l) * LOG2E

    return pl.pallas_call(
        body, name="attn_fwd", grid=(kvh, s // tq),
        in_specs=[pl.BlockSpec((tq, gw), lambda j, i: (i, j)), pl.BlockSpec((s, HEAD_DIM), lambda j, i: (0, j)), pl.BlockSpec((s, HEAD_DIM), lambda j, i: (0, j))],
        out_specs=[pl.BlockSpec((tq, gw), lambda j, i: (i, j)), pl.BlockSpec((None, tq, GROUP), lambda j, i: (j, i, 0))],
        out_shape=[_sds((s, d), BF16), _sds((kvh, s, GROUP), F32)],
        compiler_params=_params(("parallel", "parallel")))(q, k, v)


_TN = (((0,), (0,)), ((), ()))


def _attn_bwd(q, k, v, do, lse):
    s, d = q.shape
    kv = k.shape[1]
    kvh = kv // HEAD_DIM
    tq = _tile(s, Q_TILE, LANES)
    gw = GROUP * HEAD_DIM

    def body(q_ref, k_ref, v_ref, do_ref, lse_ref, dq_ref, dkt_ref, dvt_ref):
        i = pl.program_id(1)
        kk, vv = k_ref[...], v_ref[...]

        @pl.when(i == 0)
        def _():
            dkt_ref[...] = jnp.zeros_like(dkt_ref)
            dvt_ref[...] = jnp.zeros_like(dvt_ref)

        for g in range(GROUP):
            sl = slice(g * HEAD_DIM, (g + 1) * HEAD_DIM)
            qg, dog = q_ref[:, sl], do_ref[:, sl]
            sc = lax.dot_general(qg, kk, _NT, preferred_element_type=F32)
            p = jnp.exp2(sc - lse_ref[:, g:g + 1])
            dp = lax.dot_general(dog, vv, _NT, preferred_element_type=F32)
            delta = jnp.sum(p * dp, axis=-1, keepdims=True)
            ds = (p * (dp - delta)).astype(BF16)
            dq_ref[:, sl] = lax.dot_general(ds, kk, _NN, preferred_element_type=F32).astype(dq_ref.dtype)
            dkt_ref[...] += lax.dot_general(qg, ds, _TN, preferred_element_type=F32)
            dvt_ref[...] += lax.dot_general(dog, p.astype(BF16), _TN, preferred_element_type=F32)

    qspec = pl.BlockSpec((tq, gw), lambda j, i: (i, j))
    kspec = pl.BlockSpec((s, HEAD_DIM), lambda j, i: (0, j))
    stat = pl.BlockSpec((None, tq, GROUP), lambda j, i: (j, i, 0))
    tspec = pl.BlockSpec((HEAD_DIM, s), lambda j, i: (j, 0))
    return pl.pallas_call(
        body, name="attn_bwd", grid=(kvh, s // tq),
        in_specs=[qspec, kspec, kspec, qspec, stat], out_specs=[qspec, tspec, tspec],
        out_shape=[_sds((s, d), BF16), _sds((kv, s), F32), _sds((kv, s), F32)],
        compiler_params=_params(("parallel", "arbitrary")))(q, k, v, do, lse)


ELEM_BLOCK_BYTES = 2 << 20

BIG = (("w_in", True), ("w_out_conv", False), ("w_out_attn", False), ("w_merge", False), ("w_up", True), ("w_down", False))
N_BIG = len(BIG)


def _elem_tiles(rows, width):
    tc = _tile(width, 2048, LANES)
    tr = _tile(rows, max(8, ELEM_BLOCK_BYTES // (4 * tc)), 8)
    return tr, tc


def _scalar_grid(grid, in_specs, out_specs):
    return pltpu.PrefetchScalarGridSpec(num_scalar_prefetch=1, grid=grid, in_specs=in_specs, out_specs=out_specs)


def _cast_place(w_stack, layer, chip, col_sharded, name):
    _, rows, width = w_stack.shape
    tr, tc = _elem_tiles(rows, width)
    nr, nc = rows // tr, width // tc

    def body(sc_ref, x_ref, o_ref):
        o_ref[...] = x_ref[...].astype(o_ref.dtype)

    if col_sharded:
        full, out_spec = (rows, width * N_CHIPS), pl.BlockSpec((tr, tc), lambda i, j, sc: (i, sc[0] * nc + j))
    else:
        full, out_spec = (rows * N_CHIPS, width), pl.BlockSpec((tr, tc), lambda i, j, sc: (sc[0] * nr + i, j))
    return pl.pallas_call(
        body, name=name,
        grid_spec=_scalar_grid((nr, nc), [pl.BlockSpec((None, tr, tc), lambda i, j, sc: (layer, i, j))], out_spec),
        out_shape=_sds(full, BF16), compiler_params=_params(("parallel", "parallel")))(chip, w_stack)


def _add_landed(acc, landed):
    return (acc + landed,)


def _slot_of_relation(rel):
    return jnp.where(rel == 2, 0, jnp.where(rel == 1, 1, 2))


def _sum_chips(pair_sum, landed, chip, col_sharded, name):
    _, rows, width = landed.shape
    tr, tc = _elem_tiles(rows, width)
    nr, nc = rows // tr, width // tc

    def body(chip_ref, own_ref, q_ref, o_ref):
        me = chip_ref[0]
        own = own_ref[...].astype(F32)
        acc = None
        for t in range(N_CHIPS):
            rel = me ^ t
            term = jnp.where(rel == 0, own, q_ref[_slot_of_relation(rel)].astype(F32))
            acc = term if acc is None else acc + term
        o_ref[...] = acc

    if col_sharded:
        own_spec = pl.BlockSpec((tr, tc), lambda i, j, c: (i, c[0] * nc + j))
    else:
        own_spec = pl.BlockSpec((tr, tc), lambda i, j, c: (c[0] * nr + i, j))
    return pl.pallas_call(
        body, name=name,
        grid_spec=_scalar_grid((nr, nc), [own_spec, pl.BlockSpec((N_CHIPS - 1, tr, tc), lambda i, j, c: (0, i, j))],
                               pl.BlockSpec((tr, tc), lambda i, j, c: (i, j))),
        out_shape=_sds((rows, width), F32), compiler_params=_params(("parallel", "parallel")))(chip, pair_sum, landed)


def _adamw_math(w, g, m, v):
    mn = ADAM_B1 * m + (1.0 - ADAM_B1) * g
    vn = ADAM_B2 * v + (1.0 - ADAM_B2) * jnp.square(g)
    m_hat = mn / (1.0 - ADAM_B1 ** ADAM_STEP)
    v_hat = vn / (1.0 - ADAM_B2 ** ADAM_STEP)
    return -ADAM_LR * (m_hat / (jnp.sqrt(v_hat) + ADAM_EPS) + ADAM_WD * w), mn, vn


def _adamw(w, g, m, v, name):
    shape = w.shape
    width = shape[-1]
    w2, g2, m2, v2 = (a.reshape(-1, width) for a in (w, g, m, v))
    rows = w2.shape[0]
    tr, tc = _elem_tiles(rows, width)

    def body(w_ref, g_ref, m_ref, v_ref, d_ref, nm_ref, nv_ref):
        d_ref[...], nm_ref[...], nv_ref[...] = _adamw_math(w_ref[...], g_ref[...], m_ref[...], v_ref[...])

    tile = pl.BlockSpec((tr, tc), lambda i, j: (i, j))
    outs = pl.pallas_call(body, name=name, grid=(rows // tr, width // tc), in_specs=[tile] * 4, out_specs=[tile] * 3,
                          out_shape=[_sds((rows, width), F32)] * 3, compiler_params=_params(("parallel", "parallel")))(w2, g2, m2, v2)
    return tuple(o.reshape(shape) for o in outs)


def _adamw_layer(w, m, v, g_mine, g_sibling, carried, layer, core, col_sharded, name):
    depth, rows, width = w.shape
    pr, pc = g_mine.shape
    tr, tc = _elem_tiles(pr, pc)
    n_half = pr // tr if col_sharded else pc // tc
    if carried is None:
        carried = tuple(lax.empty((depth, rows, width), F32) for _ in range(4))

    def body(sc_ref, w_ref, m_ref, v_ref, gm_ref, gs_ref, *rest):
        g_ref, d_ref, nm_ref, nv_ref = rest[-4:]
        pos = pl.program_id(0) if col_sharded else pl.program_id(1)
        gv = jnp.where(pos // n_half == sc_ref[0], gm_ref[...], gs_ref[...])
        g_ref[...] = gv
        d_ref[...], nm_ref[...], nv_ref[...] = _adamw_math(w_ref[...], gv, m_ref[...], v_ref[...])

    stacked = pl.BlockSpec((None, tr, tc), lambda i, j, sc: (layer, i, j))
    n_cols = width // tc

    def half_spec(mine):
        def index(i, j, sc):
            own = sc[0] if mine else 1 - sc[0]
            pos = i if col_sharded else j
            used, before = pos // n_half == own, pos // n_half < own
            within = jnp.where(used, pos % n_half, jnp.where(before, 0, n_half - 1))
            if col_sharded:
                return within, jnp.where(used, j, jnp.where(before, 0, n_cols - 1))
            return i, within
        return pl.BlockSpec((tr, tc), index)

    return pl.pallas_call(
        body, name=name,
        grid_spec=_scalar_grid((rows // tr, width // tc), [stacked] * 3 + [half_spec(True), half_spec(False)] + [ANY] * 4, [stacked] * 4),
        out_shape=[_sds((depth, rows, width), F32)] * 4, input_output_aliases={6: 0, 7: 1, 8: 2, 9: 3},
        compiler_params=_params(("parallel", "parallel")))(core, w, m, v, g_mine, g_sibling, *carried)


_SEM = pl.BlockSpec(memory_space=pltpu.SEMAPHORE)
_HBM = pl.BlockSpec(memory_space=pltpu.HBM)
_VMEM = pl.BlockSpec(memory_space=pltpu.VMEM)
_EFFECT = pltpu.SideEffectType.DATAFLOW_SIDE_EFFECTING


def _place():
    x, y, c = lax.axis_index("x"), lax.axis_index("y"), lax.axis_index("c")
    others = [(1 - x, y), (x, 1 - y), (1 - x, 1 - y)]
    return x, y, c, others


def _chip_index(px, py):
    return 2 * px + py


def _remote(src, dst, send_sems, recv_sems, k, to):
    return pltpu.make_async_remote_copy(src_ref=src, dst_ref=dst, send_sem=send_sems.at[k], recv_sem=recv_sems.at[k],
                                        device_id=to, device_id_type=MESH)


def _phase(name, bufs, waits, wait_fn, n_start, start_fn, deps):
    nb, nd = len(bufs), len(deps)

    def body(*refs):
        buf_refs = refs[:nb]
        pos = nb
        if waits is not None:
            wait_fn(buf_refs, refs[pos], refs[pos + 1])
            pos += 2
        pos += nd
        if n_start:
            start_fn(buf_refs, refs[pos], refs[pos + 1])
            pos += 2
        token = refs[pos + nb]
        token[...] = jnp.zeros_like(token)

    n_sem_out = 2 if n_start else 0
    if waits is None:
        bufs = [pltpu.with_memory_space_constraint(b, pltpu.HBM) for b in bufs]
    outs = pl.pallas_call(
        body, name=name,
        in_specs=[_HBM] * nb + ([_SEM] * 2 if waits is not None else []) + [ANY] * nd,
        out_specs=[_SEM] * n_sem_out + [_HBM] * nb + [_VMEM],
        out_shape=[pltpu.SemaphoreType.DMA((n_start,))] * n_sem_out + [pltpu.HBM(b.shape, b.dtype) for b in bufs] + [_sds((8, LANES), F32)],
        input_output_aliases={i: n_sem_out + i for i in range(nb)},
        compiler_params=pltpu.CompilerParams(has_side_effects=_EFFECT),
    )(*bufs, *(waits if waits is not None else ()), *deps)
    sems = tuple(outs[:2]) if n_start else None
    return sems, list(outs[n_sem_out:n_sem_out + nb]), outs[-1]


class _Chains:
    def __init__(self):
        self.active = []
        self.last = None
        self.dep = None
        self.pinned = []
        self.token = None
        self.at = None

    def phase(self, name, bufs, waits, wait_fn, n_start, start_fn):
        deps = [a for a in (self.last, self.dep) if a is not None] + self.pinned
        sems, thru, token = _phase(name, bufs, waits, wait_fn, n_start, start_fn, deps)
        self.last, self.dep, self.token, self.pinned = token, None, token, []
        return sems, thru

    def pin(self, result):
        self.pinned.append(result)

    def add(self, gen):
        self.active.append(gen)

    def tick(self, dep, at=None):
        self.at, self.token, self.dep = at, None, dep
        for gen in list(self.active):
            if next(gen, "done") == "done":
                self.active.remove(gen)
        return self.token


def _after(small, token):
    return small if token is None else small + token[0, 0]


def _shard_region(ref, col_sharded, chip, n_shard):
    start = pl.multiple_of(chip * n_shard, LANES if col_sharded else 8)
    if col_sharded:
        return ref.at[:, pl.ds(start, n_shard)]
    return ref.at[pl.ds(start, n_shard), :]


def _row_half(ref, half):
    n_rows = ref.shape[0]
    return ref.at[pl.ds(pl.multiple_of(half * (n_rows // 2), 8), n_rows // 2), :]


def _gather_chain(chains, tag, group, bufs, out, wait_for):
    n_w = len(group)
    n = 3 * n_w

    def region(refs, a, chip, half):
        col = BIG[group[a]][1]
        n_shard = refs[a].shape[1] // N_CHIPS if col else refs[a].shape[0] // N_CHIPS
        return _row_half(_shard_region(refs[a], col, chip, n_shard), half)

    def start_ici(refs, send, recv):
        x, y, c, others = _place()
        for a in range(n_w):
            mine = region(refs, a, _chip_index(x, y), c)
            for j, (ox, oy) in enumerate(others):
                _remote(mine, mine, send, recv, 3 * a + j, (ox, oy, c)).start()

    def wait_ici(refs, send, recv):
        x, y, c, others = _place()
        for a in range(n_w):
            for j, (ox, oy) in enumerate(others):
                landed = region(refs, a, _chip_index(ox, oy), c)
                cp = _remote(landed, landed, send, recv, 3 * a + j, (x, y, 1 - c))
                cp.wait_recv()
                cp.wait_send()

    def start_d2d(refs, send, recv):
        x, y, c, others = _place()
        for a in range(n_w):
            for j, (ox, oy) in enumerate(others):
                landed = region(refs, a, _chip_index(ox, oy), c)
                _remote(landed, landed, send, recv, 3 * a + j, (x, y, 1 - c)).start()

    def wait_d2d(refs, send, recv):
        x, y, c, others = _place()
        for a in range(n_w):
            for j, (ox, oy) in enumerate(others):
                theirs = region(refs, a, _chip_index(ox, oy), 1 - c)
                cp = _remote(theirs, theirs, send, recv, 3 * a + j, (x, y, 1 - c))
                cp.wait_recv()
                cp.wait_send()

    sems, bufs = chains.phase(f"gather_ici_start_{tag}", bufs, None, None, n, start_ici)
    yield
    while wait_for is not None and chains.at != wait_for:
        yield
    sems, bufs = chains.phase(f"gather_forward_{tag}", bufs, sems, wait_ici, n, start_d2d)
    yield
    _, bufs = chains.phase(f"gather_done_{tag}", bufs, sems, wait_d2d, 0, None)
    for a, buf in zip(group, bufs):
        out[BIG[a][0]] = buf


def _grad_chain(chains, layer, group, pairs, core, other_core, chip, w, mom, var, carried):
    n = len(group)
    tag = f"{layer}_{group[0]}"
    kinds = [BIG[a][1] for a in group]
    names = [BIG[a][0] for a in group]
    sibling_of = lambda x, y, c: (x, y, 1 - c)

    def pair_start(refs, send, recv):
        x, y, c, _ = _place()
        for i in range(n):
            _remote(refs[i], refs[n + i], send, recv, i, sibling_of(x, y, c)).start()

    def pair_wait(refs, send, recv):
        x, y, c, _ = _place()
        for i in range(n):
            cp = _remote(refs[i], refs[n + i], send, recv, i, sibling_of(x, y, c))
            cp.wait_recv()
            cp.wait_send()

    def piece(ref, col, chip_idx):
        return _shard_region(ref, col, chip_idx, ref.shape[1] // N_CHIPS if col else ref.shape[0] // N_CHIPS)

    def scatter_start(refs, send, recv):
        x, y, c, others = _place()
        for i in range(n):
            for j, (ox, oy) in enumerate(others):
                _remote(piece(refs[i], kinds[i], _chip_index(ox, oy)), refs[n + i].at[j], send, recv, 3 * i + j, (ox, oy, c)).start()

    def scatter_wait(refs, send, recv):
        x, y, c, others = _place()
        for i in range(n):
            for j, (ox, oy) in enumerate(others):
                cp = _remote(piece(refs[i], kinds[i], _chip_index(ox, oy)), refs[n + i].at[j], send, recv, 3 * i + j, (ox, oy, c))
                cp.wait_recv()
                cp.wait_send()

    def join_start(refs, send, recv):
        x, y, c, _ = _place()
        for i in range(n):
            _remote(refs[i], refs[n + i], send, recv, i, sibling_of(x, y, c)).start()

    def join_wait(refs, send, recv):
        x, y, c, _ = _place()
        for i in range(n):
            cp = _remote(refs[i], refs[n + i], send, recv, i, sibling_of(x, y, c))
            cp.wait_recv()
            cp.wait_send()

    sides = ["m" if col else "n" for col in kinds]
    whole = [dict(tk=pairs[i][0].shape[0], tn=512) if sides[i] == "m" else {} for i in range(n)]
    sends = [_mm(a, b, "tn", [F32], name="mm_g_send_" + names[i], half=(other_core, sides[i]), **whole[i]) for i, (a, b) in enumerate(pairs)]
    half_shapes = [g.shape for g in sends]
    lands = [lax.empty(s, F32) for s in half_shapes]
    sems, bufs = chains.phase(f"pair_start_{tag}", sends + lands, None, None, n, pair_start)
    yield
    _, bufs = chains.phase(f"pair_wait_{tag}", bufs, sems, pair_wait, 0, None)
    pair_sums = [_mm(a, b, "tn", [BF16], name="mm_g_keep_" + names[i], half=(core, sides[i]), epilogue=_add_landed, extras=(bufs[n + i],), **whole[i])
                 for i, (a, b) in enumerate(pairs)]
    piece_shapes = [(s[0], s[1] // N_CHIPS) if col else (s[0] // N_CHIPS, s[1]) for col, s in zip(kinds, half_shapes)]
    slots = [lax.empty((N_CHIPS - 1, *s), BF16) for s in piece_shapes]
    sems, bufs = chains.phase(f"scatter_start_{tag}", pair_sums + slots, None, None, 3 * n, scatter_start)
    yield
    yield
    _, bufs = chains.phase(f"scatter_wait_{tag}", bufs, sems, scatter_wait, 0, None)
    reduced = [_sum_chips(bufs[i], bufs[n + i], chip, kinds[i], "sum_chips_" + names[i]) for i in range(n)]
    theirs = [lax.empty(s, F32) for s in piece_shapes]
    sems, bufs = chains.phase(f"join_start_{tag}", reduced + theirs, None, None, n, join_start)
    yield
    _, bufs = chains.phase(f"join_wait_{tag}", bufs, sems, join_wait, 0, None)
    for i in range(n):
        carried[names[i]] = _adamw_layer(w[names[i]], mom[names[i]], var[names[i]], bufs[i], bufs[n + i], carried.get(names[i]),
                                         layer, core, kinds[i], "adamw_" + names[i])
        chains.pin(carried[names[i]][0])


def _allreduce_small(vec, name):
    rows = vec.shape[0]
    masks = [(dx, dy, dc) for dx in (0, 1) for dy in (0, 1) for dc in (0, 1)][1:]

    def body(v_ref, o_ref, gather_ref, send_sems, recv_sems):
        x, y, c, _ = _place()
        me = 4 * x + 2 * y + c
        gather_ref[me] = v_ref[...]
        copies = []
        for k, (dx, dy, dc) in enumerate(masks):
            peer = (x ^ dx, y ^ dy, c ^ dc)
            copies.append(_remote(v_ref, gather_ref.at[me], send_sems, recv_sems, k, peer))
        for cp in copies:
            cp.start()
        for k, (dx, dy, dc) in enumerate(masks):
            slot = gather_ref.at[4 * (x ^ dx) + 2 * (y ^ dy) + (c ^ dc)]
            _remote(slot, slot, send_sems, recv_sems, k, (x, y, c)).wait_recv()
        for cp in copies:
            cp.wait_send()
        acc = gather_ref[0]
        for dev in range(1, N_DEV):
            acc = acc + gather_ref[dev]
        o_ref[...] = acc

    return pl.pallas_call(
        body, name=name, in_specs=[_VMEM], out_specs=_VMEM, out_shape=_sds((rows, LANES), F32),
        scratch_shapes=[pltpu.VMEM((N_DEV, rows, LANES), F32), pltpu.SemaphoreType.DMA((N_DEV - 1,)), pltpu.SemaphoreType.DMA((N_DEV - 1,))],
        compiler_params=pltpu.CompilerParams(has_side_effects=True, vmem_limit_bytes=VMEM_LIMIT_BYTES),
    )(vec)


def _relu2(acc):
    r = jnp.maximum(acc, 0.0)
    return acc, r * r


def _relu2_bwd(acc, up):
    return (acc * (2.0 * jnp.maximum(up.astype(F32), 0.0)),)


def _layer_fwd(x, h, w, cols, tables, tick, last, target=None):
    cos_t, sin_t = tables
    z = _mm(h, w("w_in"), "nn", [BF16], name="mm_in")
    a = _conv_fwd(z, w("conv_w"), cols)
    q, k, v = _rope_fwd(z, _after(w("q_norm"), tick(a, 0)), w("k_norm"), cos_t, sin_t, cols)
    o, lse = _attn_fwd(q, k, v)
    tick(o, 1)
    tick(None, 2)
    y_a = _mm(a, w("w_out_conv"), "nn", [BF16], name="mm_out_conv")
    y_b = _mm(o, w("w_out_attn"), "nn", [BF16], name="mm_out_attn")
    mix = _gate_fwd(z, w("gate_bias"), y_a, y_b, cols)
    mixed = _mm(mix, w("w_merge"), "nn", [BF16], name="mm_merge")
    x1, h2 = _resid_norm(x, mixed, _after(w("norm_mix_post"), tick(mixed, 3)), w("norm_mlp_pre"))
    up, act = _mm(h2, w("w_up"), "nn", [BF16, BF16], name="mm_up", epilogue=_relu2)
    f = _mm(act, w("w_down"), "nn", [BF16], name="mm_down")
    kept = dict(x=x, h=h, z=z, a=a, q=q, k=k, v=v, o=o, lse=lse, y_a=y_a, y_b=y_b, mix=mix, mixed=mixed, x1=x1, h2=h2, up=up, act=act, f=f)
    g_post = _after(w("norm_mlp_post"), tick(f, 4))
    if last:
        return _resid_norm_loss(x1, f, g_post, target), kept
    return _resid_norm(x1, f, g_post, w("norm_next")), kept


def _layer_bwd(dx_out, w, kept, cols, tables, tick, emit):
    cos_t, sin_t = tables
    t = kept
    df, d_norm_mlp_post = _norm_bwd(t["f"], _after(w["norm_mlp_post"], tick(dx_out, 0)), dx_out, None, BF16, "norm_bwd_mlp_post")
    dup = _mm(df, w["w_down"], "nt", [BF16], name="mm_d_act", epilogue=_relu2_bwd, extras=(t["up"],))
    emit((4, 5), [(t["h2"], dup), (t["act"], df)])
    dh2 = _mm(dup, w["w_up"], "nt", [BF16], name="mm_d_h2")
    dx1, d_norm_mlp_pre = _norm_bwd(t["x1"], _after(w["norm_mlp_pre"], tick(dh2, 1)), dh2, dx_out, F32, "norm_bwd_mlp_pre")
    dmixed, d_norm_mix_post = _norm_bwd(t["mixed"], w["norm_mix_post"], dx1, None, BF16, "norm_bwd_mix_post")
    dmix = _mm(dmixed, w["w_merge"], "nt", [BF16], name="mm_d_mix")
    dy_a, dy_b, dz_ga, dz_gb, dbias_a, dbias_b = _gate_bwd(t["z"], _after(w["gate_bias"], tick(dmix, 2)), t["y_a"], t["y_b"], dmix, cols)
    da = _mm(dy_a, w["w_out_conv"], "nt", [BF16], name="mm_d_a")
    emit((1, 2, 3), [(t["a"], dy_a), (t["o"], dy_b), (t["mix"], dmixed)])
    do = _mm(dy_b, w["w_out_attn"], "nt", [BF16], name="mm_d_o")
    dz_cb, dz_cc, dz_ci, d_conv_w = _conv_bwd(t["z"], _after(w["conv_w"], tick(do, 3)), da, cols)
    dq, dk_t, dv_t = _attn_bwd(t["q"], t["k"], t["v"], do, t["lse"])
    dz_q, dz_k, dz_v, d_q_norm, d_k_norm = _rope_bwd(t["z"], _after(w["q_norm"], tick(dv_t, 4)), w["k_norm"], cos_t, sin_t, dq, dk_t, dv_t, cols)
    dz = jnp.concatenate([dz_cb, dz_cc, dz_ci, dz_q, dz_k, dz_v, dz_ga, dz_gb], axis=1)
    emit((0,), [(t["h"], dz)])
    tick(dz, 5)
    dh = _mm(dz, w["w_in"], "nt", [BF16], name="mm_d_h", tk=3328)
    dx_in, d_norm_mix_pre = _norm_bwd(t["x"], _after(w["norm_mix_pre"], tick(dh, 6)), dh, dx1, F32, "norm_bwd_mix_pre")
    small = dict(norm_mix_pre=d_norm_mix_pre, gate_bias=jnp.concatenate([dbias_a, dbias_b], axis=1), conv_w=d_conv_w, q_norm=d_q_norm,
                 k_norm=d_k_norm, norm_mix_post=d_norm_mix_post, norm_mlp_pre=d_norm_mlp_pre, norm_mlp_post=d_norm_mlp_post)
    return dx_in, small


SMALL = ("norm_mix_pre", "gate_bias", "conv_w", "q_norm", "k_norm", "norm_mix_post", "norm_mlp_pre", "norm_mlp_post")
WEIGHTS = ("norm_mix_pre", "w_in", "gate_bias", "conv_w", "q_norm", "k_norm", "w_out_conv", "w_out_attn", "w_merge",
           "norm_mix_post", "norm_mlp_pre", "w_up", "w_down", "norm_mlp_post")


def _pack(parts):
    flat = jnp.concatenate([a.reshape(-1) for a in parts])
    rows = -(-flat.shape[0] // LANES)
    pad = (-rows) % 8
    flat = jnp.pad(flat, (0, (rows + pad) * LANES - flat.shape[0]))
    return flat.reshape(rows + pad, LANES)


def _unpack(packed, shapes):
    flat = packed.reshape(-1)
    out, off = [], 0
    for shp in shapes:
        n = math.prod(shp)
        out.append(flat[off:off + n].reshape(shp))
        off += n
    return out


def kernel(x, norm_mix_pre, w_in, gate_bias, conv_w, q_norm, k_norm, w_out_conv, w_out_attn, w_merge, norm_mix_post, norm_mlp_pre, w_up, w_down, norm_mlp_post, loss_target, m_norm_mix_pre, m_w_in, m_gate_bias, m_conv_w, m_q_norm, m_k_norm, m_w_out_conv, m_w_out_attn, m_w_merge, m_norm_mix_post, m_norm_mlp_pre, m_w_up, m_w_down, m_norm_mlp_post, v_norm_mix_pre, v_w_in, v_gate_bias, v_conv_w, v_q_norm, v_k_norm, v_w_out_conv, v_w_out_attn, v_w_merge, v_norm_mix_post, v_norm_mlp_pre, v_w_up, v_w_down, v_norm_mlp_post):
    w = dict(norm_mix_pre=norm_mix_pre, w_in=w_in, gate_bias=gate_bias, conv_w=conv_w, q_norm=q_norm, k_norm=k_norm, w_out_conv=w_out_conv,
             w_out_attn=w_out_attn, w_merge=w_merge, norm_mix_post=norm_mix_post, norm_mlp_pre=norm_mlp_pre, w_up=w_up, w_down=w_down,
             norm_mlp_post=norm_mlp_post)
    mom = dict(norm_mix_pre=m_norm_mix_pre, w_in=m_w_in, gate_bias=m_gate_bias, conv_w=m_conv_w, q_norm=m_q_norm, k_norm=m_k_norm,
               w_out_conv=m_w_out_conv, w_out_attn=m_w_out_attn, w_merge=m_w_merge, norm_mix_post=m_norm_mix_post, norm_mlp_pre=m_norm_mlp_pre,
               w_up=m_w_up, w_down=m_w_down, norm_mlp_post=m_norm_mlp_post)
    var = dict(norm_mix_pre=v_norm_mix_pre, w_in=v_w_in, gate_bias=v_gate_bias, conv_w=v_conv_w, q_norm=v_q_norm, k_norm=v_k_norm,
               w_out_conv=v_w_out_conv, w_out_attn=v_w_out_attn, w_merge=v_w_merge, norm_mix_post=v_norm_mix_post, norm_mlp_pre=v_norm_mlp_pre,
               w_up=v_w_up, w_down=v_w_down, norm_mlp_post=v_norm_mlp_post)
    depth = w_in.shape[0]
    _, s, d = x.shape
    cols = _Cols(d)
    x0 = x.reshape(s, d)
    target = loss_target.reshape(s, d)
    tables = _rope_tables(s)
    chip = (2 * lax.axis_index("x") + lax.axis_index("y")).astype(jnp.int32)
    core = lax.axis_index("c").astype(jnp.int32)
    chip_vec, core_vec, other_core_vec = chip.reshape(1), core.reshape(1), (1 - core).reshape(1)

    n_conv = conv_w.shape[-1]
    placed = lax.dynamic_update_slice_in_dim(jnp.zeros((depth, conv_w.shape[1], n_conv * N_CHIPS), F32), conv_w, chip * n_conv, axis=2)
    conv_full = _unpack(_allreduce_small(_pack([jnp.where(core == 0, placed, 0.0)]), "gather_conv_w"), [placed.shape])[0]

    chains = _Chains()
    chains.pin(conv_full)
    full = [{} for _ in range(depth)]
    everything = tuple(range(N_BIG))
    gathers = []
    for l in range(depth):
        bufs = [_cast_place(w[name], l, chip_vec, col, "cast_place_" + name) for name, col in BIG]
        for group in ((0,), everything[1:]):
            if group[0] == 0:
                wait_for = ("fwd", l - 1, 3) if l else None
            else:
                wait_for = ("fwd", l, 0 if l else 1)
            gathers.append(_gather_chain(chains, f"{l}_{group[0]}", group, [bufs[a] for a in group], full[l], wait_for))
            next(gathers[-1])
    next(gathers[0])
    next(gathers[0], None)
    for g in gathers[1:]:
        chains.add(g)

    def layer_params(l):
        p = dict(full[l])
        p["conv_w"] = conv_full[l]
        p["gate_bias"] = gate_bias[l].reshape(1, -1)
        for name in ("norm_mix_pre", "q_norm", "k_norm", "norm_mix_post", "norm_mlp_pre", "norm_mlp_post"):
            p[name] = w[name][l].reshape(1, -1)
        if l + 1 < depth:
            p["norm_next"] = w["norm_mix_pre"][l + 1].reshape(1, -1)
        return p

    kept = []
    xl, h = x0, _norm_first(x0, norm_mix_pre[0].reshape(1, -1))
    for l in range(depth):
        last = l == depth - 1
        (xl, h), t = _layer_fwd(xl, h, lambda name, l=l: layer_params(l)[name], cols, tables,
                                lambda dep, k, l=l: chains.tick(dep, ("fwd", l, k)), last, target if last else None)
        kept.append(t)
    dx, loss_local = xl, h

    carried = {}
    small = [None] * depth
    for l in reversed(range(depth)):
        def emit(group, pairs, l=l):
            chains.add(_grad_chain(chains, l, group, pairs, core_vec, other_core_vec, chip_vec, w, mom, var, carried))

        dx, small[l] = _layer_bwd(dx, layer_params(l), kept[l], cols, tables, lambda dep, k: chains.tick(dep), emit)
    while chains.active:
        chains.tick(None)
    grads = {name: carried[name][0] for name, _ in BIG}
    delta = {name: carried[name][1] for name, _ in BIG}
    new_m = {name: carried[name][2] for name, _ in BIG}
    new_v = {name: carried[name][3] for name, _ in BIG}

    small_full_shapes = [(depth,) + small[0][name].shape for name in SMALL]
    packed = _pack([jnp.stack([small[l][name] for l in range(depth)]) for name in SMALL] + [jnp.broadcast_to(loss_local.reshape(1), (LANES,))])
    small_sum = _unpack(_allreduce_small(packed, "allreduce_small"), small_full_shapes + [(LANES,)])
    loss = small_sum[-1][0]
    for name, g in zip(SMALL, small_sum[:-1]):
        if name == "conv_w":
            g = lax.dynamic_slice_in_dim(g, chip * n_conv, n_conv, axis=2)
        grads[name] = g.reshape(w[name].shape)
    small_shapes = [w[name].shape for name in SMALL]
    packs = [_pack([src[name] for name in SMALL]) for src in (w, grads, mom, var)]
    for dst, out in zip((delta, new_m, new_v), _adamw(*packs, "adamw_small")):
        for name, val in zip(SMALL, _unpack(out, small_shapes)):
            dst[name] = val

    grad_x = dx.reshape(x.shape)
    return (loss, grad_x, *[grads[n] for n in WEIGHTS], *[delta[n] for n in WEIGHTS], *[new_m[n] for n in WEIGHTS], *[new_v[n] for n in WEIGHTS])
```

```python
import math

import jax
import jax.numpy as jnp
from jax import lax
from jax.experimental import pallas as pl
from jax.experimental.pallas import tpu as pltpu

F32 = jnp.float32
BF16 = jnp.bfloat16

HEAD_DIM = 128
GROUP = 4
GRID_W = 64
ROPE_THETA = 10000.0
RMS_EPS = 1e-6
ADAM_LR = 0.001
ADAM_B1 = 0.9
ADAM_B2 = 0.999
ADAM_EPS = 1e-08
ADAM_WD = 0.01
ADAM_STEP = 10

LANES = 128
N_CHIPS = 4
N_DEV = 8
VMEM_LIMIT_BYTES = 56 * 1024 * 1024
MESH = pl.DeviceIdType.MESH
ANY = pl.BlockSpec(memory_space=pl.ANY)


def _tile(dim, cap, mult):
    if dim <= cap:
        return dim
    t = (cap // mult) * mult
    while t >= mult:
        if dim % t == 0:
            return t
        t -= mult
    raise ValueError(f"no tile for {dim} under {cap} in multiples of {mult}")


def _params(sem=None):
    return pltpu.CompilerParams(dimension_semantics=sem, vmem_limit_bytes=VMEM_LIMIT_BYTES)


def _sds(shape, dtype):
    return jax.ShapeDtypeStruct(tuple(shape), dtype)


def _rstd(x):
    return lax.rsqrt(jnp.mean(x * x, axis=-1, keepdims=True) + RMS_EPS)


_DOT_DIMS = {"nn": ((1,), (0,)), "nt": ((1,), (1,)), "tn": ((0,), (0,))}


def _mm(a, b, mode, out_dtypes, *, name, epilogue=None, extras=(), tm=1024, tn=1024, tk=2048, half=None):
    if mode == "nn":
        (m, k), (k2, n) = a.shape, b.shape
    elif mode == "nt":
        (m, k), (n, k2) = a.shape, b.shape
    else:
        (k, m), (k2, n) = a.shape, b.shape
    assert k == k2, (a.shape, b.shape, mode)
    side = half[1] if half is not None else None
    if side == "m":
        m //= 2
    elif side == "n":
        n //= 2
    tm, tn, tk = _tile(m, tm, 8), _tile(n, tn, LANES), _tile(k, tk, LANES)
    nk = k // tk
    row = (lambda i, s: i + s[0][0] * (m // tm)) if side == "m" else (lambda i, s: i)
    col = (lambda j, s: j + s[0][0] * (n // tn)) if side == "n" else (lambda j, s: j)
    a_spec = pl.BlockSpec((tk, tm), lambda i, j, kk, *s: (kk, row(i, s))) if mode == "tn" else pl.BlockSpec((tm, tk), lambda i, j, kk, *s: (row(i, s), kk))
    b_spec = pl.BlockSpec((tn, tk), lambda i, j, kk, *s: (col(j, s), kk)) if mode == "nt" else pl.BlockSpec((tk, tn), lambda i, j, kk, *s: (kk, col(j, s)))
    tile_spec = pl.BlockSpec((tm, tn), lambda i, j, kk, *s: (i, j))
    n_extra, n_out = len(extras), len(out_dtypes)
    dims = (_DOT_DIMS[mode], ((), ()))

    def body(*refs):
        if half is not None:
            refs = refs[1:]
        a_ref, b_ref = refs[:2]
        extra_refs = refs[2:2 + n_extra]
        out_refs = refs[2 + n_extra:2 + n_extra + n_out]
        part = lax.dot_general(a_ref[...].astype(BF16), b_ref[...].astype(BF16), dims, preferred_element_type=F32)

        def finish(total):
            res = epilogue(total, *[e[...] for e in extra_refs]) if epilogue is not None else (total,)
            for o, r in zip(out_refs, res):
                o[...] = r.astype(o.dtype)

        if nk == 1:
            finish(part)
        else:
            acc = refs[-1]
            kk = pl.program_id(2)

            @pl.when(kk == 0)
            def _():
                acc[...] = part

            @pl.when(kk > 0)
            def _():
                acc[...] += part

            @pl.when(kk == nk - 1)
            def _():
                finish(acc[...])

    grid = (m // tm, n // tn, nk)
    in_specs, out_specs = [a_spec, b_spec] + [tile_spec] * n_extra, [tile_spec] * n_out
    scratch = [pltpu.VMEM((tm, tn), F32)] if nk > 1 else []
    if half is None:
        layout, lead = dict(grid=grid, in_specs=in_specs, out_specs=out_specs, scratch_shapes=scratch), ()
    else:
        layout = dict(grid_spec=pltpu.PrefetchScalarGridSpec(num_scalar_prefetch=1, grid=grid, in_specs=in_specs, out_specs=out_specs,
                                                             scratch_shapes=scratch))
        lead = (half[0],)
    outs = pl.pallas_call(body, name=name, out_shape=[_sds((m, n), d) for d in out_dtypes],
                          compiler_params=_params(("parallel", "parallel", "arbitrary")), **layout)(*lead, a, b, *extras)
    return outs if n_out > 1 else outs[0]


ROW_TILE = 512
GATE_ROW_TILE = 1024


def _norm_first(x, g):
    s, d = x.shape
    ts = _tile(s, ROW_TILE, 8)

    def body(x_ref, g_ref, h_ref):
        xv = x_ref[...]
        h_ref[...] = (xv * _rstd(xv) * g_ref[...]).astype(h_ref.dtype)

    row = pl.BlockSpec((ts, d), lambda i: (i, 0))
    vec = pl.BlockSpec((1, d), lambda i: (0, 0))
    return pl.pallas_call(body, name="norm_first", grid=(s // ts,), in_specs=[row, vec], out_specs=row,
                          out_shape=_sds((s, d), BF16), compiler_params=_params(("parallel",)))(x, g)


def _resid_norm(xres, y, g_post, g_next):
    s, d = xres.shape
    ts = _tile(s, ROW_TILE, 8)

    def body(x_ref, y_ref, gp_ref, gn_ref, xn_ref, hn_ref):
        yv = y_ref[...].astype(F32)
        xn = x_ref[...] + yv * _rstd(yv) * gp_ref[...]
        xn_ref[...] = xn
        hn_ref[...] = (xn * _rstd(xn) * gn_ref[...]).astype(hn_ref.dtype)

    row = pl.BlockSpec((ts, d), lambda i: (i, 0))
    vec = pl.BlockSpec((1, d), lambda i: (0, 0))
    return pl.pallas_call(body, name="resid_norm", grid=(s // ts,), in_specs=[row, row, vec, vec], out_specs=[row, row],
                          out_shape=[_sds((s, d), F32), _sds((s, d), BF16)], compiler_params=_params(("parallel",)))(xres, y, g_post, g_next)


def _resid_norm_loss(xres, y, g_post, target):
    s, d = xres.shape
    ts = _tile(s, ROW_TILE, 8)
    n_steps = s // ts

    def body(x_ref, y_ref, gp_ref, t_ref, dout_ref, loss_ref, acc_ref):
        i = pl.program_id(0)
        yv = y_ref[...].astype(F32)
        err = x_ref[...] + yv * _rstd(yv) * gp_ref[...] - t_ref[...]
        dout_ref[...] = err / d
        part = jnp.sum(err * err, axis=0, keepdims=True)

        @pl.when(i == 0)
        def _():
            acc_ref[...] = part

        @pl.when(i > 0)
        def _():
            acc_ref[...] += part

        @pl.when(i == n_steps - 1)
        def _():
            loss_ref[...] = 0.5 * jnp.sum(acc_ref[...], axis=1, keepdims=True) / d

    row = pl.BlockSpec((ts, d), lambda i: (i, 0))
    vec = pl.BlockSpec((1, d), lambda i: (0, 0))
    one = pl.BlockSpec((1, 1), lambda i: (0, 0))
    return pl.pallas_call(body, name="resid_norm_loss", grid=(n_steps,), in_specs=[row, row, vec, row], out_specs=[row, one],
                          out_shape=[_sds((s, d), F32), _sds((1, 1), F32)], scratch_shapes=[pltpu.VMEM((1, d), F32)],
                          compiler_params=_params(("arbitrary",)))(xres, y, g_post, target)


def _norm_bwd(xin, g, dout, dres, out_dtype, name):
    s, d = xin.shape
    ts = _tile(s, ROW_TILE, 8)
    has_res = dres is not None

    def body(*refs):
        x_ref, g_ref, do_ref = refs[:3]
        dx_ref, dg_ref = refs[-2:]
        i = pl.program_id(0)
        xv, dov = x_ref[...].astype(F32), do_ref[...].astype(F32)
        r = _rstd(xv)
        xhat = xv * r
        dg = jnp.sum(dov * xhat, axis=0, keepdims=True)
        dxh = dov * g_ref[...]
        dx = r * (dxh - xhat * jnp.mean(dxh * xhat, axis=-1, keepdims=True))
        if has_res:
            dx = dx + refs[3][...]
        dx_ref[...] = dx.astype(dx_ref.dtype)

        @pl.when(i == 0)
        def _():
            dg_ref[...] = dg

        @pl.when(i > 0)
        def _():
            dg_ref[...] += dg

    row = pl.BlockSpec((ts, d), lambda i: (i, 0))
    vec = pl.BlockSpec((1, d), lambda i: (0, 0))
    ops = [xin, g, dout] + ([dres] if has_res else [])
    return pl.pallas_call(body, name=name, grid=(s // ts,), in_specs=[row, vec, row] + ([row] if has_res else []),
                          out_specs=[row, vec], out_shape=[_sds((s, d), out_dtype), _sds((1, d), F32)],
                          compiler_params=_params(("arbitrary",)))(*ops)


class _Cols:
    def __init__(self, d):
        self.d = d
        self.kv = d // GROUP
        self.cb, self.cc, self.ci, self.q = 0, d, 2 * d, 3 * d
        self.k = 4 * d
        self.v = 4 * d + self.kv
        self.ga = 4 * d + 2 * self.kv
        self.gb = 5 * d + 2 * self.kv
        self.width = 6 * d + 2 * self.kv


CONV_COLS = 128


def _shift_rows(u, down):
    s = u.shape[0]
    rows = lax.broadcasted_iota(jnp.int32, u.shape, 0)
    if down:
        return jnp.where(rows == 0, 0.0, pltpu.roll(u, 1, 0))
    return jnp.where(rows == s - 1, 0.0, pltpu.roll(u, s - 1, 0))


def _conv_fwd(z, w, cols):
    s, d = z.shape[0], cols.d
    cw = CONV_COLS

    def body(cb_ref, cc_ref, ci_ref, w_ref, a_ref):
        u = cc_ref[...].astype(F32) * ci_ref[...].astype(F32)
        wv = w_ref[...]
        conv = wv[0:1] * _shift_rows(u, True) + wv[1:2] * u + wv[2:3] * _shift_rows(u, False)
        a_ref[...] = (cb_ref[...].astype(F32) * conv).astype(a_ref.dtype)

    def zspec(off):
        return pl.BlockSpec((s, cw), lambda j: (0, off // cw + j))

    return pl.pallas_call(body, name="conv_fwd", grid=(d // cw,),
                          in_specs=[zspec(cols.cb), zspec(cols.cc), zspec(cols.ci), pl.BlockSpec((3, cw), lambda j: (0, j))],
                          out_specs=pl.BlockSpec((s, cw), lambda j: (0, j)), out_shape=_sds((s, d), BF16),
                          compiler_params=_params(("parallel",)))(z, z, z, w)


def _conv_bwd(z, w, da, cols):
    s, d = z.shape[0], cols.d
    cw = CONV_COLS

    def body(cb_ref, cc_ref, ci_ref, w_ref, da_ref, dcb_ref, dcc_ref, dci_ref, dw_ref):
        cb, cc, ci, dav = cb_ref[...].astype(F32), cc_ref[...].astype(F32), ci_ref[...].astype(F32), da_ref[...].astype(F32)
        wv = w_ref[...]
        u = cc * ci
        um, up = _shift_rows(u, True), _shift_rows(u, False)
        conv = wv[0:1] * um + wv[1:2] * u + wv[2:3] * up
        dcb_ref[...] = (dav * conv).astype(dcb_ref.dtype)
        dconv = dav * cb
        dw_ref[0:1, :] = jnp.sum(dconv * um, axis=0, keepdims=True)
        dw_ref[1:2, :] = jnp.sum(dconv * u, axis=0, keepdims=True)
        dw_ref[2:3, :] = jnp.sum(dconv * up, axis=0, keepdims=True)
        du = wv[0:1] * _shift_rows(dconv, False) + wv[1:2] * dconv + wv[2:3] * _shift_rows(dconv, True)
        dcc_ref[...] = (du * ci).astype(dcc_ref.dtype)
        dci_ref[...] = (du * cc).astype(dci_ref.dtype)

    def zspec(off):
        return pl.BlockSpec((s, cw), lambda j: (0, off // cw + j))

    col = pl.BlockSpec((s, cw), lambda j: (0, j))
    wspec = pl.BlockSpec((3, cw), lambda j: (0, j))
    return pl.pallas_call(body, name="conv_bwd", grid=(d // cw,),
                          in_specs=[zspec(cols.cb), zspec(cols.cc), zspec(cols.ci), wspec, col],
                          out_specs=[col, col, col, wspec],
                          out_shape=[_sds((s, d), BF16)] * 3 + [_sds((3, d), F32)],
                          compiler_params=_params(("parallel",)))(z, z, z, w, da)


def _gate_fwd(z, bias, y_a, y_b, cols):
    s, d = y_a.shape
    ts, cw = _tile(s, GATE_ROW_TILE, 8), cols.kv
    nj = d // cw

    def body(ga_ref, gb_ref, ba_ref, bb_ref, ya_ref, yb_ref, o_ref):
        gate_a = jax.nn.sigmoid(ga_ref[...].astype(F32) + ba_ref[...])
        gate_b = jax.nn.sigmoid(gb_ref[...].astype(F32) + bb_ref[...])
        o_ref[...] = (gate_a * ya_ref[...].astype(F32) + gate_b * yb_ref[...].astype(F32)).astype(o_ref.dtype)

    tile = pl.BlockSpec((ts, cw), lambda i, j: (i, j))
    return pl.pallas_call(
        body, name="gate_fwd", grid=(s // ts, nj),
        in_specs=[pl.BlockSpec((ts, cw), lambda i, j: (i, cols.ga // cw + j)), pl.BlockSpec((ts, cw), lambda i, j: (i, cols.gb // cw + j)),
                  pl.BlockSpec((1, cw), lambda i, j: (0, j)), pl.BlockSpec((1, cw), lambda i, j: (0, nj + j)), tile, tile],
        out_specs=tile, out_shape=_sds((s, d), BF16), compiler_params=_params(("parallel", "parallel")))(z, z, bias, bias, y_a, y_b)


def _gate_bwd(z, bias, y_a, y_b, dmix, cols):
    s, d = y_a.shape
    ts, cw = _tile(s, GATE_ROW_TILE, 8), cols.kv
    nj = d // cw

    def body(ga_ref, gb_ref, ba_ref, bb_ref, ya_ref, yb_ref, dm_ref, dya_ref, dyb_ref, dga_ref, dgb_ref, dba_ref, dbb_ref):
        i = pl.program_id(1)
        gate_a = jax.nn.sigmoid(ga_ref[...].astype(F32) + ba_ref[...])
        gate_b = jax.nn.sigmoid(gb_ref[...].astype(F32) + bb_ref[...])
        dm = dm_ref[...].astype(F32)
        dya_ref[...] = (dm * gate_a).astype(dya_ref.dtype)
        dyb_ref[...] = (dm * gate_b).astype(dyb_ref.dtype)
        dga = dm * ya_ref[...].astype(F32) * (gate_a * (1.0 - gate_a))
        dgb = dm * yb_ref[...].astype(F32) * (gate_b * (1.0 - gate_b))
        dga_ref[...] = dga.astype(dga_ref.dtype)
        dgb_ref[...] = dgb.astype(dgb_ref.dtype)
        sa = jnp.sum(dga, axis=0, keepdims=True)
        sb = jnp.sum(dgb, axis=0, keepdims=True)

        @pl.when(i == 0)
        def _():
            dba_ref[...] = sa
            dbb_ref[...] = sb

        @pl.when(i > 0)
        def _():
            dba_ref[...] += sa
            dbb_ref[...] += sb

    tile = pl.BlockSpec((ts, cw), lambda j, i: (i, j))
    vec = pl.BlockSpec((1, cw), lambda j, i: (0, j))
    return pl.pallas_call(
        body, name="gate_bwd", grid=(nj, s // ts),
        in_specs=[pl.BlockSpec((ts, cw), lambda j, i: (i, cols.ga // cw + j)), pl.BlockSpec((ts, cw), lambda j, i: (i, cols.gb // cw + j)),
                  vec, pl.BlockSpec((1, cw), lambda j, i: (0, nj + j)), tile, tile, tile],
        out_specs=[tile, tile, tile, tile, vec, vec],
        out_shape=[_sds((s, d), BF16)] * 4 + [_sds((1, d), F32)] * 2,
        compiler_params=_params(("parallel", "arbitrary")))(z, z, bias, bias, y_a, y_b, dmix)


def _rope_tables(s):
    axis_dim = HEAD_DIM // 2
    n_freq = axis_dim // 2
    rows = s // GRID_W
    row_idx = jnp.repeat(jnp.arange(rows, dtype=jnp.int32), GRID_W)
    col_idx = jnp.tile(jnp.arange(GRID_W, dtype=jnp.int32), rows)
    inv_freq = ROPE_THETA ** (-jnp.arange(0, axis_dim, 2, dtype=F32) / axis_dim)
    ang = jnp.stack([row_idx.astype(F32)[:, None] * inv_freq, col_idx.astype(F32)[:, None] * inv_freq], axis=1)
    cos, sin = jnp.cos(ang), jnp.sin(ang)
    cos_t = jnp.stack([cos, cos], axis=2).reshape(s, HEAD_DIM)
    sin_t = jnp.stack([-sin, sin], axis=2).reshape(s, HEAD_DIM)
    return cos_t, sin_t


def _partner(x):
    n = x.shape[-1]
    lane = lax.broadcasted_iota(jnp.int32, x.shape, x.ndim - 1)
    quarter = HEAD_DIM // 4
    return jnp.where(lane % (2 * quarter) < quarter, pltpu.roll(x, n - quarter, x.ndim - 1), pltpu.roll(x, quarter, x.ndim - 1))


LOG2E = math.log2(math.e)
Q_SCALE = LOG2E / math.sqrt(HEAD_DIM)


def _rope_fwd(z, qn, kn, cos_t, sin_t, cols):
    s, d, kv = z.shape[0], cols.d, cols.kv
    ts = _tile(s, ROW_TILE, 8)
    scale = Q_SCALE

    def body(q_ref, k_ref, v_ref, qn_ref, kn_ref, c_ref, s_ref, qo_ref, ko_ref, vo_ref):
        c, sn = c_ref[...], s_ref[...]

        def head(xh, g):
            xn = xh * _rstd(xh) * g
            return xn * c + _partner(xn) * sn

        for h in range(d // HEAD_DIM):
            sl = slice(h * HEAD_DIM, (h + 1) * HEAD_DIM)
            qo_ref[:, sl] = (head(q_ref[:, sl].astype(F32), qn_ref[...]) * scale).astype(qo_ref.dtype)
        for h in range(kv // HEAD_DIM):
            sl = slice(h * HEAD_DIM, (h + 1) * HEAD_DIM)
            ko_ref[:, sl] = head(k_ref[:, sl].astype(F32), kn_ref[...]).astype(ko_ref.dtype)
        vo_ref[...] = v_ref[...].astype(vo_ref.dtype)

    vec = pl.BlockSpec((1, HEAD_DIM), lambda i: (0, 0))
    tab = pl.BlockSpec((ts, HEAD_DIM), lambda i: (i, 0))
    return pl.pallas_call(
        body, name="rope_fwd", grid=(s // ts,),
        in_specs=[pl.BlockSpec((ts, d), lambda i: (i, cols.q // d)), pl.BlockSpec((ts, kv), lambda i: (i, cols.k // kv)),
                  pl.BlockSpec((ts, kv), lambda i: (i, cols.v // kv)), vec, vec, tab, tab],
        out_specs=[pl.BlockSpec((ts, d), lambda i: (i, 0)), pl.BlockSpec((ts, kv), lambda i: (i, 0)), pl.BlockSpec((ts, kv), lambda i: (i, 0))],
        out_shape=[_sds((s, d), BF16), _sds((s, kv), BF16), _sds((s, kv), BF16)],
        compiler_params=_params(("parallel",)))(z, z, z, qn, kn, cos_t, sin_t)


def _rope_bwd(z, qn, kn, cos_t, sin_t, dq, dk_t, dv_t, cols):
    s, d, kv = z.shape[0], cols.d, cols.kv
    ts = _tile(s, ROW_TILE, 8)
    scale = 1.0 / math.sqrt(HEAD_DIM)

    def body(q_ref, k_ref, qn_ref, kn_ref, c_ref, s_ref, dq_ref, dkt_ref, dvt_ref, dzq_ref, dzk_ref, dzv_ref, dqn_ref, dkn_ref):
        i = pl.program_id(0)
        c, sn = c_ref[...], s_ref[...]
        dk_all = dkt_ref[...].T * (1.0 / LOG2E)

        def head_bwd(xh, g, drot):
            dxn = drot * c + _partner(drot * sn)
            r = _rstd(xh)
            xhat = xh * r
            dgain = jnp.sum(dxn * xhat, axis=0, keepdims=True)
            dxh = dxn * g
            return r * (dxh - xhat * jnp.mean(dxh * xhat, axis=-1, keepdims=True)), dgain

        dqn = jnp.zeros((1, HEAD_DIM), F32)
        for h in range(d // HEAD_DIM):
            sl = slice(h * HEAD_DIM, (h + 1) * HEAD_DIM)
            dx, dg = head_bwd(q_ref[:, sl].astype(F32), qn_ref[...], dq_ref[:, sl].astype(F32) * scale)
            dzq_ref[:, sl] = dx.astype(dzq_ref.dtype)
            dqn = dqn + dg
        dkn = jnp.zeros((1, HEAD_DIM), F32)
        for h in range(kv // HEAD_DIM):
            sl = slice(h * HEAD_DIM, (h + 1) * HEAD_DIM)
            dx, dg = head_bwd(k_ref[:, sl].astype(F32), kn_ref[...], dk_all[:, sl])
            dzk_ref[:, sl] = dx.astype(dzk_ref.dtype)
            dkn = dkn + dg
        dzv_ref[...] = dvt_ref[...].T.astype(dzv_ref.dtype)

        @pl.when(i == 0)
        def _():
            dqn_ref[...] = dqn
            dkn_ref[...] = dkn

        @pl.when(i > 0)
        def _():
            dqn_ref[...] += dqn
            dkn_ref[...] += dkn

    vec = pl.BlockSpec((1, HEAD_DIM), lambda i: (0, 0))
    tab = pl.BlockSpec((ts, HEAD_DIM), lambda i: (i, 0))
    qrow = pl.BlockSpec((ts, d), lambda i: (i, 0))
    krow = pl.BlockSpec((ts, kv), lambda i: (i, 0))
    kcol = pl.BlockSpec((kv, ts), lambda i: (0, i))
    return pl.pallas_call(
        body, name="rope_bwd", grid=(s // ts,),
        in_specs=[pl.BlockSpec((ts, d), lambda i: (i, cols.q // d)), pl.BlockSpec((ts, kv), lambda i: (i, cols.k // kv)),
                  vec, vec, tab, tab, qrow, kcol, kcol],
        out_specs=[qrow, krow, krow, vec, vec],
        out_shape=[_sds((s, d), BF16), _sds((s, kv), BF16), _sds((s, kv), BF16), _sds((1, HEAD_DIM), F32), _sds((1, HEAD_DIM), F32)],
        compiler_params=_params(("arbitrary",)))(z, z, qn, kn, cos_t, sin_t, dq, dk_t, dv_t)


Q_TILE = 256
_NT = (((1,), (1,)), ((), ()))
_NN = (((1,), (0,)), ((), ()))


def _attn_fwd(q, k, v):
    s, d = q.shape
    kvh = k.shape[1] // HEAD_DIM
    tq = _tile(s, Q_TILE, LANES)
    gw = GROUP * HEAD_DIM

    def body(q_ref, k_ref, v_ref, o_ref, lse_ref):
        kk, vv = k_ref[...], v_ref[...]
        for g in range(GROUP):
            sl = slice(g * HEAD_DIM, (g + 1) * HEAD_DIM)
            sc = lax.dot_general(q_ref[:, sl], kk, _NT, preferred_element_type=F32)
            mx = jnp.max(sc, axis=-1, keepdims=True)
            p = jnp.exp2(sc - mx)
            l = jnp.sum(p, axis=-1, keepdims=True)
            o = lax.dot_general(p.astype(BF16), vv, _NN, preferred_element_type=F32) * (1.0 / l)
            o_ref[:, sl] = o.astype(o_ref.dtype)
            lse_ref[:, g:g + 1] = mx + jnp.log(l) * LOG2E

    return pl.pallas_call(
        body, name="attn_fwd", grid=(kvh, s // tq),
        in_specs=[pl.BlockSpec((tq, gw), lambda j, i: (i, j)), pl.BlockSpec((s, HEAD_DIM), lambda j, i: (0, j)), pl.BlockSpec((s, HEAD_DIM), lambda j, i: (0, j))],
        out_specs=[pl.BlockSpec((tq, gw), lambda j, i: (i, j)), pl.BlockSpec((None, tq, GROUP), lambda j, i: (j, i, 0))],
        out_shape=[_sds((s, d), BF16), _sds((kvh, s, GROUP), F32)],
        compiler_params=_params(("parallel", "parallel")))(q, k, v)


_TN = (((0,), (0,)), ((), ()))


def _attn_bwd(q, k, v, do, lse):
    s, d = q.shape
    kv = k.shape[1]
    kvh = kv // HEAD_DIM
    tq = _tile(s, Q_TILE, LANES)
    gw = GROUP * HEAD_DIM

    def body(q_ref, k_ref, v_ref, do_ref, lse_ref, dq_ref, dkt_ref, dvt_ref):
        i = pl.program_id(1)
        kk, vv = k_ref[...], v_ref[...]

        @pl.when(i == 0)
        def _():
            dkt_ref[...] = jnp.zeros_like(dkt_ref)
            dvt_ref[...] = jnp.zeros_like(dvt_ref)

        for g in range(GROUP):
            sl = slice(g * HEAD_DIM, (g + 1) * HEAD_DIM)
            qg, dog = q_ref[:, sl], do_ref[:, sl]
            sc = lax.dot_general(qg, kk, _NT, preferred_element_type=F32)
            p = jnp.exp2(sc - lse_ref[:, g:g + 1])
            dp = lax.dot_general(dog, vv, _NT, preferred_element_type=F32)
            delta = jnp.sum(p * dp, axis=-1, keepdims=True)
            ds = (p * (dp - delta)).astype(BF16)
            dq_ref[:, sl] = lax.dot_general(ds, kk, _NN, preferred_element_type=F32).astype(dq_ref.dtype)
            dkt_ref[...] += lax.dot_general(qg, ds, _TN, preferred_element_type=F32)
            dvt_ref[...] += lax.dot_general(dog, p.astype(BF16), _TN, preferred_element_type=F32)

    qspec = pl.BlockSpec((tq, gw), lambda j, i: (i, j))
    kspec = pl.BlockSpec((s, HEAD_DIM), lambda j, i: (0, j))
    stat = pl.BlockSpec((None, tq, GROUP), lambda j, i: (j, i, 0))
    tspec = pl.BlockSpec((HEAD_DIM, s), lambda j, i: (j, 0))
    return pl.pallas_call(
        body, name="attn_bwd", grid=(kvh, s // tq),
        in_specs=[qspec, kspec, kspec, qspec, stat], out_specs=[qspec, tspec, tspec],
        out_shape=[_sds((s, d), BF16), _sds((kv, s), F32), _sds((kv, s), F32)],
        compiler_params=_params(("parallel", "arbitrary")))(q, k, v, do, lse)


ELEM_BLOCK_BYTES = 2 << 20

BIG = (("w_in", True), ("w_out_conv", False), ("w_out_attn", False), ("w_merge", False), ("w_up", True), ("w_down", False))
N_BIG = len(BIG)


def _elem_tiles(rows, width):
    tc = _tile(width, 2048, LANES)
    tr = _tile(rows, max(8, ELEM_BLOCK_BYTES // (4 * tc)), 8)
    return tr, tc


def _scalar_grid(grid, in_specs, out_specs):
    return pltpu.PrefetchScalarGridSpec(num_scalar_prefetch=1, grid=grid, in_specs=in_specs, out_specs=out_specs)


def _cast_place(w_stack, layer, chip, col_sharded, name):
    _, rows, width = w_stack.shape
    tr, tc = _elem_tiles(rows, width)
    nr, nc = rows // tr, width // tc

    def body(sc_ref, x_ref, o_ref):
        o_ref[...] = x_ref[...].astype(o_ref.dtype)

    if col_sharded:
        full, out_spec = (rows, width * N_CHIPS), pl.BlockSpec((tr, tc), lambda i, j, sc: (i, sc[0] * nc + j))
    else:
        full, out_spec = (rows * N_CHIPS, width), pl.BlockSpec((tr, tc), lambda i, j, sc: (sc[0] * nr + i, j))
    return pl.pallas_call(
        body, name=name,
        grid_spec=_scalar_grid((nr, nc), [pl.BlockSpec((None, tr, tc), lambda i, j, sc: (layer, i, j))], out_spec),
        out_shape=_sds(full, BF16), compiler_params=_params(("parallel", "parallel")))(chip, w_stack)


def _add_landed(acc, landed):
    return (acc + landed,)


def _slot_of_relation(rel):
    return jnp.where(rel == 2, 0, jnp.where(rel == 1, 1, 2))


def _sum_chips(pair_sum, landed, chip, col_sharded, name):
    _, rows, width = landed.shape
    tr, tc = _elem_tiles(rows, width)
    nr, nc = rows // tr, width // tc

    def body(chip_ref, own_ref, q_ref, o_ref):
        me = chip_ref[0]
        own = own_ref[...].astype(F32)
        acc = None
        for t in range(N_CHIPS):
            rel = me ^ t
            term = jnp.where(rel == 0, own, q_ref[_slot_of_relation(rel)].astype(F32))
            acc = term if acc is None else acc + term
        o_ref[...] = acc

    if col_sharded:
        own_spec = pl.BlockSpec((tr, tc), lambda i, j, c: (i, c[0] * nc + j))
    else:
        own_spec = pl.BlockSpec((tr, tc), lambda i, j, c: (c[0] * nr + i, j))
    return pl.pallas_call(
        body, name=name,
        grid_spec=_scalar_grid((nr, nc), [own_spec, pl.BlockSpec((N_CHIPS - 1, tr, tc), lambda i, j, c: (0, i, j))],
                               pl.BlockSpec((tr, tc), lambda i, j, c: (i, j))),
        out_shape=_sds((rows, width), F32), compiler_params=_params(("parallel", "parallel")))(chip, pair_sum, landed)


def _adamw_math(w, g, m, v):
    mn = ADAM_B1 * m + (1.0 - ADAM_B1) * g
    vn = ADAM_B2 * v + (1.0 - ADAM_B2) * jnp.square(g)
    m_hat = mn / (1.0 - ADAM_B1 ** ADAM_STEP)
    v_hat = vn / (1.0 - ADAM_B2 ** ADAM_STEP)
    return -ADAM_LR * (m_hat / (jnp.sqrt(v_hat) + ADAM_EPS) + ADAM_WD * w), mn, vn


def _adamw(w, g, m, v, name):
    shape = w.shape
    width = shape[-1]
    w2, g2, m2, v2 = (a.reshape(-1, width) for a in (w, g, m, v))
    rows = w2.shape[0]
    tr, tc = _elem_tiles(rows, width)

    def body(w_ref, g_ref, m_ref, v_ref, d_ref, nm_ref, nv_ref):
        d_ref[...], nm_ref[...], nv_ref[...] = _adamw_math(w_ref[...], g_ref[...], m_ref[...], v_ref[...])

    tile = pl.BlockSpec((tr, tc), lambda i, j: (i, j))
    outs = pl.pallas_call(body, name=name, grid=(rows // tr, width // tc), in_specs=[tile] * 4, out_specs=[tile] * 3,
                          out_shape=[_sds((rows, width), F32)] * 3, compiler_params=_params(("parallel", "parallel")))(w2, g2, m2, v2)
    return tuple(o.reshape(shape) for o in outs)


def _adamw_layer(w, m, v, g_mine, g_sibling, carried, layer, core, col_sharded, name):
    depth, rows, width = w.shape
    pr, pc = g_mine.shape
    tr, tc = _elem_tiles(pr, pc)
    n_half = pr // tr if col_sharded else pc // tc
    if carried is None:
        carried = tuple(lax.empty((depth, rows, width), F32) for _ in range(4))

    def body(sc_ref, w_ref, m_ref, v_ref, gm_ref, gs_ref, *rest):
        g_ref, d_ref, nm_ref, nv_ref = rest[-4:]
        pos = pl.program_id(0) if col_sharded else pl.program_id(1)
        gv = jnp.where(pos // n_half == sc_ref[0], gm_ref[...], gs_ref[...])
        g_ref[...] = gv
        d_ref[...], nm_ref[...], nv_ref[...] = _adamw_math(w_ref[...], gv, m_ref[...], v_ref[...])

    stacked = pl.BlockSpec((None, tr, tc), lambda i, j, sc: (layer, i, j))
    n_cols = width // tc

    def half_spec(mine):
        def index(i, j, sc):
            own = sc[0] if mine else 1 - sc[0]
            pos = i if col_sharded else j
            used, before = pos // n_half == own, pos // n_half < own
            within = jnp.where(used, pos % n_half, jnp.where(before, 0, n_half - 1))
            if col_sharded:
                return within, jnp.where(used, j, jnp.where(before, 0, n_cols - 1))
            return i, within
        return pl.BlockSpec((tr, tc), index)

    return pl.pallas_call(
        body, name=name,
        grid_spec=_scalar_grid((rows // tr, width // tc), [stacked] * 3 + [half_spec(True), half_spec(False)] + [ANY] * 4, [stacked] * 4),
        out_shape=[_sds((depth, rows, width), F32)] * 4, input_output_aliases={6: 0, 7: 1, 8: 2, 9: 3},
        compiler_params=_params(("parallel", "parallel")))(core, w, m, v, g_mine, g_sibling, *carried)


_SEM = pl.BlockSpec(memory_space=pltpu.SEMAPHORE)
_HBM = pl.BlockSpec(memory_space=pltpu.HBM)
_VMEM = pl.BlockSpec(memory_space=pltpu.VMEM)
_EFFECT = pltpu.SideEffectType.DATAFLOW_SIDE_EFFECTING


def _place():
    x, y, c = lax.axis_index("x"), lax.axis_index("y"), lax.axis_index("c")
    others = [(1 - x, y), (x, 1 - y), (1 - x, 1 - y)]
    return x, y, c, others


def _chip_index(px, py):
    return 2 * px + py


def _remote(src, dst, send_sems, recv_sems, k, to):
    return pltpu.make_async_remote_copy(src_ref=src, dst_ref=dst, send_sem=send_sems.at[k], recv_sem=recv_sems.at[k],
                                        device_id=to, device_id_type=MESH)


def _phase(name, bufs, waits, wait_fn, n_start, start_fn, deps):
    nb, nd = len(bufs), len(deps)

    def body(*refs):
        buf_refs = refs[:nb]
        pos = nb
        if waits is not None:
            wait_fn(buf_refs, refs[pos], refs[pos + 1])
            pos += 2
        pos += nd
        if n_start:
            start_fn(buf_refs, refs[pos], refs[pos + 1])
            pos += 2
        token = refs[pos + nb]
        token[...] = jnp.zeros_like(token)

    n_sem_out = 2 if n_start else 0
    if waits is None:
        bufs = [pltpu.with_memory_space_constraint(b, pltpu.HBM) for b in bufs]
    outs = pl.pallas_call(
        body, name=name,
        in_specs=[_HBM] * nb + ([_SEM] * 2 if waits is not None else []) + [ANY] * nd,
        out_specs=[_SEM] * n_sem_out + [_HBM] * nb + [_VMEM],
        out_shape=[pltpu.SemaphoreType.DMA((n_start,))] * n_sem_out + [pltpu.HBM(b.shape, b.dtype) for b in bufs] + [_sds((8, LANES), F32)],
        input_output_aliases={i: n_sem_out + i for i in range(nb)},
        compiler_params=pltpu.CompilerParams(has_side_effects=_EFFECT),
    )(*bufs, *(waits if waits is not None else ()), *deps)
    sems = tuple(outs[:2]) if n_start else None
    return sems, list(outs[n_sem_out:n_sem_out + nb]), outs[-1]


class _Chains:
    def __init__(self):
        self.active = []
        self.last = None
        self.dep = None
        self.pinned = []
        self.token = None
        self.at = None

    def phase(self, name, bufs, waits, wait_fn, n_start, start_fn):
        deps = [a for a in (self.last, self.dep) if a is not None] + self.pinned
        sems, thru, token = _phase(name, bufs, waits, wait_fn, n_start, start_fn, deps)
        self.last, self.dep, self.token, self.pinned = token, None, token, []
        return sems, thru

    def pin(self, result):
        self.pinned.append(result)

    def add(self, gen):
        self.active.append(gen)

    def tick(self, dep, at=None):
        self.at, self.token, self.dep = at, None, dep
        for gen in list(self.active):
            if next(gen, "done") == "done":
                self.active.remove(gen)
        return self.token


def _after(small, token):
    return small if token is None else small + token[0, 0]


def _shard_region(ref, col_sharded, chip, n_shard):
    start = pl.multiple_of(chip * n_shard, LANES if col_sharded else 8)
    if col_sharded:
        return ref.at[:, pl.ds(start, n_shard)]
    return ref.at[pl.ds(start, n_shard), :]


def _row_half(ref, half):
    n_rows = ref.shape[0]
    return ref.at[pl.ds(pl.multiple_of(half * (n_rows // 2), 8), n_rows // 2), :]


def _gather_chain(chains, tag, group, bufs, out, wait_for):
    n_w = len(group)
    n = 3 * n_w

    def region(refs, a, chip, half):
        col = BIG[group[a]][1]
        n_shard = refs[a].shape[1] // N_CHIPS if col else refs[a].shape[0] // N_CHIPS
        return _row_half(_shard_region(refs[a], col, chip, n_shard), half)

    def start_ici(refs, send, recv):
        x, y, c, others = _place()
        for a in range(n_w):
            mine = region(refs, a, _chip_index(x, y), c)
            for j, (ox, oy) in enumerate(others):
                _remote(mine, mine, send, recv, 3 * a + j, (ox, oy, c)).start()

    def wait_ici(refs, send, recv):
        x, y, c, others = _place()
        for a in range(n_w):
            for j, (ox, oy) in enumerate(others):
                landed = region(refs, a, _chip_index(ox, oy), c)
                cp = _remote(landed, landed, send, recv, 3 * a + j, (x, y, 1 - c))
                cp.wait_recv()
                cp.wait_send()

    def start_d2d(refs, send, recv):
        x, y, c, others = _place()
        for a in range(n_w):
            for j, (ox, oy) in enumerate(others):
                landed = region(refs, a, _chip_index(ox, oy), c)
                _remote(landed, landed, send, recv, 3 * a + j, (x, y, 1 - c)).start()

    def wait_d2d(refs, send, recv):
        x, y, c, others = _place()
        for a in range(n_w):
            for j, (ox, oy) in enumerate(others):
                theirs = region(refs, a, _chip_index(ox, oy), 1 - c)
                cp = _remote(theirs, theirs, send, recv, 3 * a + j, (x, y, 1 - c))
                cp.wait_recv()
                cp.wait_send()

    sems, bufs = chains.phase(f"gather_ici_start_{tag}", bufs, None, None, n, start_ici)
    yield
    while wait_for is not None and chains.at != wait_for:
        yield
    sems, bufs = chains.phase(f"gather_forward_{tag}", bufs, sems, wait_ici, n, start_d2d)
    yield
    _, bufs = chains.phase(f"gather_done_{tag}", bufs, sems, wait_d2d, 0, None)
    for a, buf in zip(group, bufs):
        out[BIG[a][0]] = buf


def _grad_chain(chains, layer, group, pairs, core, other_core, chip, w, mom, var, carried):
    n = len(group)
    tag = f"{layer}_{group[0]}"
    kinds = [BIG[a][1] for a in group]
    names = [BIG[a][0] for a in group]
    sibling_of = lambda x, y, c: (x, y, 1 - c)

    def pair_start(refs, send, recv):
        x, y, c, _ = _place()
        for i in range(n):
            _remote(refs[i], refs[n + i], send, recv, i, sibling_of(x, y, c)).start()

    def pair_wait(refs, send, recv):
        x, y, c, _ = _place()
        for i in range(n):
            cp = _remote(refs[i], refs[n + i], send, recv, i, sibling_of(x, y, c))
            cp.wait_recv()
            cp.wait_send()

    def piece(ref, col, chip_idx):
        return _shard_region(ref, col, chip_idx, ref.shape[1] // N_CHIPS if col else ref.shape[0] // N_CHIPS)

    def scatter_start(refs, send, recv):
        x, y, c, others = _place()
        for i in range(n):
            for j, (ox, oy) in enumerate(others):
                _remote(piece(refs[i], kinds[i], _chip_index(ox, oy)), refs[n + i].at[j], send, recv, 3 * i + j, (ox, oy, c)).start()

    def scatter_wait(refs, send, recv):
        x, y, c, others = _place()
        for i in range(n):
            for j, (ox, oy) in enumerate(others):
                cp = _remote(piece(refs[i], kinds[i], _chip_index(ox, oy)), refs[n + i].at[j], send, recv, 3 * i + j, (ox, oy, c))
                cp.wait_recv()
                cp.wait_send()

    def join_start(refs, send, recv):
        x, y, c, _ = _place()
        for i in range(n):
            _remote(refs[i], refs[n + i], send, recv, i, sibling_of(x, y, c)).start()

    def join_wait(refs, send, recv):
        x, y, c, _ = _place()
        for i in range(n):
            cp = _remote(refs[i], refs[n + i], send, recv, i, sibling_of(x, y, c))
            cp.wait_recv()
            cp.wait_send()

    sides = ["m" if col else "n" for col in kinds]
    whole = [dict(tk=pairs[i][0].shape[0], **({"tn": 512} if sides[i] == "m" else {"tm": 512})) for i in range(n)]
    sends = [_mm(a, b, "tn", [F32], name="mm_g_send_" + names[i], half=(other_core, sides[i]), **whole[i]) for i, (a, b) in enumerate(pairs)]
    half_shapes = [g.shape for g in sends]
    lands = [lax.empty(s, F32) for s in half_shapes]
    sems, bufs = chains.phase(f"pair_start_{tag}", sends + lands, None, None, n, pair_start)
    yield
    _, bufs = chains.phase(f"pair_wait_{tag}", bufs, sems, pair_wait, 0, None)
    pair_sums = [_mm(a, b, "tn", [BF16], name="mm_g_keep_" + names[i], half=(core, sides[i]), epilogue=_add_landed, extras=(bufs[n + i],), **whole[i])
                 for i, (a, b) in enumerate(pairs)]
    piece_shapes = [(s[0], s[1] // N_CHIPS) if col else (s[0] // N_CHIPS, s[1]) for col, s in zip(kinds, half_shapes)]
    slots = [lax.empty((N_CHIPS - 1, *s), BF16) for s in piece_shapes]
    sems, bufs = chains.phase(f"scatter_start_{tag}", pair_sums + slots, None, None, 3 * n, scatter_start)
    yield
    yield
    _, bufs = chains.phase(f"scatter_wait_{tag}", bufs, sems, scatter_wait, 0, None)
    reduced = [_sum_chips(bufs[i], bufs[n + i], chip, kinds[i], "sum_chips_" + names[i]) for i in range(n)]
    theirs = [lax.empty(s, F32) for s in piece_shapes]
    sems, bufs = chains.phase(f"join_start_{tag}", reduced + theirs, None, None, n, join_start)
    yield
    _, bufs = chains.phase(f"join_wait_{tag}", bufs, sems, join_wait, 0, None)
    for i in range(n):
        carried[names[i]] = _adamw_layer(w[names[i]], mom[names[i]], var[names[i]], bufs[i], bufs[n + i], carried.get(names[i]),
                                         layer, core, kinds[i], "adamw_" + names[i])
        chains.pin(carried[names[i]][0])


def _allreduce_small(vec, name):
    rows = vec.shape[0]
    masks = [(dx, dy, dc) for dx in (0, 1) for dy in (0, 1) for dc in (0, 1)][1:]

    def body(v_ref, o_ref, gather_ref, send_sems, recv_sems):
        x, y, c, _ = _place()
        me = 4 * x + 2 * y + c
        gather_ref[me] = v_ref[...]
        copies = []
        for k, (dx, dy, dc) in enumerate(masks):
            peer = (x ^ dx, y ^ dy, c ^ dc)
            copies.append(_remote(v_ref, gather_ref.at[me], send_sems, recv_sems, k, peer))
        for cp in copies:
            cp.start()
        for k, (dx, dy, dc) in enumerate(masks):
            slot = gather_ref.at[4 * (x ^ dx) + 2 * (y ^ dy) + (c ^ dc)]
            _remote(slot, slot, send_sems, recv_sems, k, (x, y, c)).wait_recv()
        for cp in copies:
            cp.wait_send()
        acc = gather_ref[0]
        for dev in range(1, N_DEV):
            acc = acc + gather_ref[dev]
        o_ref[...] = acc

    return pl.pallas_call(
        body, name=name, in_specs=[_VMEM], out_specs=_VMEM, out_shape=_sds((rows, LANES), F32),
        scratch_shapes=[pltpu.VMEM((N_DEV, rows, LANES), F32), pltpu.SemaphoreType.DMA((N_DEV - 1,)), pltpu.SemaphoreType.DMA((N_DEV - 1,))],
        compiler_params=pltpu.CompilerParams(has_side_effects=True, vmem_limit_bytes=VMEM_LIMIT_BYTES),
    )(vec)


def _relu2(acc):
    r = jnp.maximum(acc, 0.0)
    return acc, r * r


def _relu2_bwd(acc, up):
    return (acc * (2.0 * jnp.maximum(up.astype(F32), 0.0)),)


def _layer_fwd(x, h, w, cols, tables, tick, last, target=None):
    cos_t, sin_t = tables
    z = _mm(h, w("w_in"), "nn", [BF16], name="mm_in")
    a = _conv_fwd(z, w("conv_w"), cols)
    q, k, v = _rope_fwd(z, _after(w("q_norm"), tick(a, 0)), w("k_norm"), cos_t, sin_t, cols)
    o, lse = _attn_fwd(q, k, v)
    tick(o, 1)
    tick(None, 2)
    y_a = _mm(a, w("w_out_conv"), "nn", [BF16], name="mm_out_conv")
    y_b = _mm(o, w("w_out_attn"), "nn", [BF16], name="mm_out_attn")
    mix = _gate_fwd(z, w("gate_bias"), y_a, y_b, cols)
    mixed = _mm(mix, w("w_merge"), "nn", [BF16], name="mm_merge")
    x1, h2 = _resid_norm(x, mixed, _after(w("norm_mix_post"), tick(mixed, 3)), w("norm_mlp_pre"))
    up, act = _mm(h2, w("w_up"), "nn", [BF16, BF16], name="mm_up", epilogue=_relu2)
    f = _mm(act, w("w_down"), "nn", [BF16], name="mm_down", tk=4096)
    kept = dict(x=x, h=h, z=z, a=a, q=q, k=k, v=v, o=o, lse=lse, y_a=y_a, y_b=y_b, mix=mix, mixed=mixed, x1=x1, h2=h2, up=up, act=act, f=f)
    g_post = _after(w("norm_mlp_post"), tick(f, 4))
    if last:
        return _resid_norm_loss(x1, f, g_post, target), kept
    return _resid_norm(x1, f, g_post, w("norm_next")), kept


def _layer_bwd(dx_out, w, kept, cols, tables, tick, emit):
    cos_t, sin_t = tables
    t = kept
    df, d_norm_mlp_post = _norm_bwd(t["f"], _after(w["norm_mlp_post"], tick(dx_out, 0)), dx_out, None, BF16, "norm_bwd_mlp_post")
    dup = _mm(df, w["w_down"], "nt", [BF16], name="mm_d_act", epilogue=_relu2_bwd, extras=(t["up"],))
    emit((4, 5), [(t["h2"], dup), (t["act"], df)])
    dh2 = _mm(dup, w["w_up"], "nt", [BF16], name="mm_d_h2", tk=4096)
    dx1, d_norm_mlp_pre = _norm_bwd(t["x1"], _after(w["norm_mlp_pre"], tick(dh2, 1)), dh2, dx_out, F32, "norm_bwd_mlp_pre")
    dmixed, d_norm_mix_post = _norm_bwd(t["mixed"], w["norm_mix_post"], dx1, None, BF16, "norm_bwd_mix_post")
    dmix = _mm(dmixed, w["w_merge"], "nt", [BF16], name="mm_d_mix")
    dy_a, dy_b, dz_ga, dz_gb, dbias_a, dbias_b = _gate_bwd(t["z"], _after(w["gate_bias"], tick(dmix, 2)), t["y_a"], t["y_b"], dmix, cols)
    da = _mm(dy_a, w["w_out_conv"], "nt", [BF16], name="mm_d_a")
    emit((1, 2, 3), [(t["a"], dy_a), (t["o"], dy_b), (t["mix"], dmixed)])
    do = _mm(dy_b, w["w_out_attn"], "nt", [BF16], name="mm_d_o")
    dz_cb, dz_cc, dz_ci, d_conv_w = _conv_bwd(t["z"], _after(w["conv_w"], tick(do, 3)), da, cols)
    dq, dk_t, dv_t = _attn_bwd(t["q"], t["k"], t["v"], do, t["lse"])
    dz_q, dz_k, dz_v, d_q_norm, d_k_norm = _rope_bwd(t["z"], _after(w["q_norm"], tick(dv_t, 4)), w["k_norm"], cos_t, sin_t, dq, dk_t, dv_t, cols)
    dz = jnp.concatenate([dz_cb, dz_cc, dz_ci, dz_q, dz_k, dz_v, dz_ga, dz_gb], axis=1)
    emit((0,), [(t["h"], dz)])
    tick(dz, 5)
    dh = _mm(dz, w["w_in"], "nt", [BF16], name="mm_d_h", tk=3328)
    dx_in, d_norm_mix_pre = _norm_bwd(t["x"], _after(w["norm_mix_pre"], tick(dh, 6)), dh, dx1, F32, "norm_bwd_mix_pre")
    small = dict(norm_mix_pre=d_norm_mix_pre, gate_bias=jnp.concatenate([dbias_a, dbias_b], axis=1), conv_w=d_conv_w, q_norm=d_q_norm,
                 k_norm=d_k_norm, norm_mix_post=d_norm_mix_post, norm_mlp_pre=d_norm_mlp_pre, norm_mlp_post=d_norm_mlp_post)
    return dx_in, small


SMALL = ("norm_mix_pre", "gate_bias", "conv_w", "q_norm", "k_norm", "norm_mix_post", "norm_mlp_pre", "norm_mlp_post")
WEIGHTS = ("norm_mix_pre", "w_in", "gate_bias", "conv_w", "q_norm", "k_norm", "w_out_conv", "w_out_attn", "w_merge",
           "norm_mix_post", "norm_mlp_pre", "w_up", "w_down", "norm_mlp_post")


def _pack(parts):
    flat = jnp.concatenate([a.reshape(-1) for a in parts])
    rows = -(-flat.shape[0] // LANES)
    pad = (-rows) % 8
    flat = jnp.pad(flat, (0, (rows + pad) * LANES - flat.shape[0]))
    return flat.reshape(rows + pad, LANES)


def _unpack(packed, shapes):
    flat = packed.reshape(-1)
    out, off = [], 0
    for shp in shapes:
        n = math.prod(shp)
        out.append(flat[off:off + n].reshape(shp))
        off += n
    return out


def kernel(x, norm_mix_pre, w_in, gate_bias, conv_w, q_norm, k_norm, w_out_conv, w_out_attn, w_merge, norm_mix_post, norm_mlp_pre, w_up, w_down, norm_mlp_post, loss_target, m_norm_mix_pre, m_w_in, m_gate_bias, m_conv_w, m_q_norm, m_k_norm, m_w_out_conv, m_w_out_attn, m_w_merge, m_norm_mix_post, m_norm_mlp_pre, m_w_up, m_w_down, m_norm_mlp_post, v_norm_mix_pre, v_w_in, v_gate_bias, v_conv_w, v_q_norm, v_k_norm, v_w_out_conv, v_w_out_attn, v_w_merge, v_norm_mix_post, v_norm_mlp_pre, v_w_up, v_w_down, v_norm_mlp_post):
    w = dict(norm_mix_pre=norm_mix_pre, w_in=w_in, gate_bias=gate_bias, conv_w=conv_w, q_norm=q_norm, k_norm=k_norm, w_out_conv=w_out_conv,
             w_out_attn=w_out_attn, w_merge=w_merge, norm_mix_post=norm_mix_post, norm_mlp_pre=norm_mlp_pre, w_up=w_up, w_down=w_down,
             norm_mlp_post=norm_mlp_post)
    mom = dict(norm_mix_pre=m_norm_mix_pre, w_in=m_w_in, gate_bias=m_gate_bias, conv_w=m_conv_w, q_norm=m_q_norm, k_norm=m_k_norm,
               w_out_conv=m_w_out_conv, w_out_attn=m_w_out_attn, w_merge=m_w_merge, norm_mix_post=m_norm_mix_post, norm_mlp_pre=m_norm_mlp_pre,
               w_up=m_w_up, w_down=m_w_down, norm_mlp_post=m_norm_mlp_post)
    var = dict(norm_mix_pre=v_norm_mix_pre, w_in=v_w_in, gate_bias=v_gate_bias, conv_w=v_conv_w, q_norm=v_q_norm, k_norm=v_k_norm,
               w_out_conv=v_w_out_conv, w_out_attn=v_w_out_attn, w_merge=v_w_merge, norm_mix_post=v_norm_mix_post, norm_mlp_pre=v_norm_mlp_pre,
               w_up=v_w_up, w_down=v_w_down, norm_mlp_post=v_norm_mlp_post)
    depth = w_in.shape[0]
    _, s, d = x.shape
    cols = _Cols(d)
    x0 = x.reshape(s, d)
    target = loss_target.reshape(s, d)
    tables = _rope_tables(s)
    chip = (2 * lax.axis_index("x") + lax.axis_index("y")).astype(jnp.int32)
    core = lax.axis_index("c").astype(jnp.int32)
    chip_vec, core_vec, other_core_vec = chip.reshape(1), core.reshape(1), (1 - core).reshape(1)

    n_conv = conv_w.shape[-1]
    placed = lax.dynamic_update_slice_in_dim(jnp.zeros((depth, conv_w.shape[1], n_conv * N_CHIPS), F32), conv_w, chip * n_conv, axis=2)
    conv_full = _unpack(_allreduce_small(_pack([jnp.where(core == 0, placed, 0.0)]), "gather_conv_w"), [placed.shape])[0]

    chains = _Chains()
    chains.pin(conv_full)
    full = [{} for _ in range(depth)]
    everything = tuple(range(N_BIG))
    gathers = []
    for l in range(depth):
        bufs = [_cast_place(w[name], l, chip_vec, col, "cast_place_" + name) for name, col in BIG]
        for group in ((0,), everything[1:]):
            if group[0] == 0:
                wait_for = ("fwd", l - 1, 3) if l else None
            else:
                wait_for = ("fwd", l, 0 if l else 1)
            gathers.append(_gather_chain(chains, f"{l}_{group[0]}", group, [bufs[a] for a in group], full[l], wait_for))
            next(gathers[-1])
    next(gathers[0])
    next(gathers[0], None)
    for g in gathers[1:]:
        chains.add(g)

    def layer_params(l):
        p = dict(full[l])
        p["conv_w"] = conv_full[l]
        p["gate_bias"] = gate_bias[l].reshape(1, -1)
        for name in ("norm_mix_pre", "q_norm", "k_norm", "norm_mix_post", "norm_mlp_pre", "norm_mlp_post"):
            p[name] = w[name][l].reshape(1, -1)
        if l + 1 < depth:
            p["norm_next"] = w["norm_mix_pre"][l + 1].reshape(1, -1)
        return p

    kept = []
    xl, h = x0, _norm_first(x0, norm_mix_pre[0].reshape(1, -1))
    for l in range(depth):
        last = l == depth - 1
        (xl, h), t = _layer_fwd(xl, h, lambda name, l=l: layer_params(l)[name], cols, tables,
                                lambda dep, k, l=l: chains.tick(dep, ("fwd", l, k)), last, target if last else None)
        kept.append(t)
    dx, loss_local = xl, h

    carried = {}
    small = [None] * depth
    for l in reversed(range(depth)):
        def emit(group, pairs, l=l):
            chains.add(_grad_chain(chains, l, group, pairs, core_vec, other_core_vec, chip_vec, w, mom, var, carried))

        dx, small[l] = _layer_bwd(dx, layer_params(l), kept[l], cols, tables, lambda dep, k: chains.tick(dep), emit)
    while chains.active:
        chains.tick(None)
    grads = {name: carried[name][0] for name, _ in BIG}
    delta = {name: carried[name][1] for name, _ in BIG}
    new_m = {name: carried[name][2] for name, _ in BIG}
    new_v = {name: carried[name][3] for name, _ in BIG}

    small_full_shapes = [(depth,) + small[0][name].shape for name in SMALL]
    packed = _pack([jnp.stack([small[l][name] for l in range(depth)]) for name in SMALL] + [jnp.broadcast_to(loss_local.reshape(1), (LANES,))])
    small_sum = _unpack(_allreduce_small(packed, "allreduce_small"), small_full_shapes + [(LANES,)])
    loss = small_sum[-1][0]
    for name, g in zip(SMALL, small_sum[:-1]):
        if name == "conv_w":
            g = lax.dynamic_slice_in_dim(g, chip * n_conv, n_conv, axis=2)
        grads[name] = g.reshape(w[name].shape)
    small_shapes = [w[name].shape for name in SMALL]
    packs = [_pack([src[name] for name in SMALL]) for src in (w, grads, mom, var)]
    for dst, out in zip((delta, new_m, new_v), _adamw(*packs, "adamw_small")):
        for name, val in zip(SMALL, _unpack(out, small_shapes)):
            dst[name] = val

    grad_x = dx.reshape(x.shape)
    return (loss, grad_x, *[grads[n] for n in WEIGHTS], *[delta[n] for n in WEIGHTS], *[new_m[n] for n in WEIGHTS], *[new_v[n] for n in WEIGHTS])
```

```python
import math

import jax
import jax.numpy as jnp
from jax import lax
from jax.experimental import pallas as pl
from jax.experimental.pallas import tpu as pltpu

F32 = jnp.float32
BF16 = jnp.bfloat16

HEAD_DIM = 128
GROUP = 4
GRID_W = 64
ROPE_THETA = 10000.0
RMS_EPS = 1e-6
ADAM_LR = 0.001
ADAM_B1 = 0.9
ADAM_B2 = 0.999
ADAM_EPS = 1e-08
ADAM_WD = 0.01
ADAM_STEP = 10

LANES = 128
N_CHIPS = 4
N_DEV = 8
VMEM_LIMIT_BYTES = 56 * 1024 * 1024
MESH = pl.DeviceIdType.MESH
ANY = pl.BlockSpec(memory_space=pl.ANY)


def _tile(dim, cap, mult):
    if dim <= cap:
        return dim
    t = (cap // mult) * mult
    while t >= mult:
        if dim % t == 0:
            return t
        t -= mult
    raise ValueError(f"no tile for {dim} under {cap} in multiples of {mult}")


def _params(sem=None):
    return pltpu.CompilerParams(dimension_semantics=sem, vmem_limit_bytes=VMEM_LIMIT_BYTES)


def _sds(shape, dtype):
    return jax.ShapeDtypeStruct(tuple(shape), dtype)


def _rstd(x):
    return lax.rsqrt(jnp.mean(x * x, axis=-1, keepdims=True) + RMS_EPS)


_DOT_DIMS = {"nn": ((1,), (0,)), "nt": ((1,), (1,)), "tn": ((0,), (0,))}


def _mm(a, b, mode, out_dtypes, *, name, epilogue=None, extras=(), tm=1024, tn=1024, tk=2048, half=None):
    if mode == "nn":
        (m, k), (k2, n) = a.shape, b.shape
    elif mode == "nt":
        (m, k), (n, k2) = a.shape, b.shape
    else:
        (k, m), (k2, n) = a.shape, b.shape
    assert k == k2, (a.shape, b.shape, mode)
    side = half[1] if half is not None else None
    if side == "m":
        m //= 2
    elif side == "n":
        n //= 2
    tm, tn, tk = _tile(m, tm, 8), _tile(n, tn, LANES), _tile(k, tk, LANES)
    nk = k // tk
    row = (lambda i, s: i + s[0][0] * (m // tm)) if side == "m" else (lambda i, s: i)
    col = (lambda j, s: j + s[0][0] * (n // tn)) if side == "n" else (lambda j, s: j)
    a_spec = pl.BlockSpec((tk, tm), lambda i, j, kk, *s: (kk, row(i, s))) if mode == "tn" else pl.BlockSpec((tm, tk), lambda i, j, kk, *s: (row(i, s), kk))
    b_spec = pl.BlockSpec((tn, tk), lambda i, j, kk, *s: (col(j, s), kk)) if mode == "nt" else pl.BlockSpec((tk, tn), lambda i, j, kk, *s: (kk, col(j, s)))
    tile_spec = pl.BlockSpec((tm, tn), lambda i, j, kk, *s: (i, j))
    n_extra, n_out = len(extras), len(out_dtypes)
    dims = (_DOT_DIMS[mode], ((), ()))

    def body(*refs):
        if half is not None:
            refs = refs[1:]
        a_ref, b_ref = refs[:2]
        extra_refs = refs[2:2 + n_extra]
        out_refs = refs[2 + n_extra:2 + n_extra + n_out]
        part = lax.dot_general(a_ref[...].astype(BF16), b_ref[...].astype(BF16), dims, preferred_element_type=F32)

        def finish(total):
            res = epilogue(total, *[e[...] for e in extra_refs]) if epilogue is not None else (total,)
            for o, r in zip(out_refs, res):
                o[...] = r.astype(o.dtype)

        if nk == 1:
            finish(part)
        else:
            acc = refs[-1]
            kk = pl.program_id(2)

            @pl.when(kk == 0)
            def _():
                acc[...] = part

            @pl.when(kk > 0)
            def _():
                acc[...] += part

            @pl.when(kk == nk - 1)
            def _():
                finish(acc[...])

    grid = (m // tm, n // tn, nk)
    in_specs, out_specs = [a_spec, b_spec] + [tile_spec] * n_extra, [tile_spec] * n_out
    scratch = [pltpu.VMEM((tm, tn), F32)] if nk > 1 else []
    if half is None:
        layout, lead = dict(grid=grid, in_specs=in_specs, out_specs=out_specs, scratch_shapes=scratch), ()
    else:
        layout = dict(grid_spec=pltpu.PrefetchScalarGridSpec(num_scalar_prefetch=1, grid=grid, in_specs=in_specs, out_specs=out_specs,
                                                             scratch_shapes=scratch))
        lead = (half[0],)
    outs = pl.pallas_call(body, name=name, out_shape=[_sds((m, n), d) for d in out_dtypes],
                          compiler_params=_params(("parallel", "parallel", "arbitrary")), **layout)(*lead, a, b, *extras)
    return outs if n_out > 1 else outs[0]


ROW_TILE = 512
GATE_ROW_TILE = 1024


def _norm_first(x, g):
    s, d = x.shape
    ts = _tile(s, ROW_TILE, 8)

    def body(x_ref, g_ref, h_ref):
        xv = x_ref[...]
        h_ref[...] = (xv * _rstd(xv) * g_ref[...]).astype(h_ref.dtype)

    row = pl.BlockSpec((ts, d), lambda i: (i, 0))
    vec = pl.BlockSpec((1, d), lambda i: (0, 0))
    return pl.pallas_call(body, name="norm_first", grid=(s // ts,), in_specs=[row, vec], out_specs=row,
                          out_shape=_sds((s, d), BF16), compiler_params=_params(("parallel",)))(x, g)


def _resid_norm(xres, y, g_post, g_next):
    s, d = xres.shape
    ts = _tile(s, ROW_TILE, 8)

    def body(x_ref, y_ref, gp_ref, gn_ref, xn_ref, hn_ref):
        yv = y_ref[...].astype(F32)
        xn = x_ref[...] + yv * _rstd(yv) * gp_ref[...]
        xn_ref[...] = xn
        hn_ref[...] = (xn * _rstd(xn) * gn_ref[...]).astype(hn_ref.dtype)

    row = pl.BlockSpec((ts, d), lambda i: (i, 0))
    vec = pl.BlockSpec((1, d), lambda i: (0, 0))
    return pl.pallas_call(body, name="resid_norm", grid=(s // ts,), in_specs=[row, row, vec, vec], out_specs=[row, row],
                          out_shape=[_sds((s, d), F32), _sds((s, d), BF16)], compiler_params=_params(("parallel",)))(xres, y, g_post, g_next)


def _resid_norm_loss(xres, y, g_post, target):
    s, d = xres.shape
    ts = _tile(s, ROW_TILE, 8)
    n_steps = s // ts

    def body(x_ref, y_ref, gp_ref, t_ref, dout_ref, loss_ref, acc_ref):
        i = pl.program_id(0)
        yv = y_ref[...].astype(F32)
        err = x_ref[...] + yv * _rstd(yv) * gp_ref[...] - t_ref[...]
        dout_ref[...] = err / d
        part = jnp.sum(err * err, axis=0, keepdims=True)

        @pl.when(i == 0)
        def _():
            acc_ref[...] = part

        @pl.when(i > 0)
        def _():
            acc_ref[...] += part

        @pl.when(i == n_steps - 1)
        def _():
            loss_ref[...] = 0.5 * jnp.sum(acc_ref[...], axis=1, keepdims=True) / d

    row = pl.BlockSpec((ts, d), lambda i: (i, 0))
    vec = pl.BlockSpec((1, d), lambda i: (0, 0))
    one = pl.BlockSpec((1, 1), lambda i: (0, 0))
    return pl.pallas_call(body, name="resid_norm_loss", grid=(n_steps,), in_specs=[row, row, vec, row], out_specs=[row, one],
                          out_shape=[_sds((s, d), F32), _sds((1, 1), F32)], scratch_shapes=[pltpu.VMEM((1, d), F32)],
                          compiler_params=_params(("arbitrary",)))(xres, y, g_post, target)


def _norm_bwd(xin, g, dout, dres, out_dtype, name):
    s, d = xin.shape
    ts = _tile(s, ROW_TILE, 8)
    has_res = dres is not None

    def body(*refs):
        x_ref, g_ref, do_ref = refs[:3]
        dx_ref, dg_ref = refs[-2:]
        i = pl.program_id(0)
        xv, dov = x_ref[...].astype(F32), do_ref[...].astype(F32)
        r = _rstd(xv)
        xhat = xv * r
        dg = jnp.sum(dov * xhat, axis=0, keepdims=True)
        dxh = dov * g_ref[...]
        dx = r * (dxh - xhat * jnp.mean(dxh * xhat, axis=-1, keepdims=True))
        if has_res:
            dx = dx + refs[3][...]
        dx_ref[...] = dx.astype(dx_ref.dtype)

        @pl.when(i == 0)
        def _():
            dg_ref[...] = dg

        @pl.when(i > 0)
        def _():
            dg_ref[...] += dg

    row = pl.BlockSpec((ts, d), lambda i: (i, 0))
    vec = pl.BlockSpec((1, d), lambda i: (0, 0))
    ops = [xin, g, dout] + ([dres] if has_res else [])
    return pl.pallas_call(body, name=name, grid=(s // ts,), in_specs=[row, vec, row] + ([row] if has_res else []),
                          out_specs=[row, vec], out_shape=[_sds((s, d), out_dtype), _sds((1, d), F32)],
                          compiler_params=_params(("arbitrary",)))(*ops)


class _Cols:
    def __init__(self, d):
        self.d = d
        self.kv = d // GROUP
        self.cb, self.cc, self.ci, self.q = 0, d, 2 * d, 3 * d
        self.k = 4 * d
        self.v = 4 * d + self.kv
        self.ga = 4 * d + 2 * self.kv
        self.gb = 5 * d + 2 * self.kv
        self.width = 6 * d + 2 * self.kv


CONV_COLS = 128


def _shift_rows(u, down):
    s = u.shape[0]
    rows = lax.broadcasted_iota(jnp.int32, u.shape, 0)
    if down:
        return jnp.where(rows == 0, 0.0, pltpu.roll(u, 1, 0))
    return jnp.where(rows == s - 1, 0.0, pltpu.roll(u, s - 1, 0))


def _conv_fwd(z, w, cols):
    s, d = z.shape[0], cols.d
    cw = CONV_COLS

    def body(cb_ref, cc_ref, ci_ref, w_ref, a_ref):
        u = cc_ref[...].astype(F32) * ci_ref[...].astype(F32)
        wv = w_ref[...]
        conv = wv[0:1] * _shift_rows(u, True) + wv[1:2] * u + wv[2:3] * _shift_rows(u, False)
        a_ref[...] = (cb_ref[...].astype(F32) * conv).astype(a_ref.dtype)

    def zspec(off):
        return pl.BlockSpec((s, cw), lambda j: (0, off // cw + j))

    return pl.pallas_call(body, name="conv_fwd", grid=(d // cw,),
                          in_specs=[zspec(cols.cb), zspec(cols.cc), zspec(cols.ci), pl.BlockSpec((3, cw), lambda j: (0, j))],
                          out_specs=pl.BlockSpec((s, cw), lambda j: (0, j)), out_shape=_sds((s, d), BF16),
                          compiler_params=_params(("parallel",)))(z, z, z, w)


def _conv_bwd(z, w, da, cols):
    s, d = z.shape[0], cols.d
    cw = CONV_COLS

    def body(cb_ref, cc_ref, ci_ref, w_ref, da_ref, dcb_ref, dcc_ref, dci_ref, dw_ref):
        cb, cc, ci, dav = cb_ref[...].astype(F32), cc_ref[...].astype(F32), ci_ref[...].astype(F32), da_ref[...].astype(F32)
        wv = w_ref[...]
        u = cc * ci
        um, up = _shift_rows(u, True), _shift_rows(u, False)
        conv = wv[0:1] * um + wv[1:2] * u + wv[2:3] * up
        dcb_ref[...] = (dav * conv).astype(dcb_ref.dtype)
        dconv = dav * cb
        dw_ref[0:1, :] = jnp.sum(dconv * um, axis=0, keepdims=True)
        dw_ref[1:2, :] = jnp.sum(dconv * u, axis=0, keepdims=True)
        dw_ref[2:3, :] = jnp.sum(dconv * up, axis=0, keepdims=True)
        du = wv[0:1] * _shift_rows(dconv, False) + wv[1:2] * dconv + wv[2:3] * _shift_rows(dconv, True)
        dcc_ref[...] = (du * ci).astype(dcc_ref.dtype)
        dci_ref[...] = (du * cc).astype(dci_ref.dtype)

    def zspec(off):
        return pl.BlockSpec((s, cw), lambda j: (0, off // cw + j))

    col = pl.BlockSpec((s, cw), lambda j: (0, j))
    wspec = pl.BlockSpec((3, cw), lambda j: (0, j))
    return pl.pallas_call(body, name="conv_bwd", grid=(d // cw,),
                          in_specs=[zspec(cols.cb), zspec(cols.cc), zspec(cols.ci), wspec, col],
                          out_specs=[col, col, col, wspec],
                          out_shape=[_sds((s, d), BF16)] * 3 + [_sds((3, d), F32)],
                          compiler_params=_params(("parallel",)))(z, z, z, w, da)


def _gate_fwd(z, bias, y_a, y_b, cols):
    s, d = y_a.shape
    ts, cw = _tile(s, GATE_ROW_TILE, 8), cols.kv
    nj = d // cw

    def body(ga_ref, gb_ref, ba_ref, bb_ref, ya_ref, yb_ref, o_ref):
        gate_a = jax.nn.sigmoid(ga_ref[...].astype(F32) + ba_ref[...])
        gate_b = jax.nn.sigmoid(gb_ref[...].astype(F32) + bb_ref[...])
        o_ref[...] = (gate_a * ya_ref[...].astype(F32) + gate_b * yb_ref[...].astype(F32)).astype(o_ref.dtype)

    tile = pl.BlockSpec((ts, cw), lambda i, j: (i, j))
    return pl.pallas_call(
        body, name="gate_fwd", grid=(s // ts, nj),
        in_specs=[pl.BlockSpec((ts, cw), lambda i, j: (i, cols.ga // cw + j)), pl.BlockSpec((ts, cw), lambda i, j: (i, cols.gb // cw + j)),
                  pl.BlockSpec((1, cw), lambda i, j: (0, j)), pl.BlockSpec((1, cw), lambda i, j: (0, nj + j)), tile, tile],
        out_specs=tile, out_shape=_sds((s, d), BF16), compiler_params=_params(("parallel", "parallel")))(z, z, bias, bias, y_a, y_b)


def _gate_bwd(z, bias, y_a, y_b, dmix, cols):
    s, d = y_a.shape
    ts, cw = _tile(s, GATE_ROW_TILE, 8), cols.kv
    nj = d // cw

    def body(ga_ref, gb_ref, ba_ref, bb_ref, ya_ref, yb_ref, dm_ref, dya_ref, dyb_ref, dga_ref, dgb_ref, dba_ref, dbb_ref):
        i = pl.program_id(1)
        gate_a = jax.nn.sigmoid(ga_ref[...].astype(F32) + ba_ref[...])
        gate_b = jax.nn.sigmoid(gb_ref[...].astype(F32) + bb_ref[...])
        dm = dm_ref[...].astype(F32)
        dya_ref[...] = (dm * gate_a).astype(dya_ref.dtype)
        dyb_ref[...] = (dm * gate_b).astype(dyb_ref.dtype)
        dga = dm * ya_ref[...].astype(F32) * (gate_a * (1.0 - gate_a))
        dgb = dm * yb_ref[...].astype(F32) * (gate_b * (1.0 - gate_b))
        dga_ref[...] = dga.astype(dga_ref.dtype)
        dgb_ref[...] = dgb.astype(dgb_ref.dtype)
        sa = jnp.sum(dga, axis=0, keepdims=True)
        sb = jnp.sum(dgb, axis=0, keepdims=True)

        @pl.when(i == 0)
        def _():
            dba_ref[...] = sa
            dbb_ref[...] = sb

        @pl.when(i > 0)
        def _():
            dba_ref[...] += sa
            dbb_ref[...] += sb

    tile = pl.BlockSpec((ts, cw), lambda j, i: (i, j))
    vec = pl.BlockSpec((1, cw), lambda j, i: (0, j))
    return pl.pallas_call(
        body, name="gate_bwd", grid=(nj, s // ts),
        in_specs=[pl.BlockSpec((ts, cw), lambda j, i: (i, cols.ga // cw + j)), pl.BlockSpec((ts, cw), lambda j, i: (i, cols.gb // cw + j)),
                  vec, pl.BlockSpec((1, cw), lambda j, i: (0, nj + j)), tile, tile, tile],
        out_specs=[tile, tile, tile, tile, vec, vec],
        out_shape=[_sds((s, d), BF16)] * 4 + [_sds((1, d), F32)] * 2,
        compiler_params=_params(("parallel", "arbitrary")))(z, z, bias, bias, y_a, y_b, dmix)


def _rope_tables(s):
    axis_dim = HEAD_DIM // 2
    n_freq = axis_dim // 2
    rows = s // GRID_W
    row_idx = jnp.repeat(jnp.arange(rows, dtype=jnp.int32), GRID_W)
    col_idx = jnp.tile(jnp.arange(GRID_W, dtype=jnp.int32), rows)
    inv_freq = ROPE_THETA ** (-jnp.arange(0, axis_dim, 2, dtype=F32) / axis_dim)
    ang = jnp.stack([row_idx.astype(F32)[:, None] * inv_freq, col_idx.astype(F32)[:, None] * inv_freq], axis=1)
    cos, sin = jnp.cos(ang), jnp.sin(ang)
    cos_t = jnp.stack([cos, cos], axis=2).reshape(s, HEAD_DIM)
    sin_t = jnp.stack([-sin, sin], axis=2).reshape(s, HEAD_DIM)
    return cos_t, sin_t


def _partner(x):
    n = x.shape[-1]
    lane = lax.broadcasted_iota(jnp.int32, x.shape, x.ndim - 1)
    quarter = HEAD_DIM // 4
    return jnp.where(lane % (2 * quarter) < quarter, pltpu.roll(x, n - quarter, x.ndim - 1), pltpu.roll(x, quarter, x.ndim - 1))


LOG2E = math.log2(math.e)
Q_SCALE = LOG2E / math.sqrt(HEAD_DIM)


def _rope_fwd(z, qn, kn, cos_t, sin_t, cols):
    s, d, kv = z.shape[0], cols.d, cols.kv
    ts = _tile(s, ROW_TILE, 8)
    scale = Q_SCALE

    def body(q_ref, k_ref, v_ref, qn_ref, kn_ref, c_ref, s_ref, qo_ref, ko_ref, vo_ref):
        c, sn = c_ref[...], s_ref[...]

        def head(xh, g):
            xn = xh * _rstd(xh) * g
            return xn * c + _partner(xn) * sn

        for h in range(d // HEAD_DIM):
            sl = slice(h * HEAD_DIM, (h + 1) * HEAD_DIM)
            qo_ref[:, sl] = (head(q_ref[:, sl].astype(F32), qn_ref[...]) * scale).astype(qo_ref.dtype)
        for h in range(kv // HEAD_DIM):
            sl = slice(h * HEAD_DIM, (h + 1) * HEAD_DIM)
            ko_ref[:, sl] = head(k_ref[:, sl].astype(F32), kn_ref[...]).astype(ko_ref.dtype)
        vo_ref[...] = v_ref[...].astype(vo_ref.dtype)

    vec = pl.BlockSpec((1, HEAD_DIM), lambda i: (0, 0))
    tab = pl.BlockSpec((ts, HEAD_DIM), lambda i: (i, 0))
    return pl.pallas_call(
        body, name="rope_fwd", grid=(s // ts,),
        in_specs=[pl.BlockSpec((ts, d), lambda i: (i, cols.q // d)), pl.BlockSpec((ts, kv), lambda i: (i, cols.k // kv)),
                  pl.BlockSpec((ts, kv), lambda i: (i, cols.v // kv)), vec, vec, tab, tab],
        out_specs=[pl.BlockSpec((ts, d), lambda i: (i, 0)), pl.BlockSpec((ts, kv), lambda i: (i, 0)), pl.BlockSpec((ts, kv), lambda i: (i, 0))],
        out_shape=[_sds((s, d), BF16), _sds((s, kv), BF16), _sds((s, kv), BF16)],
        compiler_params=_params(("parallel",)))(z, z, z, qn, kn, cos_t, sin_t)


def _rope_bwd(z, qn, kn, cos_t, sin_t, dq, dk_t, dv_t, cols):
    s, d, kv = z.shape[0], cols.d, cols.kv
    ts = _tile(s, ROW_TILE, 8)
    scale = 1.0 / math.sqrt(HEAD_DIM)

    def body(q_ref, k_ref, qn_ref, kn_ref, c_ref, s_ref, dq_ref, dkt_ref, dvt_ref, dzq_ref, dzk_ref, dzv_ref, dqn_ref, dkn_ref):
        i = pl.program_id(0)
        c, sn = c_ref[...], s_ref[...]
        dk_all = dkt_ref[...].T * (1.0 / LOG2E)

        def head_bwd(xh, g, drot):
            dxn = drot * c + _partner(drot * sn)
            r = _rstd(xh)
            xhat = xh * r
            dgain = jnp.sum(dxn * xhat, axis=0, keepdims=True)
            dxh = dxn * g
            return r * (dxh - xhat * jnp.mean(dxh * xhat, axis=-1, keepdims=True)), dgain

        dqn = jnp.zeros((1, HEAD_DIM), F32)
        for h in range(d // HEAD_DIM):
            sl = slice(h * HEAD_DIM, (h + 1) * HEAD_DIM)
            dx, dg = head_bwd(q_ref[:, sl].astype(F32), qn_ref[...], dq_ref[:, sl].astype(F32) * scale)
            dzq_ref[:, sl] = dx.astype(dzq_ref.dtype)
            dqn = dqn + dg
        dkn = jnp.zeros((1, HEAD_DIM), F32)
        for h in range(kv // HEAD_DIM):
            sl = slice(h * HEAD_DIM, (h + 1) * HEAD_DIM)
            dx, dg = head_bwd(k_ref[:, sl].astype(F32), kn_ref[...], dk_all[:, sl])
            dzk_ref[:, sl] = dx.astype(dzk_ref.dtype)
            dkn = dkn + dg
        dzv_ref[...] = dvt_ref[...].T.astype(dzv_ref.dtype)

        @pl.when(i == 0)
        def _():
            dqn_ref[...] = dqn
            dkn_ref[...] = dkn

        @pl.when(i > 0)
        def _():
            dqn_ref[...] += dqn
            dkn_ref[...] += dkn

    vec = pl.BlockSpec((1, HEAD_DIM), lambda i: (0, 0))
    tab = pl.BlockSpec((ts, HEAD_DIM), lambda i: (i, 0))
    qrow = pl.BlockSpec((ts, d), lambda i: (i, 0))
    krow = pl.BlockSpec((ts, kv), lambda i: (i, 0))
    kcol = pl.BlockSpec((kv, ts), lambda i: (0, i))
    return pl.pallas_call(
        body, name="rope_bwd", grid=(s // ts,),
        in_specs=[pl.BlockSpec((ts, d), lambda i: (i, cols.q // d)), pl.BlockSpec((ts, kv), lambda i: (i, cols.k // kv)),
                  vec, vec, tab, tab, qrow, kcol, kcol],
        out_specs=[qrow, krow, krow, vec, vec],
        out_shape=[_sds((s, d), BF16), _sds((s, kv), BF16), _sds((s, kv), BF16), _sds((1, HEAD_DIM), F32), _sds((1, HEAD_DIM), F32)],
        compiler_params=_params(("arbitrary",)))(z, z, qn, kn, cos_t, sin_t, dq, dk_t, dv_t)


Q_TILE = 256
_NT = (((1,), (1,)), ((), ()))
_NN = (((1,), (0,)), ((), ()))


def _attn_fwd(q, k, v):
    s, d = q.shape
    kvh = k.shape[1] // HEAD_DIM
    tq = _tile(s, Q_TILE, LANES)
    gw = GROUP * HEAD_DIM

    def body(q_ref, k_ref, v_ref, o_ref, lse_ref):
        kk, vv = k_ref[...], v_ref[...]
        for g in range(GROUP):
            sl = slice(g * HEAD_DIM, (g + 1) * HEAD_DIM)
            sc = lax.dot_general(q_ref[:, sl], kk, _NT, preferred_element_type=F32)
            mx = jnp.max(sc, axis=-1, keepdims=True)
            p = jnp.exp2(sc - mx)
            l = jnp.sum(p, axis=-1, keepdims=True)
            o = lax.dot_general(p.astype(BF16), vv, _NN, preferred_element_type=F32) * (1.0 / l)
            o_ref[:, sl] = o.astype(o_ref.dtype)
            lse_ref[:, g:g + 1] = mx + jnp.log(l) * LOG2E

    return pl.pallas_call(
        body, name="attn_fwd", grid=(kvh, s // tq),
        in_specs=[pl.BlockSpec((tq, gw), lambda j, i: (i, j)), pl.BlockSpec((s, HEAD_DIM), lambda j, i: (0, j)), pl.BlockSpec((s, HEAD_DIM), lambda j, i: (0, j))],
        out_specs=[pl.BlockSpec((tq, gw), lambda j, i: (i, j)), pl.BlockSpec((None, tq, GROUP), lambda j, i: (j, i, 0))],
        out_shape=[_sds((s, d), BF16), _sds((kvh, s, GROUP), F32)],
        compiler_params=_params(("parallel", "parallel")))(q, k, v)


_TN = (((0,), (0,)), ((), ()))


def _attn_bwd(q, k, v, do, lse):
    s, d = q.shape
    kv = k.shape[1]
    kvh = kv // HEAD_DIM
    tq = _tile(s, Q_TILE, LANES)
    gw = GROUP * HEAD_DIM

    def body(q_ref, k_ref, v_ref, do_ref, lse_ref, dq_ref, dkt_ref, dvt_ref):
        i = pl.program_id(1)
        kk, vv = k_ref[...], v_ref[...]

        @pl.when(i == 0)
        def _():
            dkt_ref[...] = jnp.zeros_like(dkt_ref)
            dvt_ref[...] = jnp.zeros_like(dvt_ref)

        for g in range(GROUP):
            sl = slice(g * HEAD_DIM, (g + 1) * HEAD_DIM)
            qg, dog = q_ref[:, sl], do_ref[:, sl]
            sc = lax.dot_general(qg, kk, _NT, preferred_element_type=F32)
            p = jnp.exp2(sc - lse_ref[:, g:g + 1])
            dp = lax.dot_general(dog, vv, _NT, preferred_element_type=F32)
            delta = jnp.sum(p * dp, axis=-1, keepdims=True)
            ds = (p * (dp - delta)).astype(BF16)
            dq_ref[:, sl] = lax.dot_general(ds, kk, _NN, preferred_element_type=F32).astype(dq_ref.dtype)
            dkt_ref[...] += lax.dot_general(qg, ds, _TN, preferred_element_type=F32)
            dvt_ref[...] += lax.dot_general(dog, p.astype(BF16), _TN, preferred_element_type=F32)

    qspec = pl.BlockSpec((tq, gw), lambda j, i: (i, j))
    kspec = pl.BlockSpec((s, HEAD_DIM), lambda j, i: (0, j))
    stat = pl.BlockSpec((None, tq, GROUP), lambda j, i: (j, i, 0))
    tspec = pl.BlockSpec((HEAD_DIM, s), lambda j, i: (j, 0))
    return pl.pallas_call(
        body, name="attn_bwd", grid=(kvh, s // tq),
        in_specs=[qspec, kspec, kspec, qspec, stat], out_specs=[qspec, tspec, tspec],
        out_shape=[_sds((s, d), BF16), _sds((kv, s), F32), _sds((kv, s), F32)],
        compiler_params=_params(("parallel", "arbitrary")))(q, k, v, do, lse)


ELEM_BLOCK_BYTES = 2 << 20

BIG = (("w_in", True), ("w_out_conv", False), ("w_out_attn", False), ("w_merge", False), ("w_up", True), ("w_down", False))
N_BIG = len(BIG)


def _elem_tiles(rows, width):
    tc = _tile(width, 2048, LANES)
    tr = _tile(rows, max(8, ELEM_BLOCK_BYTES // (4 * tc)), 8)
    return tr, tc


def _scalar_grid(grid, in_specs, out_specs):
    return pltpu.PrefetchScalarGridSpec(num_scalar_prefetch=1, grid=grid, in_specs=in_specs, out_specs=out_specs)


def _cast_place(w_stack, layer, chip, col_sharded, name):
    _, rows, width = w_stack.shape
    tr, tc = _elem_tiles(rows, width)
    nr, nc = rows // tr, width // tc

    def body(sc_ref, x_ref, o_ref):
        o_ref[...] = x_ref[...].astype(o_ref.dtype)

    if col_sharded:
        full, out_spec = (rows, width * N_CHIPS), pl.BlockSpec((tr, tc), lambda i, j, sc: (i, sc[0] * nc + j))
    else:
        full, out_spec = (rows * N_CHIPS, width), pl.BlockSpec((tr, tc), lambda i, j, sc: (sc[0] * nr + i, j))
    return pl.pallas_call(
        body, name=name,
        grid_spec=_scalar_grid((nr, nc), [pl.BlockSpec((None, tr, tc), lambda i, j, sc: (layer, i, j))], out_spec),
        out_shape=_sds(full, BF16), compiler_params=_params(("parallel", "parallel")))(chip, w_stack)


def _add_landed(acc, landed):
    return (acc + landed,)


def _slot_of_relation(rel):
    return jnp.where(rel == 2, 0, jnp.where(rel == 1, 1, 2))


def _sum_chips(pair_sum, landed, chip, col_sharded, name):
    _, rows, width = landed.shape
    tr, tc = _elem_tiles(rows, width)
    nr, nc = rows // tr, width // tc

    def body(chip_ref, own_ref, q_ref, o_ref):
        me = chip_ref[0]
        own = own_ref[...].astype(F32)
        acc = None
        for t in range(N_CHIPS):
            rel = me ^ t
            term = jnp.where(rel == 0, own, q_ref[_slot_of_relation(rel)].astype(F32))
            acc = term if acc is None else acc + term
        o_ref[...] = acc

    if col_sharded:
        own_spec = pl.BlockSpec((tr, tc), lambda i, j, c: (i, c[0] * nc + j))
    else:
        own_spec = pl.BlockSpec((tr, tc), lambda i, j, c: (c[0] * nr + i, j))
    return pl.pallas_call(
        body, name=name,
        grid_spec=_scalar_grid((nr, nc), [own_spec, pl.BlockSpec((N_CHIPS - 1, tr, tc), lambda i, j, c: (0, i, j))],
                               pl.BlockSpec((tr, tc), lambda i, j, c: (i, j))),
        out_shape=_sds((rows, width), F32), compiler_params=_params(("parallel", "parallel")))(chip, pair_sum, landed)


def _adamw_math(w, g, m, v):
    mn = ADAM_B1 * m + (1.0 - ADAM_B1) * g
    vn = ADAM_B2 * v + (1.0 - ADAM_B2) * jnp.square(g)
    m_hat = mn / (1.0 - ADAM_B1 ** ADAM_STEP)
    v_hat = vn / (1.0 - ADAM_B2 ** ADAM_STEP)
    return -ADAM_LR * (m_hat / (jnp.sqrt(v_hat) + ADAM_EPS) + ADAM_WD * w), mn, vn


def _adamw(w, g, m, v, name):
    shape = w.shape
    width = shape[-1]
    w2, g2, m2, v2 = (a.reshape(-1, width) for a in (w, g, m, v))
    rows = w2.shape[0]
    tr, tc = _elem_tiles(rows, width)

    def body(w_ref, g_ref, m_ref, v_ref, d_ref, nm_ref, nv_ref):
        d_ref[...], nm_ref[...], nv_ref[...] = _adamw_math(w_ref[...], g_ref[...], m_ref[...], v_ref[...])

    tile = pl.BlockSpec((tr, tc), lambda i, j: (i, j))
    outs = pl.pallas_call(body, name=name, grid=(rows // tr, width // tc), in_specs=[tile] * 4, out_specs=[tile] * 3,
                          out_shape=[_sds((rows, width), F32)] * 3, compiler_params=_params(("parallel", "parallel")))(w2, g2, m2, v2)
    return tuple(o.reshape(shape) for o in outs)


def _adamw_layer(w, m, v, g_mine, g_sibling, carried, layer, core, col_sharded, name):
    depth, rows, width = w.shape
    pr, pc = g_mine.shape
    tr, tc = _elem_tiles(pr, pc)
    n_half = pr // tr if col_sharded else pc // tc
    if carried is None:
        carried = tuple(lax.empty((depth, rows, width), F32) for _ in range(4))

    def body(sc_ref, w_ref, m_ref, v_ref, gm_ref, gs_ref, *rest):
        g_ref, d_ref, nm_ref, nv_ref = rest[-4:]
        pos = pl.program_id(0) if col_sharded else pl.program_id(1)
        gv = jnp.where(pos // n_half == sc_ref[0], gm_ref[...], gs_ref[...])
        g_ref[...] = gv
        d_ref[...], nm_ref[...], nv_ref[...] = _adamw_math(w_ref[...], gv, m_ref[...], v_ref[...])

    stacked = pl.BlockSpec((None, tr, tc), lambda i, j, sc: (layer, i, j))
    n_cols = width // tc

    def half_spec(mine):
        def index(i, j, sc):
            own = sc[0] if mine else 1 - sc[0]
            pos = i if col_sharded else j
            used, before = pos // n_half == own, pos // n_half < own
            within = jnp.where(used, pos % n_half, jnp.where(before, 0, n_half - 1))
            if col_sharded:
                return within, jnp.where(used, j, jnp.where(before, 0, n_cols - 1))
            return i, within
        return pl.BlockSpec((tr, tc), index)

    return pl.pallas_call(
        body, name=name,
        grid_spec=_scalar_grid((rows // tr, width // tc), [stacked] * 3 + [half_spec(True), half_spec(False)] + [ANY] * 4, [stacked] * 4),
        out_shape=[_sds((depth, rows, width), F32)] * 4, input_output_aliases={6: 0, 7: 1, 8: 2, 9: 3},
        compiler_params=_params(("parallel", "parallel")))(core, w, m, v, g_mine, g_sibling, *carried)


_SEM = pl.BlockSpec(memory_space=pltpu.SEMAPHORE)
_HBM = pl.BlockSpec(memory_space=pltpu.HBM)
_VMEM = pl.BlockSpec(memory_space=pltpu.VMEM)
_EFFECT = pltpu.SideEffectType.DATAFLOW_SIDE_EFFECTING


def _place():
    x, y, c = lax.axis_index("x"), lax.axis_index("y"), lax.axis_index("c")
    others = [(1 - x, y), (x, 1 - y), (1 - x, 1 - y)]
    return x, y, c, others


def _chip_index(px, py):
    return 2 * px + py


def _remote(src, dst, send_sems, recv_sems, k, to):
    return pltpu.make_async_remote_copy(src_ref=src, dst_ref=dst, send_sem=send_sems.at[k], recv_sem=recv_sems.at[k],
                                        device_id=to, device_id_type=MESH)


def _phase(name, bufs, waits, wait_fn, n_start, start_fn, deps):
    nb, nd = len(bufs), len(deps)

    def body(*refs):
        buf_refs = refs[:nb]
        pos = nb
        if waits is not None:
            wait_fn(buf_refs, refs[pos], refs[pos + 1])
            pos += 2
        pos += nd
        if n_start:
            start_fn(buf_refs, refs[pos], refs[pos + 1])
            pos += 2
        token = refs[pos + nb]
        token[...] = jnp.zeros_like(token)

    n_sem_out = 2 if n_start else 0
    if waits is None:
        bufs = [pltpu.with_memory_space_constraint(b, pltpu.HBM) for b in bufs]
    outs = pl.pallas_call(
        body, name=name,
        in_specs=[_HBM] * nb + ([_SEM] * 2 if waits is not None else []) + [ANY] * nd,
        out_specs=[_SEM] * n_sem_out + [_HBM] * nb + [_VMEM],
        out_shape=[pltpu.SemaphoreType.DMA((n_start,))] * n_sem_out + [pltpu.HBM(b.shape, b.dtype) for b in bufs] + [_sds((8, LANES), F32)],
        input_output_aliases={i: n_sem_out + i for i in range(nb)},
        compiler_params=pltpu.CompilerParams(has_side_effects=_EFFECT),
    )(*bufs, *(waits if waits is not None else ()), *deps)
    sems = tuple(outs[:2]) if n_start else None
    return sems, list(outs[n_sem_out:n_sem_out + nb]), outs[-1]


class _Chains:
    def __init__(self):
        self.active = []
        self.last = None
        self.dep = None
        self.pinned = []
        self.token = None
        self.at = None

    def phase(self, name, bufs, waits, wait_fn, n_start, start_fn):
        deps = [a for a in (self.last, self.dep) if a is not None] + self.pinned
        sems, thru, token = _phase(name, bufs, waits, wait_fn, n_start, start_fn, deps)
        self.last, self.dep, self.token, self.pinned = token, None, token, []
        return sems, thru

    def pin(self, result):
        self.pinned.append(result)

    def add(self, gen):
        self.active.append(gen)

    def tick(self, dep, at=None):
        self.at, self.token, self.dep = at, None, dep
        for gen in list(self.active):
            if next(gen, "done") == "done":
                self.active.remove(gen)
        return self.token


def _after(small, token):
    return small if token is None else small + token[0, 0]


def _shard_region(ref, col_sharded, chip, n_shard):
    start = pl.multiple_of(chip * n_shard, LANES if col_sharded else 8)
    if col_sharded:
        return ref.at[:, pl.ds(start, n_shard)]
    return ref.at[pl.ds(start, n_shard), :]


def _row_half(ref, half):
    n_rows = ref.shape[0]
    return ref.at[pl.ds(pl.multiple_of(half * (n_rows // 2), 8), n_rows // 2), :]


def _gather_chain(chains, tag, group, bufs, out, wait_for):
    n_w = len(group)
    n = 3 * n_w

    def region(refs, a, chip, half):
        col = BIG[group[a]][1]
        n_shard = refs[a].shape[1] // N_CHIPS if col else refs[a].shape[0] // N_CHIPS
        return _row_half(_shard_region(refs[a], col, chip, n_shard), half)

    def start_ici(refs, send, recv):
        x, y, c, others = _place()
        for a in range(n_w):
            mine = region(refs, a, _chip_index(x, y), c)
            for j, (ox, oy) in enumerate(others):
                _remote(mine, mine, send, recv, 3 * a + j, (ox, oy, c)).start()

    def wait_ici(refs, send, recv):
        x, y, c, others = _place()
        for a in range(n_w):
            for j, (ox, oy) in enumerate(others):
                landed = region(refs, a, _chip_index(ox, oy), c)
                cp = _remote(landed, landed, send, recv, 3 * a + j, (x, y, 1 - c))
                cp.wait_recv()
                cp.wait_send()

    def start_d2d(refs, send, recv):
        x, y, c, others = _place()
        for a in range(n_w):
            for j, (ox, oy) in enumerate(others):
                landed = region(refs, a, _chip_index(ox, oy), c)
                _remote(landed, landed, send, recv, 3 * a + j, (x, y, 1 - c)).start()

    def wait_d2d(refs, send, recv):
        x, y, c, others = _place()
        for a in range(n_w):
            for j, (ox, oy) in enumerate(others):
                theirs = region(refs, a, _chip_index(ox, oy), 1 - c)
                cp = _remote(theirs, theirs, send, recv, 3 * a + j, (x, y, 1 - c))
                cp.wait_recv()
                cp.wait_send()

    sems, bufs = chains.phase(f"gather_ici_start_{tag}", bufs, None, None, n, start_ici)
    yield
    while wait_for is not None and chains.at != wait_for:
        yield
    sems, bufs = chains.phase(f"gather_forward_{tag}", bufs, sems, wait_ici, n, start_d2d)
    yield
    _, bufs = chains.phase(f"gather_done_{tag}", bufs, sems, wait_d2d, 0, None)
    for a, buf in zip(group, bufs):
        out[BIG[a][0]] = buf


def _grad_chain(chains, layer, group, pairs, core, other_core, chip, w, mom, var, carried):
    n = len(group)
    tag = f"{layer}_{group[0]}"
    kinds = [BIG[a][1] for a in group]
    names = [BIG[a][0] for a in group]
    sibling_of = lambda x, y, c: (x, y, 1 - c)

    def pair_start(refs, send, recv):
        x, y, c, _ = _place()
        for i in range(n):
            _remote(refs[i], refs[n + i], send, recv, i, sibling_of(x, y, c)).start()

    def pair_wait(refs, send, recv):
        x, y, c, _ = _place()
        for i in range(n):
            cp = _remote(refs[i], refs[n + i], send, recv, i, sibling_of(x, y, c))
            cp.wait_recv()
            cp.wait_send()

    def piece(ref, col, chip_idx):
        return _shard_region(ref, col, chip_idx, ref.shape[1] // N_CHIPS if col else ref.shape[0] // N_CHIPS)

    def scatter_start(refs, send, recv):
        x, y, c, others = _place()
        for i in range(n):
            for j, (ox, oy) in enumerate(others):
                _remote(piece(refs[i], kinds[i], _chip_index(ox, oy)), refs[n + i].at[j], send, recv, 3 * i + j, (ox, oy, c)).start()

    def scatter_wait(refs, send, recv):
        x, y, c, others = _place()
        for i in range(n):
            for j, (ox, oy) in enumerate(others):
                cp = _remote(piece(refs[i], kinds[i], _chip_index(ox, oy)), refs[n + i].at[j], send, recv, 3 * i + j, (ox, oy, c))
                cp.wait_recv()
                cp.wait_send()

    def join_start(refs, send, recv):
        x, y, c, _ = _place()
        for i in range(n):
            _remote(refs[i], refs[n + i], send, recv, i, sibling_of(x, y, c)).start()

    def join_wait(refs, send, recv):
        x, y, c, _ = _place()
        for i in range(n):
            cp = _remote(refs[i], refs[n + i], send, recv, i, sibling_of(x, y, c))
            cp.wait_recv()
            cp.wait_send()

    sides = ["m" if col else "n" for col in kinds]
    whole = [dict(tk=pairs[i][0].shape[0], **({"tn": 512} if sides[i] == "m" else {"tm": 512})) for i in range(n)]
    sends = [_mm(a, b, "tn", [F32], name="mm_g_send_" + names[i], half=(other_core, sides[i]), **whole[i]) for i, (a, b) in enumerate(pairs)]
    half_shapes = [g.shape for g in sends]
    lands = [lax.empty(s, F32) for s in half_shapes]
    sems, bufs = chains.phase(f"pair_start_{tag}", sends + lands, None, None, n, pair_start)
    yield
    _, bufs = chains.phase(f"pair_wait_{tag}", bufs, sems, pair_wait, 0, None)
    pair_sums = [_mm(a, b, "tn", [BF16], name="mm_g_keep_" + names[i], half=(core, sides[i]), epilogue=_add_landed, extras=(bufs[n + i],), **whole[i])
                 for i, (a, b) in enumerate(pairs)]
    piece_shapes = [(s[0], s[1] // N_CHIPS) if col else (s[0] // N_CHIPS, s[1]) for col, s in zip(kinds, half_shapes)]
    slots = [lax.empty((N_CHIPS - 1, *s), BF16) for s in piece_shapes]
    sems, bufs = chains.phase(f"scatter_start_{tag}", pair_sums + slots, None, None, 3 * n, scatter_start)
    yield
    yield
    _, bufs = chains.phase(f"scatter_wait_{tag}", bufs, sems, scatter_wait, 0, None)
    reduced = [_sum_chips(bufs[i], bufs[n + i], chip, kinds[i], "sum_chips_" + names[i]) for i in range(n)]
    theirs = [lax.empty(s, F32) for s in piece_shapes]
    sems, bufs = chains.phase(f"join_start_{tag}", reduced + theirs, None, None, n, join_start)
    yield
    _, bufs = chains.phase(f"join_wait_{tag}", bufs, sems, join_wait, 0, None)
    for i in range(n):
        carried[names[i]] = _adamw_layer(w[names[i]], mom[names[i]], var[names[i]], bufs[i], bufs[n + i], carried.get(names[i]),
                                         layer, core, kinds[i], "adamw_" + names[i])
        chains.pin(carried[names[i]][0])


def _allreduce_small(vec, name):
    rows = vec.shape[0]
    masks = [(dx, dy, dc) for dx in (0, 1) for dy in (0, 1) for dc in (0, 1)][1:]

    def body(v_ref, o_ref, gather_ref, send_sems, recv_sems):
        x, y, c, _ = _place()
        me = 4 * x + 2 * y + c
        gather_ref[me] = v_ref[...]
        copies = []
        for k, (dx, dy, dc) in enumerate(masks):
            peer = (x ^ dx, y ^ dy, c ^ dc)
            copies.append(_remote(v_ref, gather_ref.at[me], send_sems, recv_sems, k, peer))
        for cp in copies:
            cp.start()
        for k, (dx, dy, dc) in enumerate(masks):
            slot = gather_ref.at[4 * (x ^ dx) + 2 * (y ^ dy) + (c ^ dc)]
            _remote(slot, slot, send_sems, recv_sems, k, (x, y, c)).wait_recv()
        for cp in copies:
            cp.wait_send()
        acc = gather_ref[0]
        for dev in range(1, N_DEV):
            acc = acc + gather_ref[dev]
        o_ref[...] = acc

    return pl.pallas_call(
        body, name=name, in_specs=[_VMEM], out_specs=_VMEM, out_shape=_sds((rows, LANES), F32),
        scratch_shapes=[pltpu.VMEM((N_DEV, rows, LANES), F32), pltpu.SemaphoreType.DMA((N_DEV - 1,)), pltpu.SemaphoreType.DMA((N_DEV - 1,))],
        compiler_params=pltpu.CompilerParams(has_side_effects=True, vmem_limit_bytes=VMEM_LIMIT_BYTES),
    )(vec)


def _relu2(acc):
    r = jnp.maximum(acc, 0.0)
    return acc, r * r


def _relu2_bwd(acc, up):
    return (acc * (2.0 * jnp.maximum(up.astype(F32), 0.0)),)


def _layer_fwd(x, h, w, cols, tables, tick, last, target=None):
    cos_t, sin_t = tables
    z = _mm(h, w("w_in"), "nn", [BF16], name="mm_in")
    a = _conv_fwd(z, w("conv_w"), cols)
    q, k, v = _rope_fwd(z, _after(w("q_norm"), tick(a, 0)), w("k_norm"), cos_t, sin_t, cols)
    o, lse = _attn_fwd(q, k, v)
    tick(o, 1)
    tick(None, 2)
    y_a = _mm(a, w("w_out_conv"), "nn", [BF16], name="mm_out_conv")
    y_b = _mm(o, w("w_out_attn"), "nn", [BF16], name="mm_out_attn")
    mix = _gate_fwd(z, w("gate_bias"), y_a, y_b, cols)
    mixed = _mm(mix, w("w_merge"), "nn", [BF16], name="mm_merge")
    x1, h2 = _resid_norm(x, mixed, _after(w("norm_mix_post"), tick(mixed, 3)), w("norm_mlp_pre"))
    up, act = _mm(h2, w("w_up"), "nn", [BF16, BF16], name="mm_up", epilogue=_relu2)
    f = _mm(act, w("w_down"), "nn", [BF16], name="mm_down", tk=4096)
    kept = dict(x=x, h=h, z=z, a=a, q=q, k=k, v=v, o=o, lse=lse, y_a=y_a, y_b=y_b, mix=mix, mixed=mixed, x1=x1, h2=h2, up=up, act=act, f=f)
    g_post = _after(w("norm_mlp_post"), tick(f, 4))
    if last:
        return _resid_norm_loss(x1, f, g_post, target), kept
    return _resid_norm(x1, f, g_post, w("norm_next")), kept


def _layer_bwd(dx_out, w, kept, cols, tables, tick, emit, final):
    cos_t, sin_t = tables
    t = kept
    df, d_norm_mlp_post = _norm_bwd(t["f"], _after(w["norm_mlp_post"], tick(dx_out, 0)), dx_out, None, BF16, "norm_bwd_mlp_post")
    dup = _mm(df, w["w_down"], "nt", [BF16], name="mm_d_act", epilogue=_relu2_bwd, extras=(t["up"],))
    emit((4, 5), [(t["h2"], dup), (t["act"], df)])
    dh2 = _mm(dup, w["w_up"], "nt", [BF16], name="mm_d_h2", tk=4096)
    dx1, d_norm_mlp_pre = _norm_bwd(t["x1"], _after(w["norm_mlp_pre"], tick(dh2, 1)), dh2, dx_out, F32, "norm_bwd_mlp_pre")
    dmixed, d_norm_mix_post = _norm_bwd(t["mixed"], w["norm_mix_post"], dx1, None, BF16, "norm_bwd_mix_post")
    dmix = _mm(dmixed, w["w_merge"], "nt", [BF16], name="mm_d_mix")
    dy_a, dy_b, dz_ga, dz_gb, dbias_a, dbias_b = _gate_bwd(t["z"], _after(w["gate_bias"], tick(dmix, 2)), t["y_a"], t["y_b"], dmix, cols)
    da = _mm(dy_a, w["w_out_conv"], "nt", [BF16], name="mm_d_a")
    emit((1, 2, 3), [(t["a"], dy_a), (t["o"], dy_b), (t["mix"], dmixed)])
    do = _mm(dy_b, w["w_out_attn"], "nt", [BF16], name="mm_d_o")
    dz_cb, dz_cc, dz_ci, d_conv_w = _conv_bwd(t["z"], _after(w["conv_w"], tick(do, 3)), da, cols)
    dq, dk_t, dv_t = _attn_bwd(t["q"], t["k"], t["v"], do, t["lse"])
    dz_q, dz_k, dz_v, d_q_norm, d_k_norm = _rope_bwd(t["z"], _after(w["q_norm"], tick(dv_t, 4)), w["k_norm"], cos_t, sin_t, dq, dk_t, dv_t, cols)
    dz = jnp.concatenate([dz_cb, dz_cc, dz_ci, dz_q, dz_k, dz_v, dz_ga, dz_gb], axis=1)
    emit((0,), [(t["h"], dz)])
    tick(dz, 5)
    if final:
        tick(None, 5)
    dh = _mm(dz, w["w_in"], "nt", [BF16], name="mm_d_h", tk=3328)
    dx_in, d_norm_mix_pre = _norm_bwd(t["x"], _after(w["norm_mix_pre"], tick(dh, 6)), dh, dx1, F32, "norm_bwd_mix_pre")
    small = dict(norm_mix_pre=d_norm_mix_pre, gate_bias=jnp.concatenate([dbias_a, dbias_b], axis=1), conv_w=d_conv_w, q_norm=d_q_norm,
                 k_norm=d_k_norm, norm_mix_post=d_norm_mix_post, norm_mlp_pre=d_norm_mlp_pre, norm_mlp_post=d_norm_mlp_post)
    return dx_in, small


SMALL = ("norm_mix_pre", "gate_bias", "conv_w", "q_norm", "k_norm", "norm_mix_post", "norm_mlp_pre", "norm_mlp_post")
WEIGHTS = ("norm_mix_pre", "w_in", "gate_bias", "conv_w", "q_norm", "k_norm", "w_out_conv", "w_out_attn", "w_merge",
           "norm_mix_post", "norm_mlp_pre", "w_up", "w_down", "norm_mlp_post")


def _pack(parts):
    flat = jnp.concatenate([a.reshape(-1) for a in parts])
    rows = -(-flat.shape[0] // LANES)
    pad = (-rows) % 8
    flat = jnp.pad(flat, (0, (rows + pad) * LANES - flat.shape[0]))
    return flat.reshape(rows + pad, LANES)


def _unpack(packed, shapes):
    flat = packed.reshape(-1)
    out, off = [], 0
    for shp in shapes:
        n = math.prod(shp)
        out.append(flat[off:off + n].reshape(shp))
        off += n
    return out


def kernel(x, norm_mix_pre, w_in, gate_bias, conv_w, q_norm, k_norm, w_out_conv, w_out_attn, w_merge, norm_mix_post, norm_mlp_pre, w_up, w_down, norm_mlp_post, loss_target, m_norm_mix_pre, m_w_in, m_gate_bias, m_conv_w, m_q_norm, m_k_norm, m_w_out_conv, m_w_out_attn, m_w_merge, m_norm_mix_post, m_norm_mlp_pre, m_w_up, m_w_down, m_norm_mlp_post, v_norm_mix_pre, v_w_in, v_gate_bias, v_conv_w, v_q_norm, v_k_norm, v_w_out_conv, v_w_out_attn, v_w_merge, v_norm_mix_post, v_norm_mlp_pre, v_w_up, v_w_down, v_norm_mlp_post):
    w = dict(norm_mix_pre=norm_mix_pre, w_in=w_in, gate_bias=gate_bias, conv_w=conv_w, q_norm=q_norm, k_norm=k_norm, w_out_conv=w_out_conv,
             w_out_attn=w_out_attn, w_merge=w_merge, norm_mix_post=norm_mix_post, norm_mlp_pre=norm_mlp_pre, w_up=w_up, w_down=w_down,
             norm_mlp_post=norm_mlp_post)
    mom = dict(norm_mix_pre=m_norm_mix_pre, w_in=m_w_in, gate_bias=m_gate_bias, conv_w=m_conv_w, q_norm=m_q_norm, k_norm=m_k_norm,
               w_out_conv=m_w_out_conv, w_out_attn=m_w_out_attn, w_merge=m_w_merge, norm_mix_post=m_norm_mix_post, norm_mlp_pre=m_norm_mlp_pre,
               w_up=m_w_up, w_down=m_w_down, norm_mlp_post=m_norm_mlp_post)
    var = dict(norm_mix_pre=v_norm_mix_pre, w_in=v_w_in, gate_bias=v_gate_bias, conv_w=v_conv_w, q_norm=v_q_norm, k_norm=v_k_norm,
               w_out_conv=v_w_out_conv, w_out_attn=v_w_out_attn, w_merge=v_w_merge, norm_mix_post=v_norm_mix_post, norm_mlp_pre=v_norm_mlp_pre,
               w_up=v_w_up, w_down=v_w_down, norm_mlp_post=v_norm_mlp_post)
    depth = w_in.shape[0]
    _, s, d = x.shape
    cols = _Cols(d)
    x0 = x.reshape(s, d)
    target = loss_target.reshape(s, d)
    tables = _rope_tables(s)
    chip = (2 * lax.axis_index("x") + lax.axis_index("y")).astype(jnp.int32)
    core = lax.axis_index("c").astype(jnp.int32)
    chip_vec, core_vec, other_core_vec = chip.reshape(1), core.reshape(1), (1 - core).reshape(1)

    n_conv = conv_w.shape[-1]
    placed = lax.dynamic_update_slice_in_dim(jnp.zeros((depth, conv_w.shape[1], n_conv * N_CHIPS), F32), conv_w, chip * n_conv, axis=2)
    conv_full = _unpack(_allreduce_small(_pack([jnp.where(core == 0, placed, 0.0)]), "gather_conv_w"), [placed.shape])[0]

    chains = _Chains()
    chains.pin(conv_full)
    full = [{} for _ in range(depth)]
    everything = tuple(range(N_BIG))
    gathers = []
    for l in range(depth):
        bufs = [_cast_place(w[name], l, chip_vec, col, "cast_place_" + name) for name, col in BIG]
        for group in ((0,), everything[1:]):
            if group[0] == 0:
                wait_for = ("fwd", l - 1, 3) if l else None
            else:
                wait_for = ("fwd", l, 0 if l else 1)
            gathers.append(_gather_chain(chains, f"{l}_{group[0]}", group, [bufs[a] for a in group], full[l], wait_for))
            next(gathers[-1])
    next(gathers[0])
    next(gathers[0], None)
    for g in gathers[1:]:
        chains.add(g)

    def layer_params(l):
        p = dict(full[l])
        p["conv_w"] = conv_full[l]
        p["gate_bias"] = gate_bias[l].reshape(1, -1)
        for name in ("norm_mix_pre", "q_norm", "k_norm", "norm_mix_post", "norm_mlp_pre", "norm_mlp_post"):
            p[name] = w[name][l].reshape(1, -1)
        if l + 1 < depth:
            p["norm_next"] = w["norm_mix_pre"][l + 1].reshape(1, -1)
        return p

    kept = []
    xl, h = x0, _norm_first(x0, norm_mix_pre[0].reshape(1, -1))
    for l in range(depth):
        last = l == depth - 1
        (xl, h), t = _layer_fwd(xl, h, lambda name, l=l: layer_params(l)[name], cols, tables,
                                lambda dep, k, l=l: chains.tick(dep, ("fwd", l, k)), last, target if last else None)
        kept.append(t)
    dx, loss_local = xl, h

    carried = {}
    small = [None] * depth
    for l in reversed(range(depth)):
        def emit(group, pairs, l=l):
            chains.add(_grad_chain(chains, l, group, pairs, core_vec, other_core_vec, chip_vec, w, mom, var, carried))

        dx, small[l] = _layer_bwd(dx, layer_params(l), kept[l], cols, tables, lambda dep, k: chains.tick(dep), emit, l == 0)
    while chains.active:
        chains.tick(None)
    grads = {name: carried[name][0] for name, _ in BIG}
    delta = {name: carried[name][1] for name, _ in BIG}
    new_m = {name: carried[name][2] for name, _ in BIG}
    new_v = {name: carried[name][3] for name, _ in BIG}

    small_full_shapes = [(depth,) + small[0][name].shape for name in SMALL]
    packed = _pack([jnp.stack([small[l][name] for l in range(depth)]) for name in SMALL] + [jnp.broadcast_to(loss_local.reshape(1), (LANES,))])
    small_sum = _unpack(_allreduce_small(packed, "allreduce_small"), small_full_shapes + [(LANES,)])
    loss = small_sum[-1][0]
    for name, g in zip(SMALL, small_sum[:-1]):
        if name == "conv_w":
            g = lax.dynamic_slice_in_dim(g, chip * n_conv, n_conv, axis=2)
        grads[name] = g.reshape(w[name].shape)
    small_shapes = [w[name].shape for name in SMALL]
    packs = [_pack([src[name] for name in SMALL]) for src in (w, grads, mom, var)]
    for dst, out in zip((delta, new_m, new_v), _adamw(*packs, "adamw_small")):
        for name, val in zip(SMALL, _unpack(out, small_shapes)):
            dst[name] = val

    grad_x = dx.reshape(x.shape)
    return (loss, grad_x, *[grads[n] for n in WEIGHTS], *[delta[n] for n in WEIGHTS], *[new_m[n] for n in WEIGHTS], *[new_v[n] for n in WEIGHTS])
```
